```python
import math
import jax, jax.numpy as jnp
from jax import lax
import numpy as np

D_MODEL = 1024
BATCH = 4
SEQ = 4096
DEPTH = 2
DEC_BATCH = 8
DEC_SEQ = 4096
PAST_LEN = 128

N_META = 16
N_MIXERS = 2
N_GDN_LAYERS = (DEPTH + 1) // 2
N_MLA_LAYERS = DEPTH // 2
NORM_EPS = 1e-6

GDN_HEADS = 8
GDN_DK = 128
GDN_DV = 256
GDN_CONV = 5
GDN_CHUNK = 64
GDN_QK_WIDTH = GDN_HEADS * GDN_DK
GDN_V_WIDTH = GDN_HEADS * GDN_DV
GDN_CONV_CH = 2 * GDN_QK_WIDTH + GDN_V_WIDTH
GDN_IN = GDN_CONV_CH + GDN_V_WIDTH + 4 * GDN_HEADS

MLA_HEADS = 16
MLA_Q_LORA = 512
MLA_KV_LORA = 256
MLA_NOPE = 128
MLA_ROPE = 64
MLA_DQK = MLA_NOPE + MLA_ROPE
MLA_DV = 128
MLA_V_WIDTH = MLA_HEADS * MLA_DV
MLA_IN = MLA_Q_LORA + MLA_KV_LORA + MLA_ROPE + MLA_V_WIDTH
ROPE_THETA = 10000.0
Q_BLOCK = 128

kernel_name = 'hybrid_gdn_mla_bidir_encoder'


def _rmsnorm(x, g):
    xf = x.astype(jnp.float32)
    y = xf * lax.rsqrt(jnp.mean(xf * xf, axis=-1, keepdims=True) + NORM_EPS)
    return (y * g.astype(jnp.float32)).astype(x.dtype)


def _l2norm(x):
    xf = x.astype(jnp.float32)
    return xf * lax.rsqrt(jnp.sum(xf * xf, axis=-1, keepdims=True) + NORM_EPS)


def _gated_delta_chunked(q, k, v, beta, g):
    B, T, H, DK = q.shape
    DV = v.shape[-1]
    C = GDN_CHUNK
    N = T // C

    def blk(t):
        t = jnp.moveaxis(t, 2, 1)
        return t.reshape((B, H, N, C) + t.shape[3:])

    q, k, v, beta, g = blk(q), blk(k), blk(v), blk(beta), blk(g)
    gc = jnp.cumsum(g, axis=-1)
    gl = gc[..., -1]
    diff = gc[..., :, None] - gc[..., None, :]
    idx = jnp.arange(C)
    strict = idx[:, None] > idx[None, :]
    incl = idx[:, None] >= idx[None, :]
    decay_strict = jnp.exp(jnp.where(strict, diff, -jnp.inf))
    decay_incl = jnp.exp(jnp.where(incl, diff, -jnp.inf))
    kb = k * beta[..., None]
    a = jnp.einsum('bhnid,bhnjd->bhnij', kb, k) * decay_strict + jnp.eye(C, dtype=q.dtype)
    rhs = jnp.concatenate([v * beta[..., None], kb * jnp.exp(gc)[..., None]], axis=-1)
    sol = lax.linalg.triangular_solve(a, rhs, left_side=True, lower=True, unit_diagonal=True)
    u, w = sol[..., :DV], sol[..., DV:]
    attn = jnp.einsum('bhnid,bhnjd->bhnij', q, k) * decay_incl
    qg = q * jnp.exp(gc)[..., None]
    kd = k * jnp.exp(gl[..., None] - gc)[..., None]

    def step(S, inp):
        qg_c, w_c, u_c, kd_c, attn_c, gl_c = inp
        v_new = u_c - jnp.einsum('bhcd,bhde->bhce', w_c, S)
        o = jnp.einsum('bhcd,bhde->bhce', qg_c, S) + jnp.einsum('bhij,bhje->bhie', attn_c, v_new)
        S = S * jnp.exp(gl_c)[..., None, None] + jnp.einsum('bhcd,bhce->bhde', kd_c, v_new)
        return S, o

    xs = (jnp.moveaxis(qg, 2, 0), jnp.moveaxis(w, 2, 0), jnp.moveaxis(u, 2, 0),
          jnp.moveaxis(kd, 2, 0), jnp.moveaxis(attn, 2, 0), jnp.moveaxis(gl, 2, 0))
    S0 = jnp.zeros((B, H, DK, DV), q.dtype)
    _, o = lax.scan(step, S0, xs)
    o = jnp.moveaxis(o, 0, 2).reshape(B, H, T, DV)
    return jnp.moveaxis(o, 1, 2)


def _pad_seq(t, front, back):
    return jnp.pad(t, ((0, 0), (front, back)) + ((0, 0),) * (t.ndim - 2))


def _bidir_gated_delta(q, k, v, beta, g):
    L = q.shape[1]
    P = (-N_META) % GDN_CHUNK
    fwd = _gated_delta_chunked(_pad_seq(q, P, 0), _pad_seq(k, P, 0), _pad_seq(v, P, 0),
                               _pad_seq(beta[:, :, 0], P, 0), _pad_seq(g[:, :, 0], P, 0))[:, P:]
    qr, kr, vr = jnp.flip(q, 1), jnp.flip(k, 1), jnp.flip(v, 1)
    br, grv = jnp.flip(beta[:, :, 1], 1), jnp.flip(g[:, :, 1], 1)
    bwd = _gated_delta_chunked(_pad_seq(qr, 0, P), _pad_seq(kr, 0, P), _pad_seq(vr, 0, P),
                               _pad_seq(br, 0, P), _pad_seq(grv, 0, P))[:, :L]
    return fwd + jnp.flip(bwd, 1)


def _gdn_mixer(h, w_in, conv_w, a_log, dt_bias, o_norm_g, w_out):
    B, L, _ = h.shape
    proj = h @ w_in
    qkv = proj[..., :GDN_CONV_CH]
    z = proj[..., GDN_CONV_CH:GDN_CONV_CH + GDN_V_WIDTH]
    ba = proj[..., GDN_CONV_CH + GDN_V_WIDTH:].reshape(B, L, 2, 2, GDN_HEADS)
    half = GDN_CONV // 2
    qkv_p = jnp.pad(qkv, ((0, 0), (half, half), (0, 0)))
    conv = qkv_p[:, 0:L] * conv_w[0]
    for j in range(1, GDN_CONV):
        conv = conv + qkv_p[:, j:j + L] * conv_w[j]
    qkv = jax.nn.silu(conv)
    q = _l2norm(qkv[..., :GDN_QK_WIDTH].reshape(B, L, GDN_HEADS, GDN_DK)) * (GDN_DK ** -0.5)
    k = _l2norm(qkv[..., GDN_QK_WIDTH:2 * GDN_QK_WIDTH].reshape(B, L, GDN_HEADS, GDN_DK))
    v = qkv[..., 2 * GDN_QK_WIDTH:].reshape(B, L, GDN_HEADS, GDN_DV).astype(jnp.float32)
    beta = jax.nn.sigmoid(ba[:, :, 0].astype(jnp.float32))
    g = -jnp.exp(a_log.astype(jnp.float32)) * jax.nn.softplus(
        ba[:, :, 1].astype(jnp.float32) + dt_bias.astype(jnp.float32))
    o = _bidir_gated_delta(q, k, v, beta, g)
    o = _rmsnorm(o, o_norm_g).astype(h.dtype)
    y = o.reshape(B, L, GDN_V_WIDTH) * jax.nn.silu(z)
    return y @ w_out


def _rope(x, cos, sin):
    r = x.shape[-1] // 2
    x1, x2 = x[..., :r], x[..., r:]
    return jnp.concatenate([x1 * cos - x2 * sin, x2 * cos + x1 * sin], axis=-1).astype(x.dtype)


def _block_attention(q, k, v):
    B, L, H, Dq = q.shape
    nb = -(-L // Q_BLOCK)
    qp = _pad_seq(q, 0, nb * Q_BLOCK - L)
    qb = jnp.moveaxis(qp.reshape(B, nb, Q_BLOCK, H, Dq), 1, 0)
    scale = Dq ** -0.5

    def one(qblk):
        s = jnp.einsum('bqhd,bkhd->bhqk', qblk, k, preferred_element_type=jnp.float32) * scale
        p = jax.nn.softmax(s, axis=-1)
        return jnp.einsum('bhqk,bkhd->bqhd', p.astype(v.dtype), v)

    o = lax.map(one, qb)
    return jnp.moveaxis(o, 0, 1).reshape(B, nb * Q_BLOCK, H, v.shape[-1])[:, :L]


def _mla_mixer(h, w_in, q_norm_g, kv_norm_g, w_uq, w_ukv, qk_q_g, qk_k_g, w_out):
    B, L, _ = h.shape
    proj = h @ w_in
    o1 = MLA_Q_LORA
    o2 = o1 + MLA_KV_LORA
    o3 = o2 + MLA_ROPE
    cq = _rmsnorm(proj[..., :o1], q_norm_g)
    ckv = _rmsnorm(proj[..., o1:o2], kv_norm_g)
    k_pe = proj[..., o2:o3]
    z = proj[..., o3:]
    q = (cq @ w_uq).reshape(B, L, MLA_HEADS, MLA_DQK)
    kv = (ckv @ w_ukv).reshape(B, L, MLA_HEADS, MLA_NOPE + MLA_DV)
    k_nope, v = kv[..., :MLA_NOPE], kv[..., MLA_NOPE:]
    k = jnp.concatenate([k_nope, jnp.broadcast_to(k_pe[:, :, None, :], (B, L, MLA_HEADS, MLA_ROPE))], axis=-1)
    q = _rmsnorm(q, qk_q_g)
    k = _rmsnorm(k, qk_k_g)
    pos = jnp.arange(L, dtype=jnp.float32)
    inv = ROPE_THETA ** (-jnp.arange(0, MLA_ROPE, 2, dtype=jnp.float32) / MLA_ROPE)
    ang = pos[:, None] * inv[None, :]
    cos = jnp.cos(ang)[:, None, :]
    sin = jnp.sin(ang)[:, None, :]
    q = jnp.concatenate([q[..., :MLA_NOPE], _rope(q[..., MLA_NOPE:], cos, sin)], axis=-1)
    k = jnp.concatenate([k[..., :MLA_NOPE], _rope(k[..., MLA_NOPE:], cos, sin)], axis=-1)
    o = _block_attention(q, k, v)
    y = o.reshape(B, L, MLA_V_WIDTH) * jax.nn.silu(z)
    return y @ w_out


def _trunk(x, meta_tokens, ln_g, gdn_w_in, gdn_conv_w, gdn_a_log, gdn_dt_bias, gdn_o_norm_g, gdn_w_out,
           mla_w_in, mla_q_norm_g, mla_kv_norm_g, mla_w_uq, mla_w_ukv, mla_qk_q_g, mla_qk_k_g, mla_w_out):
    B = x.shape[0]
    meta = jnp.broadcast_to(meta_tokens[None].astype(x.dtype), (B, N_META, D_MODEL))
    h = jnp.concatenate([meta, x], axis=1)
    for i in range(DEPTH):
        hn = _rmsnorm(h, ln_g[i])
        j = i // N_MIXERS
        if i % N_MIXERS == 0:
            out = _gdn_mixer(hn, gdn_w_in[j], gdn_conv_w[j], gdn_a_log[j], gdn_dt_bias[j],
                             gdn_o_norm_g[j], gdn_w_out[j])
        else:
            out = _mla_mixer(hn, mla_w_in[j], mla_q_norm_g[j], mla_kv_norm_g[j], mla_w_uq[j],
                             mla_w_ukv[j], mla_qk_q_g[j], mla_qk_k_g[j], mla_w_out[j])
        h = h + out
    return h[:, N_META:]


def setup_inputs(seed: int = 0) -> dict:
    key = jax.random.key(seed)
    ks = jax.random.split(key, 20)
    f32 = jnp.float32

    def nrm(k, shape, scale):
        return jax.random.normal(k, shape, f32) * scale

    def gain(k, shape):
        return 1.0 + 0.02 * jax.random.normal(k, shape, f32)

    dt = jnp.exp(jax.random.uniform(ks[6], (N_GDN_LAYERS, 2, GDN_HEADS), f32,
                                    math.log(0.001), math.log(0.1)))
    dt_bias = dt + jnp.log(-jnp.expm1(-dt))
    a_log = jnp.log(jax.random.uniform(ks[5], (N_GDN_LAYERS, 2, GDN_HEADS), f32, 1.0, 16.0))
    return {
        'x_prompt': nrm(ks[0], (BATCH, SEQ, D_MODEL), 1.0),
        'x_sample': nrm(ks[1], (DEC_BATCH, DEC_SEQ, D_MODEL), 1.0),
        'meta_tokens': nrm(ks[2], (N_META, D_MODEL), 1.0),
        'ln_g': gain(ks[3], (DEPTH, D_MODEL)),
        'gdn_w_in': nrm(ks[4], (N_GDN_LAYERS, D_MODEL, GDN_IN), D_MODEL ** -0.5),
        'gdn_conv_w': nrm(ks[7], (N_GDN_LAYERS, GDN_CONV, GDN_CONV_CH), GDN_CONV ** -0.5),
        'gdn_a_log': a_log,
        'gdn_dt_bias': dt_bias,
        'gdn_o_norm_g': gain(ks[8], (N_GDN_LAYERS, GDN_DV)),
        'gdn_w_out': nrm(ks[9], (N_GDN_LAYERS, GDN_V_WIDTH, D_MODEL), GDN_V_WIDTH ** -0.5),
        'mla_w_in': nrm(ks[10], (N_MLA_LAYERS, D_MODEL, MLA_IN), D_MODEL ** -0.5),
        'mla_q_norm_g': gain(ks[11], (N_MLA_LAYERS, MLA_Q_LORA)),
        'mla_kv_norm_g': gain(ks[12], (N_MLA_LAYERS, MLA_KV_LORA)),
        'mla_w_uq': nrm(ks[13], (N_MLA_LAYERS, MLA_Q_LORA, MLA_HEADS * MLA_DQK), MLA_Q_LORA ** -0.5),
        'mla_w_ukv': nrm(ks[14], (N_MLA_LAYERS, MLA_KV_LORA, MLA_HEADS * (MLA_NOPE + MLA_DV)), MLA_KV_LORA ** -0.5),
        'mla_qk_q_g': gain(ks[15], (N_MLA_LAYERS, MLA_DQK)),
        'mla_qk_k_g': gain(ks[16], (N_MLA_LAYERS, MLA_DQK)),
        'mla_w_out': nrm(ks[17], (N_MLA_LAYERS, MLA_V_WIDTH, D_MODEL), MLA_V_WIDTH ** -0.5),
    }


def reference(x_prompt, x_sample, meta_tokens, ln_g, gdn_w_in, gdn_conv_w, gdn_a_log, gdn_dt_bias,
              gdn_o_norm_g, gdn_w_out, mla_w_in, mla_q_norm_g, mla_kv_norm_g, mla_w_uq, mla_w_ukv,
              mla_qk_q_g, mla_qk_k_g, mla_w_out):
    y_prompt = _trunk(x_prompt, meta_tokens, ln_g, gdn_w_in, gdn_conv_w, gdn_a_log, gdn_dt_bias,
                      gdn_o_norm_g, gdn_w_out, mla_w_in, mla_q_norm_g, mla_kv_norm_g, mla_w_uq,
                      mla_w_ukv, mla_qk_q_g, mla_qk_k_g, mla_w_out)
    y_sample = _trunk(x_sample, meta_tokens, ln_g, gdn_w_in, gdn_conv_w, gdn_a_log, gdn_dt_bias,
                      gdn_o_norm_g, gdn_w_out, mla_w_in, mla_q_norm_g, mla_kv_norm_g, mla_w_uq,
                      mla_w_ukv, mla_qk_q_g, mla_qk_k_g, mla_w_out)
    return (y_prompt, y_sample)
```

```python
import functools
import math

import jax
import jax.numpy as jnp
from jax import lax
from jax.experimental import pallas as pl
from jax.experimental.pallas import tpu as pltpu

F32 = jnp.float32
BF16 = jnp.bfloat16

D_MODEL = 1024
N_META = 16
TAIL = 128
NORM_EPS = 1e-6

GDN_HEADS = 8
GDN_DK = 128
GDN_DV = 256
GDN_CONV = 5
GDN_CHUNK = 64
GDN_QK = GDN_HEADS * GDN_DK
GDN_V = GDN_HEADS * GDN_DV
GDN_CONV_CH = 2 * GDN_QK + GDN_V

MLA_HEADS = 16
MLA_Q_LORA = 512
MLA_KV_LORA = 256
MLA_NOPE = 128
MLA_ROPE = 64
MLA_DQK = MLA_NOPE + MLA_ROPE
MLA_DV = 128
MLA_V = MLA_HEADS * MLA_DV
MLA_DPAD = 256
ROPE_THETA = 10000.0

VMEM_LIMIT = 56 * 1024 * 1024


def _cparams(sem):
    return pltpu.CompilerParams(dimension_semantics=sem, vmem_limit_bytes=VMEM_LIMIT)


def _tile(n, target, mult):
    best = None
    t = mult
    while t <= min(n, target):
        if n % t == 0:
            best = t
        t += mult
    assert best is not None, (n, target, mult)
    return best


def _norm_matmul_kernel(x_ref, g_ref, w_ref, o_ref, xn_ref):
    @pl.when(pl.program_id(1) == 0)
    def _():
        x = x_ref[...]
        ms = jnp.mean(x * x, axis=-1, keepdims=True)
        xn_ref[...] = (x * lax.rsqrt(ms + NORM_EPS) * g_ref[...]).astype(BF16)

    o_ref[...] = jnp.dot(xn_ref[...], w_ref[...], preferred_element_type=F32)


def _norm_matmul(x, g, w, tn):
    rows, d = x.shape
    n = w.shape[1]
    tm = _tile(rows, 1408, 128)
    return pl.pallas_call(
        _norm_matmul_kernel,
        grid=(rows // tm, n // tn),
        in_specs=[
            pl.BlockSpec((tm, d), lambda i, j: (i, 0)),
            pl.BlockSpec((1, d), lambda i, j: (0, 0)),
            pl.BlockSpec((d, tn), lambda i, j: (0, j)),
        ],
        out_specs=pl.BlockSpec((tm, tn), lambda i, j: (i, j)),
        out_shape=jax.ShapeDtypeStruct((rows, n), F32),
        scratch_shapes=[pltpu.VMEM((tm, d), BF16)],
        compiler_params=_cparams(("parallel", "arbitrary")),
        name="norm_matmul",
    )(x, g, w)


CONV_SUB = 32
CONV_TC = 1024


def _conv_kernel(prev_ref, main_ref, next_ref, w_ref, o_ref, ext_ref, *, tr):
    j = pl.program_id(2)
    half = GDN_CONV // 2
    ext_ref[0:8, :] = prev_ref[0]
    ext_ref[8:8 + tr, :] = main_ref[0]
    ext_ref[8 + tr:16 + tr, :] = next_ref[0]
    is_qk = j < 2
    qscale = jnp.where(j == 0, GDN_DK ** -0.5, 1.0).astype(F32)
    for sb in range(tr // CONV_SUB):
        r0 = 8 + sb * CONV_SUB - half
        acc = ext_ref[r0:r0 + CONV_SUB, :] * w_ref[0:1, :]
        for t in range(1, GDN_CONV):
            acc = acc + ext_ref[r0 + t:r0 + t + CONV_SUB, :] * w_ref[t:t + 1, :]
        y = acc * jax.nn.sigmoid(acc)
        rows = slice(sb * CONV_SUB, (sb + 1) * CONV_SUB)
        for h in range(CONV_TC // GDN_DK):
            lanes = slice(h * GDN_DK, (h + 1) * GDN_DK)
            yh = y[:, lanes]
            ss = jnp.sum(yh * yh, axis=-1, keepdims=True)
            inv = jnp.where(is_qk, lax.rsqrt(ss + NORM_EPS) * qscale, 1.0)
            o_ref[0, rows, lanes] = (yh * inv).astype(BF16)


def _gdn_conv(proj3, conv_w8, lp):
    b = proj3.shape[0]
    tr = _tile(lp, 384, CONV_SUB)
    nb8 = lp // 8
    tb = tr // 8
    kern = functools.partial(_conv_kernel, tr=tr)
    return pl.pallas_call(
        kern,
        grid=(b, lp // tr, GDN_CONV_CH // CONV_TC),
        in_specs=[
            pl.BlockSpec((1, 8, CONV_TC), lambda bi, i, j: (bi, (i * tb + nb8 - 1) % nb8, j)),
            pl.BlockSpec((1, tr, CONV_TC), lambda bi, i, j: (bi, i, j)),
            pl.BlockSpec((1, 8, CONV_TC), lambda bi, i, j: (bi, ((i + 1) * tb) % nb8, j)),
            pl.BlockSpec((8, CONV_TC), lambda bi, i, j: (0, j)),
        ],
        out_specs=pl.BlockSpec((1, tr, CONV_TC), lambda bi, i, j: (bi, i, j)),
        out_shape=jax.ShapeDtypeStruct((b, lp, GDN_CONV_CH), BF16),
        scratch_shapes=[pltpu.VMEM((tr + 16, CONV_TC), F32)],
        compiler_params=_cparams(("parallel", "parallel", "arbitrary")),
        name="gdn_conv",
    )(proj3, proj3, proj3, conv_w8)


def _split3(x):
    hi = x.astype(BF16)
    r1 = x - hi.astype(F32)
    mid = r1.astype(BF16)
    lo = (r1 - mid.astype(F32)).astype(BF16)
    return hi, mid, lo


def _bdot(a, b):
    return jnp.dot(a.astype(BF16), b.astype(BF16), preferred_element_type=F32)


def _unit_lower_inverse(a, eye, diag16):
    ad = jnp.where(diag16, a, 0.0)
    an = a - ad
    a2 = _bdot(ad, ad)
    a4 = _bdot(a2, a2)
    a8 = _bdot(a4, a4)
    dinv = eye - ad
    dinv = dinv + _bdot(dinv, a2)
    dinv = dinv + _bdot(dinv, a4)
    dinv = dinv + _bdot(dinv, a8)
    m = _bdot(dinv, an)
    m2 = _bdot(m, m)
    x = eye - m
    x = x + _bdot(x, m2)
    return _bdot(x, dinv)


def _gdn_scan_kernel(q_ref, k_ref, v_ref, ba_ref, alog_ref, dtb_ref, o_ref, s_ref, *, nc):
    d = pl.program_id(1)
    t = pl.program_id(2)
    c = GDN_CHUNK
    c2 = 2 * c

    @pl.when(t == 0)
    def _():
        s_ref[...] = jnp.zeros_like(s_ref)

    seq = jnp.where(d == 0, t, nc - 1 - t)
    blk = jnp.where(seq < 2, nc - 2 + seq, seq - 2)
    first_valid = jnp.where(blk == nc - 2, c, jnp.where(blk == nc - 1, c - N_META, 0))
    valid = lax.broadcasted_iota(jnp.int32, (c, 128), 0) >= first_valid

    ba = ba_ref[0]
    beta = jnp.where(valid, jax.nn.sigmoid(ba), 0.0)
    xs = ba + dtb_ref[0]
    softplus = jnp.maximum(xs, 0.0) + jnp.log(1.0 + jnp.exp(-jnp.abs(xs)))
    g = jnp.where(valid, -jnp.exp(alog_ref[0]) * softplus, 0.0)

    ri = lax.broadcasted_iota(jnp.int32, (c2, c2), 0)
    ci = lax.broadcasted_iota(jnp.int32, (c2, c2), 1)
    same = (ri >> 6) == (ci >> 6)
    sgn = jnp.where(d == 0, 1, -1)
    incl = same & ((ri - ci) * sgn >= 0)
    offdiag = ri != ci
    diag16 = (ri >> 4) == (ci >> 4)
    eye = jnp.where(ri == ci, 1.0, 0.0).astype(F32)

    g2 = jnp.concatenate([g, g], axis=0)
    beta2 = jnp.concatenate([beta, beta], axis=0)
    tri = jnp.where(incl, 1.0, 0.0).astype(BF16)
    ghi, gmid, glo = _split3(g2)
    gc2 = (jnp.dot(tri, ghi, preferred_element_type=F32)
           + jnp.dot(tri, gmid, preferred_element_type=F32)
           + jnp.dot(tri, glo, preferred_element_type=F32))
    gc2t = gc2.T
    gtot = jnp.sum(g, axis=0, keepdims=True)

    top = lax.broadcasted_iota(jnp.int32, (c2, 1), 0) < c
    left = lax.broadcasted_iota(jnp.int32, (1, c2), 1) < c

    for p in range(GDN_HEADS // 2):
        h0, h1 = 2 * p, 2 * p + 1
        a0, a1 = GDN_HEADS + h0, GDN_HEADS + h1
        col = jnp.where(top, gc2[:, a0:a0 + 1], gc2[:, a1:a1 + 1])
        row = jnp.where(left, gc2t[a0:a0 + 1, :], gc2t[a1:a1 + 1, :])
        bcol = jnp.where(top, beta2[:, h0:h0 + 1], beta2[:, h1:h1 + 1])
        tot = jnp.where(top, gtot[:, a0:a0 + 1], gtot[:, a1:a1 + 1])
        dec = jnp.exp(jnp.where(incl, col - row, -jnp.inf))

        kst = jnp.concatenate([k_ref[0, :, h0 * GDN_DK:(h0 + 1) * GDN_DK],
                               k_ref[0, :, h1 * GDN_DK:(h1 + 1) * GDN_DK]], axis=0)
        qst = jnp.concatenate([q_ref[0, :, h0 * GDN_DK:(h0 + 1) * GDN_DK],
                               q_ref[0, :, h1 * GDN_DK:(h1 + 1) * GDN_DK]], axis=0)
        vst = jnp.concatenate([v_ref[0, :, h0 * GDN_DV:(h0 + 1) * GDN_DV],
                               v_ref[0, :, h1 * GDN_DV:(h1 + 1) * GDN_DV]], axis=0)
        kf = kst.astype(F32)
        kb = kf * bcol
        sc = lax.dot_general(jnp.concatenate([qst, kb.astype(BF16)], axis=0), kst,
                             (((1,), (1,)), ((), ())), preferred_element_type=F32)
        attn = sc[:c2] * dec
        a = jnp.where(offdiag, sc[c2:] * dec, 0.0)
        tinv = _unit_lower_inverse(a, eye, diag16)

        egc = jnp.exp(col)
        rhs = jnp.concatenate([vst.astype(F32) * bcol, kb * egc], axis=1)
        uw = _bdot(tinv, rhs)
        u = uw[:, :GDN_DV]
        w = uw[:, GDN_DV:]
        qg = qst.astype(F32) * egc
        kd = (kf * jnp.exp(tot - col)).astype(BF16)

        vnew = []
        qs = []
        for hh, h in ((0, h0), (1, h1)):
            rs = slice(hh * c, (hh + 1) * c)
            s_bf = s_ref[h].astype(BF16)
            qw = jnp.concatenate([qg[rs], w[rs]], axis=0).astype(BF16)
            qws = jnp.dot(qw, s_bf, preferred_element_type=F32)
            qs.append(qws[:c])
            vnew.append((u[rs] - qws[c:]).astype(BF16))
        vnew2 = jnp.concatenate(vnew, axis=0)
        o2 = jnp.concatenate(qs, axis=0) + jnp.dot(attn.astype(BF16), vnew2, preferred_element_type=F32)
        o_ref[0, 0, :, h0 * GDN_DV:(h0 + 1) * GDN_DV] = o2[:c]
        o_ref[0, 0, :, h1 * GDN_DV:(h1 + 1) * GDN_DV] = o2[c:]
        for hh, h in ((0, h0), (1, h1)):
            rs = slice(hh * c, (hh + 1) * c)
            a_h = GDN_HEADS + h
            upd = lax.dot_general(kd[rs], vnew[hh], (((0,), (0,)), ((), ())),
                                  preferred_element_type=F32)
            s_ref[h] = s_ref[h] * jnp.exp(gtot[:, a_h:a_h + 1]) + upd


def _gdn_scan(qkv, proj3, alog_row, dtb_row, lp):
    b = qkv.shape[0]
    nc = lp // GDN_CHUNK
    ba_col0 = (GDN_CONV_CH + GDN_V) // 128

    def blk_of(d, t):
        seq = jnp.where(d == 0, t, nc - 1 - t)
        return jnp.where(seq < 2, nc - 2 + seq, seq - 2)

    kern = functools.partial(_gdn_scan_kernel, nc=nc)
    return pl.pallas_call(
        kern,
        grid=(b, 2, nc),
        in_specs=[
            pl.BlockSpec((1, GDN_CHUNK, GDN_QK), lambda bi, d, t: (bi, blk_of(d, t), 0)),
            pl.BlockSpec((1, GDN_CHUNK, GDN_QK), lambda bi, d, t: (bi, blk_of(d, t), 1)),
            pl.BlockSpec((1, GDN_CHUNK, GDN_V), lambda bi, d, t: (bi, blk_of(d, t), 1)),
            pl.BlockSpec((1, GDN_CHUNK, 128), lambda bi, d, t: (bi, blk_of(d, t), ba_col0 + d)),
            pl.BlockSpec((1, 1, 128), lambda bi, d, t: (d, 0, 0)),
            pl.BlockSpec((1, 1, 128), lambda bi, d, t: (d, 0, 0)),
        ],
        out_specs=pl.BlockSpec((1, 1, GDN_CHUNK, GDN_V), lambda bi, d, t: (d, bi, blk_of(d, t), 0)),
        out_shape=jax.ShapeDtypeStruct((2, b, lp, GDN_V), F32),
        scratch_shapes=[pltpu.VMEM((GDN_HEADS, GDN_DK, GDN_DV), F32)],
        compiler_params=_cparams(("parallel", "parallel", "arbitrary")),
        name="gdn_scan",
    )(qkv, qkv, qkv, proj3, alog_row, dtb_row)


def _gdn_out_kernel(of_ref, ob_ref, z_ref, h_ref, g_ref, w_ref, o_ref):
    o = of_ref[0] + ob_ref[0]
    z = z_ref[...]
    gate = z * jax.nn.sigmoid(z)
    ys = []
    for h in range(GDN_HEADS):
        lanes = slice(h * GDN_DV, (h + 1) * GDN_DV)
        oh = o[:, lanes]
        ms = jnp.mean(oh * oh, axis=-1, keepdims=True)
        ys.append((oh * lax.rsqrt(ms + NORM_EPS) * g_ref[...] * gate[:, lanes]).astype(BF16))
    y = jnp.concatenate(ys, axis=1)
    o_ref[...] = h_ref[...] + jnp.dot(y, w_ref[...], preferred_element_type=F32)


def _gdn_out(o2, proj, h0, g_row, w_out):
    rows = h0.shape[0]
    tm = _tile(rows, 256, 128)
    zcol = GDN_CONV_CH // GDN_V
    return pl.pallas_call(
        _gdn_out_kernel,
        grid=(rows // tm,),
        in_specs=[
            pl.BlockSpec((1, tm, GDN_V), lambda i: (0, i, 0)),
            pl.BlockSpec((1, tm, GDN_V), lambda i: (1, i, 0)),
            pl.BlockSpec((tm, GDN_V), lambda i: (i, zcol)),
            pl.BlockSpec((tm, D_MODEL), lambda i: (i, 0)),
            pl.BlockSpec((1, GDN_DV), lambda i: (0, 0)),
            pl.BlockSpec((GDN_V, D_MODEL), lambda i: (0, 0)),
        ],
        out_specs=pl.BlockSpec((tm, D_MODEL), lambda i: (i, 0)),
        out_shape=jax.ShapeDtypeStruct((rows, D_MODEL), F32),
        compiler_params=_cparams(("parallel",)),
        name="gdn_out",
    )(o2, o2, proj, h0, g_row, w_out)


def _rope(r, cos, nsin_lo, sin_hi):
    return r * cos + pltpu.roll(r, 96, 1) * nsin_lo + pltpu.roll(r, 32, 1) * sin_hi


def _mla_q_kernel(cq_ref, g1_ref, w_ref, ga_ref, gr_ref, cos_ref, nsin_ref, sin_ref, qt_ref):
    cq = cq_ref[0]
    ms = jnp.mean(cq * cq, axis=-1, keepdims=True)
    cqn = (cq * lax.rsqrt(ms + NORM_EPS) * g1_ref[...]).astype(BF16)
    q = jnp.dot(cqn, w_ref[...], preferred_element_type=F32)
    cos, nsin, sin = cos_ref[...], nsin_ref[...], sin_ref[...]
    for h in range(MLA_HEADS):
        a = q[:, h * MLA_DPAD:h * MLA_DPAD + MLA_NOPE]
        r = q[:, h * MLA_DPAD + MLA_NOPE:(h + 1) * MLA_DPAD]
        ss = jnp.sum(a * a, axis=-1, keepdims=True) + jnp.sum(r * r, axis=-1, keepdims=True)
        inv = lax.rsqrt(ss * (1.0 / MLA_DQK) + NORM_EPS)
        an = a * inv * ga_ref[...]
        rn = _rope(r * inv * gr_ref[...], cos, nsin, sin)
        qt_ref[0, h, 0:MLA_NOPE, :] = an.T.astype(BF16)
        qt_ref[0, h, MLA_NOPE:MLA_DPAD, :] = rn.T.astype(BF16)


def _mla_q(proj3, g1, w_uq, ga, gr, cos, nsin, sin, lp):
    b = proj3.shape[0]
    tm = _tile(lp, 384, 128)
    cqcol = MLA_V // MLA_Q_LORA
    const = lambda bi, i: (0, 0)
    return pl.pallas_call(
        _mla_q_kernel,
        grid=(b, lp // tm),
        in_specs=[
            pl.BlockSpec((1, tm, MLA_Q_LORA), lambda bi, i: (bi, i, cqcol)),
            pl.BlockSpec((1, MLA_Q_LORA), const),
            pl.BlockSpec((MLA_Q_LORA, MLA_HEADS * MLA_DPAD), const),
            pl.BlockSpec((1, 128), const),
            pl.BlockSpec((1, 128), const),
            pl.BlockSpec((tm, 128), lambda bi, i: (i, 0)),
            pl.BlockSpec((tm, 128), lambda bi, i: (i, 0)),
            pl.BlockSpec((tm, 128), lambda bi, i: (i, 0)),
        ],
        out_specs=pl.BlockSpec((1, MLA_HEADS, MLA_DPAD, tm), lambda bi, i: (bi, 0, 0, i)),
        out_shape=jax.ShapeDtypeStruct((b, MLA_HEADS, MLA_DPAD, lp), BF16),
        compiler_params=_cparams(("parallel", "parallel")),
        name="mla_q",
    )(proj3, g1, w_uq, ga, gr, cos, nsin, sin)


def _mla_kv_kernel(ckv_ref, kpe_ref, g1_ref, w_ref, ga_ref, gr_ref, cos_ref, nsin_ref, sin_ref,
                   k_ref, vt_ref):
    ckv = ckv_ref[0]
    ms = jnp.mean(ckv * ckv, axis=-1, keepdims=True)
    cn = (ckv * lax.rsqrt(ms + NORM_EPS) * g1_ref[...]).astype(BF16)
    kv = jnp.dot(cn, w_ref[...], preferred_element_type=F32)
    kpe = kpe_ref[0]
    ss_pe = jnp.sum(kpe * kpe, axis=-1, keepdims=True)
    kr = _rope(kpe * gr_ref[...], cos_ref[...], nsin_ref[...], sin_ref[...])
    for h in range(MLA_HEADS):
        kn = kv[:, h * MLA_NOPE:(h + 1) * MLA_NOPE]
        ss = jnp.sum(kn * kn, axis=-1, keepdims=True) + ss_pe
        inv = lax.rsqrt(ss * (1.0 / MLA_DQK) + NORM_EPS)
        k_ref[0, h, :, 0:MLA_NOPE] = (kn * inv * ga_ref[...]).astype(BF16)
        k_ref[0, h, :, MLA_NOPE:MLA_DPAD] = (kr * inv).astype(BF16)
        v = kv[:, MLA_HEADS * MLA_NOPE + h * MLA_DV:MLA_HEADS * MLA_NOPE + (h + 1) * MLA_DV]
        vt_ref[0, h] = v.T.astype(BF16)


def _mla_kv(proj3, g1, w_ukv, ga, gr, cos, nsin, sin, lp):
    b = proj3.shape[0]
    tm = _tile(lp, 384, 128)
    ckvcol = (MLA_V + MLA_Q_LORA) // MLA_KV_LORA
    kpecol = (MLA_V + MLA_Q_LORA + MLA_KV_LORA) // 128
    const = lambda bi, i: (0, 0)
    return pl.pallas_call(
        _mla_kv_kernel,
        grid=(b, lp // tm),
        in_specs=[
            pl.BlockSpec((1, tm, MLA_KV_LORA), lambda bi, i: (bi, i, ckvcol)),
            pl.BlockSpec((1, tm, 128), lambda bi, i: (bi, i, kpecol)),
            pl.BlockSpec((1, MLA_KV_LORA), const),
            pl.BlockSpec((MLA_KV_LORA, MLA_HEADS * (MLA_NOPE + MLA_DV)), const),
            pl.BlockSpec((1, 128), const),
            pl.BlockSpec((1, 128), const),
            pl.BlockSpec((tm, 128), lambda bi, i: (i, 0)),
            pl.BlockSpec((tm, 128), lambda bi, i: (i, 0)),
            pl.BlockSpec((tm, 128), lambda bi, i: (i, 0)),
        ],
        out_specs=[
            pl.BlockSpec((1, MLA_HEADS, tm, MLA_DPAD), lambda bi, i: (bi, 0, i, 0)),
            pl.BlockSpec((1, MLA_HEADS, MLA_DV, tm), lambda bi, i: (bi, 0, 0, i)),
        ],
        out_shape=[
            jax.ShapeDtypeStruct((b, MLA_HEADS, lp, MLA_DPAD), BF16),
            jax.ShapeDtypeStruct((b, MLA_HEADS, MLA_DV, lp), BF16),
        ],
        compiler_params=_cparams(("parallel", "parallel")),
        name="mla_kv",
    )(proj3, proj3, g1, w_ukv, ga, gr, cos, nsin, sin)


ATT_TK = 512


def _attn_kernel(qt_ref, k_ref, vt_ref, o_ref, *, s_len):
    qt = qt_ref[0, 0]
    tq = qt.shape[1]
    m = jnp.full((1, tq), -jnp.inf, F32)
    l = jnp.zeros((1, tq), F32)
    acc = jnp.zeros((MLA_DV, tq), F32)

    def step(carry, k0, nk, bias):
        m, l, acc = carry
        st = jnp.dot(k_ref[0, 0, k0:k0 + nk, :], qt, preferred_element_type=F32)
        if bias is not None:
            st = st + bias
        m_new = jnp.maximum(m, jnp.max(st, axis=0, keepdims=True))
        alpha = jnp.exp(m - m_new)
        p = jnp.exp(st - m_new)
        l = alpha * l + jnp.sum(p, axis=0, keepdims=True)
        acc = alpha * acc + jnp.dot(vt_ref[0, 0, :, k0:k0 + nk], p.astype(BF16),
                                    preferred_element_type=F32)
        return m_new, l, acc

    carry = (m, l, acc)
    for ck in range(s_len // ATT_TK):
        carry = step(carry, ck * ATT_TK, ATT_TK, None)
    is_meta = lax.broadcasted_iota(jnp.int32, (TAIL, 1), 0) >= TAIL - N_META
    carry = step(carry, s_len, TAIL, jnp.where(is_meta, 0.0, -jnp.inf).astype(F32))
    m, l, acc = carry
    o_ref[0] = (acc / l).T.astype(o_ref.dtype)


def _attention(qt, k, vt, s_len):
    b = qt.shape[0]
    lp = k.shape[2]
    tq = _tile(s_len, 512, 128)
    kern = functools.partial(_attn_kernel, s_len=s_len)
    return pl.pallas_call(
        kern,
        grid=(b, MLA_HEADS, s_len // tq),
        in_specs=[
            pl.BlockSpec((1, 1, MLA_DPAD, tq), lambda bi, h, i: (bi, h, 0, i)),
            pl.BlockSpec((1, 1, lp, MLA_DPAD), lambda bi, h, i: (bi, h, 0, 0)),
            pl.BlockSpec((1, 1, MLA_DV, lp), lambda bi, h, i: (bi, h, 0, 0)),
        ],
        out_specs=pl.BlockSpec((1, tq, MLA_DV), lambda bi, h, i: (bi, i, h)),
        out_shape=jax.ShapeDtypeStruct((b, s_len, MLA_V), BF16),
        compiler_params=_cparams(("parallel", "parallel", "arbitrary")),
        name="mla_attention",
    )(qt, k, vt)


def _mla_out_kernel(o_ref, z_ref, h_ref, w_ref, y_ref):
    z = z_ref[0]
    y = (o_ref[0].astype(F32) * (z * jax.nn.sigmoid(z))).astype(BF16)
    y_ref[0] = h_ref[0] + jnp.dot(y, w_ref[...], preferred_element_type=F32)


def _mla_out(o, proj3, h3, w_out, b0, nb, s_len):
    tm = _tile(s_len, 512, 128)
    return pl.pallas_call(
        _mla_out_kernel,
        grid=(nb, s_len // tm),
        in_specs=[
            pl.BlockSpec((1, tm, MLA_V), lambda bi, i: (bi + b0, i, 0)),
            pl.BlockSpec((1, tm, MLA_V), lambda bi, i: (bi + b0, i, 0)),
            pl.BlockSpec((1, tm, D_MODEL), lambda bi, i: (bi + b0, i, 0)),
            pl.BlockSpec((MLA_V, D_MODEL), lambda bi, i: (0, 0)),
        ],
        out_specs=pl.BlockSpec((1, tm, D_MODEL), lambda bi, i: (bi, i, 0)),
        out_shape=jax.ShapeDtypeStruct((nb, s_len, D_MODEL), F32),
        compiler_params=_cparams(("parallel", "parallel")),
        name="mla_out",
    )(o, proj3, h3, w_out)


def _pad_cols(w, n):
    return jnp.pad(w, ((0, 0), (0, n - w.shape[1])))


def _lane_row(v, n=128):
    return jnp.pad(v.astype(F32), (0, n - v.shape[0]))[None, :]


def _trunk_all(xs, meta_tokens, ln_g, gdn_w_in, gdn_conv_w, gdn_a_log, gdn_dt_bias, gdn_o_norm_g,
               gdn_w_out, mla_w_in, mla_q_norm_g, mla_kv_norm_g, mla_w_uq, mla_w_ukv, mla_qk_q_g,
               mla_qk_k_g, mla_w_out):
    s_len = xs[0].shape[1]
    assert all(x.shape[1] == s_len for x in xs) and s_len % 128 == 0
    lp = s_len + TAIL
    x_all = jnp.concatenate(xs, axis=0)
    b = x_all.shape[0]
    meta = jnp.broadcast_to(meta_tokens[None].astype(F32), (b, N_META, D_MODEL))
    h0 = jnp.concatenate([x_all, jnp.zeros((b, TAIL - N_META, D_MODEL), F32), meta], axis=1)
    h0 = h0.reshape(b * lp, D_MODEL)

    w_in = gdn_w_in[0]
    ba = w_in[:, GDN_CONV_CH + GDN_V:].reshape(D_MODEL, 2, 2, GDN_HEADS)
    ba_dir = [_pad_cols(jnp.concatenate([ba[:, 0, d], ba[:, 1, d]], axis=1), 128) for d in range(2)]
    w0 = jnp.concatenate([w_in[:, :GDN_CONV_CH + GDN_V]] + ba_dir, axis=1).astype(BF16)
    proj0 = _norm_matmul(h0, ln_g[0][None, :], w0, 1280)

    conv_w8 = jnp.pad(gdn_conv_w[0], ((0, 8 - GDN_CONV), (0, 0)))
    proj0_3 = proj0.reshape(b, lp, proj0.shape[1])
    qkv = _gdn_conv(proj0_3, conv_w8, lp)

    lane_a = lambda v: jnp.pad(v.astype(F32), ((0, 0), (GDN_HEADS, 128 - 2 * GDN_HEADS)))[:, None, :]
    o2 = _gdn_scan(qkv, proj0_3, lane_a(gdn_a_log[0]), lane_a(gdn_dt_bias[0]), lp)
    h1 = _gdn_out(o2.reshape(2, b * lp, GDN_V), proj0, h0, gdn_o_norm_g[0][None, :],
                  gdn_w_out[0].astype(BF16))

    w_in1 = mla_w_in[0]
    o1 = MLA_Q_LORA
    o2_ = o1 + MLA_KV_LORA
    o3 = o2_ + MLA_ROPE
    w1 = jnp.concatenate([w_in1[:, o3:], w_in1[:, :o1], w_in1[:, o1:o2_],
                          _pad_cols(w_in1[:, o2_:o3], 256)], axis=1).astype(BF16)
    proj1 = _norm_matmul(h1, ln_g[1][None, :], w1, 1024)
    proj1_3 = proj1.reshape(b, lp, proj1.shape[1])

    pos = jnp.concatenate([jnp.arange(s_len, dtype=F32) + N_META, jnp.zeros((TAIL - N_META,), F32),
                           jnp.arange(N_META, dtype=F32)])
    inv = ROPE_THETA ** (-jnp.arange(0, MLA_ROPE, 2, dtype=F32) / MLA_ROPE)
    ang = pos[:, None] * inv[None, :]
    zc = jnp.zeros_like(ang)
    cos_t = jnp.concatenate([jnp.cos(ang), jnp.cos(ang), zc, zc], axis=1)
    nsin_t = jnp.concatenate([-jnp.sin(ang), zc, zc, zc], axis=1)
    sin_t = jnp.concatenate([zc, jnp.sin(ang), zc, zc], axis=1)

    scale = MLA_DQK ** -0.5
    w_uq = mla_w_uq[0].reshape(MLA_Q_LORA, MLA_HEADS, MLA_DQK)
    w_uq = jnp.pad(w_uq, ((0, 0), (0, 0), (0, MLA_DPAD - MLA_DQK))).reshape(MLA_Q_LORA, -1).astype(BF16)
    gq = mla_qk_q_g[0].astype(F32) * scale
    qt = _mla_q(proj1_3, mla_q_norm_g[0][None, :], w_uq, gq[None, :MLA_NOPE],
                _lane_row(gq[MLA_NOPE:]), cos_t, nsin_t, sin_t, lp)

    w_ukv = mla_w_ukv[0].reshape(MLA_KV_LORA, MLA_HEADS, MLA_NOPE + MLA_DV)
    w_ukv = jnp.concatenate([w_ukv[:, :, :MLA_NOPE].reshape(MLA_KV_LORA, -1),
                             w_ukv[:, :, MLA_NOPE:].reshape(MLA_KV_LORA, -1)], axis=1).astype(BF16)
    gk = mla_qk_k_g[0].astype(F32)
    k, vt = _mla_kv(proj1_3, mla_kv_norm_g[0][None, :], w_ukv, gk[None, :MLA_NOPE],
                    _lane_row(gk[MLA_NOPE:]), cos_t, nsin_t, sin_t, lp)

    o = _attention(qt, k, vt, s_len)

    h1_3 = h1.reshape(b, lp, D_MODEL)
    w_out1 = mla_w_out[0].astype(BF16)
    outs = []
    b0 = 0
    for x in xs:
        outs.append(_mla_out(o, proj1_3, h1_3, w_out1, b0, x.shape[0], s_len))
        b0 += x.shape[0]
    return tuple(outs)


def kernel(x_prompt, x_sample, meta_tokens, ln_g, gdn_w_in, gdn_conv_w, gdn_a_log, gdn_dt_bias,
           gdn_o_norm_g, gdn_w_out, mla_w_in, mla_q_norm_g, mla_kv_norm_g, mla_w_uq, mla_w_ukv,
           mla_qk_q_g, mla_qk_k_g, mla_w_out):
    return _trunk_all((x_prompt, x_sample), meta_tokens, ln_g, gdn_w_in, gdn_conv_w, gdn_a_log,
                      gdn_dt_bias, gdn_o_norm_g, gdn_w_out, mla_w_in, mla_q_norm_g, mla_kv_norm_g,
                      mla_w_uq, mla_w_ukv, mla_qk_q_g, mla_qk_k_g, mla_w_out)
```

```python
import functools
import math

import jax
import jax.numpy as jnp
from jax import lax
from jax.experimental import pallas as pl
from jax.experimental.pallas import tpu as pltpu

F32 = jnp.float32
BF16 = jnp.bfloat16

D_MODEL = 1024
N_META = 16
TAIL = 128
NORM_EPS = 1e-6

GDN_HEADS = 8
GDN_DK = 128
GDN_DV = 256
GDN_CONV = 5
GDN_CHUNK = 64
GDN_QK = GDN_HEADS * GDN_DK
GDN_V = GDN_HEADS * GDN_DV
GDN_CONV_CH = 2 * GDN_QK + GDN_V

MLA_HEADS = 16
MLA_Q_LORA = 512
MLA_KV_LORA = 256
MLA_NOPE = 128
MLA_ROPE = 64
MLA_DQK = MLA_NOPE + MLA_ROPE
MLA_DV = 128
MLA_V = MLA_HEADS * MLA_DV
MLA_DPAD = 256
ROPE_THETA = 10000.0

VMEM_LIMIT = 56 * 1024 * 1024


def _cparams(sem):
    return pltpu.CompilerParams(dimension_semantics=sem, vmem_limit_bytes=VMEM_LIMIT)


def _tile(n, target, mult):
    best = None
    t = mult
    while t <= min(n, target):
        if n % t == 0:
            best = t
        t += mult
    assert best is not None, (n, target, mult)
    return best


def _norm_matmul_kernel(x_ref, g_ref, w_ref, o_ref, xn_ref):
    @pl.when(pl.program_id(1) == 0)
    def _():
        x = x_ref[...]
        ms = jnp.mean(x * x, axis=-1, keepdims=True)
        xn_ref[...] = (x * lax.rsqrt(ms + NORM_EPS) * g_ref[...]).astype(BF16)

    o_ref[...] = jnp.dot(xn_ref[...], w_ref[...], preferred_element_type=F32)


def _norm_matmul(x, g, w, tn):
    rows, d = x.shape
    n = w.shape[1]
    tm = _tile(rows, 1408, 128)
    return pl.pallas_call(
        _norm_matmul_kernel,
        grid=(rows // tm, n // tn),
        in_specs=[
            pl.BlockSpec((tm, d), lambda i, j: (i, 0)),
            pl.BlockSpec((1, d), lambda i, j: (0, 0)),
            pl.BlockSpec((d, tn), lambda i, j: (0, j)),
        ],
        out_specs=pl.BlockSpec((tm, tn), lambda i, j: (i, j)),
        out_shape=jax.ShapeDtypeStruct((rows, n), F32),
        scratch_shapes=[pltpu.VMEM((tm, d), BF16)],
        compiler_params=_cparams(("parallel", "arbitrary")),
        name="norm_matmul",
    )(x, g, w)


CONV_SUB = 32
CONV_TC = 1024


def _conv_kernel(prev_ref, main_ref, next_ref, w_ref, o_ref, ext_ref, *, tr):
    j = pl.program_id(2)
    half = GDN_CONV // 2
    ext_ref[0:8, :] = prev_ref[0]
    ext_ref[8:8 + tr, :] = main_ref[0]
    ext_ref[8 + tr:16 + tr, :] = next_ref[0]
    is_qk = j < 2
    qscale = jnp.where(j == 0, GDN_DK ** -0.5, 1.0).astype(F32)
    for sb in range(tr // CONV_SUB):
        r0 = 8 + sb * CONV_SUB - half
        acc = ext_ref[r0:r0 + CONV_SUB, :] * w_ref[0:1, :]
        for t in range(1, GDN_CONV):
            acc = acc + ext_ref[r0 + t:r0 + t + CONV_SUB, :] * w_ref[t:t + 1, :]
        y = acc * jax.nn.sigmoid(acc)
        rows = slice(sb * CONV_SUB, (sb + 1) * CONV_SUB)
        for h in range(CONV_TC // GDN_DK):
            lanes = slice(h * GDN_DK, (h + 1) * GDN_DK)
            yh = y[:, lanes]
            ss = jnp.sum(yh * yh, axis=-1, keepdims=True)
            inv = jnp.where(is_qk, lax.rsqrt(ss + NORM_EPS) * qscale, 1.0)
            o_ref[0, rows, lanes] = (yh * inv).astype(BF16)


def _gdn_conv(proj3, conv_w8, lp):
    b = proj3.shape[0]
    tr = _tile(lp, 384, CONV_SUB)
    nb8 = lp // 8
    tb = tr // 8
    kern = functools.partial(_conv_kernel, tr=tr)
    return pl.pallas_call(
        kern,
        grid=(b, lp // tr, GDN_CONV_CH // CONV_TC),
        in_specs=[
            pl.BlockSpec((1, 8, CONV_TC), lambda bi, i, j: (bi, (i * tb + nb8 - 1) % nb8, j)),
            pl.BlockSpec((1, tr, CONV_TC), lambda bi, i, j: (bi, i, j)),
            pl.BlockSpec((1, 8, CONV_TC), lambda bi, i, j: (bi, ((i + 1) * tb) % nb8, j)),
            pl.BlockSpec((8, CONV_TC), lambda bi, i, j: (0, j)),
        ],
        out_specs=pl.BlockSpec((1, tr, CONV_TC), lambda bi, i, j: (bi, i, j)),
        out_shape=jax.ShapeDtypeStruct((b, lp, GDN_CONV_CH), BF16),
        scratch_shapes=[pltpu.VMEM((tr + 16, CONV_TC), F32)],
        compiler_params=_cparams(("parallel", "parallel", "arbitrary")),
        name="gdn_conv",
    )(proj3, proj3, proj3, conv_w8)


def _split3(x):
    hi = x.astype(BF16)
    r1 = x - hi.astype(F32)
    mid = r1.astype(BF16)
    lo = (r1 - mid.astype(F32)).astype(BF16)
    return hi, mid, lo


def _bdot(a, b):
    return jnp.dot(a.astype(BF16), b.astype(BF16), preferred_element_type=F32)


def _gdn_scan_kernel(qf_ref, kf_ref, vf_ref, baf_ref, qb_ref, kb_ref, vb_ref, bab_ref, alog_ref, dtb_ref,
                     of_ref, ob_ref, s_ref, *, nc):
    t = pl.program_id(1)
    c = GDN_CHUNK
    c2 = 2 * c
    npair = GDN_HEADS // 2

    @pl.when(t == 0)
    def _():
        s_ref[...] = jnp.zeros_like(s_ref)

    ri = lax.broadcasted_iota(jnp.int32, (c2, c2), 0)
    ci = lax.broadcasted_iota(jnp.int32, (c2, c2), 1)
    same = (ri >> 6) == (ci >> 6)
    offdiag = ri != ci
    diag16 = (ri >> 4) == (ci >> 4)
    eye = jnp.where(ri == ci, 1.0, 0.0).astype(F32)
    top = lax.broadcasted_iota(jnp.int32, (c2, 1), 0) < c
    left = lax.broadcasted_iota(jnp.int32, (1, c2), 1) < c
    row_id = lax.broadcasted_iota(jnp.int32, (c, 128), 0)

    dirs = []
    for d, (q_ref, k_ref, v_ref, ba_ref, o_ref) in enumerate(
            ((qf_ref, kf_ref, vf_ref, baf_ref, of_ref), (qb_ref, kb_ref, vb_ref, bab_ref, ob_ref))):
        seq = t if d == 0 else nc - 1 - t
        blk = jnp.where(seq < 2, nc - 2 + seq, seq - 2)
        first_valid = jnp.where(blk == nc - 2, c, jnp.where(blk == nc - 1, c - N_META, 0))
        valid = row_id >= first_valid
        ba = ba_ref[0]
        beta = jnp.where(valid, jax.nn.sigmoid(ba), 0.0)
        xs = ba + dtb_ref[d]
        softplus = jnp.maximum(xs, 0.0) + jnp.log(1.0 + jnp.exp(-jnp.abs(xs)))
        g = jnp.where(valid, -jnp.exp(alog_ref[d]) * softplus, 0.0)
        incl = same & ((ri >= ci) if d == 0 else (ri <= ci))
        tri = jnp.where(incl, 1.0, 0.0).astype(BF16)
        ghi, gmid, glo = _split3(jnp.concatenate([g, g], axis=0))
        gc2 = (jnp.dot(tri, ghi, preferred_element_type=F32)
               + jnp.dot(tri, gmid, preferred_element_type=F32)
               + jnp.dot(tri, glo, preferred_element_type=F32))
        dirs.append(dict(q=q_ref, k=k_ref, v=v_ref, o=o_ref, incl=incl, gc2=gc2, gc2t=gc2.T,
                         beta2=jnp.concatenate([beta, beta], axis=0),
                         gtot=jnp.sum(g, axis=0, keepdims=True)))

    probs = []
    for d in range(2):
        dd = dirs[d]
        for p in range(npair):
            h0, h1 = 2 * p, 2 * p + 1
            a0, a1 = GDN_HEADS + h0, GDN_HEADS + h1
            col = jnp.where(top, dd["gc2"][:, a0:a0 + 1], dd["gc2"][:, a1:a1 + 1])
            row = jnp.where(left, dd["gc2t"][a0:a0 + 1, :], dd["gc2t"][a1:a1 + 1, :])
            bcol = jnp.where(top, dd["beta2"][:, h0:h0 + 1], dd["beta2"][:, h1:h1 + 1])
            tot = jnp.where(top, dd["gtot"][:, a0:a0 + 1], dd["gtot"][:, a1:a1 + 1])
            dec = jnp.exp(jnp.where(dd["incl"], col - row, -jnp.inf))
            kst = jnp.concatenate([dd["k"][0, :, h0 * GDN_DK:(h0 + 1) * GDN_DK],
                                   dd["k"][0, :, h1 * GDN_DK:(h1 + 1) * GDN_DK]], axis=0)
            qst = jnp.concatenate([dd["q"][0, :, h0 * GDN_DK:(h0 + 1) * GDN_DK],
                                   dd["q"][0, :, h1 * GDN_DK:(h1 + 1) * GDN_DK]], axis=0)
            vst = jnp.concatenate([dd["v"][0, :, h0 * GDN_DV:(h0 + 1) * GDN_DV],
                                   dd["v"][0, :, h1 * GDN_DV:(h1 + 1) * GDN_DV]], axis=0)
            kf = kst.astype(F32)
            kb = kf * bcol
            sc = lax.dot_general(jnp.concatenate([qst, kb.astype(BF16)], axis=0), kst,
                                 (((1,), (1,)), ((), ())), preferred_element_type=F32)
            egc = jnp.exp(col)
            probs.append(dict(
                d=d, heads=(h0, h1), o=dd["o"], gtot=dd["gtot"],
                attn=(sc[:c2] * dec).astype(BF16),
                a=jnp.where(offdiag, sc[c2:] * dec, 0.0),
                rhs=jnp.concatenate([vst.astype(F32) * bcol, kb * egc], axis=1).astype(BF16),
                qg=(qst.astype(F32) * egc).astype(BF16),
                kd=(kf * jnp.exp(tot - col)).astype(BF16)))

    _unit_lower_inverse_staged(probs, eye, diag16)

    for pr in probs:
        uw = jnp.dot(pr["tinv"].astype(BF16), pr["rhs"], preferred_element_type=F32)
        pr["u"] = uw[:, :GDN_DV]
        pr["w"] = uw[:, GDN_DV:].astype(BF16)

    for pr in probs:
        pr["qs"], pr["vnew"] = [], []
        for hh, h in enumerate(pr["heads"]):
            rs = slice(hh * c, (hh + 1) * c)
            qw = jnp.concatenate([pr["qg"][rs], pr["w"][rs]], axis=0)
            qws = jnp.dot(qw, s_ref[pr["d"], h].astype(BF16), preferred_element_type=F32)
            pr["qs"].append(qws[:c])
            pr["vnew"].append((pr["u"][rs] - qws[c:]).astype(BF16))

    for pr in probs:
        h0, h1 = pr["heads"]
        vnew2 = jnp.concatenate(pr["vnew"], axis=0)
        o2 = jnp.concatenate(pr["qs"], axis=0) + jnp.dot(pr["attn"], vnew2, preferred_element_type=F32)
        pr["o"][0, :, h0 * GDN_DV:(h0 + 1) * GDN_DV] = o2[:c]
        pr["o"][0, :, h1 * GDN_DV:(h1 + 1) * GDN_DV] = o2[c:]

    for pr in probs:
        for hh, h in enumerate(pr["heads"]):
            rs = slice(hh * c, (hh + 1) * c)
            a_h = GDN_HEADS + h
            upd = lax.dot_general(pr["kd"][rs], pr["vnew"][hh], (((0,), (0,)), ((), ())),
                                  preferred_element_type=F32)
            s_ref[pr["d"], h] = s_ref[pr["d"], h] * jnp.exp(pr["gtot"][:, a_h:a_h + 1]) + upd


def _unit_lower_inverse_staged(probs, eye, diag16):
    for pr in probs:
        pr["ad"] = jnp.where(diag16, pr["a"], 0.0)
        pr["an"] = (pr["a"] - pr["ad"]).astype(BF16)
        pr["adb"] = pr["ad"].astype(BF16)
        pr["dinv"] = eye - pr["ad"]
    for pr in probs:
        pr["a2"] = jnp.dot(pr["adb"], pr["adb"], preferred_element_type=F32).astype(BF16)
    for pr in probs:
        pr["a4"] = jnp.dot(pr["a2"], pr["a2"], preferred_element_type=F32).astype(BF16)
        pr["dinv"] = pr["dinv"] + _bdot(pr["dinv"], pr["a2"])
    for pr in probs:
        pr["a8"] = jnp.dot(pr["a4"], pr["a4"], preferred_element_type=F32).astype(BF16)
        pr["dinv"] = pr["dinv"] + _bdot(pr["dinv"], pr["a4"])
    for pr in probs:
        pr["dinv"] = (pr["dinv"] + _bdot(pr["dinv"], pr["a8"])).astype(BF16)
    for pr in probs:
        pr["m"] = jnp.dot(pr["dinv"], pr["an"], preferred_element_type=F32)
    for pr in probs:
        mb = pr["m"].astype(BF16)
        pr["m2"] = jnp.dot(mb, mb, preferred_element_type=F32).astype(BF16)
    for pr in probs:
        x = eye - pr["m"]
        pr["x"] = (x + _bdot(x, pr["m2"])).astype(BF16)
    for pr in probs:
        pr["tinv"] = jnp.dot(pr["x"], pr["dinv"], preferred_element_type=F32)


def _gdn_scan(qkv, proj3, alog_rows, dtb_rows, lp):
    b = qkv.shape[0]
    nc = lp // GDN_CHUNK
    ba_col0 = (GDN_CONV_CH + GDN_V) // 128

    def blk_of(seq):
        return jnp.where(seq < 2, nc - 2 + seq, seq - 2)

    def chunk_specs(d):
        seq = (lambda t: t) if d == 0 else (lambda t: nc - 1 - t)
        return [
            pl.BlockSpec((1, GDN_CHUNK, GDN_QK), lambda bi, t: (bi, blk_of(seq(t)), 0)),
            pl.BlockSpec((1, GDN_CHUNK, GDN_QK), lambda bi, t: (bi, blk_of(seq(t)), 1)),
            pl.BlockSpec((1, GDN_CHUNK, GDN_V), lambda bi, t: (bi, blk_of(seq(t)), 1)),
            pl.BlockSpec((1, GDN_CHUNK, 128), lambda bi, t: (bi, blk_of(seq(t)), ba_col0 + d)),
        ]

    const = pl.BlockSpec((2, 1, 128), lambda bi, t: (0, 0, 0))
    kern = functools.partial(_gdn_scan_kernel, nc=nc)
    return pl.pallas_call(
        kern,
        grid=(b, nc),
        in_specs=chunk_specs(0) + chunk_specs(1) + [const, const],
        out_specs=[
            pl.BlockSpec((1, GDN_CHUNK, GDN_V), lambda bi, t: (bi, blk_of(t), 0)),
            pl.BlockSpec((1, GDN_CHUNK, GDN_V), lambda bi, t: (bi, blk_of(nc - 1 - t), 0)),
        ],
        out_shape=[jax.ShapeDtypeStruct((b, lp, GDN_V), F32)] * 2,
        scratch_shapes=[pltpu.VMEM((2, GDN_HEADS, GDN_DK, GDN_DV), F32)],
        compiler_params=_cparams(("parallel", "arbitrary")),
        name="gdn_scan",
    )(qkv, qkv, qkv, proj3, qkv, qkv, qkv, proj3, alog_rows, dtb_rows)


def _gdn_out_kernel(of_ref, ob_ref, z_ref, h_ref, g_ref, w_ref, o_ref):
    o = of_ref[...] + ob_ref[...]
    z = z_ref[...]
    gate = z * jax.nn.sigmoid(z)
    ys = []
    for h in range(GDN_HEADS):
        lanes = slice(h * GDN_DV, (h + 1) * GDN_DV)
        oh = o[:, lanes]
        ms = jnp.mean(oh * oh, axis=-1, keepdims=True)
        ys.append((oh * lax.rsqrt(ms + NORM_EPS) * g_ref[...] * gate[:, lanes]).astype(BF16))
    y = jnp.concatenate(ys, axis=1)
    o_ref[...] = h_ref[...] + jnp.dot(y, w_ref[...], preferred_element_type=F32)


def _gdn_out(o_fwd, o_bwd, proj, h0, g_row, w_out):
    rows = h0.shape[0]
    tm = _tile(rows, 256, 128)
    zcol = GDN_CONV_CH // GDN_V
    return pl.pallas_call(
        _gdn_out_kernel,
        grid=(rows // tm,),
        in_specs=[
            pl.BlockSpec((tm, GDN_V), lambda i: (i, 0)),
            pl.BlockSpec((tm, GDN_V), lambda i: (i, 0)),
            pl.BlockSpec((tm, GDN_V), lambda i: (i, zcol)),
            pl.BlockSpec((tm, D_MODEL), lambda i: (i, 0)),
            pl.BlockSpec((1, GDN_DV), lambda i: (0, 0)),
            pl.BlockSpec((GDN_V, D_MODEL), lambda i: (0, 0)),
        ],
        out_specs=pl.BlockSpec((tm, D_MODEL), lambda i: (i, 0)),
        out_shape=jax.ShapeDtypeStruct((rows, D_MODEL), F32),
        compiler_params=_cparams(("parallel",)),
        name="gdn_out",
    )(o_fwd, o_bwd, proj, h0, g_row, w_out)


def _rope(r, cos, nsin_lo, sin_hi):
    return r * cos + pltpu.roll(r, 96, 1) * nsin_lo + pltpu.roll(r, 32, 1) * sin_hi


def _mla_q_kernel(cq_ref, g1_ref, w_ref, ga_ref, gr_ref, cos_ref, nsin_ref, sin_ref, qt_ref):
    cq = cq_ref[0]
    ms = jnp.mean(cq * cq, axis=-1, keepdims=True)
    cqn = (cq * lax.rsqrt(ms + NORM_EPS) * g1_ref[...]).astype(BF16)
    q = jnp.dot(cqn, w_ref[...], preferred_element_type=F32)
    cos, nsin, sin = cos_ref[...], nsin_ref[...], sin_ref[...]
    for h in range(MLA_HEADS):
        a = q[:, h * MLA_DPAD:h * MLA_DPAD + MLA_NOPE]
        r = q[:, h * MLA_DPAD + MLA_NOPE:(h + 1) * MLA_DPAD]
        ss = jnp.sum(a * a, axis=-1, keepdims=True) + jnp.sum(r * r, axis=-1, keepdims=True)
        inv = lax.rsqrt(ss * (1.0 / MLA_DQK) + NORM_EPS)
        an = a * inv * ga_ref[...]
        rn = _rope(r * inv * gr_ref[...], cos, nsin, sin)
        qt_ref[0, h, 0:MLA_NOPE, :] = an.T.astype(BF16)
        qt_ref[0, h, MLA_NOPE:MLA_DPAD, :] = rn.T.astype(BF16)


def _mla_q(proj3, g1, w_uq, ga, gr, cos, nsin, sin, lp):
    b = proj3.shape[0]
    tm = _tile(lp, 384, 128)
    cqcol = MLA_V // MLA_Q_LORA
    const = lambda bi, i: (0, 0)
    return pl.pallas_call(
        _mla_q_kernel,
        grid=(b, lp // tm),
        in_specs=[
            pl.BlockSpec((1, tm, MLA_Q_LORA), lambda bi, i: (bi, i, cqcol)),
            pl.BlockSpec((1, MLA_Q_LORA), const),
            pl.BlockSpec((MLA_Q_LORA, MLA_HEADS * MLA_DPAD), const),
            pl.BlockSpec((1, 128), const),
            pl.BlockSpec((1, 128), const),
            pl.BlockSpec((tm, 128), lambda bi, i: (i, 0)),
            pl.BlockSpec((tm, 128), lambda bi, i: (i, 0)),
            pl.BlockSpec((tm, 128), lambda bi, i: (i, 0)),
        ],
        out_specs=pl.BlockSpec((1, MLA_HEADS, MLA_DPAD, tm), lambda bi, i: (bi, 0, 0, i)),
        out_shape=jax.ShapeDtypeStruct((b, MLA_HEADS, MLA_DPAD, lp), BF16),
        compiler_params=_cparams(("parallel", "parallel")),
        name="mla_q",
    )(proj3, g1, w_uq, ga, gr, cos, nsin, sin)


def _mla_kv_kernel(ckv_ref, kpe_ref, g1_ref, w_ref, ga_ref, gr_ref, cos_ref, nsin_ref, sin_ref,
                   k_ref, vt_ref):
    ckv = ckv_ref[0]
    ms = jnp.mean(ckv * ckv, axis=-1, keepdims=True)
    cn = (ckv * lax.rsqrt(ms + NORM_EPS) * g1_ref[...]).astype(BF16)
    kv = jnp.dot(cn, w_ref[...], preferred_element_type=F32)
    kpe = kpe_ref[0]
    ss_pe = jnp.sum(kpe * kpe, axis=-1, keepdims=True)
    kr = _rope(kpe * gr_ref[...], cos_ref[...], nsin_ref[...], sin_ref[...])
    for h in range(MLA_HEADS):
        kn = kv[:, h * MLA_NOPE:(h + 1) * MLA_NOPE]
        ss = jnp.sum(kn * kn, axis=-1, keepdims=True) + ss_pe
        inv = lax.rsqrt(ss * (1.0 / MLA_DQK) + NORM_EPS)
        k_ref[0, h, :, 0:MLA_NOPE] = (kn * inv * ga_ref[...]).astype(BF16)
        k_ref[0, h, :, MLA_NOPE:MLA_DPAD] = (kr * inv).astype(BF16)
        v = kv[:, MLA_HEADS * MLA_NOPE + h * MLA_DV:MLA_HEADS * MLA_NOPE + (h + 1) * MLA_DV]
        vt_ref[0, h] = v.T.astype(BF16)


def _mla_kv(proj3, g1, w_ukv, ga, gr, cos, nsin, sin, lp):
    b = proj3.shape[0]
    tm = _tile(lp, 384, 128)
    ckvcol = (MLA_V + MLA_Q_LORA) // MLA_KV_LORA
    kpecol = (MLA_V + MLA_Q_LORA + MLA_KV_LORA) // 128
    const = lambda bi, i: (0, 0)
    return pl.pallas_call(
        _mla_kv_kernel,
        grid=(b, lp // tm),
        in_specs=[
            pl.BlockSpec((1, tm, MLA_KV_LORA), lambda bi, i: (bi, i, ckvcol)),
            pl.BlockSpec((1, tm, 128), lambda bi, i: (bi, i, kpecol)),
            pl.BlockSpec((1, MLA_KV_LORA), const),
            pl.BlockSpec((MLA_KV_LORA, MLA_HEADS * (MLA_NOPE + MLA_DV)), const),
            pl.BlockSpec((1, 128), const),
            pl.BlockSpec((1, 128), const),
            pl.BlockSpec((tm, 128), lambda bi, i: (i, 0)),
            pl.BlockSpec((tm, 128), lambda bi, i: (i, 0)),
            pl.BlockSpec((tm, 128), lambda bi, i: (i, 0)),
        ],
        out_specs=[
            pl.BlockSpec((1, MLA_HEADS, tm, MLA_DPAD), lambda bi, i: (bi, 0, i, 0)),
            pl.BlockSpec((1, MLA_HEADS, MLA_DV, tm), lambda bi, i: (bi, 0, 0, i)),
        ],
        out_shape=[
            jax.ShapeDtypeStruct((b, MLA_HEADS, lp, MLA_DPAD), BF16),
            jax.ShapeDtypeStruct((b, MLA_HEADS, MLA_DV, lp), BF16),
        ],
        compiler_params=_cparams(("parallel", "parallel")),
        name="mla_kv",
    )(proj3, proj3, g1, w_ukv, ga, gr, cos, nsin, sin)


ATT_TK = 512


def _attn_kernel(qt_ref, k_ref, vt_ref, o_ref, *, s_len):
    qt = qt_ref[0, 0]
    tq = qt.shape[1]
    m = jnp.full((1, tq), -jnp.inf, F32)
    l = jnp.zeros((1, tq), F32)
    acc = jnp.zeros((MLA_DV, tq), F32)
    chunks = [(ck * ATT_TK, ATT_TK) for ck in range(s_len // ATT_TK)] + [(s_len, TAIL)]
    is_meta = lax.broadcasted_iota(jnp.int32, (TAIL, 1), 0) >= TAIL - N_META
    tail_bias = jnp.where(is_meta, 0.0, -jnp.inf).astype(F32)

    def scores(i):
        k0, nk = chunks[i]
        st = jnp.dot(k_ref[0, 0, k0:k0 + nk, :], qt, preferred_element_type=F32)
        return st + tail_bias if i == len(chunks) - 1 else st

    st = scores(0)
    for i, (k0, nk) in enumerate(chunks):
        st_next = scores(i + 1) if i + 1 < len(chunks) else None
        m_new = jnp.maximum(m, jnp.max(st, axis=0, keepdims=True))
        alpha = jnp.exp2(m - m_new)
        p = jnp.exp2(st - m_new)
        l = alpha * l + jnp.sum(p, axis=0, keepdims=True)
        acc = alpha * acc + jnp.dot(vt_ref[0, 0, :, k0:k0 + nk], p.astype(BF16),
                                    preferred_element_type=F32)
        m = m_new
        st = st_next
    o_ref[0] = (acc / l).T.astype(o_ref.dtype)


def _attention(qt, k, vt, s_len):
    b = qt.shape[0]
    lp = k.shape[2]
    tq = _tile(s_len, 512, 128)
    kern = functools.partial(_attn_kernel, s_len=s_len)
    return pl.pallas_call(
        kern,
        grid=(b, MLA_HEADS, s_len // tq),
        in_specs=[
            pl.BlockSpec((1, 1, MLA_DPAD, tq), lambda bi, h, i: (bi, h, 0, i)),
            pl.BlockSpec((1, 1, lp, MLA_DPAD), lambda bi, h, i: (bi, h, 0, 0)),
            pl.BlockSpec((1, 1, MLA_DV, lp), lambda bi, h, i: (bi, h, 0, 0)),
        ],
        out_specs=pl.BlockSpec((1, tq, MLA_DV), lambda bi, h, i: (bi, i, h)),
        out_shape=jax.ShapeDtypeStruct((b, s_len, MLA_V), BF16),
        compiler_params=_cparams(("parallel", "parallel", "arbitrary")),
        name="mla_attention",
    )(qt, k, vt)


def _mla_out_kernel(o_ref, z_ref, h_ref, w_ref, y_ref):
    z = z_ref[0]
    y = (o_ref[0].astype(F32) * (z * jax.nn.sigmoid(z))).astype(BF16)
    y_ref[0] = h_ref[0] + jnp.dot(y, w_ref[...], preferred_element_type=F32)


def _mla_out(o, proj3, h3, w_out, b0, nb, s_len):
    tm = _tile(s_len, 512, 128)
    return pl.pallas_call(
        _mla_out_kernel,
        grid=(nb, s_len // tm),
        in_specs=[
            pl.BlockSpec((1, tm, MLA_V), lambda bi, i: (bi + b0, i, 0)),
            pl.BlockSpec((1, tm, MLA_V), lambda bi, i: (bi + b0, i, 0)),
            pl.BlockSpec((1, tm, D_MODEL), lambda bi, i: (bi + b0, i, 0)),
            pl.BlockSpec((MLA_V, D_MODEL), lambda bi, i: (0, 0)),
        ],
        out_specs=pl.BlockSpec((1, tm, D_MODEL), lambda bi, i: (bi, i, 0)),
        out_shape=jax.ShapeDtypeStruct((nb, s_len, D_MODEL), F32),
        compiler_params=_cparams(("parallel", "parallel")),
        name="mla_out",
    )(o, proj3, h3, w_out)


def _pad_cols(w, n):
    return jnp.pad(w, ((0, 0), (0, n - w.shape[1])))


def _lane_row(v, n=128):
    return jnp.pad(v.astype(F32), (0, n - v.shape[0]))[None, :]


def _trunk_all(xs, meta_tokens, ln_g, gdn_w_in, gdn_conv_w, gdn_a_log, gdn_dt_bias, gdn_o_norm_g,
               gdn_w_out, mla_w_in, mla_q_norm_g, mla_kv_norm_g, mla_w_uq, mla_w_ukv, mla_qk_q_g,
               mla_qk_k_g, mla_w_out):
    s_len = xs[0].shape[1]
    assert all(x.shape[1] == s_len for x in xs) and s_len % 128 == 0
    lp = s_len + TAIL
    x_all = jnp.concatenate(xs, axis=0)
    b = x_all.shape[0]
    meta = jnp.broadcast_to(meta_tokens[None].astype(F32), (b, N_META, D_MODEL))
    h0 = jnp.concatenate([x_all, jnp.zeros((b, TAIL - N_META, D_MODEL), F32), meta], axis=1)
    h0 = h0.reshape(b * lp, D_MODEL)

    w_in = gdn_w_in[0]
    ba = w_in[:, GDN_CONV_CH + GDN_V:].reshape(D_MODEL, 2, 2, GDN_HEADS)
    ba_dir = [_pad_cols(jnp.concatenate([ba[:, 0, d], ba[:, 1, d]], axis=1), 128) for d in range(2)]
    w0 = jnp.concatenate([w_in[:, :GDN_CONV_CH + GDN_V]] + ba_dir, axis=1).astype(BF16)
    proj0 = _norm_matmul(h0, ln_g[0][None, :], w0, 1280)

    conv_w8 = jnp.pad(gdn_conv_w[0], ((0, 8 - GDN_CONV), (0, 0)))
    proj0_3 = proj0.reshape(b, lp, proj0.shape[1])
    qkv = _gdn_conv(proj0_3, conv_w8, lp)

    lane_a = lambda v: jnp.pad(v.astype(F32), ((0, 0), (GDN_HEADS, 128 - 2 * GDN_HEADS)))[:, None, :]
    o_fwd, o_bwd = _gdn_scan(qkv, proj0_3, lane_a(gdn_a_log[0]), lane_a(gdn_dt_bias[0]), lp)
    h1 = _gdn_out(o_fwd.reshape(b * lp, GDN_V), o_bwd.reshape(b * lp, GDN_V), proj0, h0,
                  gdn_o_norm_g[0][None, :], gdn_w_out[0].astype(BF16))

    w_in1 = mla_w_in[0]
    o1 = MLA_Q_LORA
    o2_ = o1 + MLA_KV_LORA
    o3 = o2_ + MLA_ROPE
    w1 = jnp.concatenate([w_in1[:, o3:], w_in1[:, :o1], w_in1[:, o1:o2_],
                          _pad_cols(w_in1[:, o2_:o3], 256)], axis=1).astype(BF16)
    proj1 = _norm_matmul(h1, ln_g[1][None, :], w1, 1024)
    proj1_3 = proj1.reshape(b, lp, proj1.shape[1])

    pos = jnp.concatenate([jnp.arange(s_len, dtype=F32) + N_META, jnp.zeros((TAIL - N_META,), F32),
                           jnp.arange(N_META, dtype=F32)])
    inv = ROPE_THETA ** (-jnp.arange(0, MLA_ROPE, 2, dtype=F32) / MLA_ROPE)
    ang = pos[:, None] * inv[None, :]
    zc = jnp.zeros_like(ang)
    cos_t = jnp.concatenate([jnp.cos(ang), jnp.cos(ang), zc, zc], axis=1)
    nsin_t = jnp.concatenate([-jnp.sin(ang), zc, zc, zc], axis=1)
    sin_t = jnp.concatenate([zc, jnp.sin(ang), zc, zc], axis=1)

    scale = MLA_DQK ** -0.5 * math.log2(math.e)
    w_uq = mla_w_uq[0].reshape(MLA_Q_LORA, MLA_HEADS, MLA_DQK)
    w_uq = jnp.pad(w_uq, ((0, 0), (0, 0), (0, MLA_DPAD - MLA_DQK))).reshape(MLA_Q_LORA, -1).astype(BF16)
    gq = mla_qk_q_g[0].astype(F32) * scale
    qt = _mla_q(proj1_3, mla_q_norm_g[0][None, :], w_uq, gq[None, :MLA_NOPE],
                _lane_row(gq[MLA_NOPE:]), cos_t, nsin_t, sin_t, lp)

    w_ukv = mla_w_ukv[0].reshape(MLA_KV_LORA, MLA_HEADS, MLA_NOPE + MLA_DV)
    w_ukv = jnp.concatenate([w_ukv[:, :, :MLA_NOPE].reshape(MLA_KV_LORA, -1),
                             w_ukv[:, :, MLA_NOPE:].reshape(MLA_KV_LORA, -1)], axis=1).astype(BF16)
    gk = mla_qk_k_g[0].astype(F32)
    k, vt = _mla_kv(proj1_3, mla_kv_norm_g[0][None, :], w_ukv, gk[None, :MLA_NOPE],
                    _lane_row(gk[MLA_NOPE:]), cos_t, nsin_t, sin_t, lp)

    o = _attention(qt, k, vt, s_len)

    h1_3 = h1.reshape(b, lp, D_MODEL)
    w_out1 = mla_w_out[0].astype(BF16)
    outs = []
    b0 = 0
    for x in xs:
        outs.append(_mla_out(o, proj1_3, h1_3, w_out1, b0, x.shape[0], s_len))
        b0 += x.shape[0]
    return tuple(outs)


def kernel(x_prompt, x_sample, meta_tokens, ln_g, gdn_w_in, gdn_conv_w, gdn_a_log, gdn_dt_bias,
           gdn_o_norm_g, gdn_w_out, mla_w_in, mla_q_norm_g, mla_kv_norm_g, mla_w_uq, mla_w_ukv,
           mla_qk_q_g, mla_qk_k_g, mla_w_out):
    return _trunk_all((x_prompt, x_sample), meta_tokens, ln_g, gdn_w_in, gdn_conv_w, gdn_a_log,
                      gdn_dt_bias, gdn_o_norm_g, gdn_w_out, mla_w_in, mla_q_norm_g, mla_kv_norm_g,
                      mla_w_uq, mla_w_ukv, mla_qk_q_g, mla_qk_k_g, mla_w_out)
```

```python
import functools
import math

import jax
import jax.numpy as jnp
from jax import lax
from jax.experimental import pallas as pl
from jax.experimental.pallas import tpu as pltpu

F32 = jnp.float32
BF16 = jnp.bfloat16

D_MODEL = 1024
N_META = 16
TAIL = 128
NORM_EPS = 1e-6

GDN_HEADS = 8
GDN_DK = 128
GDN_DV = 256
GDN_CONV = 5
GDN_CHUNK = 64
GDN_QK = GDN_HEADS * GDN_DK
GDN_V = GDN_HEADS * GDN_DV
GDN_CONV_CH = 2 * GDN_QK + GDN_V

MLA_HEADS = 16
MLA_Q_LORA = 512
MLA_KV_LORA = 256
MLA_NOPE = 128
MLA_ROPE = 64
MLA_DQK = MLA_NOPE + MLA_ROPE
MLA_DV = 128
MLA_V = MLA_HEADS * MLA_DV
MLA_DPAD = 256
ROPE_THETA = 10000.0

VMEM_LIMIT = 56 * 1024 * 1024


def _cparams(sem):
    return pltpu.CompilerParams(dimension_semantics=sem, vmem_limit_bytes=VMEM_LIMIT)


def _tile(n, target, mult):
    best = None
    t = mult
    while t <= min(n, target):
        if n % t == 0:
            best = t
        t += mult
    assert best is not None, (n, target, mult)
    return best


def _norm_matmul_kernel(x_ref, g_ref, w_ref, o_ref, xn_ref):
    @pl.when(pl.program_id(1) == 0)
    def _():
        x = x_ref[...]
        ms = jnp.mean(x * x, axis=-1, keepdims=True)
        xn_ref[...] = (x * lax.rsqrt(ms + NORM_EPS) * g_ref[...]).astype(BF16)

    o_ref[...] = jnp.dot(xn_ref[...], w_ref[...], preferred_element_type=F32)


def _norm_matmul(x, g, w, tn):
    rows, d = x.shape
    n = w.shape[1]
    tm = _tile(rows, 1408, 128)
    return pl.pallas_call(
        _norm_matmul_kernel,
        grid=(rows // tm, n // tn),
        in_specs=[
            pl.BlockSpec((tm, d), lambda i, j: (i, 0)),
            pl.BlockSpec((1, d), lambda i, j: (0, 0)),
            pl.BlockSpec((d, tn), lambda i, j: (0, j)),
        ],
        out_specs=pl.BlockSpec((tm, tn), lambda i, j: (i, j)),
        out_shape=jax.ShapeDtypeStruct((rows, n), F32),
        scratch_shapes=[pltpu.VMEM((tm, d), BF16)],
        compiler_params=_cparams(("parallel", "arbitrary")),
        name="norm_matmul",
    )(x, g, w)


CONV_SUB = 128
CONV_TC = 1024


def _conv_kernel(prev_ref, main_ref, next_ref, w_ref, o_ref, ext_ref, *, tr):
    j = pl.program_id(2)
    half = GDN_CONV // 2
    ext_ref[0:8, :] = prev_ref[0]
    ext_ref[8:8 + tr, :] = main_ref[0]
    ext_ref[8 + tr:16 + tr, :] = next_ref[0]
    qscale = jnp.where(j == 0, GDN_DK ** -0.5, 1.0).astype(F32)

    def conv_silu(sb):
        r0 = 8 + sb * CONV_SUB - half
        acc = ext_ref[r0:r0 + CONV_SUB, :] * w_ref[0:1, :]
        for t in range(1, GDN_CONV):
            acc = acc + ext_ref[r0 + t:r0 + t + CONV_SUB, :] * w_ref[t:t + 1, :]
        return acc * jax.nn.sigmoid(acc)

    @pl.when(j < 2)
    def _():
        for sb in range(tr // CONV_SUB):
            y = conv_silu(sb)
            rows = slice(sb * CONV_SUB, (sb + 1) * CONV_SUB)
            for h in range(CONV_TC // GDN_DK):
                lanes = slice(h * GDN_DK, (h + 1) * GDN_DK)
                yh = y[:, lanes]
                ss = jnp.sum(yh * yh, axis=-1, keepdims=True)
                o_ref[0, rows, lanes] = (yh * (lax.rsqrt(ss + NORM_EPS) * qscale)).astype(BF16)

    @pl.when(j >= 2)
    def _():
        for sb in range(tr // CONV_SUB):
            rows = slice(sb * CONV_SUB, (sb + 1) * CONV_SUB)
            o_ref[0, rows, :] = conv_silu(sb).astype(BF16)


def _gdn_conv(proj3, conv_w8, lp):
    b = proj3.shape[0]
    tr = _tile(lp, 384, CONV_SUB)
    nb8 = lp // 8
    tb = tr // 8
    kern = functools.partial(_conv_kernel, tr=tr)
    return pl.pallas_call(
        kern,
        grid=(b, lp // tr, GDN_CONV_CH // CONV_TC),
        in_specs=[
            pl.BlockSpec((1, 8, CONV_TC), lambda bi, i, j: (bi, (i * tb + nb8 - 1) % nb8, j)),
            pl.BlockSpec((1, tr, CONV_TC), lambda bi, i, j: (bi, i, j)),
            pl.BlockSpec((1, 8, CONV_TC), lambda bi, i, j: (bi, ((i + 1) * tb) % nb8, j)),
            pl.BlockSpec((8, CONV_TC), lambda bi, i, j: (0, j)),
        ],
        out_specs=pl.BlockSpec((1, tr, CONV_TC), lambda bi, i, j: (bi, i, j)),
        out_shape=jax.ShapeDtypeStruct((b, lp, GDN_CONV_CH), BF16),
        scratch_shapes=[pltpu.VMEM((tr + 16, CONV_TC), F32)],
        compiler_params=_cparams(("parallel", "parallel", "arbitrary")),
        name="gdn_conv",
    )(proj3, proj3, proj3, conv_w8)


def _split3(x):
    hi = x.astype(BF16)
    r1 = x - hi.astype(F32)
    mid = r1.astype(BF16)
    lo = (r1 - mid.astype(F32)).astype(BF16)
    return hi, mid, lo


def _bdot(a, b):
    return jnp.dot(a.astype(BF16), b.astype(BF16), preferred_element_type=F32)


def _gdn_scan_kernel(qf_ref, kf_ref, vf_ref, baf_ref, qb_ref, kb_ref, vb_ref, bab_ref, alog_ref, dtb_ref,
                     of_ref, ob_ref, s_ref, sb_ref, u_ref, qw_ref, attn_ref, kd_ref, eg_ref, *, nc):
    t = pl.program_id(1)
    c = GDN_CHUNK
    c2 = 2 * c
    npair = GDN_HEADS // 2
    o_refs = (of_ref, ob_ref)

    @pl.when(t == 0)
    def _():
        for ref in (s_ref, sb_ref, u_ref, qw_ref, attn_ref, kd_ref, eg_ref):
            ref[...] = jnp.zeros_like(ref)

    applied = [dict(d=pi // npair, heads=(2 * (pi % npair), 2 * (pi % npair) + 1),
                    qs=[None, None], vnew=[None, None]) for pi in range(2 * npair)]

    def read_state(pi, hh):
        ap = applied[pi]
        rs = slice(hh * c, (hh + 1) * c)
        qws = jnp.dot(qw_ref[pi, hh], sb_ref[ap["d"], ap["heads"][hh]], preferred_element_type=F32)
        ap["qs"][hh] = qws[:c]
        ap["vnew"][hh] = (u_ref[pi, rs, :] - qws[c:]).astype(BF16)

    def write_out(pi):
        ap = applied[pi]
        h0, h1 = ap["heads"]
        vnew2 = jnp.concatenate(ap["vnew"], axis=0)
        o2 = jnp.concatenate(ap["qs"], axis=0) + jnp.dot(attn_ref[pi], vnew2, preferred_element_type=F32)
        o_refs[ap["d"]][0, :, h0 * GDN_DV:(h0 + 1) * GDN_DV] = o2[:c]
        o_refs[ap["d"]][0, :, h1 * GDN_DV:(h1 + 1) * GDN_DV] = o2[c:]

    def update_state(pi, hh):
        ap = applied[pi]
        d, h = ap["d"], ap["heads"][hh]
        rs = slice(hh * c, (hh + 1) * c)
        a_h = GDN_HEADS + h
        upd = lax.dot_general(kd_ref[pi, rs, :], ap["vnew"][hh], (((0,), (0,)), ((), ())),
                              preferred_element_type=F32)
        s_new = s_ref[d, h] * eg_ref[d, :, a_h:a_h + 1] + upd
        s_ref[d, h] = s_new
        sb_ref[d, h] = s_new.astype(BF16)

    pairs_hh = [(pi, hh) for pi in range(2 * npair) for hh in range(2)]
    apply_ops = ([functools.partial(read_state, pi, hh) for pi, hh in pairs_hh]
                 + [functools.partial(write_out, pi) for pi in range(2 * npair)]
                 + [functools.partial(update_state, pi, hh) for pi, hh in pairs_hh])

    def emit_apply(n):
        for _ in range(min(n, len(apply_ops))):
            apply_ops.pop(0)()

    emit_apply(len(pairs_hh) // 2)
    tp = jnp.minimum(t, nc - 1)
    ri = lax.broadcasted_iota(jnp.int32, (c2, c2), 0)
    ci = lax.broadcasted_iota(jnp.int32, (c2, c2), 1)
    same = (ri >> 6) == (ci >> 6)
    offdiag = ri != ci
    top =lax.broadcasted_iota(jnp.int32, (c2, 1), 0) < c
    left = lax.broadcasted_iota(jnp.int32, (1, c2), 1) < c
    row_id = lax.broadcasted_iota(jnp.int32, (c, 128), 0)

    dirs = []
    for d, (q_ref, k_ref, v_ref, ba_ref, o_ref) in enumerate(
            ((qf_ref, kf_ref, vf_ref, baf_ref, of_ref), (qb_ref, kb_ref, vb_ref, bab_ref, ob_ref))):
        seq = tp if d == 0 else nc - 1 - tp
        blk = jnp.where(seq < 2, nc - 2 + seq, seq - 2)
        first_valid = jnp.where(blk == nc - 2, c, jnp.where(blk == nc - 1, c - N_META, 0))
        valid = row_id >= first_valid
        ba = ba_ref[0]
        beta = jnp.where(valid, jax.nn.sigmoid(ba), 0.0)
        xs = ba + dtb_ref[d]
        softplus = jnp.maximum(xs, 0.0) + jnp.log(1.0 + jnp.exp(-jnp.abs(xs)))
        g = jnp.where(valid, -jnp.exp(alog_ref[d]) * softplus, 0.0)
        incl = same & ((ri >= ci) if d == 0 else (ri <= ci))
        tri = jnp.where(incl, 1.0, 0.0).astype(BF16)
        ghi, gmid, glo = _split3(jnp.concatenate([g, g], axis=0))
        gc2 = (jnp.dot(tri, ghi, preferred_element_type=F32)
               + jnp.dot(tri, gmid, preferred_element_type=F32)
               + jnp.dot(tri, glo, preferred_element_type=F32))
        dirs.append(dict(q=q_ref, k=k_ref, v=v_ref, o=o_ref, incl=incl, gc2=gc2, gc2t=gc2.T,
                         beta2=jnp.concatenate([beta, beta], axis=0),
                         gtot=jnp.sum(g, axis=0, keepdims=True)))

    probs = []
    for d in range(2):
        dd = dirs[d]
        for p in range(npair):
            h0, h1 = 2 * p, 2 * p + 1
            a0, a1 = GDN_HEADS + h0, GDN_HEADS + h1
            col = jnp.where(top, dd["gc2"][:, a0:a0 + 1], dd["gc2"][:, a1:a1 + 1])
            row = jnp.where(left, dd["gc2t"][a0:a0 + 1, :], dd["gc2t"][a1:a1 + 1, :])
            bcol = jnp.where(top, dd["beta2"][:, h0:h0 + 1], dd["beta2"][:, h1:h1 + 1])
            tot = jnp.where(top, dd["gtot"][:, a0:a0 + 1], dd["gtot"][:, a1:a1 + 1])
            dec = jnp.exp(jnp.where(dd["incl"], col - row, -jnp.inf))
            kst = jnp.concatenate([dd["k"][0, :, h0 * GDN_DK:(h0 + 1) * GDN_DK],
                                   dd["k"][0, :, h1 * GDN_DK:(h1 + 1) * GDN_DK]], axis=0)
            qst = jnp.concatenate([dd["q"][0, :, h0 * GDN_DK:(h0 + 1) * GDN_DK],
                                   dd["q"][0, :, h1 * GDN_DK:(h1 + 1) * GDN_DK]], axis=0)
            vst = jnp.concatenate([dd["v"][0, :, h0 * GDN_DV:(h0 + 1) * GDN_DV],
                                   dd["v"][0, :, h1 * GDN_DV:(h1 + 1) * GDN_DV]], axis=0)
            kf = kst.astype(F32)
            kb = kf * bcol
            sc = lax.dot_general(jnp.concatenate([qst, kb.astype(BF16)], axis=0), kst,
                                 (((1,), (1,)), ((), ())), preferred_element_type=F32)
            egc = jnp.exp(col)
            probs.append(dict(
                d=d, heads=(h0, h1), o=dd["o"], gtot=dd["gtot"],
                attn=(sc[:c2] * dec).astype(BF16),
                a=jnp.where(offdiag, sc[c2:] * dec, 0.0),
                rhs=jnp.concatenate([vst.astype(F32) * bcol, kb * egc], axis=1).astype(BF16),
                qg=(qst.astype(F32) * egc).astype(BF16),
                kd=(kf * jnp.exp(tot - col)).astype(BF16)))

    emit_apply(len(pairs_hh) // 2)

    for _ in _unit_lower_inverse_staged(probs):
        emit_apply(3)
    emit_apply(len(apply_ops))

    for pi, pr in enumerate(probs):
        uw = jnp.dot(pr["tinv"].astype(BF16), pr["rhs"], preferred_element_type=F32)
        u_ref[pi] = uw[:, :GDN_DV]
        w = uw[:, GDN_DV:].astype(BF16)
        for hh in range(2):
            rs = slice(hh * c, (hh + 1) * c)
            qw_ref[pi, hh] = jnp.concatenate([pr["qg"][rs], w[rs]], axis=0)
        attn_ref[pi] = pr["attn"]
        kd_ref[pi] = pr["kd"]
    for d in range(2):
        eg_ref[d] = jnp.exp(dirs[d]["gtot"])


def _unit_lower_inverse_staged(probs):
    c = GDN_CHUNK
    nside = 4
    ri = lax.broadcasted_iota(jnp.int32, (c, nside * c), 0)
    ci = lax.broadcasted_iota(jnp.int32, (c, nside * c), 1)
    lane_blk = ci >> 6
    within = ci & (c - 1)
    diag16 = (ri >> 4) == (within >> 4)
    eye = jnp.where(ri == within, 1.0, 0.0).astype(F32)
    left = lax.broadcasted_iota(jnp.int32, (1, 2 * c), 1) < c

    def blockdiag(y):
        return jnp.concatenate([jnp.where(lane_blk == r, y, 0.0) for r in range(nside)],
                               axis=0).astype(BF16)

    def mm(x, ybd):
        return jnp.dot(x.astype(BF16), ybd, preferred_element_type=F32)

    groups = []
    for g0 in range(0, len(probs), 2):
        pa, pb = probs[g0], probs[g0 + 1]
        a = jnp.concatenate([pa["a"][:c] + pa["a"][c:], pb["a"][:c] + pb["a"][c:]], axis=1)
        ad = jnp.where(diag16, a, 0.0)
        groups.append(dict(pairs=(pa, pb), ad=ad, an=blockdiag(a - ad), dinv=eye - ad))
    for gr in groups:
        gr["p"] = mm(gr["ad"], blockdiag(gr["ad"]))
    yield
    for gr in groups:
        pbd = blockdiag(gr["p"])
        gr["dinv"] = gr["dinv"] + mm(gr["dinv"], pbd)
        gr["p"] = mm(gr["p"], pbd)
    yield
    for gr in groups:
        pbd = blockdiag(gr["p"])
        gr["dinv"] = gr["dinv"] + mm(gr["dinv"], pbd)
        gr["p"] = mm(gr["p"], pbd)
    yield
    for gr in groups:
        gr["dinv"] = gr["dinv"] + mm(gr["dinv"], blockdiag(gr["p"]))
    yield
    for gr in groups:
        gr["m"] = mm(gr["dinv"], gr["an"])
    yield
    for gr in groups:
        gr["m2"] = blockdiag(mm(gr["m"], blockdiag(gr["m"])))
    yield
    for gr in groups:
        x = eye - gr["m"]
        gr["x"] = x + mm(x, gr["m2"])
    yield
    for gr in groups:
        t = mm(gr["x"], blockdiag(gr["dinv"]))
        for i, pr in enumerate(gr["pairs"]):
            half = t[:, 2 * c * i:2 * c * (i + 1)]
            pr["tinv"] = jnp.concatenate([jnp.where(left, half, 0.0), jnp.where(left, 0.0, half)],
                                         axis=0)


def _gdn_scan(qkv, proj3, alog_rows, dtb_rows, lp):
    b = qkv.shape[0]
    nc = lp // GDN_CHUNK
    ba_col0 = (GDN_CONV_CH + GDN_V) // 128

    def blk_of(seq):
        return jnp.where(seq < 2, nc - 2 + seq, seq - 2)

    def chunk_specs(d):
        prep = lambda t: jnp.minimum(t, nc - 1)
        seq = prep if d == 0 else (lambda t: nc - 1 - prep(t))
        return [
            pl.BlockSpec((1, GDN_CHUNK, GDN_QK), lambda bi, t: (bi, blk_of(seq(t)), 0)),
            pl.BlockSpec((1, GDN_CHUNK, GDN_QK), lambda bi, t: (bi, blk_of(seq(t)), 1)),
            pl.BlockSpec((1, GDN_CHUNK, GDN_V), lambda bi, t: (bi, blk_of(seq(t)), 1)),
            pl.BlockSpec((1, GDN_CHUNK, 128), lambda bi, t: (bi, blk_of(seq(t)), ba_col0 + d)),
        ]

    const = pl.BlockSpec((2, 1, 128), lambda bi, t: (0, 0, 0))
    kern = functools.partial(_gdn_scan_kernel, nc=nc)
    applied = lambda t: jnp.maximum(t - 1, 0)
    nprob = GDN_HEADS
    c2 = 2 * GDN_CHUNK
    return pl.pallas_call(
        kern,
        grid=(b, nc + 1),
        in_specs=chunk_specs(0) + chunk_specs(1) + [const, const],
        out_specs=[
            pl.BlockSpec((1, GDN_CHUNK, GDN_V), lambda bi, t: (bi, blk_of(applied(t)), 0)),
            pl.BlockSpec((1, GDN_CHUNK, GDN_V), lambda bi, t: (bi, blk_of(nc - 1 - applied(t)), 0)),
        ],
        out_shape=[jax.ShapeDtypeStruct((b, lp, GDN_V), F32)] * 2,
        scratch_shapes=[
            pltpu.VMEM((2, GDN_HEADS, GDN_DK, GDN_DV), F32),
            pltpu.VMEM((2, GDN_HEADS, GDN_DK, GDN_DV), BF16),
            pltpu.VMEM((nprob, c2, GDN_DV), F32),
            pltpu.VMEM((nprob, 2, c2, GDN_DK), BF16),
            pltpu.VMEM((nprob, c2, c2), BF16),
            pltpu.VMEM((nprob, c2, GDN_DK), BF16),
            pltpu.VMEM((2, 1, 128), F32),
        ],
        compiler_params=_cparams(("parallel", "arbitrary")),
        name="gdn_scan",
    )(qkv, qkv, qkv, proj3, qkv, qkv, qkv, proj3, alog_rows, dtb_rows)


def _gdn_out_kernel(of_ref, ob_ref, z_ref, h_ref, g_ref, w_ref, o_ref):
    o = of_ref[...] + ob_ref[...]
    z = z_ref[...]
    gate = z * jax.nn.sigmoid(z)
    ys = []
    for h in range(GDN_HEADS):
        lanes = slice(h * GDN_DV, (h + 1) * GDN_DV)
        oh = o[:, lanes]
        ms = jnp.mean(oh * oh, axis=-1, keepdims=True)
        ys.append((oh * lax.rsqrt(ms + NORM_EPS) * g_ref[...] * gate[:, lanes]).astype(BF16))
    y = jnp.concatenate(ys, axis=1)
    o_ref[...] = h_ref[...] + jnp.dot(y, w_ref[...], preferred_element_type=F32)


def _gdn_out(o_fwd, o_bwd, proj, h0, g_row, w_out):
    rows = h0.shape[0]
    tm = _tile(rows, 256, 128)
    zcol = GDN_CONV_CH // GDN_V
    return pl.pallas_call(
        _gdn_out_kernel,
        grid=(rows // tm,),
        in_specs=[
            pl.BlockSpec((tm, GDN_V), lambda i: (i, 0)),
            pl.BlockSpec((tm, GDN_V), lambda i: (i, 0)),
            pl.BlockSpec((tm, GDN_V), lambda i: (i, zcol)),
            pl.BlockSpec((tm, D_MODEL), lambda i: (i, 0)),
            pl.BlockSpec((1, GDN_DV), lambda i: (0, 0)),
            pl.BlockSpec((GDN_V, D_MODEL), lambda i: (0, 0)),
        ],
        out_specs=pl.BlockSpec((tm, D_MODEL), lambda i: (i, 0)),
        out_shape=jax.ShapeDtypeStruct((rows, D_MODEL), F32),
        compiler_params=_cparams(("parallel",)),
        name="gdn_out",
    )(o_fwd, o_bwd, proj, h0, g_row, w_out)


def _rope(r, cos, nsin_lo, sin_hi):
    return r * cos + pltpu.roll(r, 96, 1) * nsin_lo + pltpu.roll(r, 32, 1) * sin_hi


def _mla_q_kernel(cq_ref, g1_ref, w_ref, ga_ref, gr_ref, cos_ref, nsin_ref, sin_ref, qt_ref):
    cq = cq_ref[0]
    ms = jnp.mean(cq * cq, axis=-1, keepdims=True)
    cqn = (cq * lax.rsqrt(ms + NORM_EPS) * g1_ref[...]).astype(BF16)
    q = jnp.dot(cqn, w_ref[...], preferred_element_type=F32)
    cos, nsin, sin = cos_ref[...], nsin_ref[...], sin_ref[...]
    for h in range(MLA_HEADS):
        a = q[:, h * MLA_DPAD:h * MLA_DPAD + MLA_NOPE]
        r = q[:, h * MLA_DPAD + MLA_NOPE:(h + 1) * MLA_DPAD]
        ss = jnp.sum(a * a + r * r, axis=-1, keepdims=True)
        inv = lax.rsqrt(ss * (1.0 / MLA_DQK) + NORM_EPS)
        an = a * inv * ga_ref[...]
        rn = _rope(r * inv * gr_ref[...], cos, nsin, sin)
        qt_ref[0, h, 0:MLA_NOPE, :] = an.T.astype(BF16)
        qt_ref[0, h, MLA_NOPE:MLA_DPAD, :] = rn.T.astype(BF16)


def _mla_q(proj3, g1, w_uq, ga, gr, cos, nsin, sin, lp):
    b = proj3.shape[0]
    tm = _tile(lp, 384, 128)
    cqcol = MLA_V // MLA_Q_LORA
    const = lambda bi, i: (0, 0)
    return pl.pallas_call(
        _mla_q_kernel,
        grid=(b, lp // tm),
        in_specs=[
            pl.BlockSpec((1, tm, MLA_Q_LORA), lambda bi, i: (bi, i, cqcol)),
            pl.BlockSpec((1, MLA_Q_LORA), const),
            pl.BlockSpec((MLA_Q_LORA, MLA_HEADS * MLA_DPAD), const),
            pl.BlockSpec((1, 128), const),
            pl.BlockSpec((1, 128), const),
            pl.BlockSpec((tm, 128), lambda bi, i: (i, 0)),
            pl.BlockSpec((tm, 128), lambda bi, i: (i, 0)),
            pl.BlockSpec((tm, 128), lambda bi, i: (i, 0)),
        ],
        out_specs=pl.BlockSpec((1, MLA_HEADS, MLA_DPAD, tm), lambda bi, i: (bi, 0, 0, i)),
        out_shape=jax.ShapeDtypeStruct((b, MLA_HEADS, MLA_DPAD, lp), BF16),
        compiler_params=_cparams(("parallel", "parallel")),
        name="mla_q",
    )(proj3, g1, w_uq, ga, gr, cos, nsin, sin)


def _mla_kv_kernel(ckv_ref, kpe_ref, g1_ref, w_ref, ga_ref, gr_ref, cos_ref, nsin_ref, sin_ref,
                   k_ref, vt_ref):
    ckv = ckv_ref[0]
    ms = jnp.mean(ckv * ckv, axis=-1, keepdims=True)
    cn = (ckv * lax.rsqrt(ms + NORM_EPS) * g1_ref[...]).astype(BF16)
    kv = jnp.dot(cn, w_ref[...], preferred_element_type=F32)
    kpe = kpe_ref[0]
    sq_pe = kpe * kpe
    kr = _rope(kpe * gr_ref[...], cos_ref[...], nsin_ref[...], sin_ref[...])
    for h in range(MLA_HEADS):
        kn = kv[:, h * MLA_NOPE:(h + 1) * MLA_NOPE]
        ss = jnp.sum(kn * kn + sq_pe, axis=-1, keepdims=True)
        inv = lax.rsqrt(ss * (1.0 / MLA_DQK) + NORM_EPS)
        k_ref[0, h, :, 0:MLA_NOPE] = (kn * inv * ga_ref[...]).astype(BF16)
        k_ref[0, h, :, MLA_NOPE:MLA_DPAD] = (kr * inv).astype(BF16)
        v = kv[:, MLA_HEADS * MLA_NOPE + h * MLA_DV:MLA_HEADS * MLA_NOPE + (h + 1) * MLA_DV]
        vt_ref[0, h] = v.T.astype(BF16)


def _mla_kv(proj3, g1, w_ukv, ga, gr, cos, nsin, sin, lp):
    b = proj3.shape[0]
    tm = _tile(lp, 384, 128)
    ckvcol = (MLA_V + MLA_Q_LORA) // MLA_KV_LORA
    kpecol = (MLA_V + MLA_Q_LORA + MLA_KV_LORA) // 128
    const = lambda bi, i: (0, 0)
    return pl.pallas_call(
        _mla_kv_kernel,
        grid=(b, lp // tm),
        in_specs=[
            pl.BlockSpec((1, tm, MLA_KV_LORA), lambda bi, i: (bi, i, ckvcol)),
            pl.BlockSpec((1, tm, 128), lambda bi, i: (bi, i, kpecol)),
            pl.BlockSpec((1, MLA_KV_LORA), const),
            pl.BlockSpec((MLA_KV_LORA, MLA_HEADS * (MLA_NOPE + MLA_DV)), const),
            pl.BlockSpec((1, 128), const),
            pl.BlockSpec((1, 128), const),
            pl.BlockSpec((tm, 128), lambda bi, i: (i, 0)),
            pl.BlockSpec((tm, 128), lambda bi, i: (i, 0)),
            pl.BlockSpec((tm, 128), lambda bi, i: (i, 0)),
        ],
        out_specs=[
            pl.BlockSpec((1, MLA_HEADS, tm, MLA_DPAD), lambda bi, i: (bi, 0, i, 0)),
            pl.BlockSpec((1, MLA_HEADS, MLA_DV, tm), lambda bi, i: (bi, 0, 0, i)),
        ],
        out_shape=[
            jax.ShapeDtypeStruct((b, MLA_HEADS, lp, MLA_DPAD), BF16),
            jax.ShapeDtypeStruct((b, MLA_HEADS, MLA_DV, lp), BF16),
        ],
        compiler_params=_cparams(("parallel", "parallel")),
        name="mla_kv",
    )(proj3, proj3, g1, w_ukv, ga, gr, cos, nsin, sin)


ATT_TK = 1024
ATT_TQ = 512
ATT_AHEAD = 2


def _attn_kernel(qt_ref, k_ref, vt_ref, o_ref, *, s_len):
    tq = qt_ref.shape[3]
    nstrip = tq // ATT_TQ
    chunks = [(ck * ATT_TK, ATT_TK) for ck in range(s_len // ATT_TK)] + [(s_len, TAIL)]
    units = [(i, s) for i in range(len(chunks)) for s in range(nstrip)]
    is_meta = lax.broadcasted_iota(jnp.int32, (TAIL, 1), 0) >= TAIL - N_META
    tail_bias = jnp.where(is_meta, 0.0, -jnp.inf).astype(F32)

    def scores(u):
        i, s = units[u]
        k0, nk = chunks[i]
        st = jnp.dot(k_ref[0, 0, k0:k0 + nk, :], qt_ref[0, 0, :, s * ATT_TQ:(s + 1) * ATT_TQ],
                     preferred_element_type=F32)
        return st + tail_bias if i == len(chunks) - 1 else st

    m = [jnp.full((1, ATT_TQ), -jnp.inf, F32) for _ in range(nstrip)]
    l = [jnp.zeros((1, ATT_TQ), F32) for _ in range(nstrip)]
    acc = [jnp.zeros((MLA_DV, ATT_TQ), F32) for _ in range(nstrip)]
    pending = [scores(u) for u in range(min(ATT_AHEAD, len(units)))]
    for u, (i, s) in enumerate(units):
        if u + ATT_AHEAD < len(units):
            pending.append(scores(u + ATT_AHEAD))
        st = pending.pop(0)
        k0, nk = chunks[i]
        m_new = jnp.maximum(m[s], jnp.max(st, axis=0, keepdims=True))
        alpha = jnp.exp2(m[s] - m_new)
        p = jnp.exp2(st - m_new)
        l[s] = alpha * l[s] + jnp.sum(p, axis=0, keepdims=True)
        acc[s] = alpha * acc[s] + jnp.dot(vt_ref[0, 0, :, k0:k0 + nk], p.astype(BF16),
                                          preferred_element_type=F32)
        m[s] = m_new
    for s in range(nstrip):
        o_ref[0, s * ATT_TQ:(s + 1) * ATT_TQ, :] = (acc[s] / l[s]).T.astype(o_ref.dtype)


def _attention(qt, k, vt, s_len):
    b = qt.shape[0]
    lp = k.shape[2]
    assert s_len % ATT_TK == 0 and lp == s_len + TAIL
    tq = _tile(s_len, 1024, ATT_TQ)
    kern = functools.partial(_attn_kernel, s_len=s_len)
    return pl.pallas_call(
        kern,
        grid=(b, MLA_HEADS, s_len // tq),
        in_specs=[
            pl.BlockSpec((1, 1, MLA_DPAD, tq), lambda bi, h, i: (bi, h, 0, i)),
            pl.BlockSpec((1, 1, lp, MLA_DPAD), lambda bi, h, i: (bi, h, 0, 0)),
            pl.BlockSpec((1, 1, MLA_DV, lp), lambda bi, h, i: (bi, h, 0, 0)),
        ],
        out_specs=pl.BlockSpec((1, tq, MLA_DV), lambda bi, h, i: (bi, i, h)),
        out_shape=jax.ShapeDtypeStruct((b, s_len, MLA_V), BF16),
        compiler_params=_cparams(("parallel", "parallel", "arbitrary")),
        name="mla_attention",
    )(qt, k, vt)


def _mla_out_kernel(o_ref, z_ref, h_ref, w_ref, y_ref):
    z = z_ref[0]
    y = (o_ref[0].astype(F32) * (z * jax.nn.sigmoid(z))).astype(BF16)
    y_ref[0] = h_ref[0] + jnp.dot(y, w_ref[...], preferred_element_type=F32)


def _mla_out(o, proj3, h3, w_out, b0, nb, s_len):
    tm = _tile(s_len, 512, 128)
    return pl.pallas_call(
        _mla_out_kernel,
        grid=(nb, s_len // tm),
        in_specs=[
            pl.BlockSpec((1, tm, MLA_V), lambda bi, i: (bi + b0, i, 0)),
            pl.BlockSpec((1, tm, MLA_V), lambda bi, i: (bi + b0, i, 0)),
            pl.BlockSpec((1, tm, D_MODEL), lambda bi, i: (bi + b0, i, 0)),
            pl.BlockSpec((MLA_V, D_MODEL), lambda bi, i: (0, 0)),
        ],
        out_specs=pl.BlockSpec((1, tm, D_MODEL), lambda bi, i: (bi, i, 0)),
        out_shape=jax.ShapeDtypeStruct((nb, s_len, D_MODEL), F32),
        compiler_params=_cparams(("parallel", "parallel")),
        name="mla_out",
    )(o, proj3, h3, w_out)


def _pad_cols(w, n):
    return jnp.pad(w, ((0, 0), (0, n - w.shape[1])))


def _lane_row(v, n=128):
    return jnp.pad(v.astype(F32), (0, n - v.shape[0]))[None, :]


def _trunk_all(xs, meta_tokens, ln_g, gdn_w_in, gdn_conv_w, gdn_a_log, gdn_dt_bias, gdn_o_norm_g,
               gdn_w_out, mla_w_in, mla_q_norm_g, mla_kv_norm_g, mla_w_uq, mla_w_ukv, mla_qk_q_g,
               mla_qk_k_g, mla_w_out):
    s_len = xs[0].shape[1]
    assert all(x.shape[1] == s_len for x in xs) and s_len % 128 == 0
    lp = s_len + TAIL
    x_all = jnp.concatenate(xs, axis=0)
    b = x_all.shape[0]
    meta = jnp.broadcast_to(meta_tokens[None].astype(F32), (b, N_META, D_MODEL))
    h0 = jnp.concatenate([x_all, jnp.zeros((b, TAIL - N_META, D_MODEL), F32), meta], axis=1)
    h0 = h0.reshape(b * lp, D_MODEL)

    w_in = gdn_w_in[0]
    ba = w_in[:, GDN_CONV_CH + GDN_V:].reshape(D_MODEL, 2, 2, GDN_HEADS)
    ba_dir = [_pad_cols(jnp.concatenate([ba[:, 0, d], ba[:, 1, d]], axis=1), 128) for d in range(2)]
    w0 = jnp.concatenate([w_in[:, :GDN_CONV_CH + GDN_V]] + ba_dir, axis=1).astype(BF16)
    proj0 = _norm_matmul(h0, ln_g[0][None, :], w0, 1280)

    conv_w8 = jnp.pad(gdn_conv_w[0], ((0, 8 - GDN_CONV), (0, 0)))
    proj0_3 = proj0.reshape(b, lp, proj0.shape[1])
    qkv = _gdn_conv(proj0_3, conv_w8, lp)

    lane_a = lambda v: jnp.pad(v.astype(F32), ((0, 0), (GDN_HEADS, 128 - 2 * GDN_HEADS)))[:, None, :]
    o_fwd, o_bwd = _gdn_scan(qkv, proj0_3, lane_a(gdn_a_log[0]), lane_a(gdn_dt_bias[0]), lp)
    h1 = _gdn_out(o_fwd.reshape(b * lp, GDN_V), o_bwd.reshape(b * lp, GDN_V), proj0, h0,
                  gdn_o_norm_g[0][None, :], gdn_w_out[0].astype(BF16))

    w_in1 = mla_w_in[0]
    o1 = MLA_Q_LORA
    o2_ = o1 + MLA_KV_LORA
    o3 = o2_ + MLA_ROPE
    w1 = jnp.concatenate([w_in1[:, o3:], w_in1[:, :o1], w_in1[:, o1:o2_],
                          _pad_cols(w_in1[:, o2_:o3], 256)], axis=1).astype(BF16)
    proj1 = _norm_matmul(h1, ln_g[1][None, :], w1, 1024)
    proj1_3 = proj1.reshape(b, lp, proj1.shape[1])

    pos = jnp.concatenate([jnp.arange(s_len, dtype=F32) + N_META, jnp.zeros((TAIL - N_META,), F32),
                           jnp.arange(N_META, dtype=F32)])
    inv = ROPE_THETA ** (-jnp.arange(0, MLA_ROPE, 2, dtype=F32) / MLA_ROPE)
    ang = pos[:, None] * inv[None, :]
    zc = jnp.zeros_like(ang)
    cos_t = jnp.concatenate([jnp.cos(ang), jnp.cos(ang), zc, zc], axis=1)
    nsin_t = jnp.concatenate([-jnp.sin(ang), zc, zc, zc], axis=1)
    sin_t = jnp.concatenate([zc, jnp.sin(ang), zc, zc], axis=1)

    scale = MLA_DQK ** -0.5 * math.log2(math.e)
    w_uq = mla_w_uq[0].reshape(MLA_Q_LORA, MLA_HEADS, MLA_DQK)
    w_uq = jnp.pad(w_uq, ((0, 0), (0, 0), (0, MLA_DPAD - MLA_DQK))).reshape(MLA_Q_LORA, -1).astype(BF16)
    gq = mla_qk_q_g[0].astype(F32) * scale
    qt = _mla_q(proj1_3, mla_q_norm_g[0][None, :], w_uq, gq[None, :MLA_NOPE],
                _lane_row(gq[MLA_NOPE:]), cos_t, nsin_t, sin_t, lp)

    w_ukv = mla_w_ukv[0].reshape(MLA_KV_LORA, MLA_HEADS, MLA_NOPE + MLA_DV)
    w_ukv = jnp.concatenate([w_ukv[:, :, :MLA_NOPE].reshape(MLA_KV_LORA, -1),
                             w_ukv[:, :, MLA_NOPE:].reshape(MLA_KV_LORA, -1)], axis=1).astype(BF16)
    gk = mla_qk_k_g[0].astype(F32)
    k, vt = _mla_kv(proj1_3, mla_kv_norm_g[0][None, :], w_ukv, gk[None, :MLA_NOPE],
                    _lane_row(gk[MLA_NOPE:]), cos_t, nsin_t, sin_t, lp)

    o = _attention(qt, k, vt, s_len)

    h1_3 = h1.reshape(b, lp, D_MODEL)
    w_out1 = mla_w_out[0].astype(BF16)
    outs = []
    b0 = 0
    for x in xs:
        outs.append(_mla_out(o, proj1_3, h1_3, w_out1, b0, x.shape[0], s_len))
        b0 += x.shape[0]
    return tuple(outs)


def kernel(x_prompt, x_sample, meta_tokens, ln_g, gdn_w_in, gdn_conv_w, gdn_a_log, gdn_dt_bias,
           gdn_o_norm_g, gdn_w_out, mla_w_in, mla_q_norm_g, mla_kv_norm_g, mla_w_uq, mla_w_ukv,
           mla_qk_q_g, mla_qk_k_g, mla_w_out):
    return _trunk_all((x_prompt, x_sample), meta_tokens, ln_g, gdn_w_in, gdn_conv_w, gdn_a_log,
                      gdn_dt_bias, gdn_o_norm_g, gdn_w_out, mla_w_in, mla_q_norm_g, mla_kv_norm_g,
                      mla_w_uq, mla_w_ukv, mla_qk_q_g, mla_qk_k_g, mla_w_out)
```

```python
import functools
import math

import jax
import jax.numpy as jnp
from jax import lax
from jax.experimental import pallas as pl
from jax.experimental.pallas import tpu as pltpu

F32 = jnp.float32
BF16 = jnp.bfloat16

D_MODEL = 1024
N_META = 16
TAIL = 128
NORM_EPS = 1e-6

GDN_HEADS = 8
GDN_DK = 128
GDN_DV = 256
GDN_CONV = 5
GDN_CHUNK = 64
GDN_QK = GDN_HEADS * GDN_DK
GDN_V = GDN_HEADS * GDN_DV
GDN_CONV_CH = 2 * GDN_QK + GDN_V

MLA_HEADS = 16
MLA_Q_LORA = 512
MLA_KV_LORA = 256
MLA_NOPE = 128
MLA_ROPE = 64
MLA_DQK = MLA_NOPE + MLA_ROPE
MLA_DV = 128
MLA_V = MLA_HEADS * MLA_DV
MLA_DPAD = 256
ROPE_THETA = 10000.0

VMEM_LIMIT = 56 * 1024 * 1024


def _cparams(sem):
    return pltpu.CompilerParams(dimension_semantics=sem, vmem_limit_bytes=VMEM_LIMIT)


def _tile(n, target, mult):
    best = None
    t = mult
    while t <= min(n, target):
        if n % t == 0:
            best = t
        t += mult
    assert best is not None, (n, target, mult)
    return best


def _norm_matmul_kernel(x_ref, g_ref, w_ref, o_ref, xn_ref):
    @pl.when(pl.program_id(1) == 0)
    def _():
        x = x_ref[...]
        ms = jnp.mean(x * x, axis=-1, keepdims=True)
        xn_ref[...] = (x * lax.rsqrt(ms + NORM_EPS) * g_ref[...]).astype(BF16)

    o_ref[...] = jnp.dot(xn_ref[...], w_ref[...], preferred_element_type=F32)


def _norm_matmul(x, g, w, tn):
    rows, d = x.shape
    n = w.shape[1]
    tm = _tile(rows, 1408, 128)
    return pl.pallas_call(
        _norm_matmul_kernel,
        grid=(rows // tm, n // tn),
        in_specs=[
            pl.BlockSpec((tm, d), lambda i, j: (i, 0)),
            pl.BlockSpec((1, d), lambda i, j: (0, 0)),
            pl.BlockSpec((d, tn), lambda i, j: (0, j)),
        ],
        out_specs=pl.BlockSpec((tm, tn), lambda i, j: (i, j)),
        out_shape=jax.ShapeDtypeStruct((rows, n), F32),
        scratch_shapes=[pltpu.VMEM((tm, d), BF16)],
        compiler_params=_cparams(("parallel", "arbitrary")),
        name="norm_matmul",
    )(x, g, w)


CONV_SUB = 128
CONV_TC = 1024


def _conv_kernel(prev_ref, main_ref, next_ref, w_ref, o_ref, ext_ref, *, tr):
    j = pl.program_id(2)
    half = GDN_CONV // 2
    ext_ref[0:8, :] = prev_ref[0]
    ext_ref[8:8 + tr, :] = main_ref[0]
    ext_ref[8 + tr:16 + tr, :] = next_ref[0]
    qscale = jnp.where(j == 0, GDN_DK ** -0.5, 1.0).astype(F32)

    def conv_silu(sb):
        r0 = 8 + sb * CONV_SUB - half
        acc = ext_ref[r0:r0 + CONV_SUB, :] * w_ref[0:1, :]
        for t in range(1, GDN_CONV):
            acc = acc + ext_ref[r0 + t:r0 + t + CONV_SUB, :] * w_ref[t:t + 1, :]
        return acc * jax.nn.sigmoid(acc)

    @pl.when(j < 2)
    def _():
        for sb in range(tr // CONV_SUB):
            y = conv_silu(sb)
            rows = slice(sb * CONV_SUB, (sb + 1) * CONV_SUB)
            for h in range(CONV_TC // GDN_DK):
                lanes = slice(h * GDN_DK, (h + 1) * GDN_DK)
                yh = y[:, lanes]
                ss = jnp.sum(yh * yh, axis=-1, keepdims=True)
                o_ref[0, rows, lanes] = (yh * (lax.rsqrt(ss + NORM_EPS) * qscale)).astype(BF16)

    @pl.when(j >= 2)
    def _():
        for sb in range(tr // CONV_SUB):
            rows = slice(sb * CONV_SUB, (sb + 1) * CONV_SUB)
            o_ref[0, rows, :] = conv_silu(sb).astype(BF16)


def _gdn_conv(proj3, conv_w8, lp):
    b = proj3.shape[0]
    tr = _tile(lp, 384, CONV_SUB)
    nb8 = lp // 8
    tb = tr // 8
    kern = functools.partial(_conv_kernel, tr=tr)
    return pl.pallas_call(
        kern,
        grid=(b, lp // tr, GDN_CONV_CH // CONV_TC),
        in_specs=[
            pl.BlockSpec((1, 8, CONV_TC), lambda bi, i, j: (bi, (i * tb + nb8 - 1) % nb8, j)),
            pl.BlockSpec((1, tr, CONV_TC), lambda bi, i, j: (bi, i, j)),
            pl.BlockSpec((1, 8, CONV_TC), lambda bi, i, j: (bi, ((i + 1) * tb) % nb8, j)),
            pl.BlockSpec((8, CONV_TC), lambda bi, i, j: (0, j)),
        ],
        out_specs=pl.BlockSpec((1, tr, CONV_TC), lambda bi, i, j: (bi, i, j)),
        out_shape=jax.ShapeDtypeStruct((b, lp, GDN_CONV_CH), BF16),
        scratch_shapes=[pltpu.VMEM((tr + 16, CONV_TC), F32)],
        compiler_params=_cparams(("parallel", "parallel", "arbitrary")),
        name="gdn_conv",
    )(proj3, proj3, proj3, conv_w8)


SCAN_NB = 2


def _split3(x):
    hi = x.astype(BF16)
    r1 = x - hi.astype(F32)
    mid = r1.astype(BF16)
    lo = (r1 - mid.astype(F32)).astype(BF16)
    return hi, mid, lo


def _bdot(a, b):
    return jnp.dot(a.astype(BF16), b.astype(BF16), preferred_element_type=F32)


def _gdn_scan_kernel(qf_ref, kf_ref, vf_ref, baf_ref, qb_ref, kb_ref, vb_ref, bab_ref, alog_ref, dtb_ref,
                     of_ref, ob_ref, s_ref, sb_ref, u_ref, qw_ref, attn_ref, kd_ref, eg_ref, *, nc):
    t = pl.program_id(1)
    c = GDN_CHUNK
    c2 = 2 * c
    npair = GDN_HEADS // 2
    o_refs = (of_ref, ob_ref)

    @pl.when(t == 0)
    def _():
        for ref in (s_ref, sb_ref, u_ref, qw_ref, attn_ref, kd_ref, eg_ref):
            ref[...] = jnp.zeros_like(ref)

    nb = of_ref.shape[0]
    nprob = nb * 2 * npair
    applied = [dict(sq=pi // npair, heads=(2 * (pi % npair), 2 * (pi % npair) + 1),
                    qs=[None, None], vnew=[None, None]) for pi in range(nprob)]

    def read_state(pi, hh):
        ap = applied[pi]
        rs = slice(hh * c, (hh + 1) * c)
        qws = jnp.dot(qw_ref[pi, hh], sb_ref[ap["sq"], ap["heads"][hh]], preferred_element_type=F32)
        ap["qs"][hh] = qws[:c]
        ap["vnew"][hh] = (u_ref[pi, rs, :] - qws[c:]).astype(BF16)

    def write_out(pi):
        ap = applied[pi]
        h0, h1 = ap["heads"]
        bi, d = divmod(ap["sq"], 2)
        vnew2 = jnp.concatenate(ap["vnew"], axis=0)
        o2 = jnp.concatenate(ap["qs"], axis=0) + jnp.dot(attn_ref[pi], vnew2, preferred_element_type=F32)
        o_refs[d][bi, :, h0 * GDN_DV:(h0 + 1) * GDN_DV] = o2[:c].astype(BF16)
        o_refs[d][bi, :, h1 * GDN_DV:(h1 + 1) * GDN_DV] = o2[c:].astype(BF16)

    def update_state(pi, hh):
        ap = applied[pi]
        sq, h = ap["sq"], ap["heads"][hh]
        rs = slice(hh * c, (hh + 1) * c)
        a_h = GDN_HEADS + h
        upd = lax.dot_general(kd_ref[pi, rs, :], ap["vnew"][hh], (((0,), (0,)), ((), ())),
                              preferred_element_type=F32)
        s_new = s_ref[sq, h] * eg_ref[sq, :, a_h:a_h + 1] + upd
        s_ref[sq, h] = s_new
        sb_ref[sq, h] = s_new.astype(BF16)

    pairs_hh = [(pi, hh) for pi in range(nprob) for hh in range(2)]
    apply_ops = ([functools.partial(read_state, pi, hh) for pi, hh in pairs_hh]
                 + [functools.partial(write_out, pi) for pi in range(nprob)]
                 + [functools.partial(update_state, pi, hh) for pi, hh in pairs_hh])

    def emit_apply(n):
        for _ in range(min(n, len(apply_ops))):
            apply_ops.pop(0)()

    emit_apply(len(pairs_hh) // 2)
    tp = jnp.minimum(t, nc - 1)
    ri = lax.broadcasted_iota(jnp.int32, (c2, c2), 0)
    ci = lax.broadcasted_iota(jnp.int32, (c2, c2), 1)
    same = (ri >> 6) == (ci >> 6)
    offdiag = ri != ci
    top =lax.broadcasted_iota(jnp.int32, (c2, 1), 0) < c
    left = lax.broadcasted_iota(jnp.int32, (1, c2), 1) < c
    row_id = lax.broadcasted_iota(jnp.int32, (c, 128), 0)

    in_refs = ((qf_ref, kf_ref, vf_ref, baf_ref), (qb_ref, kb_ref, vb_ref, bab_ref))
    seqs = []
    for sq in range(2 * nb):
        bi, d = divmod(sq, 2)
        q_ref, k_ref, v_ref, ba_ref = in_refs[d]
        seq = tp if d == 0 else nc - 1 - tp
        blk = jnp.where(seq < 2, nc - 2 + seq, seq - 2)
        first_valid = jnp.where(blk == nc - 2, c, jnp.where(blk == nc - 1, c - N_META, 0))
        valid = row_id >= first_valid
        ba = ba_ref[bi]
        beta = jnp.where(valid, jax.nn.sigmoid(ba), 0.0)
        xs = ba + dtb_ref[d]
        softplus = jnp.maximum(xs, 0.0) + jnp.log(1.0 + jnp.exp(-jnp.abs(xs)))
        g = jnp.where(valid, -jnp.exp(alog_ref[d]) * softplus, 0.0)
        incl = same & ((ri >= ci) if d == 0 else (ri <= ci))
        tri = jnp.where(incl, 1.0, 0.0).astype(BF16)
        ghi, gmid, glo = _split3(jnp.concatenate([g, g], axis=0))
        gc2 = (jnp.dot(tri, ghi, preferred_element_type=F32)
               + jnp.dot(tri, gmid, preferred_element_type=F32)
               + jnp.dot(tri, glo, preferred_element_type=F32))
        seqs.append(dict(q=q_ref.at[bi], k=k_ref.at[bi], v=v_ref.at[bi], incl=incl, gc2=gc2, gc2t=gc2.T,
                         beta2=jnp.concatenate([beta, beta], axis=0),
                         gtot=jnp.sum(g, axis=0, keepdims=True)))

    probs = []
    for dd in seqs:
        for p in range(npair):
            h0, h1 = 2 * p, 2 * p + 1
            a0, a1 = GDN_HEADS + h0, GDN_HEADS + h1
            col = jnp.where(top, dd["gc2"][:, a0:a0 + 1], dd["gc2"][:, a1:a1 + 1])
            row = jnp.where(left, dd["gc2t"][a0:a0 + 1, :], dd["gc2t"][a1:a1 + 1, :])
            bcol = jnp.where(top, dd["beta2"][:, h0:h0 + 1], dd["beta2"][:, h1:h1 + 1])
            tot = jnp.where(top, dd["gtot"][:, a0:a0 + 1], dd["gtot"][:, a1:a1 + 1])
            dec = jnp.exp(jnp.where(dd["incl"], col - row, -jnp.inf))
            kst = jnp.concatenate([dd["k"][:, h0 * GDN_DK:(h0 + 1) * GDN_DK],
                                   dd["k"][:, h1 * GDN_DK:(h1 + 1) * GDN_DK]], axis=0)
            qst = jnp.concatenate([dd["q"][:, h0 * GDN_DK:(h0 + 1) * GDN_DK],
                                   dd["q"][:, h1 * GDN_DK:(h1 + 1) * GDN_DK]], axis=0)
            vst = jnp.concatenate([dd["v"][:, h0 * GDN_DV:(h0 + 1) * GDN_DV],
                                   dd["v"][:, h1 * GDN_DV:(h1 + 1) * GDN_DV]], axis=0)
            kf = kst.astype(F32)
            kb = kf * bcol
            sc = lax.dot_general(jnp.concatenate([qst, kb.astype(BF16)], axis=0), kst,
                                 (((1,), (1,)), ((), ())), preferred_element_type=F32)
            egc = jnp.exp(col)
            probs.append(dict(
                attn=(sc[:c2] * dec).astype(BF16),
                a=jnp.where(offdiag, sc[c2:] * dec, 0.0),
                rhs=jnp.concatenate([vst.astype(F32) * bcol, kb * egc], axis=1).astype(BF16),
                qg=(qst.astype(F32) * egc).astype(BF16),
                kd=(kf * jnp.exp(tot - col)).astype(BF16)))

    emit_apply(len(pairs_hh) // 2)

    for _ in _unit_lower_inverse_staged(probs):
        emit_apply(3 * nb)
    emit_apply(len(apply_ops))

    for pi, pr in enumerate(probs):
        uw = jnp.dot(pr["tinv"].astype(BF16), pr["rhs"], preferred_element_type=F32)
        u_ref[pi] = uw[:, :GDN_DV]
        w = uw[:, GDN_DV:].astype(BF16)
        for hh in range(2):
            rs = slice(hh * c, (hh + 1) * c)
            qw_ref[pi, hh] = jnp.concatenate([pr["qg"][rs], w[rs]], axis=0)
        attn_ref[pi] = pr["attn"]
        kd_ref[pi] = pr["kd"]
    for sq, dd in enumerate(seqs):
        eg_ref[sq] = jnp.exp(dd["gtot"])


def _unit_lower_inverse_staged(probs):
    c = GDN_CHUNK
    nside = 4
    ri = lax.broadcasted_iota(jnp.int32, (c, nside * c), 0)
    ci = lax.broadcasted_iota(jnp.int32, (c, nside * c), 1)
    lane_blk = ci >> 6
    within = ci & (c - 1)
    diag16 = (ri >> 4) == (within >> 4)
    eye = jnp.where(ri == within, 1.0, 0.0).astype(F32)
    left = lax.broadcasted_iota(jnp.int32, (1, 2 * c), 1) < c

    def blockdiag(y):
        return jnp.concatenate([jnp.where(lane_blk == r, y, 0.0) for r in range(nside)],
                               axis=0).astype(BF16)

    def mm(x, ybd):
        return jnp.dot(x.astype(BF16), ybd, preferred_element_type=F32)

    groups = []
    for g0 in range(0, len(probs), 2):
        pa, pb = probs[g0], probs[g0 + 1]
        a = jnp.concatenate([pa["a"][:c] + pa["a"][c:], pb["a"][:c] + pb["a"][c:]], axis=1)
        ad = jnp.where(diag16, a, 0.0)
        groups.append(dict(pairs=(pa, pb), ad=ad, an=blockdiag(a - ad), dinv=eye - ad))
    for gr in groups:
        gr["p"] = mm(gr["ad"], blockdiag(gr["ad"]))
    yield
    for gr in groups:
        pbd = blockdiag(gr["p"])
        gr["dinv"] = gr["dinv"] + mm(gr["dinv"], pbd)
        gr["p"] = mm(gr["p"], pbd)
    yield
    for gr in groups:
        pbd = blockdiag(gr["p"])
        gr["dinv"] = gr["dinv"] + mm(gr["dinv"], pbd)
        gr["p"] = mm(gr["p"], pbd)
    yield
    for gr in groups:
        gr["dinv"] = gr["dinv"] + mm(gr["dinv"], blockdiag(gr["p"]))
    yield
    for gr in groups:
        gr["m"] = mm(gr["dinv"], gr["an"])
    yield
    for gr in groups:
        gr["m2"] = blockdiag(mm(gr["m"], blockdiag(gr["m"])))
    yield
    for gr in groups:
        x = eye - gr["m"]
        gr["x"] = x + mm(x, gr["m2"])
    yield
    for gr in groups:
        t = mm(gr["x"], blockdiag(gr["dinv"]))
        for i, pr in enumerate(gr["pairs"]):
            half = t[:, 2 * c * i:2 * c * (i + 1)]
            pr["tinv"] = jnp.concatenate([jnp.where(left, half, 0.0), jnp.where(left, 0.0, half)],
                                         axis=0)


def _gdn_scan(qkv, proj3, alog_rows, dtb_rows, lp):
    b = qkv.shape[0]
    nc = lp // GDN_CHUNK
    ba_col0 = (GDN_CONV_CH + GDN_V) // 128

    def blk_of(seq):
        return jnp.where(seq < 2, nc - 2 + seq, seq - 2)

    def chunk_specs(d):
        prep = lambda t: jnp.minimum(t, nc - 1)
        seq = prep if d == 0 else (lambda t: nc - 1 - prep(t))
        return [
            pl.BlockSpec((nb, GDN_CHUNK, GDN_QK), lambda bi, t: (bi, blk_of(seq(t)), 0)),
            pl.BlockSpec((nb, GDN_CHUNK, GDN_QK), lambda bi, t: (bi, blk_of(seq(t)), 1)),
            pl.BlockSpec((nb, GDN_CHUNK, GDN_V), lambda bi, t: (bi, blk_of(seq(t)), 1)),
            pl.BlockSpec((nb, GDN_CHUNK, 128), lambda bi, t: (bi, blk_of(seq(t)), ba_col0 + d)),
        ]

    nb = SCAN_NB if b % SCAN_NB == 0 else 1
    const = pl.BlockSpec((2, 1, 128), lambda bi, t: (0, 0, 0))
    kern = functools.partial(_gdn_scan_kernel, nc=nc)
    applied = lambda t: jnp.maximum(t - 1, 0)
    nseq = 2 * nb
    nprob = nseq * (GDN_HEADS // 2)
    c2 = 2 * GDN_CHUNK
    return pl.pallas_call(
        kern,
        grid=(b // nb, nc + 1),
        in_specs=chunk_specs(0) + chunk_specs(1) + [const, const],
        out_specs=[
            pl.BlockSpec((nb, GDN_CHUNK, GDN_V), lambda bi, t: (bi, blk_of(applied(t)), 0)),
            pl.BlockSpec((nb, GDN_CHUNK, GDN_V), lambda bi, t: (bi, blk_of(nc - 1 - applied(t)), 0)),
        ],
        out_shape=[jax.ShapeDtypeStruct((b, lp, GDN_V), BF16)] * 2,
        scratch_shapes=[
            pltpu.VMEM((nseq, GDN_HEADS, GDN_DK, GDN_DV), F32),
            pltpu.VMEM((nseq, GDN_HEADS, GDN_DK, GDN_DV), BF16),
            pltpu.VMEM((nprob, c2, GDN_DV), F32),
            pltpu.VMEM((nprob, 2, c2, GDN_DK), BF16),
            pltpu.VMEM((nprob, c2, c2), BF16),
            pltpu.VMEM((nprob, c2, GDN_DK), BF16),
            pltpu.VMEM((nseq, 1, 128), F32),
        ],
        compiler_params=_cparams(("parallel", "arbitrary")),
        name="gdn_scan",
    )(qkv, qkv, qkv, proj3, qkv, qkv, qkv, proj3, alog_rows, dtb_rows)


def _gdn_out_kernel(of_ref, ob_ref, z_ref, h_ref, g_ref, w_ref, o_ref):
    o = of_ref[...].astype(F32) + ob_ref[...].astype(F32)
    z = z_ref[...]
    gate = z * jax.nn.sigmoid(z)
    ys = []
    for h in range(GDN_HEADS):
        lanes = slice(h * GDN_DV, (h + 1) * GDN_DV)
        oh = o[:, lanes]
        ms = jnp.mean(oh * oh, axis=-1, keepdims=True)
        ys.append((oh * lax.rsqrt(ms + NORM_EPS) * g_ref[...] * gate[:, lanes]).astype(BF16))
    y = jnp.concatenate(ys, axis=1)
    o_ref[...] = h_ref[...] + jnp.dot(y, w_ref[...], preferred_element_type=F32)


def _gdn_out(o_fwd, o_bwd, proj, h0, g_row, w_out):
    rows = h0.shape[0]
    tm = _tile(rows, 512, 128)
    zcol = GDN_CONV_CH // GDN_V
    return pl.pallas_call(
        _gdn_out_kernel,
        grid=(rows // tm,),
        in_specs=[
            pl.BlockSpec((tm, GDN_V), lambda i: (i, 0)),
            pl.BlockSpec((tm, GDN_V), lambda i: (i, 0)),
            pl.BlockSpec((tm, GDN_V), lambda i: (i, zcol)),
            pl.BlockSpec((tm, D_MODEL), lambda i: (i, 0)),
            pl.BlockSpec((1, GDN_DV), lambda i: (0, 0)),
            pl.BlockSpec((GDN_V, D_MODEL), lambda i: (0, 0)),
        ],
        out_specs=pl.BlockSpec((tm, D_MODEL), lambda i: (i, 0)),
        out_shape=jax.ShapeDtypeStruct((rows, D_MODEL), F32),
        compiler_params=_cparams(("parallel",)),
        name="gdn_out",
    )(o_fwd, o_bwd, proj, h0, g_row, w_out)


def _rope(r, cos, nsin_lo, sin_hi):
    return r * cos + pltpu.roll(r, 96, 1) * nsin_lo + pltpu.roll(r, 32, 1) * sin_hi


def _mla_q_kernel(cq_ref, g1_ref, wt_ref, ga_ref, gr_ref, cos_ref, sin_ref, qt_ref):
    cq = cq_ref[0]
    tm = cq.shape[0]
    ms = jnp.mean(cq * cq, axis=-1, keepdims=True)
    cqt = (cq * lax.rsqrt(ms + NORM_EPS) * g1_ref[...]).T.astype(BF16)
    q = jnp.dot(wt_ref[...], cqt, preferred_element_type=F32)
    half = MLA_ROPE // 2
    tile = lambda g: jnp.concatenate([g] * (tm // 128), axis=1)
    ga = tile(ga_ref[...])
    gr1, gr2 = tile(gr_ref[0:half, :]), tile(gr_ref[half:MLA_ROPE, :])
    cos, sin = cos_ref[...], sin_ref[...]
    zeros = jnp.zeros((MLA_DPAD - MLA_DQK, tm), BF16)
    for h in range(MLA_HEADS):
        r0 = h * MLA_DQK
        a = q[r0:r0 + MLA_NOPE]
        x1 = q[r0 + MLA_NOPE:r0 + MLA_NOPE + half]
        x2 = q[r0 + MLA_NOPE + half:r0 + MLA_DQK]
        ss = (jnp.sum(a * a, axis=0, keepdims=True) + jnp.sum(x1 * x1, axis=0, keepdims=True)
              + jnp.sum(x2 * x2, axis=0, keepdims=True))
        inv = lax.rsqrt(ss * (1.0 / MLA_DQK) + NORM_EPS)
        x1 = x1 * inv * gr1
        x2 = x2 * inv * gr2
        qt_ref[0, h, 0:MLA_NOPE, :] = (a * inv * ga).astype(BF16)
        qt_ref[0, h, MLA_NOPE:MLA_NOPE + half, :] = (x1 * cos - x2 * sin).astype(BF16)
        qt_ref[0, h, MLA_NOPE + half:MLA_DQK, :] = (x2 * cos + x1 * sin).astype(BF16)
        qt_ref[0, h, MLA_DQK:MLA_DPAD, :] = zeros


def _mla_q(proj3, g1, w_uq_t, ga, gr, cos_t, sin_t, lp):
    b = proj3.shape[0]
    tm = _tile(lp, 384, 128)
    cqcol = MLA_V // MLA_Q_LORA
    const = lambda bi, i: (0, 0)
    return pl.pallas_call(
        _mla_q_kernel,
        grid=(b, lp // tm),
        in_specs=[
            pl.BlockSpec((1, tm, MLA_Q_LORA), lambda bi, i: (bi, i, cqcol)),
            pl.BlockSpec((1, MLA_Q_LORA), const),
            pl.BlockSpec((MLA_HEADS * MLA_DQK, MLA_Q_LORA), const),
            pl.BlockSpec((MLA_NOPE, 128), const),
            pl.BlockSpec((MLA_ROPE, 128), const),
            pl.BlockSpec((MLA_ROPE // 2, tm), lambda bi, i: (0, i)),
            pl.BlockSpec((MLA_ROPE // 2, tm), lambda bi, i: (0, i)),
        ],
        out_specs=pl.BlockSpec((1, MLA_HEADS, MLA_DPAD, tm), lambda bi, i: (bi, 0, 0, i)),
        out_shape=jax.ShapeDtypeStruct((b, MLA_HEADS, MLA_DPAD, lp), BF16),
        compiler_params=_cparams(("parallel", "parallel")),
        name="mla_q",
    )(proj3, g1, w_uq_t, ga, gr, cos_t, sin_t)


def _mla_kv_kernel(ckv_ref, kpe_ref, g1_ref, w_ref, ga_ref, gr_ref, cos_ref, nsin_ref, sin_ref,
                   k_ref, vt_ref):
    ckv = ckv_ref[0]
    ms = jnp.mean(ckv * ckv, axis=-1, keepdims=True)
    cn = (ckv * lax.rsqrt(ms + NORM_EPS) * g1_ref[...]).astype(BF16)
    kv = jnp.dot(cn, w_ref[...], preferred_element_type=F32)
    kpe = kpe_ref[0]
    sq_pe = kpe * kpe
    kr = _rope(kpe * gr_ref[...], cos_ref[...], nsin_ref[...], sin_ref[...])
    for h in range(MLA_HEADS):
        kn = kv[:, h * MLA_NOPE:(h + 1) * MLA_NOPE]
        ss = jnp.sum(kn * kn + sq_pe, axis=-1, keepdims=True)
        inv = lax.rsqrt(ss * (1.0 / MLA_DQK) + NORM_EPS)
        k_ref[0, h, :, 0:MLA_NOPE] = (kn * inv * ga_ref[...]).astype(BF16)
        k_ref[0, h, :, MLA_NOPE:MLA_DPAD] = (kr * inv).astype(BF16)
        v = kv[:, MLA_HEADS * MLA_NOPE + h * MLA_DV:MLA_HEADS * MLA_NOPE + (h + 1) * MLA_DV]
        vt_ref[0, h] = v.T.astype(BF16)


def _mla_kv(proj3, g1, w_ukv, ga, gr, cos, nsin, sin, lp):
    b = proj3.shape[0]
    tm = _tile(lp, 384, 128)
    ckvcol = (MLA_V + MLA_Q_LORA) // MLA_KV_LORA
    kpecol = (MLA_V + MLA_Q_LORA + MLA_KV_LORA) // 128
    const = lambda bi, i: (0, 0)
    return pl.pallas_call(
        _mla_kv_kernel,
        grid=(b, lp // tm),
        in_specs=[
            pl.BlockSpec((1, tm, MLA_KV_LORA), lambda bi, i: (bi, i, ckvcol)),
            pl.BlockSpec((1, tm, 128), lambda bi, i: (bi, i, kpecol)),
            pl.BlockSpec((1, MLA_KV_LORA), const),
            pl.BlockSpec((MLA_KV_LORA, MLA_HEADS * (MLA_NOPE + MLA_DV)), const),
            pl.BlockSpec((1, 128), const),
            pl.BlockSpec((1, 128), const),
            pl.BlockSpec((tm, 128), lambda bi, i: (i, 0)),
            pl.BlockSpec((tm, 128), lambda bi, i: (i, 0)),
            pl.BlockSpec((tm, 128), lambda bi, i: (i, 0)),
        ],
        out_specs=[
            pl.BlockSpec((1, MLA_HEADS, tm, MLA_DPAD), lambda bi, i: (bi, 0, i, 0)),
            pl.BlockSpec((1, MLA_HEADS, MLA_DV, tm), lambda bi, i: (bi, 0, 0, i)),
        ],
        out_shape=[
            jax.ShapeDtypeStruct((b, MLA_HEADS, lp, MLA_DPAD), BF16),
            jax.ShapeDtypeStruct((b, MLA_HEADS, MLA_DV, lp), BF16),
        ],
        compiler_params=_cparams(("parallel", "parallel")),
        name="mla_kv",
    )(proj3, proj3, g1, w_ukv, ga, gr, cos, nsin, sin)


ATT_TK = 1024
ATT_TQ = 512
ATT_AHEAD = 2


def _attn_kernel(qt_ref, k_ref, vt_ref, o_ref, *, s_len):
    tq = qt_ref.shape[3]
    nstrip = tq // ATT_TQ
    chunks = [(ck * ATT_TK, ATT_TK) for ck in range(s_len // ATT_TK)] + [(s_len, TAIL)]
    units = [(i, s) for i in range(len(chunks)) for s in range(nstrip)]
    is_meta = lax.broadcasted_iota(jnp.int32, (TAIL, 1), 0) >= TAIL - N_META
    tail_bias = jnp.where(is_meta, 0.0, -jnp.inf).astype(F32)

    def scores(u):
        i, s = units[u]
        k0, nk = chunks[i]
        st = jnp.dot(k_ref[0, 0, k0:k0 + nk, :], qt_ref[0, 0, :, s * ATT_TQ:(s + 1) * ATT_TQ],
                     preferred_element_type=F32)
        return st + tail_bias if i == len(chunks) - 1 else st

    m = [jnp.full((1, ATT_TQ), -jnp.inf, F32) for _ in range(nstrip)]
    l = [jnp.zeros((1, ATT_TQ), F32) for _ in range(nstrip)]
    acc = [jnp.zeros((MLA_DV, ATT_TQ), F32) for _ in range(nstrip)]
    pending = [scores(u) for u in range(min(ATT_AHEAD, len(units)))]
    for u, (i, s) in enumerate(units):
        if u + ATT_AHEAD < len(units):
            pending.append(scores(u + ATT_AHEAD))
        st = pending.pop(0)
        k0, nk = chunks[i]
        m_new = jnp.maximum(m[s], jnp.max(st, axis=0, keepdims=True))
        alpha = jnp.exp2(m[s] - m_new)
        p = jnp.exp2(st - m_new)
        l[s] = alpha * l[s] + jnp.sum(p, axis=0, keepdims=True)
        acc[s] = alpha * acc[s] + jnp.dot(vt_ref[0, 0, :, k0:k0 + nk], p.astype(BF16),
                                          preferred_element_type=F32)
        m[s] = m_new
    for s in range(nstrip):
        o_ref[0, s * ATT_TQ:(s + 1) * ATT_TQ, :] = (acc[s] / l[s]).T.astype(o_ref.dtype)


def _attention(qt, k, vt, s_len):
    b = qt.shape[0]
    lp = k.shape[2]
    assert s_len % ATT_TK == 0 and lp == s_len + TAIL
    tq = _tile(s_len, 1024, ATT_TQ)
    kern = functools.partial(_attn_kernel, s_len=s_len)
    return pl.pallas_call(
        kern,
        grid=(b, MLA_HEADS, s_len // tq),
        in_specs=[
            pl.BlockSpec((1, 1, MLA_DPAD, tq), lambda bi, h, i: (bi, h, 0, i)),
            pl.BlockSpec((1, 1, lp, MLA_DPAD), lambda bi, h, i: (bi, h, 0, 0)),
            pl.BlockSpec((1, 1, MLA_DV, lp), lambda bi, h, i: (bi, h, 0, 0)),
        ],
        out_specs=pl.BlockSpec((1, tq, MLA_DV), lambda bi, h, i: (bi, i, h)),
        out_shape=jax.ShapeDtypeStruct((b, s_len, MLA_V), BF16),
        compiler_params=_cparams(("parallel", "parallel", "arbitrary")),
        name="mla_attention",
    )(qt, k, vt)


def _mla_out_kernel(o_ref, z_ref, h_ref, w_ref, y_ref):
    z = z_ref[0]
    y = (o_ref[0].astype(F32) * (z * jax.nn.sigmoid(z))).astype(BF16)
    y_ref[0] = h_ref[0] + jnp.dot(y, w_ref[...], preferred_element_type=F32)


def _mla_out(o, proj3, h3, w_out, b0, nb, s_len):
    tm = _tile(s_len, 512, 128)
    return pl.pallas_call(
        _mla_out_kernel,
        grid=(nb, s_len // tm),
        in_specs=[
            pl.BlockSpec((1, tm, MLA_V), lambda bi, i: (bi + b0, i, 0)),
            pl.BlockSpec((1, tm, MLA_V), lambda bi, i: (bi + b0, i, 0)),
            pl.BlockSpec((1, tm, D_MODEL), lambda bi, i: (bi + b0, i, 0)),
            pl.BlockSpec((MLA_V, D_MODEL), lambda bi, i: (0, 0)),
        ],
        out_specs=pl.BlockSpec((1, tm, D_MODEL), lambda bi, i: (bi, i, 0)),
        out_shape=jax.ShapeDtypeStruct((nb, s_len, D_MODEL), F32),
        compiler_params=_cparams(("parallel", "parallel")),
        name="mla_out",
    )(o, proj3, h3, w_out)


def _pad_cols(w, n):
    return jnp.pad(w, ((0, 0), (0, n - w.shape[1])))


def _lane_row(v, n=128):
    return jnp.pad(v.astype(F32), (0, n - v.shape[0]))[None, :]


def _trunk_all(xs, meta_tokens, ln_g, gdn_w_in, gdn_conv_w, gdn_a_log, gdn_dt_bias, gdn_o_norm_g,
               gdn_w_out, mla_w_in, mla_q_norm_g, mla_kv_norm_g, mla_w_uq, mla_w_ukv, mla_qk_q_g,
               mla_qk_k_g, mla_w_out):
    s_len = xs[0].shape[1]
    assert all(x.shape[1] == s_len for x in xs) and s_len % 128 == 0
    lp = s_len + TAIL
    x_all = jnp.concatenate(xs, axis=0)
    b = x_all.shape[0]
    meta = jnp.broadcast_to(meta_tokens[None].astype(F32), (b, N_META, D_MODEL))
    h0 = jnp.concatenate([x_all, jnp.zeros((b, TAIL - N_META, D_MODEL), F32), meta], axis=1)
    h0 = h0.reshape(b * lp, D_MODEL)

    w_in = gdn_w_in[0]
    ba = w_in[:, GDN_CONV_CH + GDN_V:].reshape(D_MODEL, 2, 2, GDN_HEADS)
    ba_dir = [_pad_cols(jnp.concatenate([ba[:, 0, d], ba[:, 1, d]], axis=1), 128) for d in range(2)]
    w0 = jnp.concatenate([w_in[:, :GDN_CONV_CH + GDN_V]] + ba_dir, axis=1).astype(BF16)
    proj0 = _norm_matmul(h0, ln_g[0][None, :], w0, 1280)

    conv_w8 = jnp.pad(gdn_conv_w[0], ((0, 8 - GDN_CONV), (0, 0)))
    proj0_3 = proj0.reshape(b, lp, proj0.shape[1])
    qkv = _gdn_conv(proj0_3, conv_w8, lp)

    lane_a = lambda v: jnp.pad(v.astype(F32), ((0, 0), (GDN_HEADS, 128 - 2 * GDN_HEADS)))[:, None, :]
    o_fwd, o_bwd = _gdn_scan(qkv, proj0_3, lane_a(gdn_a_log[0]), lane_a(gdn_dt_bias[0]), lp)
    h1 = _gdn_out(o_fwd.reshape(b * lp, GDN_V), o_bwd.reshape(b * lp, GDN_V), proj0, h0,
                  gdn_o_norm_g[0][None, :], gdn_w_out[0].astype(BF16))

    w_in1 = mla_w_in[0]
    o1 = MLA_Q_LORA
    o2_ = o1 + MLA_KV_LORA
    o3 = o2_ + MLA_ROPE
    w1 = jnp.concatenate([w_in1[:, o3:], w_in1[:, :o1], w_in1[:, o1:o2_],
                          _pad_cols(w_in1[:, o2_:o3], 256)], axis=1).astype(BF16)
    proj1 = _norm_matmul(h1, ln_g[1][None, :], w1, 1024)
    proj1_3 = proj1.reshape(b, lp, proj1.shape[1])

    pos = jnp.concatenate([jnp.arange(s_len, dtype=F32) + N_META, jnp.zeros((TAIL - N_META,), F32),
                           jnp.arange(N_META, dtype=F32)])
    inv = ROPE_THETA ** (-jnp.arange(0, MLA_ROPE, 2, dtype=F32) / MLA_ROPE)
    ang = pos[:, None] * inv[None, :]
    zc = jnp.zeros_like(ang)
    cos_t = jnp.concatenate([jnp.cos(ang), jnp.cos(ang), zc, zc], axis=1)
    nsin_t = jnp.concatenate([-jnp.sin(ang), zc, zc, zc], axis=1)
    sin_t = jnp.concatenate([zc, jnp.sin(ang), zc, zc], axis=1)

    scale = MLA_DQK ** -0.5 * math.log2(math.e)
    gq = jnp.broadcast_to((mla_qk_q_g[0].astype(F32) * scale)[:, None], (MLA_DQK, 128))
    qt = _mla_q(proj1_3, mla_q_norm_g[0][None, :], mla_w_uq[0].T.astype(BF16), gq[:MLA_NOPE],
                gq[MLA_NOPE:], jnp.cos(ang).T, jnp.sin(ang).T, lp)

    w_ukv = mla_w_ukv[0].reshape(MLA_KV_LORA, MLA_HEADS, MLA_NOPE + MLA_DV)
    w_ukv = jnp.concatenate([w_ukv[:, :, :MLA_NOPE].reshape(MLA_KV_LORA, -1),
                             w_ukv[:, :, MLA_NOPE:].reshape(MLA_KV_LORA, -1)], axis=1).astype(BF16)
    gk = mla_qk_k_g[0].astype(F32)
    k, vt = _mla_kv(proj1_3, mla_kv_norm_g[0][None, :], w_ukv, gk[None, :MLA_NOPE],
                    _lane_row(gk[MLA_NOPE:]), cos_t, nsin_t, sin_t, lp)

    o = _attention(qt, k, vt, s_len)

    h1_3 = h1.reshape(b, lp, D_MODEL)
    w_out1 = mla_w_out[0].astype(BF16)
    outs = []
    b0 = 0
    for x in xs:
        outs.append(_mla_out(o, proj1_3, h1_3, w_out1, b0, x.shape[0], s_len))
        b0 += x.shape[0]
    return tuple(outs)


def kernel(x_prompt, x_sample, meta_tokens, ln_g, gdn_w_in, gdn_conv_w, gdn_a_log, gdn_dt_bias,
           gdn_o_norm_g, gdn_w_out, mla_w_in, mla_q_norm_g, mla_kv_norm_g, mla_w_uq, mla_w_ukv,
           mla_qk_q_g, mla_qk_k_g, mla_w_out):
    return _trunk_all((x_prompt, x_sample), meta_tokens, ln_g, gdn_w_in, gdn_conv_w, gdn_a_log,
                      gdn_dt_bias, gdn_o_norm_g, gdn_w_out, mla_w_in, mla_q_norm_g, mla_kv_norm_g,
                      mla_w_uq, mla_w_ukv, mla_qk_q_g, mla_qk_k_g, mla_w_out)
```

```python
import functools
import math

import jax
import jax.numpy as jnp
from jax import lax
from jax.experimental import pallas as pl
from jax.experimental.pallas import tpu as pltpu

F32 = jnp.float32
BF16 = jnp.bfloat16

D_MODEL = 1024
N_META = 16
TAIL = 128
NORM_EPS = 1e-6

GDN_HEADS = 8
GDN_DK = 128
GDN_DV = 256
GDN_CONV = 5
GDN_CHUNK = 64
GDN_QK = GDN_HEADS * GDN_DK
GDN_V = GDN_HEADS * GDN_DV
GDN_CONV_CH = 2 * GDN_QK + GDN_V

MLA_HEADS = 16
MLA_Q_LORA = 512
MLA_KV_LORA = 256
MLA_NOPE = 128
MLA_ROPE = 64
MLA_DQK = MLA_NOPE + MLA_ROPE
MLA_DV = 128
MLA_V = MLA_HEADS * MLA_DV
MLA_DPAD = 256
ROPE_THETA = 10000.0

VMEM_LIMIT = 56 * 1024 * 1024


def _cparams(sem):
    return pltpu.CompilerParams(dimension_semantics=sem, vmem_limit_bytes=VMEM_LIMIT)


def _tile(n, target, mult):
    best = None
    t = mult
    while t <= min(n, target):
        if n % t == 0:
            best = t
        t += mult
    assert best is not None, (n, target, mult)
    return best


def _norm_matmul_kernel(x_ref, g_ref, w_ref, o_ref, xn_ref):
    @pl.when(pl.program_id(1) == 0)
    def _():
        x = x_ref[...]
        ms = jnp.mean(x * x, axis=-1, keepdims=True)
        xn_ref[...] = (x * lax.rsqrt(ms + NORM_EPS) * g_ref[...]).astype(BF16)

    o_ref[...] = jnp.dot(xn_ref[...], w_ref[...], preferred_element_type=F32)


def _norm_matmul(x, g, w, tn):
    rows, d = x.shape
    n = w.shape[1]
    tm = _tile(rows, 1408, 128)
    return pl.pallas_call(
        _norm_matmul_kernel,
        grid=(rows // tm, n // tn),
        in_specs=[
            pl.BlockSpec((tm, d), lambda i, j: (i, 0)),
            pl.BlockSpec((1, d), lambda i, j: (0, 0)),
            pl.BlockSpec((d, tn), lambda i, j: (0, j)),
        ],
        out_specs=pl.BlockSpec((tm, tn), lambda i, j: (i, j)),
        out_shape=jax.ShapeDtypeStruct((rows, n), F32),
        scratch_shapes=[pltpu.VMEM((tm, d), BF16)],
        compiler_params=_cparams(("parallel", "arbitrary")),
        name="norm_matmul",
    )(x, g, w)


CONV_SUB = 128
CONV_TC = 1024


def _conv_kernel(prev_ref, main_ref, next_ref, w_ref, o_ref, ext_ref, *, tr):
    j = pl.program_id(2)
    half = GDN_CONV // 2
    ext_ref[0:8, :] = prev_ref[0]
    ext_ref[8:8 + tr, :] = main_ref[0]
    ext_ref[8 + tr:16 + tr, :] = next_ref[0]
    qscale = jnp.where(j == 0, GDN_DK ** -0.5, 1.0).astype(F32)

    def conv_silu(sb):
        r0 = 8 + sb * CONV_SUB - half
        acc = ext_ref[r0:r0 + CONV_SUB, :] * w_ref[0:1, :]
        for t in range(1, GDN_CONV):
            acc = acc + ext_ref[r0 + t:r0 + t + CONV_SUB, :] * w_ref[t:t + 1, :]
        return acc * jax.nn.sigmoid(acc)

    @pl.when(j < 2)
    def _():
        for sb in range(tr // CONV_SUB):
            y = conv_silu(sb)
            rows = slice(sb * CONV_SUB, (sb + 1) * CONV_SUB)
            for h in range(CONV_TC // GDN_DK):
                lanes = slice(h * GDN_DK, (h + 1) * GDN_DK)
                yh = y[:, lanes]
                ss = jnp.sum(yh * yh, axis=-1, keepdims=True)
                o_ref[0, rows, lanes] = (yh * (lax.rsqrt(ss + NORM_EPS) * qscale)).astype(BF16)

    @pl.when(j >= 2)
    def _():
        for sb in range(tr // CONV_SUB):
            rows = slice(sb * CONV_SUB, (sb + 1) * CONV_SUB)
            o_ref[0, rows, :] = conv_silu(sb).astype(BF16)


def _gdn_conv(proj3, conv_w8, lp):
    b = proj3.shape[0]
    tr = _tile(lp, 384, CONV_SUB)
    nb8 = lp // 8
    tb = tr // 8
    kern = functools.partial(_conv_kernel, tr=tr)
    return pl.pallas_call(
        kern,
        grid=(b, lp // tr, GDN_CONV_CH // CONV_TC),
        in_specs=[
            pl.BlockSpec((1, 8, CONV_TC), lambda bi, i, j: (bi, (i * tb + nb8 - 1) % nb8, j)),
            pl.BlockSpec((1, tr, CONV_TC), lambda bi, i, j: (bi, i, j)),
            pl.BlockSpec((1, 8, CONV_TC), lambda bi, i, j: (bi, ((i + 1) * tb) % nb8, j)),
            pl.BlockSpec((8, CONV_TC), lambda bi, i, j: (0, j)),
        ],
        out_specs=pl.BlockSpec((1, tr, CONV_TC), lambda bi, i, j: (bi, i, j)),
        out_shape=jax.ShapeDtypeStruct((b, lp, GDN_CONV_CH), BF16),
        scratch_shapes=[pltpu.VMEM((tr + 16, CONV_TC), F32)],
        compiler_params=_cparams(("parallel", "parallel", "arbitrary")),
        name="gdn_conv",
    )(proj3, proj3, proj3, conv_w8)


SCAN_NB = 2


def _split3(x):
    hi = x.astype(BF16)
    r1 = x - hi.astype(F32)
    mid = r1.astype(BF16)
    lo = (r1 - mid.astype(F32)).astype(BF16)
    return hi, mid, lo


def _bdot(a, b):
    return jnp.dot(a.astype(BF16), b.astype(BF16), preferred_element_type=F32)


def _gdn_scan_kernel(qf_ref, kf_ref, vf_ref, baf_ref, qb_ref, kb_ref, vb_ref, bab_ref, alog_ref, dtb_ref,
                     of_ref, ob_ref, s_ref, sb_ref, u_ref, qw_ref, attn_ref, kd_ref, eg_ref, *, nc):
    t = pl.program_id(1)
    c = GDN_CHUNK
    c2 = 2 * c
    npair = GDN_HEADS // 2
    o_refs = (of_ref, ob_ref)

    @pl.when(t == 0)
    def _():
        for ref in (s_ref, sb_ref, u_ref, qw_ref, attn_ref, kd_ref, eg_ref):
            ref[...] = jnp.zeros_like(ref)

    nb = of_ref.shape[0]
    nprob = nb * 2 * npair
    applied = [dict(sq=pi // npair, heads=(2 * (pi % npair), 2 * (pi % npair) + 1),
                    qs=[None, None], vnew=[None, None]) for pi in range(nprob)]

    def read_state(pi, hh):
        ap = applied[pi]
        rs = slice(hh * c, (hh + 1) * c)
        qws = jnp.dot(qw_ref[pi, hh], sb_ref[ap["sq"], ap["heads"][hh]], preferred_element_type=F32)
        ap["qs"][hh] = qws[:c]
        ap["vnew"][hh] = (u_ref[pi, rs, :] - qws[c:]).astype(BF16)

    def write_out(pi):
        ap = applied[pi]
        h0, h1 = ap["heads"]
        bi, d = divmod(ap["sq"], 2)
        vnew2 = jnp.concatenate(ap["vnew"], axis=0)
        o2 = jnp.concatenate(ap["qs"], axis=0) + jnp.dot(attn_ref[pi], vnew2, preferred_element_type=F32)
        o_refs[d][bi, :, h0 * GDN_DV:(h0 + 1) * GDN_DV] = o2[:c].astype(BF16)
        o_refs[d][bi, :, h1 * GDN_DV:(h1 + 1) * GDN_DV] = o2[c:].astype(BF16)

    def update_state(pi, hh):
        ap = applied[pi]
        sq, h = ap["sq"], ap["heads"][hh]
        rs = slice(hh * c, (hh + 1) * c)
        a_h = GDN_HEADS + h
        upd = lax.dot_general(kd_ref[pi, rs, :], ap["vnew"][hh], (((0,), (0,)), ((), ())),
                              preferred_element_type=F32)
        s_new = s_ref[sq, h] * eg_ref[sq, :, a_h:a_h + 1] + upd
        s_ref[sq, h] = s_new
        sb_ref[sq, h] = s_new.astype(BF16)

    pairs_hh = [(pi, hh) for pi in range(nprob) for hh in range(2)]
    apply_ops = ([functools.partial(read_state, pi, hh) for pi, hh in pairs_hh]
                 + [functools.partial(write_out, pi) for pi in range(nprob)]
                 + [functools.partial(update_state, pi, hh) for pi, hh in pairs_hh])

    def emit_apply(n):
        for _ in range(min(n, len(apply_ops))):
            apply_ops.pop(0)()

    emit_apply(len(pairs_hh) // 2)
    tp = jnp.minimum(t, nc - 1)
    ri = lax.broadcasted_iota(jnp.int32, (c2, c2), 0)
    ci = lax.broadcasted_iota(jnp.int32, (c2, c2), 1)
    same = (ri >> 6) == (ci >> 6)
    offdiag = ri != ci
    top =lax.broadcasted_iota(jnp.int32, (c2, 1), 0) < c
    left = lax.broadcasted_iota(jnp.int32, (1, c2), 1) < c
    row_id = lax.broadcasted_iota(jnp.int32, (c, 128), 0)

    in_refs = ((qf_ref, kf_ref, vf_ref, baf_ref), (qb_ref, kb_ref, vb_ref, bab_ref))
    seqs = []
    for sq in range(2 * nb):
        bi, d = divmod(sq, 2)
        q_ref, k_ref, v_ref, ba_ref = in_refs[d]
        seq = tp if d == 0 else nc - 1 - tp
        blk = jnp.where(seq < 2, nc - 2 + seq, seq - 2)
        first_valid = jnp.where(blk == nc - 2, c, jnp.where(blk == nc - 1, c - N_META, 0))
        valid = row_id >= first_valid
        ba = ba_ref[bi]
        beta = jnp.where(valid, jax.nn.sigmoid(ba), 0.0)
        xs = ba + dtb_ref[d]
        softplus = jnp.maximum(xs, 0.0) + jnp.log(1.0 + jnp.exp(-jnp.abs(xs)))
        g = jnp.where(valid, -jnp.exp(alog_ref[d]) * softplus, 0.0)
        incl = same & ((ri >= ci) if d == 0 else (ri <= ci))
        tri = jnp.where(incl, 1.0, 0.0).astype(BF16)
        ghi, gmid, glo = _split3(jnp.concatenate([g, g], axis=0))
        gc2 = (jnp.dot(tri, ghi, preferred_element_type=F32)
               + jnp.dot(tri, gmid, preferred_element_type=F32)
               + jnp.dot(tri, glo, preferred_element_type=F32))
        seqs.append(dict(q=q_ref.at[bi], k=k_ref.at[bi], v=v_ref.at[bi], incl=incl, gc2=gc2, gc2t=gc2.T,
                         beta2=jnp.concatenate([beta, beta], axis=0),
                         gtot=jnp.sum(g, axis=0, keepdims=True)))

    probs = []
    for dd in seqs:
        for p in range(npair):
            h0, h1 = 2 * p, 2 * p + 1
            a0, a1 = GDN_HEADS + h0, GDN_HEADS + h1
            col = jnp.where(top, dd["gc2"][:, a0:a0 + 1], dd["gc2"][:, a1:a1 + 1])
            row = jnp.where(left, dd["gc2t"][a0:a0 + 1, :], dd["gc2t"][a1:a1 + 1, :])
            bcol = jnp.where(top, dd["beta2"][:, h0:h0 + 1], dd["beta2"][:, h1:h1 + 1])
            tot = jnp.where(top, dd["gtot"][:, a0:a0 + 1], dd["gtot"][:, a1:a1 + 1])
            dec = jnp.exp(jnp.where(dd["incl"], col - row, -jnp.inf))
            kst = jnp.concatenate([dd["k"][:, h0 * GDN_DK:(h0 + 1) * GDN_DK],
                                   dd["k"][:, h1 * GDN_DK:(h1 + 1) * GDN_DK]], axis=0)
            qst = jnp.concatenate([dd["q"][:, h0 * GDN_DK:(h0 + 1) * GDN_DK],
                                   dd["q"][:, h1 * GDN_DK:(h1 + 1) * GDN_DK]], axis=0)
            vst = jnp.concatenate([dd["v"][:, h0 * GDN_DV:(h0 + 1) * GDN_DV],
                                   dd["v"][:, h1 * GDN_DV:(h1 + 1) * GDN_DV]], axis=0)
            kf = kst.astype(F32)
            kb = kf * bcol
            sc = lax.dot_general(jnp.concatenate([qst, kb.astype(BF16)], axis=0), kst,
                                 (((1,), (1,)), ((), ())), preferred_element_type=F32)
            egc = jnp.exp(col)
            probs.append(dict(
                attn=(sc[:c2] * dec).astype(BF16),
                a=jnp.where(offdiag, sc[c2:] * dec, 0.0),
                rhs=jnp.concatenate([vst.astype(F32) * bcol, kb * egc], axis=1).astype(BF16),
                qg=(qst.astype(F32) * egc).astype(BF16),
                kd=(kf * jnp.exp(tot - col)).astype(BF16)))

    emit_apply(len(pairs_hh) // 2)

    for _ in _unit_lower_inverse_staged(probs):
        emit_apply(3 * nb)
    emit_apply(len(apply_ops))

    for pi, pr in enumerate(probs):
        uw = jnp.dot(pr["tinv"].astype(BF16), pr["rhs"], preferred_element_type=F32)
        u_ref[pi] = uw[:, :GDN_DV]
        w = uw[:, GDN_DV:].astype(BF16)
        for hh in range(2):
            rs = slice(hh * c, (hh + 1) * c)
            qw_ref[pi, hh] = jnp.concatenate([pr["qg"][rs], w[rs]], axis=0)
        attn_ref[pi] = pr["attn"]
        kd_ref[pi] = pr["kd"]
    for sq, dd in enumerate(seqs):
        eg_ref[sq] = jnp.exp(dd["gtot"])


def _unit_lower_inverse_staged(probs):
    c = GDN_CHUNK
    nside = 4
    ri = lax.broadcasted_iota(jnp.int32, (c, nside * c), 0)
    ci = lax.broadcasted_iota(jnp.int32, (c, nside * c), 1)
    lane_blk = ci >> 6
    within = ci & (c - 1)
    diag16 = (ri >> 4) == (within >> 4)
    eye = jnp.where(ri == within, 1.0, 0.0).astype(F32)
    left = lax.broadcasted_iota(jnp.int32, (1, 2 * c), 1) < c

    def blockdiag(y):
        return jnp.concatenate([jnp.where(lane_blk == r, y, 0.0) for r in range(nside)],
                               axis=0).astype(BF16)

    def mm(x, ybd):
        return jnp.dot(x.astype(BF16), ybd, preferred_element_type=F32)

    groups = []
    for g0 in range(0, len(probs), 2):
        pa, pb = probs[g0], probs[g0 + 1]
        a = jnp.concatenate([pa["a"][:c] + pa["a"][c:], pb["a"][:c] + pb["a"][c:]], axis=1)
        ad = jnp.where(diag16, a, 0.0)
        groups.append(dict(pairs=(pa, pb), ad=ad, an=blockdiag(a - ad), dinv=eye - ad))
    for gr in groups:
        gr["p"] = mm(gr["ad"], blockdiag(gr["ad"]))
    yield
    for gr in groups:
        pbd = blockdiag(gr["p"])
        gr["dinv"] = gr["dinv"] + mm(gr["dinv"], pbd)
        gr["p"] = mm(gr["p"], pbd)
    yield
    for gr in groups:
        pbd = blockdiag(gr["p"])
        gr["dinv"] = gr["dinv"] + mm(gr["dinv"], pbd)
        gr["p"] = mm(gr["p"], pbd)
    yield
    for gr in groups:
        gr["dinv"] = gr["dinv"] + mm(gr["dinv"], blockdiag(gr["p"]))
    yield
    for gr in groups:
        gr["m"] = mm(gr["dinv"], gr["an"])
    yield
    for gr in groups:
        gr["m2"] = blockdiag(mm(gr["m"], blockdiag(gr["m"])))
    yield
    for gr in groups:
        x = eye - gr["m"]
        gr["x"] = x + mm(x, gr["m2"])
    yield
    for gr in groups:
        t = mm(gr["x"], blockdiag(gr["dinv"]))
        for i, pr in enumerate(gr["pairs"]):
            half = t[:, 2 * c * i:2 * c * (i + 1)]
            pr["tinv"] = jnp.concatenate([jnp.where(left, half, 0.0), jnp.where(left, 0.0, half)],
                                         axis=0)


def _gdn_scan(qkv, proj3, alog_rows, dtb_rows, lp):
    b = qkv.shape[0]
    nc = lp // GDN_CHUNK
    ba_col0 = (GDN_CONV_CH + GDN_V) // 128

    def blk_of(seq):
        return jnp.where(seq < 2, nc - 2 + seq, seq - 2)

    def chunk_specs(d):
        prep = lambda t: jnp.minimum(t, nc - 1)
        seq = prep if d == 0 else (lambda t: nc - 1 - prep(t))
        return [
            pl.BlockSpec((nb, GDN_CHUNK, GDN_QK), lambda bi, t: (bi, blk_of(seq(t)), 0)),
            pl.BlockSpec((nb, GDN_CHUNK, GDN_QK), lambda bi, t: (bi, blk_of(seq(t)), 1)),
            pl.BlockSpec((nb, GDN_CHUNK, GDN_V), lambda bi, t: (bi, blk_of(seq(t)), 1)),
            pl.BlockSpec((nb, GDN_CHUNK, 128), lambda bi, t: (bi, blk_of(seq(t)), ba_col0 + d)),
        ]

    nb = SCAN_NB if b % SCAN_NB == 0 else 1
    const = pl.BlockSpec((2, 1, 128), lambda bi, t: (0, 0, 0))
    kern = functools.partial(_gdn_scan_kernel, nc=nc)
    applied = lambda t: jnp.maximum(t - 1, 0)
    nseq = 2 * nb
    nprob = nseq * (GDN_HEADS // 2)
    c2 = 2 * GDN_CHUNK
    return pl.pallas_call(
        kern,
        grid=(b // nb, nc + 1),
        in_specs=chunk_specs(0) + chunk_specs(1) + [const, const],
        out_specs=[
            pl.BlockSpec((nb, GDN_CHUNK, GDN_V), lambda bi, t: (bi, blk_of(applied(t)), 0)),
            pl.BlockSpec((nb, GDN_CHUNK, GDN_V), lambda bi, t: (bi, blk_of(nc - 1 - applied(t)), 0)),
        ],
        out_shape=[jax.ShapeDtypeStruct((b, lp, GDN_V), BF16)] * 2,
        scratch_shapes=[
            pltpu.VMEM((nseq, GDN_HEADS, GDN_DK, GDN_DV), F32),
            pltpu.VMEM((nseq, GDN_HEADS, GDN_DK, GDN_DV), BF16),
            pltpu.VMEM((nprob, c2, GDN_DV), F32),
            pltpu.VMEM((nprob, 2, c2, GDN_DK), BF16),
            pltpu.VMEM((nprob, c2, c2), BF16),
            pltpu.VMEM((nprob, c2, GDN_DK), BF16),
            pltpu.VMEM((nseq, 1, 128), F32),
        ],
        compiler_params=_cparams(("parallel", "arbitrary")),
        name="gdn_scan",
    )(qkv, qkv, qkv, proj3, qkv, qkv, qkv, proj3, alog_rows, dtb_rows)


def _gdn_out_kernel(of_ref, ob_ref, z_ref, h_ref, g_ref, w_ref, o_ref):
    o = of_ref[...].astype(F32) + ob_ref[...].astype(F32)
    z = z_ref[...]
    gate = z * jax.nn.sigmoid(z)
    ys = []
    for h in range(GDN_HEADS):
        lanes = slice(h * GDN_DV, (h + 1) * GDN_DV)
        oh = o[:, lanes]
        ms = jnp.mean(oh * oh, axis=-1, keepdims=True)
        ys.append((oh * lax.rsqrt(ms + NORM_EPS) * g_ref[...] * gate[:, lanes]).astype(BF16))
    y = jnp.concatenate(ys, axis=1)
    o_ref[...] = h_ref[...] + jnp.dot(y, w_ref[...], preferred_element_type=F32)


def _gdn_out(o_fwd, o_bwd, proj, h0, g_row, w_out):
    rows = h0.shape[0]
    tm = _tile(rows, 512, 128)
    zcol = GDN_CONV_CH // GDN_V
    return pl.pallas_call(
        _gdn_out_kernel,
        grid=(rows // tm,),
        in_specs=[
            pl.BlockSpec((tm, GDN_V), lambda i: (i, 0)),
            pl.BlockSpec((tm, GDN_V), lambda i: (i, 0)),
            pl.BlockSpec((tm, GDN_V), lambda i: (i, zcol)),
            pl.BlockSpec((tm, D_MODEL), lambda i: (i, 0)),
            pl.BlockSpec((1, GDN_DV), lambda i: (0, 0)),
            pl.BlockSpec((GDN_V, D_MODEL), lambda i: (0, 0)),
        ],
        out_specs=pl.BlockSpec((tm, D_MODEL), lambda i: (i, 0)),
        out_shape=jax.ShapeDtypeStruct((rows, D_MODEL), F32),
        compiler_params=_cparams(("parallel",)),
        name="gdn_out",
    )(o_fwd, o_bwd, proj, h0, g_row, w_out)


def _rope(r, cos, nsin_lo, sin_hi):
    return r * cos + pltpu.roll(r, 96, 1) * nsin_lo + pltpu.roll(r, 32, 1) * sin_hi


def _mla_q_kernel(cq_ref, g1_ref, wt_ref, ga_ref, gr_ref, cos_ref, sin_ref, shift_ref, qt_ref):
    cq = cq_ref[0]
    tm = cq.shape[0]
    ms = jnp.mean(cq * cq, axis=-1, keepdims=True)
    cqt = (cq * lax.rsqrt(ms + NORM_EPS) * g1_ref[...]).T.astype(BF16)
    q = jnp.dot(wt_ref[...], cqt, preferred_element_type=F32)
    half = MLA_ROPE // 2
    tile = lambda g: jnp.concatenate([g] * (tm // 128), axis=1)
    ga = tile(ga_ref[...])
    gr1, gr2 = tile(gr_ref[0:half, :]), tile(gr_ref[half:MLA_ROPE, :])
    cos, sin = cos_ref[...], sin_ref[...]
    pad_rows = tile(shift_ref[...]).astype(BF16)
    for h in range(MLA_HEADS):
        r0 = h * MLA_DQK
        a = q[r0:r0 + MLA_NOPE]
        x1 = q[r0 + MLA_NOPE:r0 + MLA_NOPE + half]
        x2 = q[r0 + MLA_NOPE + half:r0 + MLA_DQK]
        ss = (jnp.sum(a * a, axis=0, keepdims=True) + jnp.sum(x1 * x1, axis=0, keepdims=True)
              + jnp.sum(x2 * x2, axis=0, keepdims=True))
        inv = lax.rsqrt(ss * (1.0 / MLA_DQK) + NORM_EPS)
        x1 = x1 * inv * gr1
        x2 = x2 * inv * gr2
        qt_ref[0, h, 0:MLA_NOPE, :] = (a * inv * ga).astype(BF16)
        qt_ref[0, h, MLA_NOPE:MLA_NOPE + half, :] = (x1 * cos - x2 * sin).astype(BF16)
        qt_ref[0, h, MLA_NOPE + half:MLA_DQK, :] = (x2 * cos + x1 * sin).astype(BF16)
        qt_ref[0, h, MLA_DQK:MLA_DPAD, :] = pad_rows


def _mla_q(proj3, g1, w_uq_t, ga, gr, cos_t, sin_t, shift_rows, lp):
    b = proj3.shape[0]
    tm = _tile(lp, 384, 128)
    cqcol = MLA_V // MLA_Q_LORA
    const = lambda bi, i: (0, 0)
    return pl.pallas_call(
        _mla_q_kernel,
        grid=(b, lp // tm),
        in_specs=[
            pl.BlockSpec((1, tm, MLA_Q_LORA), lambda bi, i: (bi, i, cqcol)),
            pl.BlockSpec((1, MLA_Q_LORA), const),
            pl.BlockSpec((MLA_HEADS * MLA_DQK, MLA_Q_LORA), const),
            pl.BlockSpec((MLA_NOPE, 128), const),
            pl.BlockSpec((MLA_ROPE, 128), const),
            pl.BlockSpec((MLA_ROPE // 2, tm), lambda bi, i: (0, i)),
            pl.BlockSpec((MLA_ROPE // 2, tm), lambda bi, i: (0, i)),
            pl.BlockSpec((MLA_DPAD - MLA_DQK, 128), const),
        ],
        out_specs=pl.BlockSpec((1, MLA_HEADS, MLA_DPAD, tm), lambda bi, i: (bi, 0, 0, i)),
        out_shape=jax.ShapeDtypeStruct((b, MLA_HEADS, MLA_DPAD, lp), BF16),
        compiler_params=_cparams(("parallel", "parallel")),
        name="mla_q",
    )(proj3, g1, w_uq_t, ga, gr, cos_t, sin_t, shift_rows)


def _mla_kv_kernel(ckv_ref, kpe_ref, g1_ref, w_ref, ga_ref, gr_ref, cos_ref, nsin_ref, sin_ref,
                   k_ref, vt_ref):
    ckv = ckv_ref[0]
    ms = jnp.mean(ckv * ckv, axis=-1, keepdims=True)
    cn = (ckv * lax.rsqrt(ms + NORM_EPS) * g1_ref[...]).astype(BF16)
    kv = jnp.dot(cn, w_ref[...], preferred_element_type=F32)
    kpe = kpe_ref[0]
    sq_pe = kpe * kpe
    kr = _rope(kpe * gr_ref[...], cos_ref[...], nsin_ref[...], sin_ref[...])
    one_hot = jnp.where(lax.broadcasted_iota(jnp.int32, (1, 128), 1) == MLA_ROPE, 1.0, 0.0).astype(F32)
    for h in range(MLA_HEADS):
        kn = kv[:, h * MLA_NOPE:(h + 1) * MLA_NOPE]
        ss = jnp.sum(kn * kn + sq_pe, axis=-1, keepdims=True)
        inv = lax.rsqrt(ss * (1.0 / MLA_DQK) + NORM_EPS)
        k_ref[0, h, :, 0:MLA_NOPE] = (kn * inv * ga_ref[...]).astype(BF16)
        k_ref[0, h, :, MLA_NOPE:MLA_DPAD] = (kr * inv + one_hot).astype(BF16)
        v = kv[:, MLA_HEADS * MLA_NOPE + h * MLA_DV:MLA_HEADS * MLA_NOPE + (h + 1) * MLA_DV]
        vt_ref[0, h] = v.T.astype(BF16)


def _mla_kv(proj3, g1, w_ukv, ga, gr, cos, nsin, sin, lp):
    b = proj3.shape[0]
    tm = _tile(lp, 384, 128)
    ckvcol = (MLA_V + MLA_Q_LORA) // MLA_KV_LORA
    kpecol = (MLA_V + MLA_Q_LORA + MLA_KV_LORA) // 128
    const = lambda bi, i: (0, 0)
    return pl.pallas_call(
        _mla_kv_kernel,
        grid=(b, lp // tm),
        in_specs=[
            pl.BlockSpec((1, tm, MLA_KV_LORA), lambda bi, i: (bi, i, ckvcol)),
            pl.BlockSpec((1, tm, 128), lambda bi, i: (bi, i, kpecol)),
            pl.BlockSpec((1, MLA_KV_LORA), const),
            pl.BlockSpec((MLA_KV_LORA, MLA_HEADS * (MLA_NOPE + MLA_DV)), const),
            pl.BlockSpec((1, 128), const),
            pl.BlockSpec((1, 128), const),
            pl.BlockSpec((tm, 128), lambda bi, i: (i, 0)),
            pl.BlockSpec((tm, 128), lambda bi, i: (i, 0)),
            pl.BlockSpec((tm, 128), lambda bi, i: (i, 0)),
        ],
        out_specs=[
            pl.BlockSpec((1, MLA_HEADS, tm, MLA_DPAD), lambda bi, i: (bi, 0, i, 0)),
            pl.BlockSpec((1, MLA_HEADS, MLA_DV, tm), lambda bi, i: (bi, 0, 0, i)),
        ],
        out_shape=[
            jax.ShapeDtypeStruct((b, MLA_HEADS, lp, MLA_DPAD), BF16),
            jax.ShapeDtypeStruct((b, MLA_HEADS, MLA_DV, lp), BF16),
        ],
        compiler_params=_cparams(("parallel", "parallel")),
        name="mla_kv",
    )(proj3, proj3, g1, w_ukv, ga, gr, cos, nsin, sin)


ATT_TK = 1024
ATT_TQ = 512
ATT_AHEAD = 2
ATT_BOUND_MAX = 60.0


def _attn_kernel(bounded_ref, qt_ref, k_ref, vt_ref, o_ref, *, s_len):
    tq = qt_ref.shape[3]
    nstrip = tq // ATT_TQ
    chunks = [(ck * ATT_TK, ATT_TK) for ck in range(s_len // ATT_TK)] + [(s_len, TAIL)]
    units = [(i, s) for i in range(len(chunks)) for s in range(nstrip)]
    is_meta = lax.broadcasted_iota(jnp.int32, (TAIL, 1), 0) >= TAIL - N_META
    tail_bias = jnp.where(is_meta, 0.0, -jnp.inf).astype(F32)

    def scores(u):
        i, s = units[u]
        k0, nk = chunks[i]
        st = jnp.dot(k_ref[0, 0, k0:k0 + nk, :], qt_ref[0, 0, :, s * ATT_TQ:(s + 1) * ATT_TQ],
                     preferred_element_type=F32)
        return st + tail_bias if i == len(chunks) - 1 else st

    def run(bounded):
        m = [jnp.full((1, ATT_TQ), -jnp.inf, F32) for _ in range(nstrip)]
        l = [jnp.zeros((1, ATT_TQ), F32) for _ in range(nstrip)]
        acc = [jnp.zeros((MLA_DV, ATT_TQ), F32) for _ in range(nstrip)]
        pending = [scores(u) for u in range(min(ATT_AHEAD, len(units)))]
        for u, (i, s) in enumerate(units):
            if u + ATT_AHEAD < len(units):
                pending.append(scores(u + ATT_AHEAD))
            st = pending.pop(0)
            k0, nk = chunks[i]
            if bounded:
                p = jnp.exp2(st)
                l[s] = l[s] + jnp.sum(p, axis=0, keepdims=True)
                acc[s] = acc[s] + jnp.dot(vt_ref[0, 0, :, k0:k0 + nk], p.astype(BF16),
                                          preferred_element_type=F32)
            else:
                m_new = jnp.maximum(m[s], jnp.max(st, axis=0, keepdims=True))
                alpha = jnp.exp2(m[s] - m_new)
                p = jnp.exp2(st - m_new)
                l[s] = alpha * l[s] + jnp.sum(p, axis=0, keepdims=True)
                acc[s] = alpha * acc[s] + jnp.dot(vt_ref[0, 0, :, k0:k0 + nk], p.astype(BF16),
                                                  preferred_element_type=F32)
                m[s] = m_new
        for s in range(nstrip):
            o_ref[0, s * ATT_TQ:(s + 1) * ATT_TQ, :] = (acc[s] / l[s]).T.astype(o_ref.dtype)

    @pl.when(bounded_ref[0] == 1)
    def _():
        run(True)

    @pl.when(bounded_ref[0] != 1)
    def _():
        run(False)


def _attention(bounded, qt, k, vt, s_len):
    b = qt.shape[0]
    lp = k.shape[2]
    assert s_len % ATT_TK == 0 and lp == s_len + TAIL
    tq = _tile(s_len, 1024, ATT_TQ)
    kern = functools.partial(_attn_kernel, s_len=s_len)
    return pl.pallas_call(
        kern,
        grid=(b, MLA_HEADS, s_len // tq),
        in_specs=[
            pl.BlockSpec(memory_space=pltpu.SMEM),
            pl.BlockSpec((1, 1, MLA_DPAD, tq), lambda bi, h, i: (bi, h, 0, i)),
            pl.BlockSpec((1, 1, lp, MLA_DPAD), lambda bi, h, i: (bi, h, 0, 0)),
            pl.BlockSpec((1, 1, MLA_DV, lp), lambda bi, h, i: (bi, h, 0, 0)),
        ],
        out_specs=pl.BlockSpec((1, tq, MLA_DV), lambda bi, h, i: (bi, i, h)),
        out_shape=jax.ShapeDtypeStruct((b, s_len, MLA_V), BF16),
        compiler_params=_cparams(("parallel", "parallel", "arbitrary")),
        name="mla_attention",
    )(bounded, qt, k, vt)


def _mla_out_kernel(o_ref, z_ref, h_ref, w_ref, y_ref):
    z = z_ref[0]
    y = (o_ref[0].astype(F32) * (z * jax.nn.sigmoid(z))).astype(BF16)
    y_ref[0] = h_ref[0] + jnp.dot(y, w_ref[...], preferred_element_type=F32)


def _mla_out(o, proj3, h3, w_out, b0, nb, s_len):
    tm = _tile(s_len, 512, 128)
    return pl.pallas_call(
        _mla_out_kernel,
        grid=(nb, s_len // tm),
        in_specs=[
            pl.BlockSpec((1, tm, MLA_V), lambda bi, i: (bi + b0, i, 0)),
            pl.BlockSpec((1, tm, MLA_V), lambda bi, i: (bi + b0, i, 0)),
            pl.BlockSpec((1, tm, D_MODEL), lambda bi, i: (bi + b0, i, 0)),
            pl.BlockSpec((MLA_V, D_MODEL), lambda bi, i: (0, 0)),
        ],
        out_specs=pl.BlockSpec((1, tm, D_MODEL), lambda bi, i: (bi, i, 0)),
        out_shape=jax.ShapeDtypeStruct((nb, s_len, D_MODEL), F32),
        compiler_params=_cparams(("parallel", "parallel")),
        name="mla_out",
    )(o, proj3, h3, w_out)


def _pad_cols(w, n):
    return jnp.pad(w, ((0, 0), (0, n - w.shape[1])))


def _lane_row(v, n=128):
    return jnp.pad(v.astype(F32), (0, n - v.shape[0]))[None, :]


def _trunk_all(xs, meta_tokens, ln_g, gdn_w_in, gdn_conv_w, gdn_a_log, gdn_dt_bias, gdn_o_norm_g,
               gdn_w_out, mla_w_in, mla_q_norm_g, mla_kv_norm_g, mla_w_uq, mla_w_ukv, mla_qk_q_g,
               mla_qk_k_g, mla_w_out):
    s_len = xs[0].shape[1]
    assert all(x.shape[1] == s_len for x in xs) and s_len % 128 == 0
    lp = s_len + TAIL
    x_all = jnp.concatenate(xs, axis=0)
    b = x_all.shape[0]
    meta = jnp.broadcast_to(meta_tokens[None].astype(F32), (b, N_META, D_MODEL))
    h0 = jnp.concatenate([x_all, jnp.zeros((b, TAIL - N_META, D_MODEL), F32), meta], axis=1)
    h0 = h0.reshape(b * lp, D_MODEL)

    w_in = gdn_w_in[0]
    ba = w_in[:, GDN_CONV_CH + GDN_V:].reshape(D_MODEL, 2, 2, GDN_HEADS)
    ba_dir = [_pad_cols(jnp.concatenate([ba[:, 0, d], ba[:, 1, d]], axis=1), 128) for d in range(2)]
    w0 = jnp.concatenate([w_in[:, :GDN_CONV_CH + GDN_V]] + ba_dir, axis=1).astype(BF16)
    proj0 = _norm_matmul(h0, ln_g[0][None, :], w0, 1280)

    conv_w8 = jnp.pad(gdn_conv_w[0], ((0, 8 - GDN_CONV), (0, 0)))
    proj0_3 = proj0.reshape(b, lp, proj0.shape[1])
    qkv = _gdn_conv(proj0_3, conv_w8, lp)

    lane_a = lambda v: jnp.pad(v.astype(F32), ((0, 0), (GDN_HEADS, 128 - 2 * GDN_HEADS)))[:, None, :]
    o_fwd, o_bwd = _gdn_scan(qkv, proj0_3, lane_a(gdn_a_log[0]), lane_a(gdn_dt_bias[0]), lp)
    h1 = _gdn_out(o_fwd.reshape(b * lp, GDN_V), o_bwd.reshape(b * lp, GDN_V), proj0, h0,
                  gdn_o_norm_g[0][None, :], gdn_w_out[0].astype(BF16))

    w_in1 = mla_w_in[0]
    o1 = MLA_Q_LORA
    o2_ = o1 + MLA_KV_LORA
    o3 = o2_ + MLA_ROPE
    w1 = jnp.concatenate([w_in1[:, o3:], w_in1[:, :o1], w_in1[:, o1:o2_],
                          _pad_cols(w_in1[:, o2_:o3], 256)], axis=1).astype(BF16)
    proj1 = _norm_matmul(h1, ln_g[1][None, :], w1, 1024)
    proj1_3 = proj1.reshape(b, lp, proj1.shape[1])

    pos = jnp.concatenate([jnp.arange(s_len, dtype=F32) + N_META, jnp.zeros((TAIL - N_META,), F32),
                           jnp.arange(N_META, dtype=F32)])
    inv = ROPE_THETA ** (-jnp.arange(0, MLA_ROPE, 2, dtype=F32) / MLA_ROPE)
    ang = pos[:, None] * inv[None, :]
    zc = jnp.zeros_like(ang)
    cos_t = jnp.concatenate([jnp.cos(ang), jnp.cos(ang), zc, zc], axis=1)
    nsin_t = jnp.concatenate([-jnp.sin(ang), zc, zc, zc], axis=1)
    sin_t = jnp.concatenate([zc, jnp.sin(ang), zc, zc], axis=1)

    scale = MLA_DQK ** -0.5 * math.log2(math.e)
    gq = jnp.broadcast_to((mla_qk_q_g[0].astype(F32) * scale)[:, None], (MLA_DQK, 128))
    score_bound = (1.02 * MLA_DQK * scale * jnp.max(jnp.abs(mla_qk_q_g[0].astype(F32)))
                   * jnp.max(jnp.abs(mla_qk_k_g[0].astype(F32))))
    bounded = score_bound < ATT_BOUND_MAX
    shift_rows = jnp.zeros((MLA_DPAD - MLA_DQK, 128), F32).at[0, :].set(jnp.where(bounded, -score_bound, 0.0))
    qt = _mla_q(proj1_3, mla_q_norm_g[0][None, :], mla_w_uq[0].T.astype(BF16), gq[:MLA_NOPE],
                gq[MLA_NOPE:], jnp.cos(ang).T, jnp.sin(ang).T, shift_rows, lp)

    w_ukv = mla_w_ukv[0].reshape(MLA_KV_LORA, MLA_HEADS, MLA_NOPE + MLA_DV)
    w_ukv = jnp.concatenate([w_ukv[:, :, :MLA_NOPE].reshape(MLA_KV_LORA, -1),
                             w_ukv[:, :, MLA_NOPE:].reshape(MLA_KV_LORA, -1)], axis=1).astype(BF16)
    gk = mla_qk_k_g[0].astype(F32)
    k, vt = _mla_kv(proj1_3, mla_kv_norm_g[0][None, :], w_ukv, gk[None, :MLA_NOPE],
                    _lane_row(gk[MLA_NOPE:]), cos_t, nsin_t, sin_t, lp)

    o = _attention(bounded.astype(jnp.int32).reshape(1), qt, k, vt, s_len)

    h1_3 = h1.reshape(b, lp, D_MODEL)
    w_out1 = mla_w_out[0].astype(BF16)
    outs = []
    b0 = 0
    for x in xs:
        outs.append(_mla_out(o, proj1_3, h1_3, w_out1, b0, x.shape[0], s_len))
        b0 += x.shape[0]
    return tuple(outs)


def kernel(x_prompt, x_sample, meta_tokens, ln_g, gdn_w_in, gdn_conv_w, gdn_a_log, gdn_dt_bias,
           gdn_o_norm_g, gdn_w_out, mla_w_in, mla_q_norm_g, mla_kv_norm_g, mla_w_uq, mla_w_ukv,
           mla_qk_q_g, mla_qk_k_g, mla_w_out):
    return _trunk_all((x_prompt, x_sample), meta_tokens, ln_g, gdn_w_in, gdn_conv_w, gdn_a_log,
                      gdn_dt_bias, gdn_o_norm_g, gdn_w_out, mla_w_in, mla_q_norm_g, mla_kv_norm_g,
                      mla_w_uq, mla_w_ukv, mla_qk_q_g, mla_qk_k_g, mla_w_out)
```

```python
import functools
import math

import jax
import jax.numpy as jnp
from jax import lax
from jax.experimental import pallas as pl
from jax.experimental.pallas import tpu as pltpu

F32 = jnp.float32
BF16 = jnp.bfloat16

D_MODEL = 1024
N_META = 16
TAIL = 128
NORM_EPS = 1e-6

GDN_HEADS = 8
GDN_DK = 128
GDN_DV = 256
GDN_CONV = 5
GDN_CHUNK = 64
GDN_QK = GDN_HEADS * GDN_DK
GDN_V = GDN_HEADS * GDN_DV
GDN_CONV_CH = 2 * GDN_QK + GDN_V

MLA_HEADS = 16
MLA_Q_LORA = 512
MLA_KV_LORA = 256
MLA_NOPE = 128
MLA_ROPE = 64
MLA_DQK = MLA_NOPE + MLA_ROPE
MLA_DV = 128
MLA_V = MLA_HEADS * MLA_DV
MLA_DPAD = 256
ROPE_THETA = 10000.0

VMEM_LIMIT = 56 * 1024 * 1024


def _cparams(sem):
    return pltpu.CompilerParams(dimension_semantics=sem, vmem_limit_bytes=VMEM_LIMIT)


def _tile(n, target, mult):
    best = None
    t = mult
    while t <= min(n, target):
        if n % t == 0:
            best = t
        t += mult
    assert best is not None, (n, target, mult)
    return best


def _norm_matmul_kernel(x_ref, g_ref, w_ref, o_ref, xn_ref):
    @pl.when(pl.program_id(1) == 0)
    def _():
        x = x_ref[...]
        ms = jnp.mean(x * x, axis=-1, keepdims=True)
        xn_ref[...] = (x * lax.rsqrt(ms + NORM_EPS) * g_ref[...]).astype(BF16)

    o_ref[...] = jnp.dot(xn_ref[...], w_ref[...], preferred_element_type=F32)


def _norm_matmul(x, g, w, tn):
    rows, d = x.shape
    n = w.shape[1]
    tm = _tile(rows, 1408, 128)
    return pl.pallas_call(
        _norm_matmul_kernel,
        grid=(rows // tm, n // tn),
        in_specs=[
            pl.BlockSpec((tm, d), lambda i, j: (i, 0)),
            pl.BlockSpec((1, d), lambda i, j: (0, 0)),
            pl.BlockSpec((d, tn), lambda i, j: (0, j)),
        ],
        out_specs=pl.BlockSpec((tm, tn), lambda i, j: (i, j)),
        out_shape=jax.ShapeDtypeStruct((rows, n), F32),
        scratch_shapes=[pltpu.VMEM((tm, d), BF16)],
        compiler_params=_cparams(("parallel", "arbitrary")),
        name="norm_matmul",
    )(x, g, w)


CONV_SUB = 128
CONV_TC = 1024


def _qkv_conv_kernel(prev_ref, main_ref, next_ref, g_ref, w_ref, cw_ref, o_ref, ext_ref, *, tr):
    half = GDN_CONV // 2
    nblk = GDN_CONV_CH // CONV_TC
    x = jnp.concatenate([prev_ref[0], main_ref[0], next_ref[0]], axis=0)
    ms = jnp.mean(x * x, axis=-1, keepdims=True)
    xn = (x * lax.rsqrt(ms + NORM_EPS) * g_ref[...]).astype(BF16)

    def project(j):
        ext_ref[j] = jnp.dot(xn, w_ref[:, j * CONV_TC:(j + 1) * CONV_TC], preferred_element_type=F32)

    ng = CONV_SUB // 8
    sub = lax.broadcasted_iota(jnp.int32, (1, 8, CONV_TC), 1)

    def conv_silu(j, sb):
        lanes = slice(j * CONV_TC, (j + 1) * CONV_TC)
        x3 = ext_ref[j, sb * CONV_SUB:(sb + 1) * CONV_SUB + 16, :].reshape(ng + 2, 8, CONV_TC)
        acc = x3[1:ng + 1] * cw_ref[half:half + 1, lanes]
        for s in range(1, half + 1):
            down = pltpu.roll(x3, s, 1)
            acc = acc + jnp.where(sub >= s, down[1:ng + 1], down[0:ng]) * cw_ref[half - s:half - s + 1, lanes]
            up = pltpu.roll(x3, 8 - s, 1)
            acc = acc + jnp.where(sub < 8 - s, up[1:ng + 1], up[2:ng + 2]) * cw_ref[half + s:half + s + 1, lanes]
        acc = acc.reshape(CONV_SUB, CONV_TC)
        return acc * jax.nn.sigmoid(acc)

    def finish(j):
        for sb in range(tr // CONV_SUB):
            y = conv_silu(j, sb)
            rows = slice(sb * CONV_SUB, (sb + 1) * CONV_SUB)
            if j >= 2:
                o_ref[0, rows, j * CONV_TC:(j + 1) * CONV_TC] = y.astype(BF16)
                continue
            qscale = GDN_DK ** -0.5 if j == 0 else 1.0
            for h in range(CONV_TC // GDN_DK):
                yh = y[:, h * GDN_DK:(h + 1) * GDN_DK]
                ss = jnp.sum(yh * yh, axis=-1, keepdims=True)
                lanes = slice(j * CONV_TC + h * GDN_DK, j * CONV_TC + (h + 1) * GDN_DK)
                o_ref[0, rows, lanes] = (yh * (lax.rsqrt(ss + NORM_EPS) * qscale)).astype(BF16)

    project(0)
    for j in range(nblk):
        if j + 1 < nblk:
            project(j + 1)
        finish(j)


def _gdn_qkv_conv(h3, g, w_qkv, conv_w8, lp):
    b = h3.shape[0]
    tr = _tile(lp, 384, CONV_SUB)
    nb8 = lp // 8
    tb = tr // 8
    kern = functools.partial(_qkv_conv_kernel, tr=tr)
    const = lambda bi, i: (0, 0)
    return pl.pallas_call(
        kern,
        grid=(b, lp // tr),
        in_specs=[
            pl.BlockSpec((1, 8, D_MODEL), lambda bi, i: (bi, (i * tb + nb8 - 1) % nb8, 0)),
            pl.BlockSpec((1, tr, D_MODEL), lambda bi, i: (bi, i, 0)),
            pl.BlockSpec((1, 8, D_MODEL), lambda bi, i: (bi, ((i + 1) * tb) % nb8, 0)),
            pl.BlockSpec((1, D_MODEL), const),
            pl.BlockSpec((D_MODEL, GDN_CONV_CH), const),
            pl.BlockSpec((8, GDN_CONV_CH), const),
        ],
        out_specs=pl.BlockSpec((1, tr, GDN_CONV_CH), lambda bi, i: (bi, i, 0)),
        out_shape=jax.ShapeDtypeStruct((b, lp, GDN_CONV_CH), BF16),
        scratch_shapes=[pltpu.VMEM((GDN_CONV_CH // CONV_TC, tr + 16, CONV_TC), F32)],
        compiler_params=_cparams(("parallel", "parallel")),
        name="gdn_qkv_conv",
    )(h3, h3, h3, g, w_qkv, conv_w8)


SCAN_NB = 2


def _split3(x):
    hi = x.astype(BF16)
    r1 = x - hi.astype(F32)
    mid = r1.astype(BF16)
    lo = (r1 - mid.astype(F32)).astype(BF16)
    return hi, mid, lo


def _bdot(a, b):
    return jnp.dot(a.astype(BF16), b.astype(BF16), preferred_element_type=F32)


def _gdn_scan_kernel(qf_ref, kf_ref, vf_ref, baf_ref, qb_ref, kb_ref, vb_ref, bab_ref, alog_ref, dtb_ref,
                     of_ref, ob_ref, s_ref, sb_ref, u_ref, qw_ref, attn_ref, kd_ref, eg_ref, *, nc):
    t = pl.program_id(1)
    c = GDN_CHUNK
    c2 = 2 * c
    npair = GDN_HEADS // 2
    o_refs = (of_ref, ob_ref)

    @pl.when(t == 0)
    def _():
        for ref in (s_ref, sb_ref, u_ref, qw_ref, attn_ref, kd_ref, eg_ref):
            ref[...] = jnp.zeros_like(ref)

    nb = of_ref.shape[0]
    nprob = nb * 2 * npair
    applied = [dict(sq=pi // npair, heads=(2 * (pi % npair), 2 * (pi % npair) + 1),
                    qs=[None, None], vnew=[None, None]) for pi in range(nprob)]

    def read_state(pi, hh):
        ap = applied[pi]
        rs = slice(hh * c, (hh + 1) * c)
        qws = jnp.dot(qw_ref[pi, hh], sb_ref[ap["sq"], ap["heads"][hh]], preferred_element_type=F32)
        ap["qs"][hh] = qws[:c]
        ap["vnew"][hh] = (u_ref[pi, rs, :] - qws[c:]).astype(BF16)

    def write_out(pi):
        ap = applied[pi]
        h0, h1 = ap["heads"]
        bi, d = divmod(ap["sq"], 2)
        vnew2 = jnp.concatenate(ap["vnew"], axis=0)
        o2 = jnp.concatenate(ap["qs"], axis=0) + jnp.dot(attn_ref[pi], vnew2, preferred_element_type=F32)
        o_refs[d][bi, :, h0 * GDN_DV:(h0 + 1) * GDN_DV] = o2[:c].astype(BF16)
        o_refs[d][bi, :, h1 * GDN_DV:(h1 + 1) * GDN_DV] = o2[c:].astype(BF16)

    def update_state(pi, hh):
        ap = applied[pi]
        sq, h = ap["sq"], ap["heads"][hh]
        rs = slice(hh * c, (hh + 1) * c)
        a_h = GDN_HEADS + h
        upd = lax.dot_general(kd_ref[pi, rs, :], ap["vnew"][hh], (((0,), (0,)), ((), ())),
                              preferred_element_type=F32)
        s_new = s_ref[sq, h] * eg_ref[sq, :, a_h:a_h + 1] + upd
        s_ref[sq, h] = s_new
        sb_ref[sq, h] = s_new.astype(BF16)

    pairs_hh = [(pi, hh) for pi in range(nprob) for hh in range(2)]
    apply_ops = ([functools.partial(read_state, pi, hh) for pi, hh in pairs_hh]
                 + [functools.partial(write_out, pi) for pi in range(nprob)]
                 + [functools.partial(update_state, pi, hh) for pi, hh in pairs_hh])

    def emit_apply(n):
        for _ in range(min(n, len(apply_ops))):
            apply_ops.pop(0)()

    emit_apply(len(pairs_hh) // 2)
    tp = jnp.minimum(t, nc - 1)
    ri = lax.broadcasted_iota(jnp.int32, (c2, c2), 0)
    ci = lax.broadcasted_iota(jnp.int32, (c2, c2), 1)
    same = (ri >> 6) == (ci >> 6)
    offdiag = ri != ci
    top =lax.broadcasted_iota(jnp.int32, (c2, 1), 0) < c
    left = lax.broadcasted_iota(jnp.int32, (1, c2), 1) < c
    row_id = lax.broadcasted_iota(jnp.int32, (c, 128), 0)

    in_refs = ((qf_ref, kf_ref, vf_ref, baf_ref), (qb_ref, kb_ref, vb_ref, bab_ref))
    seqs = []
    for sq in range(2 * nb):
        bi, d = divmod(sq, 2)
        q_ref, k_ref, v_ref, ba_ref = in_refs[d]
        seq = tp if d == 0 else nc - 1 - tp
        blk = jnp.where(seq < 2, nc - 2 + seq, seq - 2)
        first_valid = jnp.where(blk == nc - 2, c, jnp.where(blk == nc - 1, c - N_META, 0))
        valid = row_id >= first_valid
        ba = ba_ref[bi]
        beta = jnp.where(valid, jax.nn.sigmoid(ba), 0.0)
        xs = ba + dtb_ref[d]
        softplus = jnp.maximum(xs, 0.0) + jnp.log(1.0 + jnp.exp(-jnp.abs(xs)))
        g = jnp.where(valid, -jnp.exp(alog_ref[d]) * softplus, 0.0)
        incl = same & ((ri >= ci) if d == 0 else (ri <= ci))
        tri = jnp.where(incl, 1.0, 0.0).astype(BF16)
        ghi, gmid, glo = _split3(jnp.concatenate([g, g], axis=0))
        gc2 = (jnp.dot(tri, ghi, preferred_element_type=F32)
               + jnp.dot(tri, gmid, preferred_element_type=F32)
               + jnp.dot(tri, glo, preferred_element_type=F32))
        seqs.append(dict(q=q_ref.at[bi], k=k_ref.at[bi], v=v_ref.at[bi], incl=incl, gc2=gc2, gc2t=gc2.T,
                         beta2=jnp.concatenate([beta, beta], axis=0),
                         gtot=jnp.sum(g, axis=0, keepdims=True)))

    probs = []
    for dd in seqs:
        for p in range(npair):
            h0, h1 = 2 * p, 2 * p + 1
            a0, a1 = GDN_HEADS + h0, GDN_HEADS + h1
            col = jnp.where(top, dd["gc2"][:, a0:a0 + 1], dd["gc2"][:, a1:a1 + 1])
            row = jnp.where(left, dd["gc2t"][a0:a0 + 1, :], dd["gc2t"][a1:a1 + 1, :])
            bcol = jnp.where(top, dd["beta2"][:, h0:h0 + 1], dd["beta2"][:, h1:h1 + 1])
            tot = jnp.where(top, dd["gtot"][:, a0:a0 + 1], dd["gtot"][:, a1:a1 + 1])
            dec = jnp.exp(jnp.where(dd["incl"], col - row, -jnp.inf))
            kst = jnp.concatenate([dd["k"][:, h0 * GDN_DK:(h0 + 1) * GDN_DK],
                                   dd["k"][:, h1 * GDN_DK:(h1 + 1) * GDN_DK]], axis=0)
            qst = jnp.concatenate([dd["q"][:, h0 * GDN_DK:(h0 + 1) * GDN_DK],
                                   dd["q"][:, h1 * GDN_DK:(h1 + 1) * GDN_DK]], axis=0)
            vst = jnp.concatenate([dd["v"][:, h0 * GDN_DV:(h0 + 1) * GDN_DV],
                                   dd["v"][:, h1 * GDN_DV:(h1 + 1) * GDN_DV]], axis=0)
            kf = kst.astype(F32)
            kb = kf * bcol
            sc = lax.dot_general(jnp.concatenate([qst, kb.astype(BF16)], axis=0), kst,
                                 (((1,), (1,)), ((), ())), preferred_element_type=F32)
            egc = jnp.exp(col)
            probs.append(dict(
                attn=(sc[:c2] * dec).astype(BF16),
                a=jnp.where(offdiag, sc[c2:] * dec, 0.0),
                rhs=jnp.concatenate([vst.astype(F32) * bcol, kb * egc], axis=1).astype(BF16),
                qg=(qst.astype(F32) * egc).astype(BF16),
                kd=(kf * jnp.exp(tot - col)).astype(BF16)))

    emit_apply(len(pairs_hh) // 2)

    for _ in _unit_lower_inverse_staged(probs):
        emit_apply(3 * nb)
    emit_apply(len(apply_ops))

    for pi, pr in enumerate(probs):
        uw = jnp.dot(pr["tinv"].astype(BF16), pr["rhs"], preferred_element_type=F32)
        u_ref[pi] = uw[:, :GDN_DV]
        w = uw[:, GDN_DV:].astype(BF16)
        for hh in range(2):
            rs = slice(hh * c, (hh + 1) * c)
            qw_ref[pi, hh] = jnp.concatenate([pr["qg"][rs], w[rs]], axis=0)
        attn_ref[pi] = pr["attn"]
        kd_ref[pi] = pr["kd"]
    for sq, dd in enumerate(seqs):
        eg_ref[sq] = jnp.exp(dd["gtot"])


def _unit_lower_inverse_staged(probs):
    c = GDN_CHUNK
    nside = 4
    ri = lax.broadcasted_iota(jnp.int32, (c, nside * c), 0)
    ci = lax.broadcasted_iota(jnp.int32, (c, nside * c), 1)
    lane_blk = ci >> 6
    within = ci & (c - 1)
    diag16 = (ri >> 4) == (within >> 4)
    eye = jnp.where(ri == within, 1.0, 0.0).astype(F32)
    left = lax.broadcasted_iota(jnp.int32, (1, 2 * c), 1) < c

    def blockdiag(y):
        return jnp.concatenate([jnp.where(lane_blk == r, y, 0.0) for r in range(nside)],
                               axis=0).astype(BF16)

    def mm(x, ybd):
        return jnp.dot(x.astype(BF16), ybd, preferred_element_type=F32)

    groups = []
    for g0 in range(0, len(probs), 2):
        pa, pb = probs[g0], probs[g0 + 1]
        a = jnp.concatenate([pa["a"][:c] + pa["a"][c:], pb["a"][:c] + pb["a"][c:]], axis=1)
        ad = jnp.where(diag16, a, 0.0)
        groups.append(dict(pairs=(pa, pb), ad=ad, an=blockdiag(a - ad), dinv=eye - ad))
    for gr in groups:
        gr["p"] = mm(gr["ad"], blockdiag(gr["ad"]))
    yield
    for gr in groups:
        pbd = blockdiag(gr["p"])
        gr["dinv"] = gr["dinv"] + mm(gr["dinv"], pbd)
        gr["p"] = mm(gr["p"], pbd)
    yield
    for gr in groups:
        pbd = blockdiag(gr["p"])
        gr["dinv"] = gr["dinv"] + mm(gr["dinv"], pbd)
        gr["p"] = mm(gr["p"], pbd)
    yield
    for gr in groups:
        gr["dinv"] = gr["dinv"] + mm(gr["dinv"], blockdiag(gr["p"]))
    yield
    for gr in groups:
        gr["m"] = mm(gr["dinv"], gr["an"])
    yield
    for gr in groups:
        gr["m2"] = blockdiag(mm(gr["m"], blockdiag(gr["m"])))
    yield
    for gr in groups:
        x = eye - gr["m"]
        gr["x"] = x + mm(x, gr["m2"])
    yield
    for gr in groups:
        t = mm(gr["x"], blockdiag(gr["dinv"]))
        for i, pr in enumerate(gr["pairs"]):
            half = t[:, 2 * c * i:2 * c * (i + 1)]
            pr["tinv"] = jnp.concatenate([jnp.where(left, half, 0.0), jnp.where(left, 0.0, half)],
                                         axis=0)


def _gdn_scan(qkv, proj3, alog_rows, dtb_rows, lp):
    b = qkv.shape[0]
    nc = lp // GDN_CHUNK
    ba_col0 = GDN_V // 128

    def blk_of(seq):
        return jnp.where(seq < 2, nc - 2 + seq, seq - 2)

    def chunk_specs(d):
        prep = lambda t: jnp.minimum(t, nc - 1)
        seq = prep if d == 0 else (lambda t: nc - 1 - prep(t))
        return [
            pl.BlockSpec((nb, GDN_CHUNK, GDN_QK), lambda bi, t: (bi, blk_of(seq(t)), 0)),
            pl.BlockSpec((nb, GDN_CHUNK, GDN_QK), lambda bi, t: (bi, blk_of(seq(t)), 1)),
            pl.BlockSpec((nb, GDN_CHUNK, GDN_V), lambda bi, t: (bi, blk_of(seq(t)), 1)),
            pl.BlockSpec((nb, GDN_CHUNK, 128), lambda bi, t: (bi, blk_of(seq(t)), ba_col0 + d)),
        ]

    nb = SCAN_NB if b % SCAN_NB == 0 else 1
    const = pl.BlockSpec((2, 1, 128), lambda bi, t: (0, 0, 0))
    kern = functools.partial(_gdn_scan_kernel, nc=nc)
    applied = lambda t: jnp.maximum(t - 1, 0)
    nseq = 2 * nb
    nprob = nseq * (GDN_HEADS // 2)
    c2 = 2 * GDN_CHUNK
    return pl.pallas_call(
        kern,
        grid=(b // nb, nc + 1),
        in_specs=chunk_specs(0) + chunk_specs(1) + [const, const],
        out_specs=[
            pl.BlockSpec((nb, GDN_CHUNK, GDN_V), lambda bi, t: (bi, blk_of(applied(t)), 0)),
            pl.BlockSpec((nb, GDN_CHUNK, GDN_V), lambda bi, t: (bi, blk_of(nc - 1 - applied(t)), 0)),
        ],
        out_shape=[jax.ShapeDtypeStruct((b, lp, GDN_V), BF16)] * 2,
        scratch_shapes=[
            pltpu.VMEM((nseq, GDN_HEADS, GDN_DK, GDN_DV), F32),
            pltpu.VMEM((nseq, GDN_HEADS, GDN_DK, GDN_DV), BF16),
            pltpu.VMEM((nprob, c2, GDN_DV), F32),
            pltpu.VMEM((nprob, 2, c2, GDN_DK), BF16),
            pltpu.VMEM((nprob, c2, c2), BF16),
            pltpu.VMEM((nprob, c2, GDN_DK), BF16),
            pltpu.VMEM((nseq, 1, 128), F32),
        ],
        compiler_params=_cparams(("parallel", "arbitrary")),
        name="gdn_scan",
    )(qkv, qkv, qkv, proj3, qkv, qkv, qkv, proj3, alog_rows, dtb_rows)


def _gdn_out_kernel(of_ref, ob_ref, z_ref, h_ref, g_ref, w_ref, o_ref):
    o = of_ref[...].astype(F32) + ob_ref[...].astype(F32)
    z = z_ref[...]
    gate = z * jax.nn.sigmoid(z)
    ys = []
    for h in range(GDN_HEADS):
        lanes = slice(h * GDN_DV, (h + 1) * GDN_DV)
        oh = o[:, lanes]
        ms = jnp.mean(oh * oh, axis=-1, keepdims=True)
        ys.append((oh * lax.rsqrt(ms + NORM_EPS) * g_ref[...] * gate[:, lanes]).astype(BF16))
    y = jnp.concatenate(ys, axis=1)
    o_ref[...] = h_ref[...] + jnp.dot(y, w_ref[...], preferred_element_type=F32)


def _gdn_out(o_fwd, o_bwd, proj, h0, g_row, w_out):
    rows = h0.shape[0]
    tm = _tile(rows, 512, 128)
    zcol = 0
    return pl.pallas_call(
        _gdn_out_kernel,
        grid=(rows // tm,),
        in_specs=[
            pl.BlockSpec((tm, GDN_V), lambda i: (i, 0)),
            pl.BlockSpec((tm, GDN_V), lambda i: (i, 0)),
            pl.BlockSpec((tm, GDN_V), lambda i: (i, zcol)),
            pl.BlockSpec((tm, D_MODEL), lambda i: (i, 0)),
            pl.BlockSpec((1, GDN_DV), lambda i: (0, 0)),
            pl.BlockSpec((GDN_V, D_MODEL), lambda i: (0, 0)),
        ],
        out_specs=pl.BlockSpec((tm, D_MODEL), lambda i: (i, 0)),
        out_shape=jax.ShapeDtypeStruct((rows, D_MODEL), F32),
        compiler_params=_cparams(("parallel",)),
        name="gdn_out",
    )(o_fwd, o_bwd, proj, h0, g_row, w_out)


def _rope(r, cos, nsin_lo, sin_hi):
    return r * cos + pltpu.roll(r, 96, 1) * nsin_lo + pltpu.roll(r, 32, 1) * sin_hi


def _mla_q_kernel(cq_ref, g1_ref, wt_ref, ga_ref, gr_ref, cos_ref, sin_ref, shift_ref, qt_ref):
    cq = cq_ref[0]
    tm = cq.shape[0]
    ms = jnp.mean(cq * cq, axis=-1, keepdims=True)
    cqt = (cq * lax.rsqrt(ms + NORM_EPS) * g1_ref[...]).T.astype(BF16)
    q = jnp.dot(wt_ref[...], cqt, preferred_element_type=F32)
    half = MLA_ROPE // 2
    tile = lambda g: jnp.concatenate([g] * (tm // 128), axis=1)
    ga = tile(ga_ref[...])
    gr1, gr2 = tile(gr_ref[0:half, :]), tile(gr_ref[half:MLA_ROPE, :])
    cos, sin = cos_ref[...], sin_ref[...]
    pad_rows = tile(shift_ref[...]).astype(BF16)
    for h in range(MLA_HEADS):
        r0 = h * MLA_DQK
        a = q[r0:r0 + MLA_NOPE]
        x1 = q[r0 + MLA_NOPE:r0 + MLA_NOPE + half]
        x2 = q[r0 + MLA_NOPE + half:r0 + MLA_DQK]
        ss = (jnp.sum(a * a, axis=0, keepdims=True) + jnp.sum(x1 * x1, axis=0, keepdims=True)
              + jnp.sum(x2 * x2, axis=0, keepdims=True))
        inv = lax.rsqrt(ss * (1.0 / MLA_DQK) + NORM_EPS)
        x1 = x1 * inv * gr1
        x2 = x2 * inv * gr2
        qt_ref[0, h, 0:MLA_NOPE, :] = (a * inv * ga).astype(BF16)
        qt_ref[0, h, MLA_NOPE:MLA_NOPE + half, :] = (x1 * cos - x2 * sin).astype(BF16)
        qt_ref[0, h, MLA_NOPE + half:MLA_DQK, :] = (x2 * cos + x1 * sin).astype(BF16)
        qt_ref[0, h, MLA_DQK:MLA_DPAD, :] = pad_rows


def _mla_q(proj3, g1, w_uq_t, ga, gr, cos_t, sin_t, shift_rows, lp):
    b = proj3.shape[0]
    tm = _tile(lp, 384, 128)
    cqcol = MLA_V // MLA_Q_LORA
    const = lambda bi, i: (0, 0)
    return pl.pallas_call(
        _mla_q_kernel,
        grid=(b, lp // tm),
        in_specs=[
            pl.BlockSpec((1, tm, MLA_Q_LORA), lambda bi, i: (bi, i, cqcol)),
            pl.BlockSpec((1, MLA_Q_LORA), const),
            pl.BlockSpec((MLA_HEADS * MLA_DQK, MLA_Q_LORA), const),
            pl.BlockSpec((MLA_NOPE, 128), const),
            pl.BlockSpec((MLA_ROPE, 128), const),
            pl.BlockSpec((MLA_ROPE // 2, tm), lambda bi, i: (0, i)),
            pl.BlockSpec((MLA_ROPE // 2, tm), lambda bi, i: (0, i)),
            pl.BlockSpec((MLA_DPAD - MLA_DQK, 128), const),
        ],
        out_specs=pl.BlockSpec((1, MLA_HEADS, MLA_DPAD, tm), lambda bi, i: (bi, 0, 0, i)),
        out_shape=jax.ShapeDtypeStruct((b, MLA_HEADS, MLA_DPAD, lp), BF16),
        compiler_params=_cparams(("parallel", "parallel")),
        name="mla_q",
    )(proj3, g1, w_uq_t, ga, gr, cos_t, sin_t, shift_rows)


def _mla_kv_kernel(ckv_ref, kpe_ref, g1_ref, w_ref, ga_ref, gr_ref, cos_ref, nsin_ref, sin_ref,
                   k_ref, vt_ref):
    ckv = ckv_ref[0]
    ms = jnp.mean(ckv * ckv, axis=-1, keepdims=True)
    cn = (ckv * lax.rsqrt(ms + NORM_EPS) * g1_ref[...]).astype(BF16)
    kv = jnp.dot(cn, w_ref[...], preferred_element_type=F32)
    kpe = kpe_ref[0]
    sq_pe = kpe * kpe
    kr = _rope(kpe * gr_ref[...], cos_ref[...], nsin_ref[...], sin_ref[...])
    one_hot = jnp.where(lax.broadcasted_iota(jnp.int32, (1, 128), 1) == MLA_ROPE, 1.0, 0.0).astype(F32)
    for h in range(MLA_HEADS):
        kn = kv[:, h * MLA_NOPE:(h + 1) * MLA_NOPE]
        ss = jnp.sum(kn * kn + sq_pe, axis=-1, keepdims=True)
        inv = lax.rsqrt(ss * (1.0 / MLA_DQK) + NORM_EPS)
        k_ref[0, h, :, 0:MLA_NOPE] = (kn * inv * ga_ref[...]).astype(BF16)
        k_ref[0, h, :, MLA_NOPE:MLA_DPAD] = (kr * inv + one_hot).astype(BF16)
        v = kv[:, MLA_HEADS * MLA_NOPE + h * MLA_DV:MLA_HEADS * MLA_NOPE + (h + 1) * MLA_DV]
        vt_ref[0, h] = v.T.astype(BF16)


def _mla_kv(proj3, g1, w_ukv, ga, gr, cos, nsin, sin, lp):
    b = proj3.shape[0]
    tm = _tile(lp, 384, 128)
    ckvcol = (MLA_V + MLA_Q_LORA) // MLA_KV_LORA
    kpecol = (MLA_V + MLA_Q_LORA + MLA_KV_LORA) // 128
    const = lambda bi, i: (0, 0)
    return pl.pallas_call(
        _mla_kv_kernel,
        grid=(b, lp // tm),
        in_specs=[
            pl.BlockSpec((1, tm, MLA_KV_LORA), lambda bi, i: (bi, i, ckvcol)),
            pl.BlockSpec((1, tm, 128), lambda bi, i: (bi, i, kpecol)),
            pl.BlockSpec((1, MLA_KV_LORA), const),
            pl.BlockSpec((MLA_KV_LORA, MLA_HEADS * (MLA_NOPE + MLA_DV)), const),
            pl.BlockSpec((1, 128), const),
            pl.BlockSpec((1, 128), const),
            pl.BlockSpec((tm, 128), lambda bi, i: (i, 0)),
            pl.BlockSpec((tm, 128), lambda bi, i: (i, 0)),
            pl.BlockSpec((tm, 128), lambda bi, i: (i, 0)),
        ],
        out_specs=[
            pl.BlockSpec((1, MLA_HEADS, tm, MLA_DPAD), lambda bi, i: (bi, 0, i, 0)),
            pl.BlockSpec((1, MLA_HEADS, MLA_DV, tm), lambda bi, i: (bi, 0, 0, i)),
        ],
        out_shape=[
            jax.ShapeDtypeStruct((b, MLA_HEADS, lp, MLA_DPAD), BF16),
            jax.ShapeDtypeStruct((b, MLA_HEADS, MLA_DV, lp), BF16),
        ],
        compiler_params=_cparams(("parallel", "parallel")),
        name="mla_kv",
    )(proj3, proj3, g1, w_ukv, ga, gr, cos, nsin, sin)


ATT_TK = 1024
ATT_TQ = 512
ATT_AHEAD = 2
ATT_BOUND_MAX = 60.0


def _attn_kernel(bounded_ref, qt_ref, k_ref, vt_ref, o_ref, *, s_len):
    tq = qt_ref.shape[3]
    nstrip = tq // ATT_TQ
    chunks = [(ck * ATT_TK, ATT_TK) for ck in range(s_len // ATT_TK)] + [(s_len, TAIL)]
    units = [(i, s) for i in range(len(chunks)) for s in range(nstrip)]
    is_meta = lax.broadcasted_iota(jnp.int32, (TAIL, 1), 0) >= TAIL - N_META
    tail_bias = jnp.where(is_meta, 0.0, -jnp.inf).astype(F32)

    def scores(u):
        i, s = units[u]
        k0, nk = chunks[i]
        st = jnp.dot(k_ref[0, 0, k0:k0 + nk, :], qt_ref[0, 0, :, s * ATT_TQ:(s + 1) * ATT_TQ],
                     preferred_element_type=F32)
        return st + tail_bias if i == len(chunks) - 1 else st

    def run(bounded):
        m = [jnp.full((1, ATT_TQ), -jnp.inf, F32) for _ in range(nstrip)]
        l = [jnp.zeros((1, ATT_TQ), F32) for _ in range(nstrip)]
        acc = [jnp.zeros((MLA_DV, ATT_TQ), F32) for _ in range(nstrip)]
        pending = [scores(u) for u in range(min(ATT_AHEAD, len(units)))]
        for u, (i, s) in enumerate(units):
            if u + ATT_AHEAD < len(units):
                pending.append(scores(u + ATT_AHEAD))
            st = pending.pop(0)
            k0, nk = chunks[i]
            if bounded:
                p = jnp.exp2(st)
                l[s] = l[s] + jnp.sum(p, axis=0, keepdims=True)
                acc[s] = acc[s] + jnp.dot(vt_ref[0, 0, :, k0:k0 + nk], p.astype(BF16),
                                          preferred_element_type=F32)
            else:
                m_new = jnp.maximum(m[s], jnp.max(st, axis=0, keepdims=True))
                alpha = jnp.exp2(m[s] - m_new)
                p = jnp.exp2(st - m_new)
                l[s] = alpha * l[s] + jnp.sum(p, axis=0, keepdims=True)
                acc[s] = alpha * acc[s] + jnp.dot(vt_ref[0, 0, :, k0:k0 + nk], p.astype(BF16),
                                                  preferred_element_type=F32)
                m[s] = m_new
        for s in range(nstrip):
            o_ref[0, s * ATT_TQ:(s + 1) * ATT_TQ, :] = (acc[s] / l[s]).T.astype(o_ref.dtype)

    @pl.when(bounded_ref[0] == 1)
    def _():
        run(True)

    @pl.when(bounded_ref[0] != 1)
    def _():
        run(False)


def _attention(bounded, qt, k, vt, s_len):
    b = qt.shape[0]
    lp = k.shape[2]
    assert s_len % ATT_TK == 0 and lp == s_len + TAIL
    tq = _tile(s_len, 1024, ATT_TQ)
    kern = functools.partial(_attn_kernel, s_len=s_len)
    return pl.pallas_call(
        kern,
        grid=(b, MLA_HEADS, s_len // tq),
        in_specs=[
            pl.BlockSpec(memory_space=pltpu.SMEM),
            pl.BlockSpec((1, 1, MLA_DPAD, tq), lambda bi, h, i: (bi, h, 0, i)),
            pl.BlockSpec((1, 1, lp, MLA_DPAD), lambda bi, h, i: (bi, h, 0, 0)),
            pl.BlockSpec((1, 1, MLA_DV, lp), lambda bi, h, i: (bi, h, 0, 0)),
        ],
        out_specs=pl.BlockSpec((1, tq, MLA_DV), lambda bi, h, i: (bi, i, h)),
        out_shape=jax.ShapeDtypeStruct((b, s_len, MLA_V), BF16),
        compiler_params=_cparams(("parallel", "parallel", "arbitrary")),
        name="mla_attention",
    )(bounded, qt, k, vt)


def _mla_out_kernel(o_ref, z_ref, h_ref, w_ref, y_ref):
    z = z_ref[0]
    y = (o_ref[0].astype(F32) * (z * jax.nn.sigmoid(z))).astype(BF16)
    y_ref[0] = h_ref[0] + jnp.dot(y, w_ref[...], preferred_element_type=F32)


def _mla_out(o, proj3, h3, w_out, b0, nb, s_len):
    tm = _tile(s_len, 512, 128)
    return pl.pallas_call(
        _mla_out_kernel,
        grid=(nb, s_len // tm),
        in_specs=[
            pl.BlockSpec((1, tm, MLA_V), lambda bi, i: (bi + b0, i, 0)),
            pl.BlockSpec((1, tm, MLA_V), lambda bi, i: (bi + b0, i, 0)),
            pl.BlockSpec((1, tm, D_MODEL), lambda bi, i: (bi + b0, i, 0)),
            pl.BlockSpec((MLA_V, D_MODEL), lambda bi, i: (0, 0)),
        ],
        out_specs=pl.BlockSpec((1, tm, D_MODEL), lambda bi, i: (bi, i, 0)),
        out_shape=jax.ShapeDtypeStruct((nb, s_len, D_MODEL), F32),
        compiler_params=_cparams(("parallel", "parallel")),
        name="mla_out",
    )(o, proj3, h3, w_out)


def _pad_cols(w, n):
    return jnp.pad(w, ((0, 0), (0, n - w.shape[1])))


def _lane_row(v, n=128):
    return jnp.pad(v.astype(F32), (0, n - v.shape[0]))[None, :]


def _trunk_all(xs, meta_tokens, ln_g, gdn_w_in, gdn_conv_w, gdn_a_log, gdn_dt_bias, gdn_o_norm_g,
               gdn_w_out, mla_w_in, mla_q_norm_g, mla_kv_norm_g, mla_w_uq, mla_w_ukv, mla_qk_q_g,
               mla_qk_k_g, mla_w_out):
    s_len = xs[0].shape[1]
    assert all(x.shape[1] == s_len for x in xs) and s_len % 128 == 0
    lp = s_len + TAIL
    x_all = jnp.concatenate(xs, axis=0)
    b = x_all.shape[0]
    meta = jnp.broadcast_to(meta_tokens[None].astype(F32), (b, N_META, D_MODEL))
    h0 = jnp.concatenate([x_all, jnp.zeros((b, TAIL - N_META, D_MODEL), F32), meta], axis=1)
    h0 = h0.reshape(b * lp, D_MODEL)

    w_in = gdn_w_in[0]
    ba = w_in[:, GDN_CONV_CH + GDN_V:].reshape(D_MODEL, 2, 2, GDN_HEADS)
    ba_dir = [_pad_cols(jnp.concatenate([ba[:, 0, d], ba[:, 1, d]], axis=1), 128) for d in range(2)]
    conv_w8 = jnp.pad(gdn_conv_w[0], ((0, 8 - GDN_CONV), (0, 0)))
    qkv = _gdn_qkv_conv(h0.reshape(b, lp, D_MODEL), ln_g[0][None, :], w_in[:, :GDN_CONV_CH].astype(BF16),
                        conv_w8, lp)
    w_zb = jnp.concatenate([w_in[:, GDN_CONV_CH:GDN_CONV_CH + GDN_V]] + ba_dir, axis=1).astype(BF16)
    zb = _norm_matmul(h0, ln_g[0][None, :], w_zb, w_zb.shape[1] // 2)

    lane_a = lambda v: jnp.pad(v.astype(F32), ((0, 0), (GDN_HEADS, 128 - 2 * GDN_HEADS)))[:, None, :]
    o_fwd, o_bwd = _gdn_scan(qkv, zb.reshape(b, lp, zb.shape[1]), lane_a(gdn_a_log[0]),
                             lane_a(gdn_dt_bias[0]), lp)
    h1 = _gdn_out(o_fwd.reshape(b * lp, GDN_V), o_bwd.reshape(b * lp, GDN_V), zb, h0,
                  gdn_o_norm_g[0][None, :], gdn_w_out[0].astype(BF16))

    w_in1 = mla_w_in[0]
    o1 = MLA_Q_LORA
    o2_ = o1 + MLA_KV_LORA
    o3 = o2_ + MLA_ROPE
    w1 = jnp.concatenate([w_in1[:, o3:], w_in1[:, :o1], w_in1[:, o1:o2_],
                          _pad_cols(w_in1[:, o2_:o3], 256)], axis=1).astype(BF16)
    proj1 = _norm_matmul(h1, ln_g[1][None, :], w1, 1024)
    proj1_3 = proj1.reshape(b, lp, proj1.shape[1])

    pos = jnp.concatenate([jnp.arange(s_len, dtype=F32) + N_META, jnp.zeros((TAIL - N_META,), F32),
                           jnp.arange(N_META, dtype=F32)])
    inv = ROPE_THETA ** (-jnp.arange(0, MLA_ROPE, 2, dtype=F32) / MLA_ROPE)
    ang = pos[:, None] * inv[None, :]
    zc = jnp.zeros_like(ang)
    cos_t = jnp.concatenate([jnp.cos(ang), jnp.cos(ang), zc, zc], axis=1)
    nsin_t = jnp.concatenate([-jnp.sin(ang), zc, zc, zc], axis=1)
    sin_t = jnp.concatenate([zc, jnp.sin(ang), zc, zc], axis=1)

    scale = MLA_DQK ** -0.5 * math.log2(math.e)
    gq = jnp.broadcast_to((mla_qk_q_g[0].astype(F32) * scale)[:, None], (MLA_DQK, 128))
    score_bound = (1.02 * MLA_DQK * scale * jnp.max(jnp.abs(mla_qk_q_g[0].astype(F32)))
                   * jnp.max(jnp.abs(mla_qk_k_g[0].astype(F32))))
    bounded = score_bound < ATT_BOUND_MAX
    shift_rows = jnp.zeros((MLA_DPAD - MLA_DQK, 128), F32).at[0, :].set(jnp.where(bounded, -score_bound, 0.0))
    qt = _mla_q(proj1_3, mla_q_norm_g[0][None, :], mla_w_uq[0].T.astype(BF16), gq[:MLA_NOPE],
                gq[MLA_NOPE:], jnp.cos(ang).T, jnp.sin(ang).T, shift_rows, lp)

    w_ukv = mla_w_ukv[0].reshape(MLA_KV_LORA, MLA_HEADS, MLA_NOPE + MLA_DV)
    w_ukv = jnp.concatenate([w_ukv[:, :, :MLA_NOPE].reshape(MLA_KV_LORA, -1),
                             w_ukv[:, :, MLA_NOPE:].reshape(MLA_KV_LORA, -1)], axis=1).astype(BF16)
    gk = mla_qk_k_g[0].astype(F32)
    k, vt = _mla_kv(proj1_3, mla_kv_norm_g[0][None, :], w_ukv, gk[None, :MLA_NOPE],
                    _lane_row(gk[MLA_NOPE:]), cos_t, nsin_t, sin_t, lp)

    o = _attention(bounded.astype(jnp.int32).reshape(1), qt, k, vt, s_len)

    h1_3 = h1.reshape(b, lp, D_MODEL)
    w_out1 = mla_w_out[0].astype(BF16)
    outs = []
    b0 = 0
    for x in xs:
        outs.append(_mla_out(o, proj1_3, h1_3, w_out1, b0, x.shape[0], s_len))
        b0 += x.shape[0]
    return tuple(outs)


def kernel(x_prompt, x_sample, meta_tokens, ln_g, gdn_w_in, gdn_conv_w, gdn_a_log, gdn_dt_bias,
           gdn_o_norm_g, gdn_w_out, mla_w_in, mla_q_norm_g, mla_kv_norm_g, mla_w_uq, mla_w_ukv,
           mla_qk_q_g, mla_qk_k_g, mla_w_out):
    return _trunk_all((x_prompt, x_sample), meta_tokens, ln_g, gdn_w_in, gdn_conv_w, gdn_a_log,
                      gdn_dt_bias, gdn_o_norm_g, gdn_w_out, mla_w_in, mla_q_norm_g, mla_kv_norm_g,
                      mla_w_uq, mla_w_ukv, mla_qk_q_g, mla_qk_k_g, mla_w_out)
```

```python
import functools
import math

import jax
import jax.numpy as jnp
from jax import lax
from jax.experimental import pallas as pl
from jax.experimental.pallas import tpu as pltpu

F32 = jnp.float32
BF16 = jnp.bfloat16

D_MODEL = 1024
N_META = 16
TAIL = 128
NORM_EPS = 1e-6

GDN_HEADS = 8
GDN_DK = 128
GDN_DV = 256
GDN_CONV = 5
GDN_CHUNK = 64
GDN_QK = GDN_HEADS * GDN_DK
GDN_V = GDN_HEADS * GDN_DV
GDN_CONV_CH = 2 * GDN_QK + GDN_V

MLA_HEADS = 16
MLA_Q_LORA = 512
MLA_KV_LORA = 256
MLA_NOPE = 128
MLA_ROPE = 64
MLA_DQK = MLA_NOPE + MLA_ROPE
MLA_DV = 128
MLA_V = MLA_HEADS * MLA_DV
MLA_DPAD = 256
ROPE_THETA = 10000.0

VMEM_LIMIT = 56 * 1024 * 1024


def _cparams(sem):
    return pltpu.CompilerParams(dimension_semantics=sem, vmem_limit_bytes=VMEM_LIMIT)


def _tile(n, target, mult):
    best = None
    t = mult
    while t <= min(n, target):
        if n % t == 0:
            best = t
        t += mult
    assert best is not None, (n, target, mult)
    return best


def _norm_matmul_kernel(x_ref, g_ref, w_ref, o_ref, xn_ref):
    @pl.when(pl.program_id(1) == 0)
    def _():
        x = x_ref[...]
        ms = jnp.mean(x * x, axis=-1, keepdims=True)
        xn_ref[...] = (x * lax.rsqrt(ms + NORM_EPS) * g_ref[...]).astype(BF16)

    o_ref[...] = jnp.dot(xn_ref[...], w_ref[...], preferred_element_type=F32).astype(o_ref.dtype)


def _norm_matmul(x, g, w, tn, out_dtype=F32):
    rows, d = x.shape
    n = w.shape[1]
    tm = _tile(rows, 1408, 128)
    return pl.pallas_call(
        _norm_matmul_kernel,
        grid=(rows // tm, n // tn),
        in_specs=[
            pl.BlockSpec((tm, d), lambda i, j: (i, 0)),
            pl.BlockSpec((1, d), lambda i, j: (0, 0)),
            pl.BlockSpec((d, tn), lambda i, j: (0, j)),
        ],
        out_specs=pl.BlockSpec((tm, tn), lambda i, j: (i, j)),
        out_shape=jax.ShapeDtypeStruct((rows, n), out_dtype),
        scratch_shapes=[pltpu.VMEM((tm, d), BF16)],
        compiler_params=_cparams(("parallel", "arbitrary")),
        name="norm_matmul",
    )(x, g, w)


CONV_SUB = 128
CONV_TC = 1024


def _qkv_conv_kernel(prev_ref, main_ref, next_ref, g_ref, w_ref, wba_ref, cw_ref, o_ref, ba_ref, ext_ref,
                     *, tr):
    half = GDN_CONV // 2
    nblk = GDN_CONV_CH // CONV_TC
    x = jnp.concatenate([prev_ref[0], main_ref[0], next_ref[0]], axis=0)
    ms = jnp.mean(x * x, axis=-1, keepdims=True)
    xn = (x * lax.rsqrt(ms + NORM_EPS) * g_ref[...]).astype(BF16)
    ba_ref[0] = jnp.dot(xn, wba_ref[...], preferred_element_type=F32)[8:8 + tr]

    def project(j):
        ext_ref[j] = jnp.dot(xn, w_ref[:, j * CONV_TC:(j + 1) * CONV_TC], preferred_element_type=F32)

    def conv_silu(j, sb):
        lanes = slice(j * CONV_TC, (j + 1) * CONV_TC)
        nrow = CONV_SUB + 16
        x2 = ext_ref[j, sb * CONV_SUB:sb * CONV_SUB + nrow, :]
        acc = x2[8:8 + CONV_SUB] * cw_ref[half:half + 1, lanes]
        for s in range(1, half + 1):
            down = pltpu.roll(x2, s, 0)
            acc = acc + down[8:8 + CONV_SUB] * cw_ref[half - s:half - s + 1, lanes]
            up = pltpu.roll(x2, nrow - s, 0)
            acc = acc + up[8:8 + CONV_SUB] * cw_ref[half + s:half + s + 1, lanes]
        return acc * jax.nn.sigmoid(acc)

    def finish(j):
        for sb in range(tr // CONV_SUB):
            y = conv_silu(j, sb)
            rows = slice(sb * CONV_SUB, (sb + 1) * CONV_SUB)
            if j >= 2:
                o_ref[0, rows, j * CONV_TC:(j + 1) * CONV_TC] = y.astype(BF16)
                continue
            qscale = GDN_DK ** -0.5 if j == 0 else 1.0
            for h in range(CONV_TC // GDN_DK):
                yh = y[:, h * GDN_DK:(h + 1) * GDN_DK]
                ss = jnp.sum(yh * yh, axis=-1, keepdims=True)
                lanes = slice(j * CONV_TC + h * GDN_DK, j * CONV_TC + (h + 1) * GDN_DK)
                o_ref[0, rows, lanes] = (yh * (lax.rsqrt(ss + NORM_EPS) * qscale)).astype(BF16)

    project(0)
    for j in range(nblk):
        if j + 1 < nblk:
            project(j + 1)
        finish(j)


def _gdn_qkv_conv(h3, g, w_qkv, w_ba, conv_w8, lp):
    b = h3.shape[0]
    nba = w_ba.shape[1]
    tr = _tile(lp, 384, CONV_SUB)
    nb8 = lp // 8
    tb = tr // 8
    kern = functools.partial(_qkv_conv_kernel, tr=tr)
    const = lambda bi, i: (0, 0)
    return pl.pallas_call(
        kern,
        grid=(b, lp // tr),
        in_specs=[
            pl.BlockSpec((1, 8, D_MODEL), lambda bi, i: (bi, (i * tb + nb8 - 1) % nb8, 0)),
            pl.BlockSpec((1, tr, D_MODEL), lambda bi, i: (bi, i, 0)),
            pl.BlockSpec((1, 8, D_MODEL), lambda bi, i: (bi, ((i + 1) * tb) % nb8, 0)),
            pl.BlockSpec((1, D_MODEL), const),
            pl.BlockSpec((D_MODEL, GDN_CONV_CH), const),
            pl.BlockSpec((D_MODEL, nba), const),
            pl.BlockSpec((8, GDN_CONV_CH), const),
        ],
        out_specs=[
            pl.BlockSpec((1, tr, GDN_CONV_CH), lambda bi, i: (bi, i, 0)),
            pl.BlockSpec((1, tr, nba), lambda bi, i: (bi, i, 0)),
        ],
        out_shape=[
            jax.ShapeDtypeStruct((b, lp, GDN_CONV_CH), BF16),
            jax.ShapeDtypeStruct((b, lp, nba), F32),
        ],
        scratch_shapes=[pltpu.VMEM((GDN_CONV_CH // CONV_TC, tr + 16, CONV_TC), F32)],
        compiler_params=_cparams(("parallel", "parallel")),
        name="gdn_qkv_conv",
    )(h3, h3, h3, g, w_qkv, w_ba, conv_w8)


SCAN_NB = 2


def _split3(x):
    hi = x.astype(BF16)
    r1 = x - hi.astype(F32)
    mid = r1.astype(BF16)
    lo = (r1 - mid.astype(F32)).astype(BF16)
    return hi, mid, lo


def _bdot(a, b):
    return jnp.dot(a.astype(BF16), b.astype(BF16), preferred_element_type=F32)


def _gdn_scan_kernel(qf_ref, kf_ref, vf_ref, baf_ref, qb_ref, kb_ref, vb_ref, bab_ref, alog_ref, dtb_ref,
                     of_ref, ob_ref, s_ref, sb_ref, u_ref, qw_ref, attn_ref, kd_ref, eg_ref, *, nc):
    t = pl.program_id(1)
    c = GDN_CHUNK
    c2 = 2 * c
    npair = GDN_HEADS // 2
    o_refs = (of_ref, ob_ref)

    @pl.when(t == 0)
    def _():
        for ref in (s_ref, sb_ref, u_ref, qw_ref, attn_ref, kd_ref, eg_ref):
            ref[...] = jnp.zeros_like(ref)

    nb = of_ref.shape[0]
    nprob = nb * 2 * npair
    applied = [dict(sq=pi // npair, heads=(2 * (pi % npair), 2 * (pi % npair) + 1),
                    qs=[None, None], vnew=[None, None]) for pi in range(nprob)]

    def read_state(pi, hh):
        ap = applied[pi]
        rs = slice(hh * c, (hh + 1) * c)
        qws = jnp.dot(qw_ref[pi, hh], sb_ref[ap["sq"], ap["heads"][hh]], preferred_element_type=F32)
        ap["qs"][hh] = qws[:c]
        ap["vnew"][hh] = (u_ref[pi, rs, :] - qws[c:]).astype(BF16)

    def write_out(pi):
        ap = applied[pi]
        h0, h1 = ap["heads"]
        bi, d = divmod(ap["sq"], 2)
        vnew2 = jnp.concatenate(ap["vnew"], axis=0)
        o2 = jnp.concatenate(ap["qs"], axis=0) + jnp.dot(attn_ref[pi], vnew2, preferred_element_type=F32)
        o_refs[d][bi, :, h0 * GDN_DV:(h0 + 1) * GDN_DV] = o2[:c].astype(BF16)
        o_refs[d][bi, :, h1 * GDN_DV:(h1 + 1) * GDN_DV] = o2[c:].astype(BF16)

    def update_state(pi, hh):
        ap = applied[pi]
        sq, h = ap["sq"], ap["heads"][hh]
        rs = slice(hh * c, (hh + 1) * c)
        a_h = GDN_HEADS + h
        upd = lax.dot_general(kd_ref[pi, rs, :], ap["vnew"][hh], (((0,), (0,)), ((), ())),
                              preferred_element_type=F32)
        s_new = s_ref[sq, h] * eg_ref[sq, :, a_h:a_h + 1] + upd
        s_ref[sq, h] = s_new
        sb_ref[sq, h] = s_new.astype(BF16)

    pairs_hh = [(pi, hh) for pi in range(nprob) for hh in range(2)]
    apply_ops = ([functools.partial(read_state, pi, hh) for pi, hh in pairs_hh]
                 + [functools.partial(write_out, pi) for pi in range(nprob)]
                 + [functools.partial(update_state, pi, hh) for pi, hh in pairs_hh])

    def emit_apply(n):
        for _ in range(min(n, len(apply_ops))):
            apply_ops.pop(0)()

    emit_apply(len(pairs_hh) // 2)
    tp = jnp.minimum(t, nc - 1)
    ri = lax.broadcasted_iota(jnp.int32, (c2, c2), 0)
    ci = lax.broadcasted_iota(jnp.int32, (c2, c2), 1)
    same = (ri >> 6) == (ci >> 6)
    offdiag = ri != ci
    top =lax.broadcasted_iota(jnp.int32, (c2, 1), 0) < c
    left = lax.broadcasted_iota(jnp.int32, (1, c2), 1) < c
    row_id = lax.broadcasted_iota(jnp.int32, (c, 128), 0)

    in_refs = ((qf_ref, kf_ref, vf_ref, baf_ref), (qb_ref, kb_ref, vb_ref, bab_ref))
    seqs = []
    for sq in range(2 * nb):
        bi, d = divmod(sq, 2)
        q_ref, k_ref, v_ref, ba_ref = in_refs[d]
        seq = tp if d == 0 else nc - 1 - tp
        blk = jnp.where(seq < 2, nc - 2 + seq, seq - 2)
        first_valid = jnp.where(blk == nc - 2, c, jnp.where(blk == nc - 1, c - N_META, 0))
        valid = row_id >= first_valid
        ba = ba_ref[bi]
        beta = jnp.where(valid, jax.nn.sigmoid(ba), 0.0)
        xs = ba + dtb_ref[d]
        softplus = jnp.maximum(xs, 0.0) + jnp.log(1.0 + jnp.exp(-jnp.abs(xs)))
        g = jnp.where(valid, -jnp.exp(alog_ref[d]) * softplus, 0.0)
        incl = same & ((ri >= ci) if d == 0 else (ri <= ci))
        tri = jnp.where(incl, 1.0, 0.0).astype(BF16)
        ghi, gmid, glo = _split3(jnp.concatenate([g, g], axis=0))
        gc2 = (jnp.dot(tri, ghi, preferred_element_type=F32)
               + jnp.dot(tri, gmid, preferred_element_type=F32)
               + jnp.dot(tri, glo, preferred_element_type=F32))
        seqs.append(dict(q=q_ref.at[bi], k=k_ref.at[bi], v=v_ref.at[bi], incl=incl, gc2=gc2, gc2t=gc2.T,
                         beta2=jnp.concatenate([beta, beta], axis=0),
                         gtot=jnp.sum(g, axis=0, keepdims=True)))

    def stacked(ref, h0, h1, width):
        return jnp.concatenate([ref[:, h0 * width:(h0 + 1) * width],
                                ref[:, h1 * width:(h1 + 1) * width]], axis=0)

    probs = []
    for dd in seqs:
        for p in range(npair):
            h0, h1 = 2 * p, 2 * p + 1
            a0, a1 = GDN_HEADS + h0, GDN_HEADS + h1
            col = jnp.where(top, dd["gc2"][:, a0:a0 + 1], dd["gc2"][:, a1:a1 + 1])
            row = jnp.where(left, dd["gc2t"][a0:a0 + 1, :], dd["gc2t"][a1:a1 + 1, :])
            bcol = jnp.where(top, dd["beta2"][:, h0:h0 + 1], dd["beta2"][:, h1:h1 + 1])
            tot = jnp.where(top, dd["gtot"][:, a0:a0 + 1], dd["gtot"][:, a1:a1 + 1])
            dec = jnp.exp(jnp.where(dd["incl"], col - row, -jnp.inf))
            kst = stacked(dd["k"], h0, h1, GDN_DK)
            qst = stacked(dd["q"], h0, h1, GDN_DK)
            vst = stacked(dd["v"], h0, h1, GDN_DV)
            kf = kst.astype(F32)
            kb = kf * bcol
            sc = lax.dot_general(jnp.concatenate([qst, kb.astype(BF16)], axis=0), kst,
                                 (((1,), (1,)), ((), ())), preferred_element_type=F32)
            egc = jnp.exp(col)
            probs.append(dict(
                attn=(sc[:c2] * dec).astype(BF16),
                a=jnp.where(offdiag, sc[c2:] * dec, 0.0),
                rhs=jnp.concatenate([vst.astype(F32) * bcol, kb * egc], axis=1).astype(BF16),
                qg=(qst.astype(F32) * egc).astype(BF16),
                kd=(kf * jnp.exp(tot - col)).astype(BF16)))

    emit_apply(len(pairs_hh) // 2)

    for _ in _unit_lower_inverse_staged(probs):
        emit_apply(3 * nb)
    emit_apply(len(apply_ops))

    for pi, pr in enumerate(probs):
        uw = jnp.dot(pr["tinv"].astype(BF16), pr["rhs"], preferred_element_type=F32)
        u_ref[pi] = uw[:, :GDN_DV]
        w = uw[:, GDN_DV:].astype(BF16)
        for hh in range(2):
            rs = slice(hh * c, (hh + 1) * c)
            qw_ref[pi, hh] = jnp.concatenate([pr["qg"][rs], w[rs]], axis=0)
        attn_ref[pi] = pr["attn"]
        kd_ref[pi] = pr["kd"]
    for sq, dd in enumerate(seqs):
        eg_ref[sq] = jnp.exp(dd["gtot"])


def _unit_lower_inverse_staged(probs):
    c = GDN_CHUNK
    nside = 4
    ri = lax.broadcasted_iota(jnp.int32, (c, nside * c), 0)
    ci = lax.broadcasted_iota(jnp.int32, (c, nside * c), 1)
    lane_blk = ci >> 6
    within = ci & (c - 1)
    diag16 = (ri >> 4) == (within >> 4)
    eye = jnp.where(ri == within, 1.0, 0.0).astype(F32)
    left = lax.broadcasted_iota(jnp.int32, (1, 2 * c), 1) < c

    def blockdiag(y):
        return jnp.concatenate([jnp.where(lane_blk == r, y, 0.0) for r in range(nside)],
                               axis=0).astype(BF16)

    def mm(x, ybd):
        return jnp.dot(x.astype(BF16), ybd, preferred_element_type=F32)

    groups = []
    for g0 in range(0, len(probs), 2):
        pa, pb = probs[g0], probs[g0 + 1]
        a = jnp.concatenate([pa["a"][:c] + pa["a"][c:], pb["a"][:c] + pb["a"][c:]], axis=1)
        ad = jnp.where(diag16, a, 0.0)
        groups.append(dict(pairs=(pa, pb), ad=ad, an=blockdiag(a - ad), dinv=eye - ad))
    for gr in groups:
        gr["p"] = mm(gr["ad"], blockdiag(gr["ad"]))
    yield
    for gr in groups:
        pbd = blockdiag(gr["p"])
        gr["dinv"] = gr["dinv"] + mm(gr["dinv"], pbd)
        gr["p"] = mm(gr["p"], pbd)
    yield
    for gr in groups:
        pbd = blockdiag(gr["p"])
        gr["dinv"] = gr["dinv"] + mm(gr["dinv"], pbd)
        gr["p"] = mm(gr["p"], pbd)
    yield
    for gr in groups:
        gr["dinv"] = gr["dinv"] + mm(gr["dinv"], blockdiag(gr["p"]))
    yield
    for gr in groups:
        gr["m"] = mm(gr["dinv"], gr["an"])
    yield
    for gr in groups:
        gr["m2"] = blockdiag(mm(gr["m"], blockdiag(gr["m"])))
    yield
    for gr in groups:
        x = eye - gr["m"]
        gr["x"] = x + mm(x, gr["m2"])
    yield
    for gr in groups:
        t = mm(gr["x"], blockdiag(gr["dinv"]))
        for i, pr in enumerate(gr["pairs"]):
            half = t[:, 2 * c * i:2 * c * (i + 1)]
            pr["tinv"] = jnp.concatenate([jnp.where(left, half, 0.0), jnp.where(left, 0.0, half)],
                                         axis=0)


def _gdn_scan(qkv, proj3, alog_rows, dtb_rows, lp):
    b = qkv.shape[0]
    nc = lp // GDN_CHUNK
    ba_col0 = 0

    def blk_of(seq):
        return jnp.where(seq < 2, nc - 2 + seq, seq - 2)

    def chunk_specs(d):
        prep = lambda t: jnp.minimum(t, nc - 1)
        seq = prep if d == 0 else (lambda t: nc - 1 - prep(t))
        return [
            pl.BlockSpec((nb, GDN_CHUNK, GDN_QK), lambda bi, t: (bi, blk_of(seq(t)), 0)),
            pl.BlockSpec((nb, GDN_CHUNK, GDN_QK), lambda bi, t: (bi, blk_of(seq(t)), 1)),
            pl.BlockSpec((nb, GDN_CHUNK, GDN_V), lambda bi, t: (bi, blk_of(seq(t)), 1)),
            pl.BlockSpec((nb, GDN_CHUNK, 128), lambda bi, t: (bi, blk_of(seq(t)), ba_col0 + d)),
        ]

    nb = SCAN_NB if b % SCAN_NB == 0 else 1
    const = pl.BlockSpec((2, 1, 128), lambda bi, t: (0, 0, 0))
    kern = functools.partial(_gdn_scan_kernel, nc=nc)
    applied = lambda t: jnp.maximum(t - 1, 0)
    nseq = 2 * nb
    nprob = nseq * (GDN_HEADS // 2)
    c2 = 2 * GDN_CHUNK
    return pl.pallas_call(
        kern,
        grid=(b // nb, nc + 1),
        in_specs=chunk_specs(0) + chunk_specs(1) + [const, const],
        out_specs=[
            pl.BlockSpec((nb, GDN_CHUNK, GDN_V), lambda bi, t: (bi, blk_of(applied(t)), 0)),
            pl.BlockSpec((nb, GDN_CHUNK, GDN_V), lambda bi, t: (bi, blk_of(nc - 1 - applied(t)), 0)),
        ],
        out_shape=[jax.ShapeDtypeStruct((b, lp, GDN_V), BF16)] * 2,
        scratch_shapes=[
            pltpu.VMEM((nseq, GDN_HEADS, GDN_DK, GDN_DV), F32),
            pltpu.VMEM((nseq, GDN_HEADS, GDN_DK, GDN_DV), BF16),
            pltpu.VMEM((nprob, c2, GDN_DV), F32),
            pltpu.VMEM((nprob, 2, c2, GDN_DK), BF16),
            pltpu.VMEM((nprob, c2, c2), BF16),
            pltpu.VMEM((nprob, c2, GDN_DK), BF16),
            pltpu.VMEM((nseq, 1, 128), F32),
        ],
        compiler_params=_cparams(("parallel", "arbitrary")),
        name="gdn_scan",
    )(qkv, qkv, qkv, proj3, qkv, qkv, qkv, proj3, alog_rows, dtb_rows)


OUT_SUB = 256


def _gdn_out_kernel(of_ref, ob_ref, h_ref, ln_ref, wz_ref, g_ref, w_ref, o_ref):
    nsub = o_ref.shape[0] // OUT_SUB
    rows = [slice(sb * OUT_SUB, (sb + 1) * OUT_SUB) for sb in range(nsub)]

    def gate_logits(sb):
        x = h_ref[rows[sb], :]
        ms = jnp.mean(x * x, axis=-1, keepdims=True)
        xn = (x * lax.rsqrt(ms + NORM_EPS) * ln_ref[...]).astype(BF16)
        return jnp.dot(xn, wz_ref[...], preferred_element_type=F32)

    z_next = gate_logits(0)
    for sb in range(nsub):
        z = z_next
        if sb + 1 < nsub:
            z_next = gate_logits(sb + 1)
        o = of_ref[rows[sb], :].astype(F32) + ob_ref[rows[sb], :].astype(F32)
        gate = z * jax.nn.sigmoid(z)
        ys = []
        for h in range(GDN_HEADS):
            lanes = slice(h * GDN_DV, (h + 1) * GDN_DV)
            oh = o[:, lanes]
            ms = jnp.mean(oh * oh, axis=-1, keepdims=True)
            ys.append((oh * lax.rsqrt(ms + NORM_EPS) * g_ref[...] * gate[:, lanes]).astype(BF16))
        y = jnp.concatenate(ys, axis=1)
        o_ref[rows[sb], :] = h_ref[rows[sb], :] + jnp.dot(y, w_ref[...], preferred_element_type=F32)


def _gdn_out(o_fwd, o_bwd, h0, ln_row, w_z, g_row, w_out):
    rows = h0.shape[0]
    tm = _tile(rows, 512, OUT_SUB)
    const = lambda i: (0, 0)
    return pl.pallas_call(
        _gdn_out_kernel,
        grid=(rows // tm,),
        in_specs=[
            pl.BlockSpec((tm, GDN_V), lambda i: (i, 0)),
            pl.BlockSpec((tm, GDN_V), lambda i: (i, 0)),
            pl.BlockSpec((tm, D_MODEL), lambda i: (i, 0)),
            pl.BlockSpec((1, D_MODEL), const),
            pl.BlockSpec((D_MODEL, GDN_V), const),
            pl.BlockSpec((1, GDN_DV), const),
            pl.BlockSpec((GDN_V, D_MODEL), const),
        ],
        out_specs=pl.BlockSpec((tm, D_MODEL), lambda i: (i, 0)),
        out_shape=jax.ShapeDtypeStruct((rows, D_MODEL), F32),
        compiler_params=_cparams(("parallel",)),
        name="gdn_out",
    )(o_fwd, o_bwd, h0, ln_row, w_z, g_row, w_out)


def _rope(r, cos, nsin_lo, sin_hi):
    return r * cos + pltpu.roll(r, 96, 1) * nsin_lo + pltpu.roll(r, 32, 1) * sin_hi


def _mla_q_kernel(cq_ref, g1_ref, wt_ref, ga_ref, gr_ref, cos_ref, sin_ref, shift_ref, qt_ref):
    cq = cq_ref[0].astype(F32)
    tm = cq.shape[0]
    ms = jnp.mean(cq * cq, axis=-1, keepdims=True)
    cqt = (cq * lax.rsqrt(ms + NORM_EPS) * g1_ref[...]).T.astype(BF16)
    q = jnp.dot(wt_ref[...], cqt, preferred_element_type=F32)
    half = MLA_ROPE // 2
    tile = lambda g: jnp.concatenate([g] * (tm // 128), axis=1)
    ga = tile(ga_ref[...])
    gr1, gr2 = tile(gr_ref[0:half, :]), tile(gr_ref[half:MLA_ROPE, :])
    cos, sin = cos_ref[...], sin_ref[...]
    pad_rows = tile(shift_ref[...]).astype(BF16)
    for h in range(MLA_HEADS):
        r0 = h * MLA_DQK
        a = q[r0:r0 + MLA_NOPE]
        x1 = q[r0 + MLA_NOPE:r0 + MLA_NOPE + half]
        x2 = q[r0 + MLA_NOPE + half:r0 + MLA_DQK]
        ss = (jnp.sum(a * a, axis=0, keepdims=True) + jnp.sum(x1 * x1, axis=0, keepdims=True)
              + jnp.sum(x2 * x2, axis=0, keepdims=True))
        inv = lax.rsqrt(ss * (1.0 / MLA_DQK) + NORM_EPS)
        x1 = x1 * inv * gr1
        x2 = x2 * inv * gr2
        qt_ref[0, h, 0:MLA_NOPE, :] = (a * inv * ga).astype(BF16)
        qt_ref[0, h, MLA_NOPE:MLA_NOPE + half, :] = (x1 * cos - x2 * sin).astype(BF16)
        qt_ref[0, h, MLA_NOPE + half:MLA_DQK, :] = (x2 * cos + x1 * sin).astype(BF16)
        qt_ref[0, h, MLA_DQK:MLA_DPAD, :] = pad_rows


def _mla_q(proj3, g1, w_uq_t, ga, gr, cos_t, sin_t, shift_rows, lp):
    b = proj3.shape[0]
    tm = _tile(lp, 384, 128)
    cqcol = MLA_V // MLA_Q_LORA
    const = lambda bi, i: (0, 0)
    return pl.pallas_call(
        _mla_q_kernel,
        grid=(b, lp // tm),
        in_specs=[
            pl.BlockSpec((1, tm, MLA_Q_LORA), lambda bi, i: (bi, i, cqcol)),
            pl.BlockSpec((1, MLA_Q_LORA), const),
            pl.BlockSpec((MLA_HEADS * MLA_DQK, MLA_Q_LORA), const),
            pl.BlockSpec((MLA_NOPE, 128), const),
            pl.BlockSpec((MLA_ROPE, 128), const),
            pl.BlockSpec((MLA_ROPE // 2, tm), lambda bi, i: (0, i)),
            pl.BlockSpec((MLA_ROPE // 2, tm), lambda bi, i: (0, i)),
            pl.BlockSpec((MLA_DPAD - MLA_DQK, 128), const),
        ],
        out_specs=pl.BlockSpec((1, MLA_HEADS, MLA_DPAD, tm), lambda bi, i: (bi, 0, 0, i)),
        out_shape=jax.ShapeDtypeStruct((b, MLA_HEADS, MLA_DPAD, lp), BF16),
        compiler_params=_cparams(("parallel", "parallel")),
        name="mla_q",
    )(proj3, g1, w_uq_t, ga, gr, cos_t, sin_t, shift_rows)


def _mla_kv_kernel(ckv_ref, kpe_ref, g1_ref, w_ref, ga_ref, gr_ref, cos_ref, nsin_ref, sin_ref,
                   k_ref, vt_ref):
    ckv = ckv_ref[0].astype(F32)
    ms = jnp.mean(ckv * ckv, axis=-1, keepdims=True)
    cn = (ckv * lax.rsqrt(ms + NORM_EPS) * g1_ref[...]).astype(BF16)
    kv = jnp.dot(cn, w_ref[...], preferred_element_type=F32)
    kpe = kpe_ref[0].astype(F32)
    sq_pe = kpe * kpe
    kr = _rope(kpe * gr_ref[...], cos_ref[...], nsin_ref[...], sin_ref[...])
    one_hot = jnp.where(lax.broadcasted_iota(jnp.int32, (1, 128), 1) == MLA_ROPE, 1.0, 0.0).astype(F32)
    for h in range(MLA_HEADS):
        kn = kv[:, h * MLA_NOPE:(h + 1) * MLA_NOPE]
        ss = jnp.sum(kn * kn + sq_pe, axis=-1, keepdims=True)
        inv = lax.rsqrt(ss * (1.0 / MLA_DQK) + NORM_EPS)
        k_ref[0, h, :, 0:MLA_NOPE] = (kn * inv * ga_ref[...]).astype(BF16)
        k_ref[0, h, :, MLA_NOPE:MLA_DPAD] = (kr * inv + one_hot).astype(BF16)
        v = kv[:, MLA_HEADS * MLA_NOPE + h * MLA_DV:MLA_HEADS * MLA_NOPE + (h + 1) * MLA_DV]
        vt_ref[0, h] = v.T.astype(BF16)


def _mla_kv(proj3, g1, w_ukv, ga, gr, cos, nsin, sin, lp):
    b = proj3.shape[0]
    tm = _tile(lp, 384, 128)
    ckvcol = (MLA_V + MLA_Q_LORA) // MLA_KV_LORA
    kpecol = (MLA_V + MLA_Q_LORA + MLA_KV_LORA) // 128
    const = lambda bi, i: (0, 0)
    return pl.pallas_call(
        _mla_kv_kernel,
        grid=(b, lp // tm),
        in_specs=[
            pl.BlockSpec((1, tm, MLA_KV_LORA), lambda bi, i: (bi, i, ckvcol)),
            pl.BlockSpec((1, tm, 128), lambda bi, i: (bi, i, kpecol)),
            pl.BlockSpec((1, MLA_KV_LORA), const),
            pl.BlockSpec((MLA_KV_LORA, MLA_HEADS * (MLA_NOPE + MLA_DV)), const),
            pl.BlockSpec((1, 128), const),
            pl.BlockSpec((1, 128), const),
            pl.BlockSpec((tm, 128), lambda bi, i: (i, 0)),
            pl.BlockSpec((tm, 128), lambda bi, i: (i, 0)),
            pl.BlockSpec((tm, 128), lambda bi, i: (i, 0)),
        ],
        out_specs=[
            pl.BlockSpec((1, MLA_HEADS, tm, MLA_DPAD), lambda bi, i: (bi, 0, i, 0)),
            pl.BlockSpec((1, MLA_HEADS, MLA_DV, tm), lambda bi, i: (bi, 0, 0, i)),
        ],
        out_shape=[
            jax.ShapeDtypeStruct((b, MLA_HEADS, lp, MLA_DPAD), BF16),
            jax.ShapeDtypeStruct((b, MLA_HEADS, MLA_DV, lp), BF16),
        ],
        compiler_params=_cparams(("parallel", "parallel")),
        name="mla_kv",
    )(proj3, proj3, g1, w_ukv, ga, gr, cos, nsin, sin)


ATT_TK = 1024
ATT_TQ = 512
ATT_AHEAD = 2
ATT_BOUND_MAX = 60.0


def _attn_kernel(bounded_ref, qt_ref, k_ref, vt_ref, o_ref, *, s_len):
    tq = qt_ref.shape[3]
    nstrip = tq // ATT_TQ
    chunks = [(ck * ATT_TK, ATT_TK) for ck in range(s_len // ATT_TK)] + [(s_len, TAIL)]
    units = [(i, s) for i in range(len(chunks)) for s in range(nstrip)]
    is_meta = lax.broadcasted_iota(jnp.int32, (TAIL, 1), 0) >= TAIL - N_META
    tail_bias = jnp.where(is_meta, 0.0, -jnp.inf).astype(F32)

    def scores(u):
        i, s = units[u]
        k0, nk = chunks[i]
        st = jnp.dot(k_ref[0, 0, k0:k0 + nk, :], qt_ref[0, 0, :, s * ATT_TQ:(s + 1) * ATT_TQ],
                     preferred_element_type=F32)
        return st + tail_bias if i == len(chunks) - 1 else st

    def run(bounded):
        m = [jnp.full((1, ATT_TQ), -jnp.inf, F32) for _ in range(nstrip)]
        l = [jnp.zeros((1, ATT_TQ), F32) for _ in range(nstrip)]
        acc = [jnp.zeros((MLA_DV, ATT_TQ), F32) for _ in range(nstrip)]
        pending = [scores(u) for u in range(min(ATT_AHEAD, len(units)))]
        for u, (i, s) in enumerate(units):
            if u + ATT_AHEAD < len(units):
                pending.append(scores(u + ATT_AHEAD))
            st = pending.pop(0)
            k0, nk = chunks[i]
            if bounded:
                p = jnp.exp2(st)
                l[s] = l[s] + jnp.sum(p, axis=0, keepdims=True)
                acc[s] = acc[s] + jnp.dot(vt_ref[0, 0, :, k0:k0 + nk], p.astype(BF16),
                                          preferred_element_type=F32)
            else:
                m_new = jnp.maximum(m[s], jnp.max(st, axis=0, keepdims=True))
                alpha = jnp.exp2(m[s] - m_new)
                p = jnp.exp2(st - m_new)
                l[s] = alpha * l[s] + jnp.sum(p, axis=0, keepdims=True)
                acc[s] = alpha * acc[s] + jnp.dot(vt_ref[0, 0, :, k0:k0 + nk], p.astype(BF16),
                                                  preferred_element_type=F32)
                m[s] = m_new
        for s in range(nstrip):
            o_ref[0, s * ATT_TQ:(s + 1) * ATT_TQ, :] = (acc[s] / l[s]).T.astype(o_ref.dtype)

    @pl.when(bounded_ref[0] == 1)
    def _():
        run(True)

    @pl.when(bounded_ref[0] != 1)
    def _():
        run(False)


def _attention(bounded, qt, k, vt, s_len):
    b = qt.shape[0]
    lp = k.shape[2]
    assert s_len % ATT_TK == 0 and lp == s_len + TAIL
    tq = _tile(s_len, 1024, ATT_TQ)
    kern = functools.partial(_attn_kernel, s_len=s_len)
    return pl.pallas_call(
        kern,
        grid=(b, MLA_HEADS, s_len // tq),
        in_specs=[
            pl.BlockSpec(memory_space=pltpu.SMEM),
            pl.BlockSpec((1, 1, MLA_DPAD, tq), lambda bi, h, i: (bi, h, 0, i)),
            pl.BlockSpec((1, 1, lp, MLA_DPAD), lambda bi, h, i: (bi, h, 0, 0)),
            pl.BlockSpec((1, 1, MLA_DV, lp), lambda bi, h, i: (bi, h, 0, 0)),
        ],
        out_specs=pl.BlockSpec((1, tq, MLA_DV), lambda bi, h, i: (bi, i, h)),
        out_shape=jax.ShapeDtypeStruct((b, s_len, MLA_V), BF16),
        compiler_params=_cparams(("parallel", "parallel", "arbitrary")),
        name="mla_attention",
    )(bounded, qt, k, vt)


def _mla_out_kernel(o_ref, z_ref, h_ref, w_ref, y_ref):
    z = z_ref[0].astype(F32)
    y = (o_ref[0].astype(F32) * (z * jax.nn.sigmoid(z))).astype(BF16)
    y_ref[0] = h_ref[0] + jnp.dot(y, w_ref[...], preferred_element_type=F32)


def _mla_out(o, proj3, h3, w_out, b0, nb, s_len):
    tm = _tile(s_len, 512, 128)
    return pl.pallas_call(
        _mla_out_kernel,
        grid=(nb, s_len // tm),
        in_specs=[
            pl.BlockSpec((1, tm, MLA_V), lambda bi, i: (bi + b0, i, 0)),
            pl.BlockSpec((1, tm, MLA_V), lambda bi, i: (bi + b0, i, 0)),
            pl.BlockSpec((1, tm, D_MODEL), lambda bi, i: (bi + b0, i, 0)),
            pl.BlockSpec((MLA_V, D_MODEL), lambda bi, i: (0, 0)),
        ],
        out_specs=pl.BlockSpec((1, tm, D_MODEL), lambda bi, i: (bi, i, 0)),
        out_shape=jax.ShapeDtypeStruct((nb, s_len, D_MODEL), F32),
        compiler_params=_cparams(("parallel", "parallel")),
        name="mla_out",
    )(o, proj3, h3, w_out)


def _pad_cols(w, n):
    return jnp.pad(w, ((0, 0), (0, n - w.shape[1])))


def _lane_row(v, n=128):
    return jnp.pad(v.astype(F32), (0, n - v.shape[0]))[None, :]


def _trunk_all(xs, meta_tokens, ln_g, gdn_w_in, gdn_conv_w, gdn_a_log, gdn_dt_bias, gdn_o_norm_g,
               gdn_w_out, mla_w_in, mla_q_norm_g, mla_kv_norm_g, mla_w_uq, mla_w_ukv, mla_qk_q_g,
               mla_qk_k_g, mla_w_out):
    s_len = xs[0].shape[1]
    assert all(x.shape[1] == s_len for x in xs) and s_len % 128 == 0
    lp = s_len + TAIL
    x_all = jnp.concatenate(xs, axis=0)
    b = x_all.shape[0]
    meta = jnp.broadcast_to(meta_tokens[None].astype(F32), (b, N_META, D_MODEL))
    h0 = jnp.concatenate([x_all, jnp.zeros((b, TAIL - N_META, D_MODEL), F32), meta], axis=1)
    h0 = h0.reshape(b * lp, D_MODEL)

    w_in = gdn_w_in[0]
    ba = w_in[:, GDN_CONV_CH + GDN_V:].reshape(D_MODEL, 2, 2, GDN_HEADS)
    ba_dir = [_pad_cols(jnp.concatenate([ba[:, 0, d], ba[:, 1, d]], axis=1), 128) for d in range(2)]
    conv_w8 = jnp.pad(gdn_conv_w[0], ((0, 8 - GDN_CONV), (0, 0)))
    qkv, gates = _gdn_qkv_conv(h0.reshape(b, lp, D_MODEL), ln_g[0][None, :],
                               w_in[:, :GDN_CONV_CH].astype(BF16),
                               jnp.concatenate(ba_dir, axis=1).astype(BF16), conv_w8, lp)

    lane_a = lambda v: jnp.pad(v.astype(F32), ((0, 0), (GDN_HEADS, 128 - 2 * GDN_HEADS)))[:, None, :]
    o_fwd, o_bwd = _gdn_scan(qkv, gates, lane_a(gdn_a_log[0]), lane_a(gdn_dt_bias[0]), lp)
    h1 = _gdn_out(o_fwd.reshape(b * lp, GDN_V), o_bwd.reshape(b * lp, GDN_V), h0, ln_g[0][None, :],
                  w_in[:, GDN_CONV_CH:GDN_CONV_CH + GDN_V].astype(BF16),
                  gdn_o_norm_g[0][None, :], gdn_w_out[0].astype(BF16))

    w_in1 = mla_w_in[0]
    o1 = MLA_Q_LORA
    o2_ = o1 + MLA_KV_LORA
    o3 = o2_ + MLA_ROPE
    w1 = jnp.concatenate([w_in1[:, o3:], w_in1[:, :o1], w_in1[:, o1:o2_],
                          _pad_cols(w_in1[:, o2_:o3], 256)], axis=1).astype(BF16)
    proj1 = _norm_matmul(h1, ln_g[1][None, :], w1, 1024, BF16)
    proj1_3 = proj1.reshape(b, lp, proj1.shape[1])

    pos = jnp.concatenate([jnp.arange(s_len, dtype=F32) + N_META, jnp.zeros((TAIL - N_META,), F32),
                           jnp.arange(N_META, dtype=F32)])
    inv = ROPE_THETA ** (-jnp.arange(0, MLA_ROPE, 2, dtype=F32) / MLA_ROPE)
    ang = pos[:, None] * inv[None, :]
    zc = jnp.zeros_like(ang)
    cos_t = jnp.concatenate([jnp.cos(ang), jnp.cos(ang), zc, zc], axis=1)
    nsin_t = jnp.concatenate([-jnp.sin(ang), zc, zc, zc], axis=1)
    sin_t = jnp.concatenate([zc, jnp.sin(ang), zc, zc], axis=1)

    scale = MLA_DQK ** -0.5 * math.log2(math.e)
    gq = jnp.broadcast_to((mla_qk_q_g[0].astype(F32) * scale)[:, None], (MLA_DQK, 128))
    score_bound = (1.02 * MLA_DQK * scale * jnp.max(jnp.abs(mla_qk_q_g[0].astype(F32)))
                   * jnp.max(jnp.abs(mla_qk_k_g[0].astype(F32))))
    bounded = score_bound < ATT_BOUND_MAX
    shift_rows = jnp.zeros((MLA_DPAD - MLA_DQK, 128), F32).at[0, :].set(jnp.where(bounded, -score_bound, 0.0))
    qt = _mla_q(proj1_3, mla_q_norm_g[0][None, :], mla_w_uq[0].T.astype(BF16), gq[:MLA_NOPE],
                gq[MLA_NOPE:], jnp.cos(ang).T, jnp.sin(ang).T, shift_rows, lp)

    w_ukv = mla_w_ukv[0].reshape(MLA_KV_LORA, MLA_HEADS, MLA_NOPE + MLA_DV)
    w_ukv = jnp.concatenate([w_ukv[:, :, :MLA_NOPE].reshape(MLA_KV_LORA, -1),
                             w_ukv[:, :, MLA_NOPE:].reshape(MLA_KV_LORA, -1)], axis=1).astype(BF16)
    gk = mla_qk_k_g[0].astype(F32)
    k, vt = _mla_kv(proj1_3, mla_kv_norm_g[0][None, :], w_ukv, gk[None, :MLA_NOPE],
                    _lane_row(gk[MLA_NOPE:]), cos_t, nsin_t, sin_t, lp)

    o = _attention(bounded.astype(jnp.int32).reshape(1), qt, k, vt, s_len)

    h1_3 = h1.reshape(b, lp, D_MODEL)
    w_out1 = mla_w_out[0].astype(BF16)
    outs = []
    b0 = 0
    for x in xs:
        outs.append(_mla_out(o, proj1_3, h1_3, w_out1, b0, x.shape[0], s_len))
        b0 += x.shape[0]
    return tuple(outs)


def kernel(x_prompt, x_sample, meta_tokens, ln_g, gdn_w_in, gdn_conv_w, gdn_a_log, gdn_dt_bias,
           gdn_o_norm_g, gdn_w_out, mla_w_in, mla_q_norm_g, mla_kv_norm_g, mla_w_uq, mla_w_ukv,
           mla_qk_q_g, mla_qk_k_g, mla_w_out):
    return _trunk_all((x_prompt, x_sample), meta_tokens, ln_g, gdn_w_in, gdn_conv_w, gdn_a_log,
                      gdn_dt_bias, gdn_o_norm_g, gdn_w_out, mla_w_in, mla_q_norm_g, mla_kv_norm_g,
                      mla_w_uq, mla_w_ukv, mla_qk_q_g, mla_qk_k_g, mla_w_out)
```

```python
import functools
import math

import jax
import jax.numpy as jnp
from jax import lax
from jax.experimental import pallas as pl
from jax.experimental.pallas import tpu as pltpu

F32 = jnp.float32
BF16 = jnp.bfloat16

D_MODEL = 1024
N_META = 16
TAIL = 128
NORM_EPS = 1e-6

GDN_HEADS = 8
GDN_DK = 128
GDN_DV = 256
GDN_CONV = 5
GDN_CHUNK = 64
GDN_QK = GDN_HEADS * GDN_DK
GDN_V = GDN_HEADS * GDN_DV
GDN_CONV_CH = 2 * GDN_QK + GDN_V

MLA_HEADS = 16
MLA_Q_LORA = 512
MLA_KV_LORA = 256
MLA_NOPE = 128
MLA_ROPE = 64
MLA_DQK = MLA_NOPE + MLA_ROPE
MLA_DV = 128
MLA_V = MLA_HEADS * MLA_DV
MLA_DPAD = 256
ROPE_THETA = 10000.0

VMEM_LIMIT = 56 * 1024 * 1024


def _cparams(sem):
    return pltpu.CompilerParams(dimension_semantics=sem, vmem_limit_bytes=VMEM_LIMIT)


def _tile(n, target, mult):
    best = None
    t = mult
    while t <= min(n, target):
        if n % t == 0:
            best = t
        t += mult
    assert best is not None, (n, target, mult)
    return best


def _norm_matmul_kernel(x_ref, g_ref, w_ref, o_ref, xn_ref):
    @pl.when(pl.program_id(1) == 0)
    def _():
        x = x_ref[...]
        ms = jnp.mean(x * x, axis=-1, keepdims=True)
        xn_ref[...] = (x * lax.rsqrt(ms + NORM_EPS) * g_ref[...]).astype(BF16)

    o_ref[...] = jnp.dot(xn_ref[...], w_ref[...], preferred_element_type=F32).astype(o_ref.dtype)


def _norm_matmul(x, g, w, tn, out_dtype=F32):
    rows, d = x.shape
    n = w.shape[1]
    tm = _tile(rows, 1408, 128)
    return pl.pallas_call(
        _norm_matmul_kernel,
        grid=(rows // tm, n // tn),
        in_specs=[
            pl.BlockSpec((tm, d), lambda i, j: (i, 0)),
            pl.BlockSpec((1, d), lambda i, j: (0, 0)),
            pl.BlockSpec((d, tn), lambda i, j: (0, j)),
        ],
        out_specs=pl.BlockSpec((tm, tn), lambda i, j: (i, j)),
        out_shape=jax.ShapeDtypeStruct((rows, n), out_dtype),
        scratch_shapes=[pltpu.VMEM((tm, d), BF16)],
        compiler_params=_cparams(("parallel", "arbitrary")),
        name="norm_matmul",
    )(x, g, w)


CONV_SUB = 64
CONV_LANES = GDN_DK
CONV_TC = 1024


def _qkv_conv_kernel(prev_ref, main_ref, next_ref, g_ref, w_ref, wba_ref, cw_ref, o_ref, ba_ref, ext_ref,
                     *, tr):
    half = GDN_CONV // 2
    nblk = GDN_CONV_CH // CONV_TC
    x = jnp.concatenate([prev_ref[0], main_ref[0], next_ref[0]], axis=0)
    ms = jnp.mean(x * x, axis=-1, keepdims=True)
    xn = (x * lax.rsqrt(ms + NORM_EPS) * g_ref[...]).astype(BF16)
    ba_ref[0] = jnp.dot(xn, wba_ref[...], preferred_element_type=F32)[8:8 + tr]

    def project(j):
        ext_ref[j] = jnp.dot(xn, w_ref[:, j * CONV_TC:(j + 1) * CONV_TC], preferred_element_type=F32)

    def conv_silu(j, sb, lb):
        lanes = slice(j * CONV_TC + lb * CONV_LANES, j * CONV_TC + (lb + 1) * CONV_LANES)
        nrow = CONV_SUB + 16
        x2 = ext_ref[j, sb * CONV_SUB:sb * CONV_SUB + nrow, lb * CONV_LANES:(lb + 1) * CONV_LANES]
        acc = x2[8:8 + CONV_SUB] * cw_ref[half:half + 1, lanes]
        for s in range(1, half + 1):
            down = pltpu.roll(x2, s, 0)
            acc = acc + down[8:8 + CONV_SUB] * cw_ref[half - s:half - s + 1, lanes]
            up = pltpu.roll(x2, nrow - s, 0)
            acc = acc + up[8:8 + CONV_SUB] * cw_ref[half + s:half + s + 1, lanes]
        return acc * jax.nn.sigmoid(acc)

    def finish(j):
        for sb in range(tr // CONV_SUB):
            rows = slice(sb * CONV_SUB, (sb + 1) * CONV_SUB)
            for lb in range(CONV_TC // CONV_LANES):
                y = conv_silu(j, sb, lb)
                lanes = slice(j * CONV_TC + lb * CONV_LANES, j * CONV_TC + (lb + 1) * CONV_LANES)
                if j < 2:
                    qscale = GDN_DK ** -0.5 if j == 0 else 1.0
                    ss = jnp.sum(y * y, axis=-1, keepdims=True)
                    y = y * (lax.rsqrt(ss + NORM_EPS) * qscale)
                o_ref[0, rows, lanes] = y.astype(BF16)

    project(0)
    for j in range(nblk):
        if j + 1 < nblk:
            project(j + 1)
        finish(j)


def _gdn_qkv_conv(h3, g, w_qkv, w_ba, conv_w8, lp):
    b = h3.shape[0]
    nba = w_ba.shape[1]
    tr = _tile(lp, 384, CONV_SUB)
    nb8 = lp // 8
    tb = tr // 8
    kern = functools.partial(_qkv_conv_kernel, tr=tr)
    const = lambda bi, i: (0, 0)
    return pl.pallas_call(
        kern,
        grid=(b, lp // tr),
        in_specs=[
            pl.BlockSpec((1, 8, D_MODEL), lambda bi, i: (bi, (i * tb + nb8 - 1) % nb8, 0)),
            pl.BlockSpec((1, tr, D_MODEL), lambda bi, i: (bi, i, 0)),
            pl.BlockSpec((1, 8, D_MODEL), lambda bi, i: (bi, ((i + 1) * tb) % nb8, 0)),
            pl.BlockSpec((1, D_MODEL), const),
            pl.BlockSpec((D_MODEL, GDN_CONV_CH), const),
            pl.BlockSpec((D_MODEL, nba), const),
            pl.BlockSpec((8, GDN_CONV_CH), const),
        ],
        out_specs=[
            pl.BlockSpec((1, tr, GDN_CONV_CH), lambda bi, i: (bi, i, 0)),
            pl.BlockSpec((1, tr, nba), lambda bi, i: (bi, i, 0)),
        ],
        out_shape=[
            jax.ShapeDtypeStruct((b, lp, GDN_CONV_CH), BF16),
            jax.ShapeDtypeStruct((b, lp, nba), F32),
        ],
        scratch_shapes=[pltpu.VMEM((GDN_CONV_CH // CONV_TC, tr + 16, CONV_TC), F32)],
        compiler_params=_cparams(("parallel", "parallel")),
        name="gdn_qkv_conv",
    )(h3, h3, h3, g, w_qkv, w_ba, conv_w8)


SCAN_NB = 2


def _split3(x):
    hi = x.astype(BF16)
    r1 = x - hi.astype(F32)
    mid = r1.astype(BF16)
    lo = (r1 - mid.astype(F32)).astype(BF16)
    return hi, mid, lo


def _bdot(a, b):
    return jnp.dot(a.astype(BF16), b.astype(BF16), preferred_element_type=F32)


def _gdn_scan_kernel(qf_ref, kf_ref, vf_ref, baf_ref, qb_ref, kb_ref, vb_ref, bab_ref, alog_ref, dtb_ref,
                     of_ref, ob_ref, s_ref, sb_ref, u_ref, qw_ref, attn_ref, kd_ref, eg_ref, *, nc):
    t = pl.program_id(1)
    c = GDN_CHUNK
    c2 = 2 * c
    npair = GDN_HEADS // 2
    o_refs = (of_ref, ob_ref)

    @pl.when(t == 0)
    def _():
        for ref in (s_ref, sb_ref, u_ref, qw_ref, attn_ref, kd_ref, eg_ref):
            ref[...] = jnp.zeros_like(ref)

    nb = of_ref.shape[0]
    nprob = nb * 2 * npair
    applied = [dict(sq=pi // npair, heads=(2 * (pi % npair), 2 * (pi % npair) + 1),
                    qs=[None, None], vnew=[None, None]) for pi in range(nprob)]

    def read_state(pi, hh):
        ap = applied[pi]
        rs = slice(hh * c, (hh + 1) * c)
        qws = jnp.dot(qw_ref[pi, hh], sb_ref[ap["sq"], ap["heads"][hh]], preferred_element_type=F32)
        ap["qs"][hh] = qws[:c]
        ap["vnew"][hh] = (u_ref[pi, rs, :] - qws[c:]).astype(BF16)

    def write_out(pi):
        ap = applied[pi]
        h0, h1 = ap["heads"]
        bi, d = divmod(ap["sq"], 2)
        vnew2 = jnp.concatenate(ap["vnew"], axis=0)
        o2 = jnp.concatenate(ap["qs"], axis=0) + jnp.dot(attn_ref[pi], vnew2, preferred_element_type=F32)
        o_refs[d][bi, :, h0 * GDN_DV:(h0 + 1) * GDN_DV] = o2[:c].astype(BF16)
        o_refs[d][bi, :, h1 * GDN_DV:(h1 + 1) * GDN_DV] = o2[c:].astype(BF16)

    def update_state(pi, hh):
        ap = applied[pi]
        sq, h = ap["sq"], ap["heads"][hh]
        rs = slice(hh * c, (hh + 1) * c)
        a_h = GDN_HEADS + h
        upd = lax.dot_general(kd_ref[pi, rs, :], ap["vnew"][hh], (((0,), (0,)), ((), ())),
                              preferred_element_type=F32)
        s_new = s_ref[sq, h] * eg_ref[sq, :, a_h:a_h + 1] + upd
        s_ref[sq, h] = s_new
        sb_ref[sq, h] = s_new.astype(BF16)

    pairs_hh = [(pi, hh) for pi in range(nprob) for hh in range(2)]
    apply_ops = ([functools.partial(read_state, pi, hh) for pi, hh in pairs_hh]
                 + [functools.partial(write_out, pi) for pi in range(nprob)]
                 + [functools.partial(update_state, pi, hh) for pi, hh in pairs_hh])

    def emit_apply(n):
        for _ in range(min(n, len(apply_ops))):
            apply_ops.pop(0)()

    emit_apply(len(pairs_hh) // 2)
    tp = jnp.minimum(t, nc - 1)
    ri = lax.broadcasted_iota(jnp.int32, (c2, c2), 0)
    ci = lax.broadcasted_iota(jnp.int32, (c2, c2), 1)
    same = (ri >> 6) == (ci >> 6)
    offdiag = ri != ci
    top =lax.broadcasted_iota(jnp.int32, (c2, 1), 0) < c
    left = lax.broadcasted_iota(jnp.int32, (1, c2), 1) < c
    row_id = lax.broadcasted_iota(jnp.int32, (c, 128), 0)

    in_refs = ((qf_ref, kf_ref, vf_ref, baf_ref), (qb_ref, kb_ref, vb_ref, bab_ref))
    seqs = []
    for sq in range(2 * nb):
        bi, d = divmod(sq, 2)
        q_ref, k_ref, v_ref, ba_ref = in_refs[d]
        seq = tp if d == 0 else nc - 1 - tp
        blk = jnp.where(seq < 2, nc - 2 + seq, seq - 2)
        first_valid = jnp.where(blk == nc - 2, c, jnp.where(blk == nc - 1, c - N_META, 0))
        valid = row_id >= first_valid
        ba = ba_ref[bi]
        beta = jnp.where(valid, jax.nn.sigmoid(ba), 0.0)
        xs = ba + dtb_ref[d]
        softplus = jnp.maximum(xs, 0.0) + jnp.log(1.0 + jnp.exp(-jnp.abs(xs)))
        g = jnp.where(valid, -jnp.exp(alog_ref[d]) * softplus, 0.0)
        incl = same & ((ri >= ci) if d == 0 else (ri <= ci))
        tri = jnp.where(incl, 1.0, 0.0).astype(BF16)
        ghi, gmid, glo = _split3(jnp.concatenate([g, g], axis=0))
        gc2 = (jnp.dot(tri, ghi, preferred_element_type=F32)
               + jnp.dot(tri, gmid, preferred_element_type=F32)
               + jnp.dot(tri, glo, preferred_element_type=F32))
        seqs.append(dict(q=q_ref.at[bi], k=k_ref.at[bi], v=v_ref.at[bi], incl=incl, gc2=gc2, gc2t=gc2.T,
                         beta2=jnp.concatenate([beta, beta], axis=0),
                         gtot=jnp.sum(g, axis=0, keepdims=True)))

    def stacked(ref, h0, h1, width):
        return jnp.concatenate([ref[:, h0 * width:(h0 + 1) * width],
                                ref[:, h1 * width:(h1 + 1) * width]], axis=0)

    probs = []
    for dd in seqs:
        for p in range(npair):
            h0, h1 = 2 * p, 2 * p + 1
            a0, a1 = GDN_HEADS + h0, GDN_HEADS + h1
            col = jnp.where(top, dd["gc2"][:, a0:a0 + 1], dd["gc2"][:, a1:a1 + 1])
            row = jnp.where(left, dd["gc2t"][a0:a0 + 1, :], dd["gc2t"][a1:a1 + 1, :])
            bcol = jnp.where(top, dd["beta2"][:, h0:h0 + 1], dd["beta2"][:, h1:h1 + 1])
            tot = jnp.where(top, dd["gtot"][:, a0:a0 + 1], dd["gtot"][:, a1:a1 + 1])
            dec = jnp.exp(jnp.where(dd["incl"], col - row, -jnp.inf))
            kst = stacked(dd["k"], h0, h1, GDN_DK)
            qst = stacked(dd["q"], h0, h1, GDN_DK)
            vst = stacked(dd["v"], h0, h1, GDN_DV)
            kf = kst.astype(F32)
            kb = kf * bcol
            sc = lax.dot_general(jnp.concatenate([qst, kb.astype(BF16)], axis=0), kst,
                                 (((1,), (1,)), ((), ())), preferred_element_type=F32)
            egc = jnp.exp(col)
            probs.append(dict(
                attn=(sc[:c2] * dec).astype(BF16),
                a=jnp.where(offdiag, sc[c2:] * dec, 0.0),
                rhs=jnp.concatenate([vst.astype(F32) * bcol, kb * egc], axis=1).astype(BF16),
                qg=(qst.astype(F32) * egc).astype(BF16),
                kd=(kf * jnp.exp(tot - col)).astype(BF16)))

    emit_apply(len(pairs_hh) // 2)

    for _ in _unit_lower_inverse_staged(probs):
        emit_apply(3 * nb)
    emit_apply(len(apply_ops))

    for pi, pr in enumerate(probs):
        uw = jnp.dot(pr["tinv"].astype(BF16), pr["rhs"], preferred_element_type=F32)
        u_ref[pi] = uw[:, :GDN_DV]
        w = uw[:, GDN_DV:].astype(BF16)
        for hh in range(2):
            rs = slice(hh * c, (hh + 1) * c)
            qw_ref[pi, hh] = jnp.concatenate([pr["qg"][rs], w[rs]], axis=0)
        attn_ref[pi] = pr["attn"]
        kd_ref[pi] = pr["kd"]
    for sq, dd in enumerate(seqs):
        eg_ref[sq] = jnp.exp(dd["gtot"])


def _unit_lower_inverse_staged(probs):
    c = GDN_CHUNK
    nside = 4
    ri = lax.broadcasted_iota(jnp.int32, (c, nside * c), 0)
    ci = lax.broadcasted_iota(jnp.int32, (c, nside * c), 1)
    lane_blk = ci >> 6
    within = ci & (c - 1)
    diag16 = (ri >> 4) == (within >> 4)
    eye = jnp.where(ri == within, 1.0, 0.0).astype(F32)
    left = lax.broadcasted_iota(jnp.int32, (1, 2 * c), 1) < c

    def blockdiag(y):
        return jnp.concatenate([jnp.where(lane_blk == r, y, 0.0) for r in range(nside)],
                               axis=0).astype(BF16)

    def mm(x, ybd):
        return jnp.dot(x.astype(BF16), ybd, preferred_element_type=F32)

    groups = []
    for g0 in range(0, len(probs), 2):
        pa, pb = probs[g0], probs[g0 + 1]
        a = jnp.concatenate([pa["a"][:c] + pa["a"][c:], pb["a"][:c] + pb["a"][c:]], axis=1)
        ad = jnp.where(diag16, a, 0.0)
        groups.append(dict(pairs=(pa, pb), ad=ad, an=blockdiag(a - ad), dinv=eye - ad))
    for gr in groups:
        gr["p"] = mm(gr["ad"], blockdiag(gr["ad"]))
    yield
    for gr in groups:
        pbd = blockdiag(gr["p"])
        gr["dinv"] = gr["dinv"] + mm(gr["dinv"], pbd)
        gr["p"] = mm(gr["p"], pbd)
    yield
    for gr in groups:
        pbd = blockdiag(gr["p"])
        gr["dinv"] = gr["dinv"] + mm(gr["dinv"], pbd)
        gr["p"] = mm(gr["p"], pbd)
    yield
    for gr in groups:
        gr["dinv"] = gr["dinv"] + mm(gr["dinv"], blockdiag(gr["p"]))
    yield
    for gr in groups:
        gr["m"] = mm(gr["dinv"], gr["an"])
    yield
    for gr in groups:
        gr["m2"] = blockdiag(mm(gr["m"], blockdiag(gr["m"])))
    yield
    for gr in groups:
        x = eye - gr["m"]
        gr["x"] = x + mm(x, gr["m2"])
    yield
    for gr in groups:
        t = mm(gr["x"], blockdiag(gr["dinv"]))
        for i, pr in enumerate(gr["pairs"]):
            half = t[:, 2 * c * i:2 * c * (i + 1)]
            pr["tinv"] = jnp.concatenate([jnp.where(left, half, 0.0), jnp.where(left, 0.0, half)],
                                         axis=0)


def _gdn_scan(qkv, proj3, alog_rows, dtb_rows, lp):
    b = qkv.shape[0]
    nc = lp // GDN_CHUNK
    ba_col0 = 0

    def blk_of(seq):
        return jnp.where(seq < 2, nc - 2 + seq, seq - 2)

    def chunk_specs(d):
        prep = lambda t: jnp.minimum(t, nc - 1)
        seq = prep if d == 0 else (lambda t: nc - 1 - prep(t))
        return [
            pl.BlockSpec((nb, GDN_CHUNK, GDN_QK), lambda bi, t: (bi, blk_of(seq(t)), 0)),
            pl.BlockSpec((nb, GDN_CHUNK, GDN_QK), lambda bi, t: (bi, blk_of(seq(t)), 1)),
            pl.BlockSpec((nb, GDN_CHUNK, GDN_V), lambda bi, t: (bi, blk_of(seq(t)), 1)),
            pl.BlockSpec((nb, GDN_CHUNK, 128), lambda bi, t: (bi, blk_of(seq(t)), ba_col0 + d)),
        ]

    nb = SCAN_NB if b % SCAN_NB == 0 else 1
    const = pl.BlockSpec((2, 1, 128), lambda bi, t: (0, 0, 0))
    kern = functools.partial(_gdn_scan_kernel, nc=nc)
    applied = lambda t: jnp.maximum(t - 1, 0)
    nseq = 2 * nb
    nprob = nseq * (GDN_HEADS // 2)
    c2 = 2 * GDN_CHUNK
    return pl.pallas_call(
        kern,
        grid=(b // nb, nc + 1),
        in_specs=chunk_specs(0) + chunk_specs(1) + [const, const],
        out_specs=[
            pl.BlockSpec((nb, GDN_CHUNK, GDN_V), lambda bi, t: (bi, blk_of(applied(t)), 0)),
            pl.BlockSpec((nb, GDN_CHUNK, GDN_V), lambda bi, t: (bi, blk_of(nc - 1 - applied(t)), 0)),
        ],
        out_shape=[jax.ShapeDtypeStruct((b, lp, GDN_V), BF16)] * 2,
        scratch_shapes=[
            pltpu.VMEM((nseq, GDN_HEADS, GDN_DK, GDN_DV), F32),
            pltpu.VMEM((nseq, GDN_HEADS, GDN_DK, GDN_DV), BF16),
            pltpu.VMEM((nprob, c2, GDN_DV), F32),
            pltpu.VMEM((nprob, 2, c2, GDN_DK), BF16),
            pltpu.VMEM((nprob, c2, c2), BF16),
            pltpu.VMEM((nprob, c2, GDN_DK), BF16),
            pltpu.VMEM((nseq, 1, 128), F32),
        ],
        compiler_params=_cparams(("parallel", "arbitrary")),
        name="gdn_scan",
    )(qkv, qkv, qkv, proj3, qkv, qkv, qkv, proj3, alog_rows, dtb_rows)


OUT_SUB = 256


def _gdn_out_kernel(of_ref, ob_ref, h_ref, ln_ref, wz_ref, g_ref, w_ref, o_ref):
    nsub = o_ref.shape[0] // OUT_SUB
    rows = [slice(sb * OUT_SUB, (sb + 1) * OUT_SUB) for sb in range(nsub)]

    def gate_logits(sb):
        x = h_ref[rows[sb], :]
        ms = jnp.mean(x * x, axis=-1, keepdims=True)
        xn = (x * lax.rsqrt(ms + NORM_EPS) * ln_ref[...]).astype(BF16)
        return jnp.dot(xn, wz_ref[...], preferred_element_type=F32)

    z_next = gate_logits(0)
    for sb in range(nsub):
        z = z_next
        if sb + 1 < nsub:
            z_next = gate_logits(sb + 1)
        o = of_ref[rows[sb], :].astype(F32) + ob_ref[rows[sb], :].astype(F32)
        gate = z * jax.nn.sigmoid(z)
        ys = []
        for h in range(GDN_HEADS):
            lanes = slice(h * GDN_DV, (h + 1) * GDN_DV)
            oh = o[:, lanes]
            ms = jnp.mean(oh * oh, axis=-1, keepdims=True)
            ys.append((oh * lax.rsqrt(ms + NORM_EPS) * g_ref[...] * gate[:, lanes]).astype(BF16))
        y = jnp.concatenate(ys, axis=1)
        o_ref[rows[sb], :] = h_ref[rows[sb], :] + jnp.dot(y, w_ref[...], preferred_element_type=F32)


def _gdn_out(o_fwd, o_bwd, h0, ln_row, w_z, g_row, w_out):
    rows = h0.shape[0]
    tm = _tile(rows, 768, OUT_SUB)
    const = lambda i: (0, 0)
    return pl.pallas_call(
        _gdn_out_kernel,
        grid=(rows // tm,),
        in_specs=[
            pl.BlockSpec((tm, GDN_V), lambda i: (i, 0)),
            pl.BlockSpec((tm, GDN_V), lambda i: (i, 0)),
            pl.BlockSpec((tm, D_MODEL), lambda i: (i, 0)),
            pl.BlockSpec((1, D_MODEL), const),
            pl.BlockSpec((D_MODEL, GDN_V), const),
            pl.BlockSpec((1, GDN_DV), const),
            pl.BlockSpec((GDN_V, D_MODEL), const),
        ],
        out_specs=pl.BlockSpec((tm, D_MODEL), lambda i: (i, 0)),
        out_shape=jax.ShapeDtypeStruct((rows, D_MODEL), F32),
        compiler_params=_cparams(("parallel",)),
        name="gdn_out",
    )(o_fwd, o_bwd, h0, ln_row, w_z, g_row, w_out)


def _rope(r, cos, nsin_lo, sin_hi):
    return r * cos + pltpu.roll(r, 96, 1) * nsin_lo + pltpu.roll(r, 32, 1) * sin_hi


def _mla_q_kernel(cq_ref, g1_ref, wt_ref, ga_ref, gr_ref, cos_ref, sin_ref, shift_ref, qt_ref):
    cq = cq_ref[0].astype(F32)
    tm = cq.shape[0]
    ms = jnp.mean(cq * cq, axis=-1, keepdims=True)
    cqt = (cq * lax.rsqrt(ms + NORM_EPS) * g1_ref[...]).T.astype(BF16)
    q = jnp.dot(wt_ref[...], cqt, preferred_element_type=F32)
    half = MLA_ROPE // 2
    tile = lambda g: jnp.concatenate([g] * (tm // 128), axis=1)
    ga = tile(ga_ref[...])
    gr1, gr2 = tile(gr_ref[0:half, :]), tile(gr_ref[half:MLA_ROPE, :])
    cos, sin = cos_ref[...], sin_ref[...]
    pad_rows = tile(shift_ref[...]).astype(BF16)
    for h in range(MLA_HEADS):
        r0 = h * MLA_DQK
        a = q[r0:r0 + MLA_NOPE]
        x1 = q[r0 + MLA_NOPE:r0 + MLA_NOPE + half]
        x2 = q[r0 + MLA_NOPE + half:r0 + MLA_DQK]
        ss = (jnp.sum(a * a, axis=0, keepdims=True) + jnp.sum(x1 * x1, axis=0, keepdims=True)
              + jnp.sum(x2 * x2, axis=0, keepdims=True))
        inv = lax.rsqrt(ss * (1.0 / MLA_DQK) + NORM_EPS)
        x1 = x1 * inv * gr1
        x2 = x2 * inv * gr2
        qt_ref[0, h, 0:MLA_NOPE, :] = (a * inv * ga).astype(BF16)
        qt_ref[0, h, MLA_NOPE:MLA_NOPE + half, :] = (x1 * cos - x2 * sin).astype(BF16)
        qt_ref[0, h, MLA_NOPE + half:MLA_DQK, :] = (x2 * cos + x1 * sin).astype(BF16)
        qt_ref[0, h, MLA_DQK:MLA_DPAD, :] = pad_rows


def _mla_q(proj3, g1, w_uq_t, ga, gr, cos_t, sin_t, shift_rows, lp):
    b = proj3.shape[0]
    tm = _tile(lp, 384, 128)
    cqcol = MLA_V // MLA_Q_LORA
    const = lambda bi, i: (0, 0)
    return pl.pallas_call(
        _mla_q_kernel,
        grid=(b, lp // tm),
        in_specs=[
            pl.BlockSpec((1, tm, MLA_Q_LORA), lambda bi, i: (bi, i, cqcol)),
            pl.BlockSpec((1, MLA_Q_LORA), const),
            pl.BlockSpec((MLA_HEADS * MLA_DQK, MLA_Q_LORA), const),
            pl.BlockSpec((MLA_NOPE, 128), const),
            pl.BlockSpec((MLA_ROPE, 128), const),
            pl.BlockSpec((MLA_ROPE // 2, tm), lambda bi, i: (0, i)),
            pl.BlockSpec((MLA_ROPE // 2, tm), lambda bi, i: (0, i)),
            pl.BlockSpec((MLA_DPAD - MLA_DQK, 128), const),
        ],
        out_specs=pl.BlockSpec((1, MLA_HEADS, MLA_DPAD, tm), lambda bi, i: (bi, 0, 0, i)),
        out_shape=jax.ShapeDtypeStruct((b, MLA_HEADS, MLA_DPAD, lp), BF16),
        compiler_params=_cparams(("parallel", "parallel")),
        name="mla_q",
    )(proj3, g1, w_uq_t, ga, gr, cos_t, sin_t, shift_rows)


def _mla_kv_kernel(ckv_ref, kpe_ref, g1_ref, w_ref, ga_ref, gr_ref, cos_ref, nsin_ref, sin_ref,
                   k_ref, vt_ref):
    ckv = ckv_ref[0].astype(F32)
    ms = jnp.mean(ckv * ckv, axis=-1, keepdims=True)
    cn = (ckv * lax.rsqrt(ms + NORM_EPS) * g1_ref[...]).astype(BF16)
    kv = jnp.dot(cn, w_ref[...], preferred_element_type=F32)
    kpe = kpe_ref[0].astype(F32)
    sq_pe = kpe * kpe
    kr = _rope(kpe * gr_ref[...], cos_ref[...], nsin_ref[...], sin_ref[...])
    one_hot = jnp.where(lax.broadcasted_iota(jnp.int32, (1, 128), 1) == MLA_ROPE, 1.0, 0.0).astype(F32)
    for h in range(MLA_HEADS):
        kn = kv[:, h * MLA_NOPE:(h + 1) * MLA_NOPE]
        ss = jnp.sum(kn * kn + sq_pe, axis=-1, keepdims=True)
        inv = lax.rsqrt(ss * (1.0 / MLA_DQK) + NORM_EPS)
        k_ref[0, h, :, 0:MLA_NOPE] = (kn * inv * ga_ref[...]).astype(BF16)
        k_ref[0, h, :, MLA_NOPE:MLA_DPAD] = (kr * inv + one_hot).astype(BF16)
        v = kv[:, MLA_HEADS * MLA_NOPE + h * MLA_DV:MLA_HEADS * MLA_NOPE + (h + 1) * MLA_DV]
        vt_ref[0, h] = v.T.astype(BF16)


def _mla_kv(proj3, g1, w_ukv, ga, gr, cos, nsin, sin, lp):
    b = proj3.shape[0]
    tm = _tile(lp, 384, 128)
    ckvcol = (MLA_V + MLA_Q_LORA) // MLA_KV_LORA
    kpecol = (MLA_V + MLA_Q_LORA + MLA_KV_LORA) // 128
    const = lambda bi, i: (0, 0)
    return pl.pallas_call(
        _mla_kv_kernel,
        grid=(b, lp // tm),
        in_specs=[
            pl.BlockSpec((1, tm, MLA_KV_LORA), lambda bi, i: (bi, i, ckvcol)),
            pl.BlockSpec((1, tm, 128), lambda bi, i: (bi, i, kpecol)),
            pl.BlockSpec((1, MLA_KV_LORA), const),
            pl.BlockSpec((MLA_KV_LORA, MLA_HEADS * (MLA_NOPE + MLA_DV)), const),
            pl.BlockSpec((1, 128), const),
            pl.BlockSpec((1, 128), const),
            pl.BlockSpec((tm, 128), lambda bi, i: (i, 0)),
            pl.BlockSpec((tm, 128), lambda bi, i: (i, 0)),
            pl.BlockSpec((tm, 128), lambda bi, i: (i, 0)),
        ],
        out_specs=[
            pl.BlockSpec((1, MLA_HEADS, tm, MLA_DPAD), lambda bi, i: (bi, 0, i, 0)),
            pl.BlockSpec((1, MLA_HEADS, MLA_DV, tm), lambda bi, i: (bi, 0, 0, i)),
        ],
        out_shape=[
            jax.ShapeDtypeStruct((b, MLA_HEADS, lp, MLA_DPAD), BF16),
            jax.ShapeDtypeStruct((b, MLA_HEADS, MLA_DV, lp), BF16),
        ],
        compiler_params=_cparams(("parallel", "parallel")),
        name="mla_kv",
    )(proj3, proj3, g1, w_ukv, ga, gr, cos, nsin, sin)


ATT_TK = 1024
ATT_TQ = 512
ATT_AHEAD = 2
ATT_BOUND_MAX = 60.0


def _attn_kernel(bounded_ref, qt_ref, k_ref, vt_ref, o_ref, *, s_len):
    tq = qt_ref.shape[3]
    nstrip = tq // ATT_TQ
    chunks = [(ck * ATT_TK, ATT_TK) for ck in range(s_len // ATT_TK)] + [(s_len, TAIL)]
    units = [(i, s) for i in range(len(chunks)) for s in range(nstrip)]
    is_meta = lax.broadcasted_iota(jnp.int32, (TAIL, 1), 0) >= TAIL - N_META
    tail_bias = jnp.where(is_meta, 0.0, -jnp.inf).astype(F32)

    def scores(u):
        i, s = units[u]
        k0, nk = chunks[i]
        st = jnp.dot(k_ref[0, 0, k0:k0 + nk, :], qt_ref[0, 0, :, s * ATT_TQ:(s + 1) * ATT_TQ],
                     preferred_element_type=F32)
        return st + tail_bias if i == len(chunks) - 1 else st

    def run(bounded):
        m = [jnp.full((1, ATT_TQ), -jnp.inf, F32) for _ in range(nstrip)]
        l = [jnp.zeros((1, ATT_TQ), F32) for _ in range(nstrip)]
        acc = [jnp.zeros((MLA_DV, ATT_TQ), F32) for _ in range(nstrip)]
        pending = [scores(u) for u in range(min(ATT_AHEAD, len(units)))]
        for u, (i, s) in enumerate(units):
            if u + ATT_AHEAD < len(units):
                pending.append(scores(u + ATT_AHEAD))
            st = pending.pop(0)
            k0, nk = chunks[i]
            if bounded:
                p = jnp.exp2(st)
                l[s] = l[s] + jnp.sum(p, axis=0, keepdims=True)
                acc[s] = acc[s] + jnp.dot(vt_ref[0, 0, :, k0:k0 + nk], p.astype(BF16),
                                          preferred_element_type=F32)
            else:
                m_new = jnp.maximum(m[s], jnp.max(st, axis=0, keepdims=True))
                alpha = jnp.exp2(m[s] - m_new)
                p = jnp.exp2(st - m_new)
                l[s] = alpha * l[s] + jnp.sum(p, axis=0, keepdims=True)
                acc[s] = alpha * acc[s] + jnp.dot(vt_ref[0, 0, :, k0:k0 + nk], p.astype(BF16),
                                                  preferred_element_type=F32)
                m[s] = m_new
        for s in range(nstrip):
            o_ref[0, s * ATT_TQ:(s + 1) * ATT_TQ, :] = (acc[s] / l[s]).T.astype(o_ref.dtype)

    @pl.when(bounded_ref[0] == 1)
    def _():
        run(True)

    @pl.when(bounded_ref[0] != 1)
    def _():
        run(False)


def _attention(bounded, qt, k, vt, s_len):
    b = qt.shape[0]
    lp = k.shape[2]
    assert s_len % ATT_TK == 0 and lp == s_len + TAIL
    tq = _tile(s_len, 1024, ATT_TQ)
    kern = functools.partial(_attn_kernel, s_len=s_len)
    return pl.pallas_call(
        kern,
        grid=(b, MLA_HEADS, s_len // tq),
        in_specs=[
            pl.BlockSpec(memory_space=pltpu.SMEM),
            pl.BlockSpec((1, 1, MLA_DPAD, tq), lambda bi, h, i: (bi, h, 0, i)),
            pl.BlockSpec((1, 1, lp, MLA_DPAD), lambda bi, h, i: (bi, h, 0, 0)),
            pl.BlockSpec((1, 1, MLA_DV, lp), lambda bi, h, i: (bi, h, 0, 0)),
        ],
        out_specs=pl.BlockSpec((1, tq, MLA_DV), lambda bi, h, i: (bi, i, h)),
        out_shape=jax.ShapeDtypeStruct((b, s_len, MLA_V), BF16),
        compiler_params=_cparams(("parallel", "parallel", "arbitrary")),
        name="mla_attention",
    )(bounded, qt, k, vt)


def _mla_out_kernel(o_ref, z_ref, h_ref, w_ref, y_ref):
    z = z_ref[0].astype(F32)
    y = (o_ref[0].astype(F32) * (z * jax.nn.sigmoid(z))).astype(BF16)
    y_ref[0] = h_ref[0] + jnp.dot(y, w_ref[...], preferred_element_type=F32)


def _mla_out(o, proj3, h3, w_out, b0, nb, s_len):
    tm = _tile(s_len, 512, 128)
    return pl.pallas_call(
        _mla_out_kernel,
        grid=(nb, s_len // tm),
        in_specs=[
            pl.BlockSpec((1, tm, MLA_V), lambda bi, i: (bi + b0, i, 0)),
            pl.BlockSpec((1, tm, MLA_V), lambda bi, i: (bi + b0, i, 0)),
            pl.BlockSpec((1, tm, D_MODEL), lambda bi, i: (bi + b0, i, 0)),
            pl.BlockSpec((MLA_V, D_MODEL), lambda bi, i: (0, 0)),
        ],
        out_specs=pl.BlockSpec((1, tm, D_MODEL), lambda bi, i: (bi, i, 0)),
        out_shape=jax.ShapeDtypeStruct((nb, s_len, D_MODEL), F32),
        compiler_params=_cparams(("parallel", "parallel")),
        name="mla_out",
    )(o, proj3, h3, w_out)


def _pad_cols(w, n):
    return jnp.pad(w, ((0, 0), (0, n - w.shape[1])))


def _lane_row(v, n=128):
    return jnp.pad(v.astype(F32), (0, n - v.shape[0]))[None, :]


def _trunk_all(xs, meta_tokens, ln_g, gdn_w_in, gdn_conv_w, gdn_a_log, gdn_dt_bias, gdn_o_norm_g,
               gdn_w_out, mla_w_in, mla_q_norm_g, mla_kv_norm_g, mla_w_uq, mla_w_ukv, mla_qk_q_g,
               mla_qk_k_g, mla_w_out):
    s_len = xs[0].shape[1]
    assert all(x.shape[1] == s_len for x in xs) and s_len % 128 == 0
    lp = s_len + TAIL
    x_all = jnp.concatenate(xs, axis=0)
    b = x_all.shape[0]
    meta = jnp.broadcast_to(meta_tokens[None].astype(F32), (b, N_META, D_MODEL))
    h0 = jnp.concatenate([x_all, jnp.zeros((b, TAIL - N_META, D_MODEL), F32), meta], axis=1)
    h0 = h0.reshape(b * lp, D_MODEL)

    w_in = gdn_w_in[0]
    ba = w_in[:, GDN_CONV_CH + GDN_V:].reshape(D_MODEL, 2, 2, GDN_HEADS)
    ba_dir = [_pad_cols(jnp.concatenate([ba[:, 0, d], ba[:, 1, d]], axis=1), 128) for d in range(2)]
    conv_w8 = jnp.pad(gdn_conv_w[0], ((0, 8 - GDN_CONV), (0, 0)))
    qkv, gates = _gdn_qkv_conv(h0.reshape(b, lp, D_MODEL), ln_g[0][None, :],
                               w_in[:, :GDN_CONV_CH].astype(BF16),
                               jnp.concatenate(ba_dir, axis=1).astype(BF16), conv_w8, lp)

    lane_a = lambda v: jnp.pad(v.astype(F32), ((0, 0), (GDN_HEADS, 128 - 2 * GDN_HEADS)))[:, None, :]
    o_fwd, o_bwd = _gdn_scan(qkv, gates, lane_a(gdn_a_log[0]), lane_a(gdn_dt_bias[0]), lp)
    h1 = _gdn_out(o_fwd.reshape(b * lp, GDN_V), o_bwd.reshape(b * lp, GDN_V), h0, ln_g[0][None, :],
                  w_in[:, GDN_CONV_CH:GDN_CONV_CH + GDN_V].astype(BF16),
                  gdn_o_norm_g[0][None, :], gdn_w_out[0].astype(BF16))

    w_in1 = mla_w_in[0]
    o1 = MLA_Q_LORA
    o2_ = o1 + MLA_KV_LORA
    o3 = o2_ + MLA_ROPE
    w1 = jnp.concatenate([w_in1[:, o3:], w_in1[:, :o1], w_in1[:, o1:o2_],
                          _pad_cols(w_in1[:, o2_:o3], 256)], axis=1).astype(BF16)
    proj1 = _norm_matmul(h1, ln_g[1][None, :], w1, 1536, BF16)
    proj1_3 = proj1.reshape(b, lp, proj1.shape[1])

    pos = jnp.concatenate([jnp.arange(s_len, dtype=F32) + N_META, jnp.zeros((TAIL - N_META,), F32),
                           jnp.arange(N_META, dtype=F32)])
    inv = ROPE_THETA ** (-jnp.arange(0, MLA_ROPE, 2, dtype=F32) / MLA_ROPE)
    ang = pos[:, None] * inv[None, :]
    zc = jnp.zeros_like(ang)
    cos_t = jnp.concatenate([jnp.cos(ang), jnp.cos(ang), zc, zc], axis=1)
    nsin_t = jnp.concatenate([-jnp.sin(ang), zc, zc, zc], axis=1)
    sin_t = jnp.concatenate([zc, jnp.sin(ang), zc, zc], axis=1)

    scale = MLA_DQK ** -0.5 * math.log2(math.e)
    gq = jnp.broadcast_to((mla_qk_q_g[0].astype(F32) * scale)[:, None], (MLA_DQK, 128))
    score_bound = (1.02 * MLA_DQK * scale * jnp.max(jnp.abs(mla_qk_q_g[0].astype(F32)))
                   * jnp.max(jnp.abs(mla_qk_k_g[0].astype(F32))))
    bounded = score_bound < ATT_BOUND_MAX
    shift_rows = jnp.zeros((MLA_DPAD - MLA_DQK, 128), F32).at[0, :].set(jnp.where(bounded, -score_bound, 0.0))
    qt = _mla_q(proj1_3, mla_q_norm_g[0][None, :], mla_w_uq[0].T.astype(BF16), gq[:MLA_NOPE],
                gq[MLA_NOPE:], jnp.cos(ang).T, jnp.sin(ang).T, shift_rows, lp)

    w_ukv = mla_w_ukv[0].reshape(MLA_KV_LORA, MLA_HEADS, MLA_NOPE + MLA_DV)
    w_ukv = jnp.concatenate([w_ukv[:, :, :MLA_NOPE].reshape(MLA_KV_LORA, -1),
                             w_ukv[:, :, MLA_NOPE:].reshape(MLA_KV_LORA, -1)], axis=1).astype(BF16)
    gk = mla_qk_k_g[0].astype(F32)
    k, vt = _mla_kv(proj1_3, mla_kv_norm_g[0][None, :], w_ukv, gk[None, :MLA_NOPE],
                    _lane_row(gk[MLA_NOPE:]), cos_t, nsin_t, sin_t, lp)

    o = _attention(bounded.astype(jnp.int32).reshape(1), qt, k, vt, s_len)

    h1_3 = h1.reshape(b, lp, D_MODEL)
    w_out1 = mla_w_out[0].astype(BF16)
    outs = []
    b0 = 0
    for x in xs:
        outs.append(_mla_out(o, proj1_3, h1_3, w_out1, b0, x.shape[0], s_len))
        b0 += x.shape[0]
    return tuple(outs)


def kernel(x_prompt, x_sample, meta_tokens, ln_g, gdn_w_in, gdn_conv_w, gdn_a_log, gdn_dt_bias,
           gdn_o_norm_g, gdn_w_out, mla_w_in, mla_q_norm_g, mla_kv_norm_g, mla_w_uq, mla_w_ukv,
           mla_qk_q_g, mla_qk_k_g, mla_w_out):
    return _trunk_all((x_prompt, x_sample), meta_tokens, ln_g, gdn_w_in, gdn_conv_w, gdn_a_log,
                      gdn_dt_bias, gdn_o_norm_g, gdn_w_out, mla_w_in, mla_q_norm_g, mla_kv_norm_g,
                      mla_w_uq, mla_w_ukv, mla_qk_q_g, mla_qk_k_g, mla_w_out)
```

```python
import functools
import math

import jax
import jax.numpy as jnp
from jax import lax
from jax.experimental import pallas as pl
from jax.experimental.pallas import tpu as pltpu

F32 = jnp.float32
BF16 = jnp.bfloat16

D_MODEL = 1024
N_META = 16
TAIL = 128
NORM_EPS = 1e-6

GDN_HEADS = 8
GDN_DK = 128
GDN_DV = 256
GDN_CONV = 5
GDN_CHUNK = 64
GDN_QK = GDN_HEADS * GDN_DK
GDN_V = GDN_HEADS * GDN_DV
GDN_CONV_CH = 2 * GDN_QK + GDN_V

MLA_HEADS = 16
MLA_Q_LORA = 512
MLA_KV_LORA = 256
MLA_NOPE = 128
MLA_ROPE = 64
MLA_DQK = MLA_NOPE + MLA_ROPE
MLA_DV = 128
MLA_V = MLA_HEADS * MLA_DV
MLA_DPAD = 256
ROPE_THETA = 10000.0

VMEM_LIMIT = 56 * 1024 * 1024


def _cparams(sem):
    return pltpu.CompilerParams(dimension_semantics=sem, vmem_limit_bytes=VMEM_LIMIT)


def _tile(n, target, mult):
    best = None
    t = mult
    while t <= min(n, target):
        if n % t == 0:
            best = t
        t += mult
    assert best is not None, (n, target, mult)
    return best


def _norm_matmul_kernel(x_ref, g_ref, w_ref, o_ref, xn_ref):
    @pl.when(pl.program_id(1) == 0)
    def _():
        x = x_ref[...]
        ms = jnp.mean(x * x, axis=-1, keepdims=True)
        xn_ref[...] = (x * lax.rsqrt(ms + NORM_EPS) * g_ref[...]).astype(BF16)

    o_ref[...] = jnp.dot(xn_ref[...], w_ref[...], preferred_element_type=F32).astype(o_ref.dtype)


def _norm_matmul(x, g, w, tn, out_dtype=F32):
    rows, d = x.shape
    n = w.shape[1]
    tm = _tile(rows, 1408, 128)
    return pl.pallas_call(
        _norm_matmul_kernel,
        grid=(rows // tm, n // tn),
        in_specs=[
            pl.BlockSpec((tm, d), lambda i, j: (i, 0)),
            pl.BlockSpec((1, d), lambda i, j: (0, 0)),
            pl.BlockSpec((d, tn), lambda i, j: (0, j)),
        ],
        out_specs=pl.BlockSpec((tm, tn), lambda i, j: (i, j)),
        out_shape=jax.ShapeDtypeStruct((rows, n), out_dtype),
        scratch_shapes=[pltpu.VMEM((tm, d), BF16)],
        compiler_params=_cparams(("parallel", "arbitrary")),
        name="norm_matmul",
    )(x, g, w)


CONV_SUB = 64
CONV_LANES = GDN_DK
CONV_TC = 1024


def _qkv_conv_kernel(prev_ref, main_ref, next_ref, g_ref, w_ref, wba_ref, cw_ref, o_ref, ba_ref, ext_ref,
                     *, tr):
    half = GDN_CONV // 2
    nblk = GDN_CONV_CH // CONV_TC
    x = jnp.concatenate([prev_ref[0], main_ref[0], next_ref[0]], axis=0)
    ms = jnp.mean(x * x, axis=-1, keepdims=True)
    xn = (x * lax.rsqrt(ms + NORM_EPS) * g_ref[...]).astype(BF16)
    ba_ref[0] = jnp.dot(xn, wba_ref[...], preferred_element_type=F32)[8:8 + tr]

    def project(j):
        ext_ref[j] = jnp.dot(xn, w_ref[:, j * CONV_TC:(j + 1) * CONV_TC], preferred_element_type=F32)

    def conv_silu(j, sb, lb):
        lanes = slice(j * CONV_TC + lb * CONV_LANES, j * CONV_TC + (lb + 1) * CONV_LANES)
        nrow = CONV_SUB + 16
        x2 = ext_ref[j, sb * CONV_SUB:sb * CONV_SUB + nrow, lb * CONV_LANES:(lb + 1) * CONV_LANES]
        acc = x2[8:8 + CONV_SUB] * cw_ref[half:half + 1, lanes]
        for s in range(1, half + 1):
            down = pltpu.roll(x2, s, 0)
            acc = acc + down[8:8 + CONV_SUB] * cw_ref[half - s:half - s + 1, lanes]
            up = pltpu.roll(x2, nrow - s, 0)
            acc = acc + up[8:8 + CONV_SUB] * cw_ref[half + s:half + s + 1, lanes]
        return acc * jax.nn.sigmoid(acc)

    def finish(j):
        for sb in range(tr // CONV_SUB):
            rows = slice(sb * CONV_SUB, (sb + 1) * CONV_SUB)
            for lb in range(CONV_TC // CONV_LANES):
                y = conv_silu(j, sb, lb)
                lanes = slice(j * CONV_TC + lb * CONV_LANES, j * CONV_TC + (lb + 1) * CONV_LANES)
                if j < 2:
                    qscale = GDN_DK ** -0.5 if j == 0 else 1.0
                    ss = jnp.sum(y * y, axis=-1, keepdims=True)
                    y = y * (lax.rsqrt(ss + NORM_EPS) * qscale)
                o_ref[0, rows, lanes] = y.astype(BF16)

    project(0)
    for j in range(nblk):
        if j + 1 < nblk:
            project(j + 1)
        finish(j)


def _gdn_qkv_conv(h3, g, w_qkv, w_ba, conv_w8, lp):
    b = h3.shape[0]
    nba = w_ba.shape[1]
    tr = _tile(lp, 384, CONV_SUB)
    nb8 = lp // 8
    tb = tr // 8
    kern = functools.partial(_qkv_conv_kernel, tr=tr)
    const = lambda bi, i: (0, 0)
    return pl.pallas_call(
        kern,
        grid=(b, lp // tr),
        in_specs=[
            pl.BlockSpec((1, 8, D_MODEL), lambda bi, i: (bi, (i * tb + nb8 - 1) % nb8, 0)),
            pl.BlockSpec((1, tr, D_MODEL), lambda bi, i: (bi, i, 0)),
            pl.BlockSpec((1, 8, D_MODEL), lambda bi, i: (bi, ((i + 1) * tb) % nb8, 0)),
            pl.BlockSpec((1, D_MODEL), const),
            pl.BlockSpec((D_MODEL, GDN_CONV_CH), const),
            pl.BlockSpec((D_MODEL, nba), const),
            pl.BlockSpec((8, GDN_CONV_CH), const),
        ],
        out_specs=[
            pl.BlockSpec((1, tr, GDN_CONV_CH), lambda bi, i: (bi, i, 0)),
            pl.BlockSpec((1, tr, nba), lambda bi, i: (bi, i, 0)),
        ],
        out_shape=[
            jax.ShapeDtypeStruct((b, lp, GDN_CONV_CH), BF16),
            jax.ShapeDtypeStruct((b, lp, nba), F32),
        ],
        scratch_shapes=[pltpu.VMEM((GDN_CONV_CH // CONV_TC, tr + 16, CONV_TC), F32)],
        compiler_params=_cparams(("parallel", "parallel")),
        name="gdn_qkv_conv",
    )(h3, h3, h3, g, w_qkv, w_ba, conv_w8)


SCAN_NB = 2


def _split3(x):
    hi = x.astype(BF16)
    r1 = x - hi.astype(F32)
    mid = r1.astype(BF16)
    lo = (r1 - mid.astype(F32)).astype(BF16)
    return hi, mid, lo


def _gdn_scan_kernel(qf_ref, kf_ref, vf_ref, baf_ref, qb_ref, kb_ref, vb_ref, bab_ref, alog_ref, dtb_ref,
                     of_ref, ob_ref, s_ref, sb_ref, u_ref, qw_ref, attn_ref, kd_ref, eg_ref, *, nc):
    t = pl.program_id(1)
    c = GDN_CHUNK
    c2 = 2 * c
    npair = GDN_HEADS // 2
    o_refs = (of_ref, ob_ref)

    @pl.when(t == 0)
    def _():
        for ref in (s_ref, sb_ref, u_ref, qw_ref, attn_ref, kd_ref, eg_ref):
            ref[...] = jnp.zeros_like(ref)

    nb = of_ref.shape[0]
    nprob = nb * 2 * npair
    applied = [dict(sq=pi // npair, heads=(2 * (pi % npair), 2 * (pi % npair) + 1),
                    qs=[None, None], vnew=[None, None]) for pi in range(nprob)]

    def read_state(pi, hh):
        ap = applied[pi]
        rs = slice(hh * c, (hh + 1) * c)
        qws = jnp.dot(qw_ref[pi, hh], sb_ref[ap["sq"], ap["heads"][hh]], preferred_element_type=F32)
        ap["qs"][hh] = qws[:c]
        ap["vnew"][hh] = (u_ref[pi, rs, :] - qws[c:]).astype(BF16)

    def write_out(pi):
        ap = applied[pi]
        h0, h1 = ap["heads"]
        bi, d = divmod(ap["sq"], 2)
        vnew2 = jnp.concatenate(ap["vnew"], axis=0)
        o2 = jnp.concatenate(ap["qs"], axis=0) + jnp.dot(attn_ref[pi], vnew2, preferred_element_type=F32)
        o_refs[d][bi, :, h0 * GDN_DV:(h0 + 1) * GDN_DV] = o2[:c].astype(BF16)
        o_refs[d][bi, :, h1 * GDN_DV:(h1 + 1) * GDN_DV] = o2[c:].astype(BF16)

    def update_state(pi, hh):
        ap = applied[pi]
        sq, h = ap["sq"], ap["heads"][hh]
        rs = slice(hh * c, (hh + 1) * c)
        a_h = GDN_HEADS + h
        upd = lax.dot_general(kd_ref[pi, rs, :], ap["vnew"][hh], (((0,), (0,)), ((), ())),
                              preferred_element_type=F32)
        s_new = s_ref[sq, h] * eg_ref[sq, :, a_h:a_h + 1] + upd
        s_ref[sq, h] = s_new
        sb_ref[sq, h] = s_new.astype(BF16)

    pairs_hh = [(pi, hh) for pi in range(nprob) for hh in range(2)]
    apply_ops = ([functools.partial(read_state, pi, hh) for pi, hh in pairs_hh]
                 + [functools.partial(write_out, pi) for pi in range(nprob)]
                 + [functools.partial(update_state, pi, hh) for pi, hh in pairs_hh])

    def emit_apply(n):
        for _ in range(min(n, len(apply_ops))):
            apply_ops.pop(0)()

    emit_apply(len(pairs_hh) // 2)
    tp = jnp.minimum(t, nc - 1)
    ri = lax.broadcasted_iota(jnp.int32, (c2, c2), 0)
    ci = lax.broadcasted_iota(jnp.int32, (c2, c2), 1)
    same = (ri >> 6) == (ci >> 6)
    offdiag = ri != ci
    top =lax.broadcasted_iota(jnp.int32, (c2, 1), 0) < c
    left = lax.broadcasted_iota(jnp.int32, (1, c2), 1) < c
    row_id = lax.broadcasted_iota(jnp.int32, (c, 128), 0)

    in_refs = ((qf_ref, kf_ref, vf_ref, baf_ref), (qb_ref, kb_ref, vb_ref, bab_ref))
    seqs = []
    for sq in range(2 * nb):
        bi, d = divmod(sq, 2)
        q_ref, k_ref, v_ref, ba_ref = in_refs[d]
        seq = tp if d == 0 else nc - 1 - tp
        blk = jnp.where(seq < 2, nc - 2 + seq, seq - 2)
        first_valid = jnp.where(blk == nc - 2, c, jnp.where(blk == nc - 1, c - N_META, 0))
        valid = row_id >= first_valid
        ba = ba_ref[bi]
        beta = jnp.where(valid, jax.nn.sigmoid(ba), 0.0)
        xs = ba + dtb_ref[d]
        softplus = jnp.maximum(xs, 0.0) + jnp.log(1.0 + jnp.exp(-jnp.abs(xs)))
        g = jnp.where(valid, -jnp.exp(alog_ref[d]) * softplus, 0.0)
        incl = same & ((ri >= ci) if d == 0 else (ri <= ci))
        tri = jnp.where(incl, 1.0, 0.0).astype(BF16)
        ghi, gmid, glo = _split3(jnp.concatenate([g, g], axis=0))
        gc2 = (jnp.dot(tri, ghi, preferred_element_type=F32)
               + jnp.dot(tri, gmid, preferred_element_type=F32)
               + jnp.dot(tri, glo, preferred_element_type=F32))
        seqs.append(dict(q=q_ref.at[bi], k=k_ref.at[bi], v=v_ref.at[bi], incl=incl, gc2=gc2, gc2t=gc2.T,
                         beta2=jnp.concatenate([beta, beta], axis=0),
                         gtot=jnp.sum(g, axis=0, keepdims=True)))

    def stacked(ref, h0, h1, width):
        return jnp.concatenate([ref[:, h0 * width:(h0 + 1) * width],
                                ref[:, h1 * width:(h1 + 1) * width]], axis=0)

    probs = []
    for dd in seqs:
        for p in range(npair):
            h0, h1 = 2 * p, 2 * p + 1
            a0, a1 = GDN_HEADS + h0, GDN_HEADS + h1
            col = jnp.where(top, dd["gc2"][:, a0:a0 + 1], dd["gc2"][:, a1:a1 + 1])
            row = jnp.where(left, dd["gc2t"][a0:a0 + 1, :], dd["gc2t"][a1:a1 + 1, :])
            bcol = jnp.where(top, dd["beta2"][:, h0:h0 + 1], dd["beta2"][:, h1:h1 + 1])
            tot = jnp.where(top, dd["gtot"][:, a0:a0 + 1], dd["gtot"][:, a1:a1 + 1])
            dec = jnp.exp(jnp.where(dd["incl"], col - row, -jnp.inf))
            kst = stacked(dd["k"], h0, h1, GDN_DK)
            qst = stacked(dd["q"], h0, h1, GDN_DK)
            vst = stacked(dd["v"], h0, h1, GDN_DV)
            kf = kst.astype(F32)
            kb = kf * bcol
            sc = lax.dot_general(jnp.concatenate([qst, kb.astype(BF16)], axis=0), kst,
                                 (((1,), (1,)), ((), ())), preferred_element_type=F32)
            egc = jnp.exp(col)
            probs.append(dict(
                attn=(sc[:c2] * dec).astype(BF16),
                a=jnp.where(offdiag, sc[c2:] * dec, 0.0),
                rhs=jnp.concatenate([vst.astype(F32) * bcol, kb * egc], axis=1).astype(BF16),
                qg=(qst.astype(F32) * egc).astype(BF16),
                kd=(kf * jnp.exp(tot - col)).astype(BF16)))

    emit_apply(len(pairs_hh) // 2)

    for _ in _unit_lower_inverse_staged(probs):
        emit_apply(3 * nb)
    emit_apply(len(apply_ops))

    for pi, pr in enumerate(probs):
        uw = jnp.dot(pr["tinv"].astype(BF16), pr["rhs"], preferred_element_type=F32)
        u_ref[pi] = uw[:, :GDN_DV]
        w = uw[:, GDN_DV:].astype(BF16)
        for hh in range(2):
            rs = slice(hh * c, (hh + 1) * c)
            qw_ref[pi, hh] = jnp.concatenate([pr["qg"][rs], w[rs]], axis=0)
        attn_ref[pi] = pr["attn"]
        kd_ref[pi] = pr["kd"]
    for sq, dd in enumerate(seqs):
        eg_ref[sq] = jnp.exp(dd["gtot"])


def _unit_lower_inverse_staged(probs):
    c = GDN_CHUNK
    nside = 4
    ri = lax.broadcasted_iota(jnp.int32, (c, nside * c), 0)
    ci = lax.broadcasted_iota(jnp.int32, (c, nside * c), 1)
    lane_blk = ci >> 6
    within = ci & (c - 1)
    diag16 = (ri >> 4) == (within >> 4)
    eye = jnp.where(ri == within, 1.0, 0.0).astype(F32)
    left = lax.broadcasted_iota(jnp.int32, (1, 2 * c), 1) < c

    def blockdiag(y):
        return jnp.concatenate([jnp.where(lane_blk == r, y, 0.0) for r in range(nside)],
                               axis=0).astype(BF16)

    def mm(x, ybd):
        return jnp.dot(x.astype(BF16), ybd, preferred_element_type=F32)

    groups = []
    for g0 in range(0, len(probs), 2):
        pa, pb = probs[g0], probs[g0 + 1]
        a = jnp.concatenate([pa["a"][:c] + pa["a"][c:], pb["a"][:c] + pb["a"][c:]], axis=1)
        ad = jnp.where(diag16, a, 0.0)
        groups.append(dict(pairs=(pa, pb), ad=ad, an=blockdiag(a - ad), dinv=eye - ad))
    for gr in groups:
        gr["p"] = mm(gr["ad"], blockdiag(gr["ad"]))
    yield
    for gr in groups:
        pbd = blockdiag(gr["p"])
        gr["dinv"] = gr["dinv"] + mm(gr["dinv"], pbd)
        gr["p"] = mm(gr["p"], pbd)
    yield
    for gr in groups:
        pbd = blockdiag(gr["p"])
        gr["dinv"] = gr["dinv"] + mm(gr["dinv"], pbd)
        gr["p"] = mm(gr["p"], pbd)
    yield
    for gr in groups:
        gr["dinv"] = gr["dinv"] + mm(gr["dinv"], blockdiag(gr["p"]))
    yield
    for gr in groups:
        gr["m"] = mm(gr["dinv"], gr["an"])
    yield
    for gr in groups:
        gr["m2"] = blockdiag(mm(gr["m"], blockdiag(gr["m"])))
    yield
    for gr in groups:
        x = eye - gr["m"]
        gr["x"] = x + mm(x, gr["m2"])
    yield
    for gr in groups:
        t = mm(gr["x"], blockdiag(gr["dinv"]))
        for i, pr in enumerate(gr["pairs"]):
            half = t[:, 2 * c * i:2 * c * (i + 1)]
            pr["tinv"] = jnp.concatenate([jnp.where(left, half, 0.0), jnp.where(left, 0.0, half)],
                                         axis=0)


def _gdn_scan(qkv, proj3, alog_rows, dtb_rows, lp):
    b = qkv.shape[0]
    nc = lp // GDN_CHUNK
    ba_col0 = 0

    def blk_of(seq):
        return jnp.where(seq < 2, nc - 2 + seq, seq - 2)

    def chunk_specs(d):
        prep = lambda t: jnp.minimum(t, nc - 1)
        seq = prep if d == 0 else (lambda t: nc - 1 - prep(t))
        return [
            pl.BlockSpec((nb, GDN_CHUNK, GDN_QK), lambda bi, t: (bi, blk_of(seq(t)), 0)),
            pl.BlockSpec((nb, GDN_CHUNK, GDN_QK), lambda bi, t: (bi, blk_of(seq(t)), 1)),
            pl.BlockSpec((nb, GDN_CHUNK, GDN_V), lambda bi, t: (bi, blk_of(seq(t)), 1)),
            pl.BlockSpec((nb, GDN_CHUNK, 128), lambda bi, t: (bi, blk_of(seq(t)), ba_col0 + d)),
        ]

    nb = SCAN_NB if b % SCAN_NB == 0 else 1
    const = pl.BlockSpec((2, 1, 128), lambda bi, t: (0, 0, 0))
    kern = functools.partial(_gdn_scan_kernel, nc=nc)
    applied = lambda t: jnp.maximum(t - 1, 0)
    nseq = 2 * nb
    nprob = nseq * (GDN_HEADS // 2)
    c2 = 2 * GDN_CHUNK
    return pl.pallas_call(
        kern,
        grid=(b // nb, nc + 1),
        in_specs=chunk_specs(0) + chunk_specs(1) + [const, const],
        out_specs=[
            pl.BlockSpec((nb, GDN_CHUNK, GDN_V), lambda bi, t: (bi, blk_of(applied(t)), 0)),
            pl.BlockSpec((nb, GDN_CHUNK, GDN_V), lambda bi, t: (bi, blk_of(nc - 1 - applied(t)), 0)),
        ],
        out_shape=[jax.ShapeDtypeStruct((b, lp, GDN_V), BF16)] * 2,
        scratch_shapes=[
            pltpu.VMEM((nseq, GDN_HEADS, GDN_DK, GDN_DV), F32),
            pltpu.VMEM((nseq, GDN_HEADS, GDN_DK, GDN_DV), BF16),
            pltpu.VMEM((nprob, c2, GDN_DV), F32),
            pltpu.VMEM((nprob, 2, c2, GDN_DK), BF16),
            pltpu.VMEM((nprob, c2, c2), BF16),
            pltpu.VMEM((nprob, c2, GDN_DK), BF16),
            pltpu.VMEM((nseq, 1, 128), F32),
        ],
        compiler_params=_cparams(("parallel", "arbitrary")),
        name="gdn_scan",
    )(qkv, qkv, qkv, proj3, qkv, qkv, qkv, proj3, alog_rows, dtb_rows)


OUT_SUB = 256


def _gdn_out_kernel(of_ref, ob_ref, h_ref, ln_ref, wz_ref, g_ref, w_ref, o_ref):
    nsub = o_ref.shape[0] // OUT_SUB
    rows = [slice(sb * OUT_SUB, (sb + 1) * OUT_SUB) for sb in range(nsub)]

    def gate_logits(sb):
        x = h_ref[rows[sb], :]
        ms = jnp.mean(x * x, axis=-1, keepdims=True)
        xn = (x * lax.rsqrt(ms + NORM_EPS) * ln_ref[...]).astype(BF16)
        return jnp.dot(xn, wz_ref[...], preferred_element_type=F32)

    z_next = gate_logits(0)
    for sb in range(nsub):
        z = z_next
        if sb + 1 < nsub:
            z_next = gate_logits(sb + 1)
        o = of_ref[rows[sb], :].astype(F32) + ob_ref[rows[sb], :].astype(F32)
        gate = z * jax.nn.sigmoid(z)
        ys = []
        for h in range(GDN_HEADS):
            lanes = slice(h * GDN_DV, (h + 1) * GDN_DV)
            oh = o[:, lanes]
            ms = jnp.mean(oh * oh, axis=-1, keepdims=True)
            ys.append((oh * lax.rsqrt(ms + NORM_EPS) * g_ref[...] * gate[:, lanes]).astype(BF16))
        y = jnp.concatenate(ys, axis=1)
        o_ref[rows[sb], :] = h_ref[rows[sb], :] + jnp.dot(y, w_ref[...], preferred_element_type=F32)


def _gdn_out(o_fwd, o_bwd, h0, ln_row, w_z, g_row, w_out):
    rows = h0.shape[0]
    tm = _tile(rows, 768, OUT_SUB)
    const = lambda i: (0, 0)
    return pl.pallas_call(
        _gdn_out_kernel,
        grid=(rows // tm,),
        in_specs=[
            pl.BlockSpec((tm, GDN_V), lambda i: (i, 0)),
            pl.BlockSpec((tm, GDN_V), lambda i: (i, 0)),
            pl.BlockSpec((tm, D_MODEL), lambda i: (i, 0)),
            pl.BlockSpec((1, D_MODEL), const),
            pl.BlockSpec((D_MODEL, GDN_V), const),
            pl.BlockSpec((1, GDN_DV), const),
            pl.BlockSpec((GDN_V, D_MODEL), const),
        ],
        out_specs=pl.BlockSpec((tm, D_MODEL), lambda i: (i, 0)),
        out_shape=jax.ShapeDtypeStruct((rows, D_MODEL), F32),
        compiler_params=_cparams(("parallel",)),
        name="gdn_out",
    )(o_fwd, o_bwd, h0, ln_row, w_z, g_row, w_out)


def _rope(r, cos, nsin_lo, sin_hi):
    return r * cos + pltpu.roll(r, 96, 1) * nsin_lo + pltpu.roll(r, 32, 1) * sin_hi


def _mla_q_kernel(cq_ref, g1_ref, wt_ref, ga_ref, gr_ref, cos_ref, sin_ref, shift_ref, qt_ref):
    cq = cq_ref[0].astype(F32)
    tm = cq.shape[0]
    ms = jnp.mean(cq * cq, axis=-1, keepdims=True)
    cqt = (cq * lax.rsqrt(ms + NORM_EPS) * g1_ref[...]).T.astype(BF16)
    q = jnp.dot(wt_ref[...], cqt, preferred_element_type=F32)
    half = MLA_ROPE // 2
    tile = lambda g: jnp.concatenate([g] * (tm // 128), axis=1)
    ga = tile(ga_ref[...])
    gr1, gr2 = tile(gr_ref[0:half, :]), tile(gr_ref[half:MLA_ROPE, :])
    cos, sin = cos_ref[...], sin_ref[...]
    pad_rows = tile(shift_ref[...]).astype(BF16)
    for h in range(MLA_HEADS):
        r0 = h * MLA_DQK
        a = q[r0:r0 + MLA_NOPE]
        x1 = q[r0 + MLA_NOPE:r0 + MLA_NOPE + half]
        x2 = q[r0 + MLA_NOPE + half:r0 + MLA_DQK]
        ss = (jnp.sum(a * a, axis=0, keepdims=True) + jnp.sum(x1 * x1, axis=0, keepdims=True)
              + jnp.sum(x2 * x2, axis=0, keepdims=True))
        inv = lax.rsqrt(ss * (1.0 / MLA_DQK) + NORM_EPS)
        x1 = x1 * inv * gr1
        x2 = x2 * inv * gr2
        qt_ref[0, h, 0:MLA_NOPE, :] = (a * inv * ga).astype(BF16)
        qt_ref[0, h, MLA_NOPE:MLA_NOPE + half, :] = (x1 * cos - x2 * sin).astype(BF16)
        qt_ref[0, h, MLA_NOPE + half:MLA_DQK, :] = (x2 * cos + x1 * sin).astype(BF16)
        qt_ref[0, h, MLA_DQK:MLA_DPAD, :] = pad_rows


def _mla_q(proj3, g1, w_uq_t, ga, gr, cos_t, sin_t, shift_rows, lp):
    b = proj3.shape[0]
    tm = _tile(lp, 384, 128)
    cqcol = MLA_V // MLA_Q_LORA
    const = lambda bi, i: (0, 0)
    return pl.pallas_call(
        _mla_q_kernel,
        grid=(b, lp // tm),
        in_specs=[
            pl.BlockSpec((1, tm, MLA_Q_LORA), lambda bi, i: (bi, i, cqcol)),
            pl.BlockSpec((1, MLA_Q_LORA), const),
            pl.BlockSpec((MLA_HEADS * MLA_DQK, MLA_Q_LORA), const),
            pl.BlockSpec((MLA_NOPE, 128), const),
            pl.BlockSpec((MLA_ROPE, 128), const),
            pl.BlockSpec((MLA_ROPE // 2, tm), lambda bi, i: (0, i)),
            pl.BlockSpec((MLA_ROPE // 2, tm), lambda bi, i: (0, i)),
            pl.BlockSpec((MLA_DPAD - MLA_DQK, 128), const),
        ],
        out_specs=pl.BlockSpec((1, MLA_HEADS, MLA_DPAD, tm), lambda bi, i: (bi, 0, 0, i)),
        out_shape=jax.ShapeDtypeStruct((b, MLA_HEADS, MLA_DPAD, lp), BF16),
        compiler_params=_cparams(("parallel", "parallel")),
        name="mla_q",
    )(proj3, g1, w_uq_t, ga, gr, cos_t, sin_t, shift_rows)


def _mla_kv_kernel(ckv_ref, kpe_ref, g1_ref, w_ref, ga_ref, gr_ref, cos_ref, nsin_ref, sin_ref,
                   k_ref, vt_ref):
    ckv = ckv_ref[0].astype(F32)
    ms = jnp.mean(ckv * ckv, axis=-1, keepdims=True)
    cn = (ckv * lax.rsqrt(ms + NORM_EPS) * g1_ref[...]).astype(BF16)
    kv = jnp.dot(cn, w_ref[...], preferred_element_type=F32)
    kpe = kpe_ref[0].astype(F32)
    sq_pe = kpe * kpe
    kr = _rope(kpe * gr_ref[...], cos_ref[...], nsin_ref[...], sin_ref[...])
    one_hot = jnp.where(lax.broadcasted_iota(jnp.int32, (1, 128), 1) == MLA_ROPE, 1.0, 0.0).astype(F32)
    for h in range(MLA_HEADS):
        kn = kv[:, h * MLA_NOPE:(h + 1) * MLA_NOPE]
        ss = jnp.sum(kn * kn + sq_pe, axis=-1, keepdims=True)
        inv = lax.rsqrt(ss * (1.0 / MLA_DQK) + NORM_EPS)
        k_ref[0, h, :, 0:MLA_NOPE] = (kn * inv * ga_ref[...]).astype(BF16)
        k_ref[0, h, :, MLA_NOPE:MLA_DPAD] = (kr * inv + one_hot).astype(BF16)
        v = kv[:, MLA_HEADS * MLA_NOPE + h * MLA_DV:MLA_HEADS * MLA_NOPE + (h + 1) * MLA_DV]
        vt_ref[0, h] = v.T.astype(BF16)


def _mla_kv(proj3, g1, w_ukv, ga, gr, cos, nsin, sin, lp):
    b = proj3.shape[0]
    tm = _tile(lp, 384, 128)
    ckvcol = (MLA_V + MLA_Q_LORA) // MLA_KV_LORA
    kpecol = (MLA_V + MLA_Q_LORA + MLA_KV_LORA) // 128
    const = lambda bi, i: (0, 0)
    return pl.pallas_call(
        _mla_kv_kernel,
        grid=(b, lp // tm),
        in_specs=[
            pl.BlockSpec((1, tm, MLA_KV_LORA), lambda bi, i: (bi, i, ckvcol)),
            pl.BlockSpec((1, tm, 128), lambda bi, i: (bi, i, kpecol)),
            pl.BlockSpec((1, MLA_KV_LORA), const),
            pl.BlockSpec((MLA_KV_LORA, MLA_HEADS * (MLA_NOPE + MLA_DV)), const),
            pl.BlockSpec((1, 128), const),
            pl.BlockSpec((1, 128), const),
            pl.BlockSpec((tm, 128), lambda bi, i: (i, 0)),
            pl.BlockSpec((tm, 128), lambda bi, i: (i, 0)),
            pl.BlockSpec((tm, 128), lambda bi, i: (i, 0)),
        ],
        out_specs=[
            pl.BlockSpec((1, MLA_HEADS, tm, MLA_DPAD), lambda bi, i: (bi, 0, i, 0)),
            pl.BlockSpec((1, MLA_HEADS, MLA_DV, tm), lambda bi, i: (bi, 0, 0, i)),
        ],
        out_shape=[
            jax.ShapeDtypeStruct((b, MLA_HEADS, lp, MLA_DPAD), BF16),
            jax.ShapeDtypeStruct((b, MLA_HEADS, MLA_DV, lp), BF16),
        ],
        compiler_params=_cparams(("parallel", "parallel")),
        name="mla_kv",
    )(proj3, proj3, g1, w_ukv, ga, gr, cos, nsin, sin)


ATT_TK = 1024
ATT_TQ = 512
ATT_AHEAD = 2
ATT_BOUND_MAX = 60.0


def _attn_kernel(bounded_ref, qt_ref, k_ref, vt_ref, o_ref, *, s_len):
    tq = qt_ref.shape[3]
    nstrip = tq // ATT_TQ
    chunks = [(ck * ATT_TK, ATT_TK) for ck in range(s_len // ATT_TK)] + [(s_len, TAIL)]
    units = [(i, s) for i in range(len(chunks)) for s in range(nstrip)]
    npad = TAIL - N_META

    def scores(u):
        i, s = units[u]
        k0, nk = chunks[i]
        if i == len(chunks) - 1:
            k0, nk = k0 + npad, N_META
        return jnp.dot(k_ref[0, 0, k0:k0 + nk, :], qt_ref[0, 0, :, s * ATT_TQ:(s + 1) * ATT_TQ],
                       preferred_element_type=F32)

    def weights(p, i):
        p = p.astype(BF16)
        if i == len(chunks) - 1:
            p = jnp.concatenate([jnp.zeros((npad, ATT_TQ), BF16), p], axis=0)
        return p

    def run(bounded):
        m = [jnp.full((1, ATT_TQ), -jnp.inf, F32) for _ in range(nstrip)]
        l = [jnp.zeros((1, ATT_TQ), F32) for _ in range(nstrip)]
        acc = [jnp.zeros((MLA_DV, ATT_TQ), F32) for _ in range(nstrip)]
        pending = [scores(u) for u in range(min(ATT_AHEAD, len(units)))]
        for u, (i, s) in enumerate(units):
            if u + ATT_AHEAD < len(units):
                pending.append(scores(u + ATT_AHEAD))
            st = pending.pop(0)
            k0, nk = chunks[i]
            if bounded:
                p = jnp.exp2(st)
                l[s] = l[s] + jnp.sum(p, axis=0, keepdims=True)
                acc[s] = acc[s] + jnp.dot(vt_ref[0, 0, :, k0:k0 + nk], weights(p, i),
                                          preferred_element_type=F32)
            else:
                m_new = jnp.maximum(m[s], jnp.max(st, axis=0, keepdims=True))
                alpha = jnp.exp2(m[s] - m_new)
                p = jnp.exp2(st - m_new)
                l[s] = alpha * l[s] + jnp.sum(p, axis=0, keepdims=True)
                acc[s] = alpha * acc[s] + jnp.dot(vt_ref[0, 0, :, k0:k0 + nk], weights(p, i),
                                                  preferred_element_type=F32)
                m[s] = m_new
        for s in range(nstrip):
            o_ref[0, s * ATT_TQ:(s + 1) * ATT_TQ, :] = (acc[s] / l[s]).T.astype(o_ref.dtype)

    @pl.when(bounded_ref[0] == 1)
    def _():
        run(True)

    @pl.when(bounded_ref[0] != 1)
    def _():
        run(False)


def _attention(bounded, qt, k, vt, s_len):
    b = qt.shape[0]
    lp = k.shape[2]
    assert s_len % ATT_TK == 0 and lp == s_len + TAIL
    tq = _tile(s_len, 1024, ATT_TQ)
    kern = functools.partial(_attn_kernel, s_len=s_len)
    return pl.pallas_call(
        kern,
        grid=(b, MLA_HEADS, s_len // tq),
        in_specs=[
            pl.BlockSpec(memory_space=pltpu.SMEM),
            pl.BlockSpec((1, 1, MLA_DPAD, tq), lambda bi, h, i: (bi, h, 0, i)),
            pl.BlockSpec((1, 1, lp, MLA_DPAD), lambda bi, h, i: (bi, h, 0, 0)),
            pl.BlockSpec((1, 1, MLA_DV, lp), lambda bi, h, i: (bi, h, 0, 0)),
        ],
        out_specs=pl.BlockSpec((1, tq, MLA_DV), lambda bi, h, i: (bi, i, h)),
        out_shape=jax.ShapeDtypeStruct((b, s_len, MLA_V), BF16),
        compiler_params=_cparams(("parallel", "parallel", "arbitrary")),
        name="mla_attention",
    )(bounded, qt, k, vt)


def _mla_out_kernel(o_ref, z_ref, h_ref, w_ref, y_ref):
    z = z_ref[0].astype(F32)
    y = (o_ref[0].astype(F32) * (z * jax.nn.sigmoid(z))).astype(BF16)
    y_ref[0] = h_ref[0] + jnp.dot(y, w_ref[...], preferred_element_type=F32)


def _mla_out(o, proj3, h3, w_out, b0, nb, s_len):
    tm = _tile(s_len, 1024, 128)
    return pl.pallas_call(
        _mla_out_kernel,
        grid=(nb, s_len // tm),
        in_specs=[
            pl.BlockSpec((1, tm, MLA_V), lambda bi, i: (bi + b0, i, 0)),
            pl.BlockSpec((1, tm, MLA_V), lambda bi, i: (bi + b0, i, 0)),
            pl.BlockSpec((1, tm, D_MODEL), lambda bi, i: (bi + b0, i, 0)),
            pl.BlockSpec((MLA_V, D_MODEL), lambda bi, i: (0, 0)),
        ],
        out_specs=pl.BlockSpec((1, tm, D_MODEL), lambda bi, i: (bi, i, 0)),
        out_shape=jax.ShapeDtypeStruct((nb, s_len, D_MODEL), F32),
        compiler_params=_cparams(("parallel", "parallel")),
        name="mla_out",
    )(o, proj3, h3, w_out)


def _pad_cols(w, n):
    return jnp.pad(w, ((0, 0), (0, n - w.shape[1])))


def _lane_row(v, n=128):
    return jnp.pad(v.astype(F32), (0, n - v.shape[0]))[None, :]


def _trunk_all(xs, meta_tokens, ln_g, gdn_w_in, gdn_conv_w, gdn_a_log, gdn_dt_bias, gdn_o_norm_g,
               gdn_w_out, mla_w_in, mla_q_norm_g, mla_kv_norm_g, mla_w_uq, mla_w_ukv, mla_qk_q_g,
               mla_qk_k_g, mla_w_out):
    s_len = xs[0].shape[1]
    assert all(x.shape[1] == s_len for x in xs) and s_len % 128 == 0
    lp = s_len + TAIL
    x_all = jnp.concatenate(xs, axis=0)
    b = x_all.shape[0]
    meta = jnp.broadcast_to(meta_tokens[None].astype(F32), (b, N_META, D_MODEL))
    h0 = jnp.concatenate([x_all, jnp.zeros((b, TAIL - N_META, D_MODEL), F32), meta], axis=1)
    h0 = h0.reshape(b * lp, D_MODEL)

    w_in = gdn_w_in[0]
    ba = w_in[:, GDN_CONV_CH + GDN_V:].reshape(D_MODEL, 2, 2, GDN_HEADS)
    ba_dir = [_pad_cols(jnp.concatenate([ba[:, 0, d], ba[:, 1, d]], axis=1), 128) for d in range(2)]
    conv_w8 = jnp.pad(gdn_conv_w[0], ((0, 8 - GDN_CONV), (0, 0)))
    qkv, gates = _gdn_qkv_conv(h0.reshape(b, lp, D_MODEL), ln_g[0][None, :],
                               w_in[:, :GDN_CONV_CH].astype(BF16),
                               jnp.concatenate(ba_dir, axis=1).astype(BF16), conv_w8, lp)

    lane_a = lambda v: jnp.pad(v.astype(F32), ((0, 0), (GDN_HEADS, 128 - 2 * GDN_HEADS)))[:, None, :]
    o_fwd, o_bwd = _gdn_scan(qkv, gates, lane_a(gdn_a_log[0]), lane_a(gdn_dt_bias[0]), lp)
    h1 = _gdn_out(o_fwd.reshape(b * lp, GDN_V), o_bwd.reshape(b * lp, GDN_V), h0, ln_g[0][None, :],
                  w_in[:, GDN_CONV_CH:GDN_CONV_CH + GDN_V].astype(BF16),
                  gdn_o_norm_g[0][None, :], gdn_w_out[0].astype(BF16))

    w_in1 = mla_w_in[0]
    o1 = MLA_Q_LORA
    o2_ = o1 + MLA_KV_LORA
    o3 = o2_ + MLA_ROPE
    w1 = jnp.concatenate([w_in1[:, o3:], w_in1[:, :o1], w_in1[:, o1:o2_],
                          _pad_cols(w_in1[:, o2_:o3], 256)], axis=1).astype(BF16)
    proj1 = _norm_matmul(h1, ln_g[1][None, :], w1, 1536, BF16)
    proj1_3 = proj1.reshape(b, lp, proj1.shape[1])

    pos = jnp.concatenate([jnp.arange(s_len, dtype=F32) + N_META, jnp.zeros((TAIL - N_META,), F32),
                           jnp.arange(N_META, dtype=F32)])
    inv = ROPE_THETA ** (-jnp.arange(0, MLA_ROPE, 2, dtype=F32) / MLA_ROPE)
    ang = pos[:, None] * inv[None, :]
    zc = jnp.zeros_like(ang)
    cos_t = jnp.concatenate([jnp.cos(ang), jnp.cos(ang), zc, zc], axis=1)
    nsin_t = jnp.concatenate([-jnp.sin(ang), zc, zc, zc], axis=1)
    sin_t = jnp.concatenate([zc, jnp.sin(ang), zc, zc], axis=1)

    scale = MLA_DQK ** -0.5 * math.log2(math.e)
    gq = jnp.broadcast_to((mla_qk_q_g[0].astype(F32) * scale)[:, None], (MLA_DQK, 128))
    score_bound = (1.02 * MLA_DQK * scale * jnp.max(jnp.abs(mla_qk_q_g[0].astype(F32)))
                   * jnp.max(jnp.abs(mla_qk_k_g[0].astype(F32))))
    bounded = score_bound < ATT_BOUND_MAX
    shift_rows = jnp.zeros((MLA_DPAD - MLA_DQK, 128), F32).at[0, :].set(jnp.where(bounded, -score_bound, 0.0))
    qt = _mla_q(proj1_3, mla_q_norm_g[0][None, :], mla_w_uq[0].T.astype(BF16), gq[:MLA_NOPE],
                gq[MLA_NOPE:], jnp.cos(ang).T, jnp.sin(ang).T, shift_rows, lp)

    w_ukv = mla_w_ukv[0].reshape(MLA_KV_LORA, MLA_HEADS, MLA_NOPE + MLA_DV)
    w_ukv = jnp.concatenate([w_ukv[:, :, :MLA_NOPE].reshape(MLA_KV_LORA, -1),
                             w_ukv[:, :, MLA_NOPE:].reshape(MLA_KV_LORA, -1)], axis=1).astype(BF16)
    gk = mla_qk_k_g[0].astype(F32)
    k, vt = _mla_kv(proj1_3, mla_kv_norm_g[0][None, :], w_ukv, gk[None, :MLA_NOPE],
                    _lane_row(gk[MLA_NOPE:]), cos_t, nsin_t, sin_t, lp)

    o = _attention(bounded.astype(jnp.int32).reshape(1), qt, k, vt, s_len)

    h1_3 = h1.reshape(b, lp, D_MODEL)
    w_out1 = mla_w_out[0].astype(BF16)
    outs = []
    b0 = 0
    for x in xs:
        outs.append(_mla_out(o, proj1_3, h1_3, w_out1, b0, x.shape[0], s_len))
        b0 += x.shape[0]
    return tuple(outs)


def kernel(x_prompt, x_sample, meta_tokens, ln_g, gdn_w_in, gdn_conv_w, gdn_a_log, gdn_dt_bias,
           gdn_o_norm_g, gdn_w_out, mla_w_in, mla_q_norm_g, mla_kv_norm_g, mla_w_uq, mla_w_ukv,
           mla_qk_q_g, mla_qk_k_g, mla_w_out):
    return _trunk_all((x_prompt, x_sample), meta_tokens, ln_g, gdn_w_in, gdn_conv_w, gdn_a_log,
                      gdn_dt_bias, gdn_o_norm_g, gdn_w_out, mla_w_in, mla_q_norm_g, mla_kv_norm_g,
                      mla_w_uq, mla_w_ukv, mla_qk_q_g, mla_qk_k_g, mla_w_out)
```

```python
import functools
import math

import jax
import jax.numpy as jnp
from jax import lax
from jax.experimental import pallas as pl
from jax.experimental.pallas import tpu as pltpu

F32 = jnp.float32
BF16 = jnp.bfloat16

D_MODEL = 1024
N_META = 16
TAIL = 128
NORM_EPS = 1e-6

GDN_HEADS = 8
GDN_DK = 128
GDN_DV = 256
GDN_CONV = 5
GDN_CHUNK = 64
GDN_QK = GDN_HEADS * GDN_DK
GDN_V = GDN_HEADS * GDN_DV
GDN_CONV_CH = 2 * GDN_QK + GDN_V

MLA_HEADS = 16
MLA_Q_LORA = 512
MLA_KV_LORA = 256
MLA_NOPE = 128
MLA_ROPE = 64
MLA_DQK = MLA_NOPE + MLA_ROPE
MLA_DV = 128
MLA_V = MLA_HEADS * MLA_DV
MLA_DPAD = 256
ROPE_THETA = 10000.0

VMEM_LIMIT = 56 * 1024 * 1024


def _cparams(sem):
    return pltpu.CompilerParams(dimension_semantics=sem, vmem_limit_bytes=VMEM_LIMIT)


def _tile(n, target, mult):
    best = None
    t = mult
    while t <= min(n, target):
        if n % t == 0:
            best = t
        t += mult
    assert best is not None, (n, target, mult)
    return best


def _norm_matmul_kernel(x_ref, g_ref, w_ref, o_ref, xn_ref):
    @pl.when(pl.program_id(1) == 0)
    def _():
        x = x_ref[...]
        ms = jnp.mean(x * x, axis=-1, keepdims=True)
        xn_ref[...] = (x * lax.rsqrt(ms + NORM_EPS) * g_ref[...]).astype(BF16)

    o_ref[...] = jnp.dot(xn_ref[...], w_ref[...], preferred_element_type=F32).astype(o_ref.dtype)


def _norm_matmul(x, g, w, tn, out_dtype=F32):
    rows, d = x.shape
    n = w.shape[1]
    tm = _tile(rows, 1408, 128)
    return pl.pallas_call(
        _norm_matmul_kernel,
        grid=(rows // tm, n // tn),
        in_specs=[
            pl.BlockSpec((tm, d), lambda i, j: (i, 0)),
            pl.BlockSpec((1, d), lambda i, j: (0, 0)),
            pl.BlockSpec((d, tn), lambda i, j: (0, j)),
        ],
        out_specs=pl.BlockSpec((tm, tn), lambda i, j: (i, j)),
        out_shape=jax.ShapeDtypeStruct((rows, n), out_dtype),
        scratch_shapes=[pltpu.VMEM((tm, d), BF16)],
        compiler_params=_cparams(("parallel", "arbitrary")),
        name="norm_matmul",
    )(x, g, w)


CONV_SUB = 64
CONV_LANES = GDN_DK
CONV_TC = 1024


def _qkv_conv_kernel(prev_ref, main_ref, next_ref, g_ref, w_ref, wba_ref, cw_ref, o_ref, ba_ref, ext_ref,
                     *, tr):
    half = GDN_CONV // 2
    nblk = GDN_CONV_CH // CONV_TC
    x = jnp.concatenate([prev_ref[0], main_ref[0], next_ref[0]], axis=0)
    ms = jnp.mean(x * x, axis=-1, keepdims=True)
    xn = (x * lax.rsqrt(ms + NORM_EPS) * g_ref[...]).astype(BF16)
    ba_ref[0] = jnp.dot(xn, wba_ref[...], preferred_element_type=F32)[8:8 + tr]

    def project(j):
        ext_ref[j] = jnp.dot(xn, w_ref[:, j * CONV_TC:(j + 1) * CONV_TC], preferred_element_type=F32)

    def conv_silu(j, sb, lb):
        lanes = slice(j * CONV_TC + lb * CONV_LANES, j * CONV_TC + (lb + 1) * CONV_LANES)
        nrow = CONV_SUB + 16
        x2 = ext_ref[j, sb * CONV_SUB:sb * CONV_SUB + nrow, lb * CONV_LANES:(lb + 1) * CONV_LANES]
        acc = x2[8:8 + CONV_SUB] * cw_ref[half:half + 1, lanes]
        for s in range(1, half + 1):
            down = pltpu.roll(x2, s, 0)
            acc = acc + down[8:8 + CONV_SUB] * cw_ref[half - s:half - s + 1, lanes]
            up = pltpu.roll(x2, nrow - s, 0)
            acc = acc + up[8:8 + CONV_SUB] * cw_ref[half + s:half + s + 1, lanes]
        return acc * jax.nn.sigmoid(acc)

    def finish(j):
        for sb in range(tr // CONV_SUB):
            rows = slice(sb * CONV_SUB, (sb + 1) * CONV_SUB)
            for lb in range(CONV_TC // CONV_LANES):
                y = conv_silu(j, sb, lb)
                lanes = slice(j * CONV_TC + lb * CONV_LANES, j * CONV_TC + (lb + 1) * CONV_LANES)
                if j < 2:
                    qscale = GDN_DK ** -0.5 if j == 0 else 1.0
                    ss = jnp.sum(y * y, axis=-1, keepdims=True)
                    y = y * (lax.rsqrt(ss + NORM_EPS) * qscale)
                o_ref[0, rows, lanes] = y.astype(BF16)

    project(0)
    for j in range(nblk):
        if j + 1 < nblk:
            project(j + 1)
        finish(j)


def _gdn_qkv_conv(h3, g, w_qkv, w_ba, conv_w8, lp):
    b = h3.shape[0]
    nba = w_ba.shape[1]
    tr = _tile(lp, 384, CONV_SUB)
    nb8 = lp // 8
    tb = tr // 8
    kern = functools.partial(_qkv_conv_kernel, tr=tr)
    const = lambda bi, i: (0, 0)
    return pl.pallas_call(
        kern,
        grid=(b, lp // tr),
        in_specs=[
            pl.BlockSpec((1, 8, D_MODEL), lambda bi, i: (bi, (i * tb + nb8 - 1) % nb8, 0)),
            pl.BlockSpec((1, tr, D_MODEL), lambda bi, i: (bi, i, 0)),
            pl.BlockSpec((1, 8, D_MODEL), lambda bi, i: (bi, ((i + 1) * tb) % nb8, 0)),
            pl.BlockSpec((1, D_MODEL), const),
            pl.BlockSpec((D_MODEL, GDN_CONV_CH), const),
            pl.BlockSpec((D_MODEL, nba), const),
            pl.BlockSpec((8, GDN_CONV_CH), const),
        ],
        out_specs=[
            pl.BlockSpec((1, tr, GDN_CONV_CH), lambda bi, i: (bi, i, 0)),
            pl.BlockSpec((1, tr, nba), lambda bi, i: (bi, i, 0)),
        ],
        out_shape=[
            jax.ShapeDtypeStruct((b, lp, GDN_CONV_CH), BF16),
            jax.ShapeDtypeStruct((b, lp, nba), F32),
        ],
        scratch_shapes=[pltpu.VMEM((GDN_CONV_CH // CONV_TC, tr + 16, CONV_TC), F32)],
        compiler_params=_cparams(("parallel", "parallel")),
        name="gdn_qkv_conv",
    )(h3, h3, h3, g, w_qkv, w_ba, conv_w8)


SCAN_NB = 2


def _split3(x):
    hi = x.astype(BF16)
    r1 = x - hi.astype(F32)
    mid = r1.astype(BF16)
    lo = (r1 - mid.astype(F32)).astype(BF16)
    return hi, mid, lo


def _gdn_scan_kernel(qf_ref, kf_ref, vf_ref, baf_ref, qb_ref, kb_ref, vb_ref, bab_ref, alog_ref, dtb_ref,
                     of_ref, ob_ref, s_ref, sb_ref, u_ref, qw_ref, attn_ref, kd_ref, eg_ref, *, nc):
    t = pl.program_id(1)
    c = GDN_CHUNK
    c2 = 2 * c
    npair = GDN_HEADS // 2
    o_refs = (of_ref, ob_ref)

    @pl.when(t == 0)
    def _():
        for ref in (s_ref, sb_ref, u_ref, qw_ref, attn_ref, kd_ref, eg_ref):
            ref[...] = jnp.zeros_like(ref)

    nb = of_ref.shape[0]
    nprob = nb * 2 * npair
    applied = [dict(sq=pi // npair, heads=(2 * (pi % npair), 2 * (pi % npair) + 1),
                    qs=[None, None], vnew=[None, None]) for pi in range(nprob)]

    def read_state(pi, hh):
        ap = applied[pi]
        rs = slice(hh * c, (hh + 1) * c)
        qws = jnp.dot(qw_ref[pi, hh], sb_ref[ap["sq"], ap["heads"][hh]], preferred_element_type=F32)
        ap["qs"][hh] = qws[:c]
        ap["vnew"][hh] = (u_ref[pi, rs, :] - qws[c:]).astype(BF16)

    def write_out(pi):
        ap = applied[pi]
        h0, h1 = ap["heads"]
        bi, d = divmod(ap["sq"], 2)
        vnew2 = jnp.concatenate(ap["vnew"], axis=0)
        o2 = jnp.concatenate(ap["qs"], axis=0) + jnp.dot(attn_ref[pi], vnew2, preferred_element_type=F32)
        o_refs[d][bi, :, h0 * GDN_DV:(h0 + 1) * GDN_DV] = o2[:c].astype(BF16)
        o_refs[d][bi, :, h1 * GDN_DV:(h1 + 1) * GDN_DV] = o2[c:].astype(BF16)

    def update_state(pi, hh):
        ap = applied[pi]
        sq, h = ap["sq"], ap["heads"][hh]
        rs = slice(hh * c, (hh + 1) * c)
        a_h = GDN_HEADS + h
        upd = lax.dot_general(kd_ref[pi, rs, :], ap["vnew"][hh], (((0,), (0,)), ((), ())),
                              preferred_element_type=F32)
        s_new = s_ref[sq, h] * eg_ref[sq, :, a_h:a_h + 1] + upd
        s_ref[sq, h] = s_new
        sb_ref[sq, h] = s_new.astype(BF16)

    pairs_hh = [(pi, hh) for pi in range(nprob) for hh in range(2)]
    apply_ops = ([functools.partial(read_state, pi, hh) for pi, hh in pairs_hh]
                 + [functools.partial(write_out, pi) for pi in range(nprob)]
                 + [functools.partial(update_state, pi, hh) for pi, hh in pairs_hh])

    def emit_apply(n):
        for _ in range(min(n, len(apply_ops))):
            apply_ops.pop(0)()

    emit_apply(len(pairs_hh) // 2)
    tp = jnp.minimum(t, nc - 1)
    ri = lax.broadcasted_iota(jnp.int32, (c2, c2), 0)
    ci = lax.broadcasted_iota(jnp.int32, (c2, c2), 1)
    same = (ri >> 6) == (ci >> 6)
    offdiag = ri != ci
    top =lax.broadcasted_iota(jnp.int32, (c2, 1), 0) < c
    left = lax.broadcasted_iota(jnp.int32, (1, c2), 1) < c
    row_id = lax.broadcasted_iota(jnp.int32, (c, 128), 0)

    in_refs = ((qf_ref, kf_ref, vf_ref, baf_ref), (qb_ref, kb_ref, vb_ref, bab_ref))
    seqs = []
    for sq in range(2 * nb):
        bi, d = divmod(sq, 2)
        q_ref, k_ref, v_ref, ba_ref = in_refs[d]
        seq = tp if d == 0 else nc - 1 - tp
        blk = jnp.where(seq < 2, nc - 2 + seq, seq - 2)
        first_valid = jnp.where(blk == nc - 2, c, jnp.where(blk == nc - 1, c - N_META, 0))
        valid = row_id >= first_valid
        ba = ba_ref[bi]
        beta = jnp.where(valid, jax.nn.sigmoid(ba), 0.0)
        xs = ba + dtb_ref[d]
        softplus = jnp.maximum(xs, 0.0) + jnp.log(1.0 + jnp.exp(-jnp.abs(xs)))
        g = jnp.where(valid, -jnp.exp(alog_ref[d]) * softplus, 0.0)
        incl = same & ((ri >= ci) if d == 0 else (ri <= ci))
        tri = jnp.where(incl, 1.0, 0.0).astype(BF16)
        ghi, gmid, glo = _split3(jnp.concatenate([g, g], axis=0))
        gc2 = (jnp.dot(tri, ghi, preferred_element_type=F32)
               + jnp.dot(tri, gmid, preferred_element_type=F32)
               + jnp.dot(tri, glo, preferred_element_type=F32))
        seqs.append(dict(q=q_ref.at[bi], k=k_ref.at[bi], v=v_ref.at[bi], incl=incl, gc2=gc2, gc2t=gc2.T,
                         beta2=jnp.concatenate([beta, beta], axis=0),
                         gtot=jnp.sum(g, axis=0, keepdims=True)))

    def stacked(ref, h0, h1, width):
        return jnp.concatenate([ref[:, h0 * width:(h0 + 1) * width],
                                ref[:, h1 * width:(h1 + 1) * width]], axis=0)

    probs = []
    for dd in seqs:
        for p in range(npair):
            h0, h1 = 2 * p, 2 * p + 1
            a0, a1 = GDN_HEADS + h0, GDN_HEADS + h1
            col = jnp.where(top, dd["gc2"][:, a0:a0 + 1], dd["gc2"][:, a1:a1 + 1])
            row = jnp.where(left, dd["gc2t"][a0:a0 + 1, :], dd["gc2t"][a1:a1 + 1, :])
            bcol = jnp.where(top, dd["beta2"][:, h0:h0 + 1], dd["beta2"][:, h1:h1 + 1])
            tot = jnp.where(top, dd["gtot"][:, a0:a0 + 1], dd["gtot"][:, a1:a1 + 1])
            dec = jnp.exp(jnp.where(dd["incl"], col - row, -jnp.inf))
            kst = stacked(dd["k"], h0, h1, GDN_DK)
            qst = stacked(dd["q"], h0, h1, GDN_DK)
            vst = stacked(dd["v"], h0, h1, GDN_DV)
            kf = kst.astype(F32)
            kb = kf * bcol
            sc = lax.dot_general(jnp.concatenate([qst, kb.astype(BF16)], axis=0), kst,
                                 (((1,), (1,)), ((), ())), preferred_element_type=F32)
            egc = jnp.exp(col)
            probs.append(dict(
                attn=(sc[:c2] * dec).astype(BF16),
                a=jnp.where(offdiag, sc[c2:] * dec, 0.0),
                rhs=jnp.concatenate([vst.astype(F32) * bcol, kb * egc], axis=1).astype(BF16),
                qg=(qst.astype(F32) * egc).astype(BF16),
                kd=(kf * jnp.exp(tot - col)).astype(BF16)))

    emit_apply(len(pairs_hh) // 2)

    for _ in _unit_lower_inverse_staged(probs):
        emit_apply(3 * nb)
    emit_apply(len(apply_ops))

    for pi, pr in enumerate(probs):
        uw = jnp.dot(pr["tinv"].astype(BF16), pr["rhs"], preferred_element_type=F32)
        u_ref[pi] = uw[:, :GDN_DV]
        w = uw[:, GDN_DV:].astype(BF16)
        for hh in range(2):
            rs = slice(hh * c, (hh + 1) * c)
            qw_ref[pi, hh] = jnp.concatenate([pr["qg"][rs], w[rs]], axis=0)
        attn_ref[pi] = pr["attn"]
        kd_ref[pi] = pr["kd"]
    for sq, dd in enumerate(seqs):
        eg_ref[sq] = jnp.exp(dd["gtot"])


def _unit_lower_inverse_staged(probs):
    c = GDN_CHUNK
    nside = 4
    ri = lax.broadcasted_iota(jnp.int32, (c, nside * c), 0)
    ci = lax.broadcasted_iota(jnp.int32, (c, nside * c), 1)
    lane_blk = ci >> 6
    within = ci & (c - 1)
    diag16 = (ri >> 4) == (within >> 4)
    eye = jnp.where(ri == within, 1.0, 0.0).astype(F32)
    left = lax.broadcasted_iota(jnp.int32, (1, 2 * c), 1) < c

    def blockdiag(y):
        return jnp.concatenate([jnp.where(lane_blk == r, y, 0.0) for r in range(nside)],
                               axis=0).astype(BF16)

    def mm(x, ybd):
        return jnp.dot(x.astype(BF16), ybd, preferred_element_type=F32)

    groups = []
    for g0 in range(0, len(probs), 2):
        pa, pb = probs[g0], probs[g0 + 1]
        a = jnp.concatenate([pa["a"][:c] + pa["a"][c:], pb["a"][:c] + pb["a"][c:]], axis=1)
        ad = jnp.where(diag16, a, 0.0)
        groups.append(dict(pairs=(pa, pb), ad=ad, an=blockdiag(a - ad), dinv=eye - ad))
    for gr in groups:
        gr["p"] = mm(gr["ad"], blockdiag(gr["ad"]))
    yield
    for gr in groups:
        pbd = blockdiag(gr["p"])
        gr["dinv"] = gr["dinv"] + mm(gr["dinv"], pbd)
        gr["p"] = mm(gr["p"], pbd)
    yield
    for gr in groups:
        pbd = blockdiag(gr["p"])
        gr["dinv"] = gr["dinv"] + mm(gr["dinv"], pbd)
        gr["p"] = mm(gr["p"], pbd)
    yield
    for gr in groups:
        gr["dinv"] = gr["dinv"] + mm(gr["dinv"], blockdiag(gr["p"]))
    yield
    for gr in groups:
        gr["m"] = mm(gr["dinv"], gr["an"])
    yield
    for gr in groups:
        gr["m2"] = blockdiag(mm(gr["m"], blockdiag(gr["m"])))
    yield
    for gr in groups:
        x = eye - gr["m"]
        gr["x"] = x + mm(x, gr["m2"])
    yield
    for gr in groups:
        t = mm(gr["x"], blockdiag(gr["dinv"]))
        for i, pr in enumerate(gr["pairs"]):
            half = t[:, 2 * c * i:2 * c * (i + 1)]
            pr["tinv"] = jnp.concatenate([jnp.where(left, half, 0.0), jnp.where(left, 0.0, half)],
                                         axis=0)


def _gdn_scan(qkv, proj3, alog_rows, dtb_rows, lp):
    b = qkv.shape[0]
    nc = lp // GDN_CHUNK
    ba_col0 = 0

    def blk_of(seq):
        return jnp.where(seq < 2, nc - 2 + seq, seq - 2)

    def chunk_specs(d):
        prep = lambda t: jnp.minimum(t, nc - 1)
        seq = prep if d == 0 else (lambda t: nc - 1 - prep(t))
        return [
            pl.BlockSpec((nb, GDN_CHUNK, GDN_QK), lambda bi, t: (bi, blk_of(seq(t)), 0)),
            pl.BlockSpec((nb, GDN_CHUNK, GDN_QK), lambda bi, t: (bi, blk_of(seq(t)), 1)),
            pl.BlockSpec((nb, GDN_CHUNK, GDN_V), lambda bi, t: (bi, blk_of(seq(t)), 1)),
            pl.BlockSpec((nb, GDN_CHUNK, 128), lambda bi, t: (bi, blk_of(seq(t)), ba_col0 + d)),
        ]

    nb = SCAN_NB if b % SCAN_NB == 0 else 1
    const = pl.BlockSpec((2, 1, 128), lambda bi, t: (0, 0, 0))
    kern = functools.partial(_gdn_scan_kernel, nc=nc)
    applied = lambda t: jnp.maximum(t - 1, 0)
    nseq = 2 * nb
    nprob = nseq * (GDN_HEADS // 2)
    c2 = 2 * GDN_CHUNK
    return pl.pallas_call(
        kern,
        grid=(b // nb, nc + 1),
        in_specs=chunk_specs(0) + chunk_specs(1) + [const, const],
        out_specs=[
            pl.BlockSpec((nb, GDN_CHUNK, GDN_V), lambda bi, t: (bi, blk_of(applied(t)), 0)),
            pl.BlockSpec((nb, GDN_CHUNK, GDN_V), lambda bi, t: (bi, blk_of(nc - 1 - applied(t)), 0)),
        ],
        out_shape=[jax.ShapeDtypeStruct((b, lp, GDN_V), BF16)] * 2,
        scratch_shapes=[
            pltpu.VMEM((nseq, GDN_HEADS, GDN_DK, GDN_DV), F32),
            pltpu.VMEM((nseq, GDN_HEADS, GDN_DK, GDN_DV), BF16),
            pltpu.VMEM((nprob, c2, GDN_DV), F32),
            pltpu.VMEM((nprob, 2, c2, GDN_DK), BF16),
            pltpu.VMEM((nprob, c2, c2), BF16),
            pltpu.VMEM((nprob, c2, GDN_DK), BF16),
            pltpu.VMEM((nseq, 1, 128), F32),
        ],
        compiler_params=_cparams(("parallel", "arbitrary")),
        name="gdn_scan",
    )(qkv, qkv, qkv, proj3, qkv, qkv, qkv, proj3, alog_rows, dtb_rows)


OUT_SUB = 256


def _gdn_out_kernel(of_ref, ob_ref, h_ref, ln_ref, wz_ref, g_ref, w_ref, o_ref):
    nsub = o_ref.shape[0] // OUT_SUB
    rows = [slice(sb * OUT_SUB, (sb + 1) * OUT_SUB) for sb in range(nsub)]

    def gate_logits(sb):
        x = h_ref[rows[sb], :]
        ms = jnp.mean(x * x, axis=-1, keepdims=True)
        xn = (x * lax.rsqrt(ms + NORM_EPS) * ln_ref[...]).astype(BF16)
        return jnp.dot(xn, wz_ref[...], preferred_element_type=F32)

    z_next = gate_logits(0)
    for sb in range(nsub):
        z = z_next
        if sb + 1 < nsub:
            z_next = gate_logits(sb + 1)
        o = of_ref[rows[sb], :].astype(F32) + ob_ref[rows[sb], :].astype(F32)
        gate = z * jax.nn.sigmoid(z)
        ys = []
        for h in range(GDN_HEADS):
            lanes = slice(h * GDN_DV, (h + 1) * GDN_DV)
            oh = o[:, lanes]
            ms = jnp.mean(oh * oh, axis=-1, keepdims=True)
            ys.append((oh * lax.rsqrt(ms + NORM_EPS) * g_ref[...] * gate[:, lanes]).astype(BF16))
        y = jnp.concatenate(ys, axis=1)
        o_ref[rows[sb], :] = h_ref[rows[sb], :] + jnp.dot(y, w_ref[...], preferred_element_type=F32)


def _gdn_out(o_fwd, o_bwd, h0, ln_row, w_z, g_row, w_out):
    rows = h0.shape[0]
    tm = _tile(rows, 768, OUT_SUB)
    const = lambda i: (0, 0)
    return pl.pallas_call(
        _gdn_out_kernel,
        grid=(rows // tm,),
        in_specs=[
            pl.BlockSpec((tm, GDN_V), lambda i: (i, 0)),
            pl.BlockSpec((tm, GDN_V), lambda i: (i, 0)),
            pl.BlockSpec((tm, D_MODEL), lambda i: (i, 0)),
            pl.BlockSpec((1, D_MODEL), const),
            pl.BlockSpec((D_MODEL, GDN_V), const),
            pl.BlockSpec((1, GDN_DV), const),
            pl.BlockSpec((GDN_V, D_MODEL), const),
        ],
        out_specs=pl.BlockSpec((tm, D_MODEL), lambda i: (i, 0)),
        out_shape=jax.ShapeDtypeStruct((rows, D_MODEL), F32),
        compiler_params=_cparams(("parallel",)),
        name="gdn_out",
    )(o_fwd, o_bwd, h0, ln_row, w_z, g_row, w_out)


def _rope(r, cos, nsin_lo, sin_hi):
    return r * cos + pltpu.roll(r, 96, 1) * nsin_lo + pltpu.roll(r, 32, 1) * sin_hi


def _mla_q_kernel(cq_ref, g1_ref, wt_ref, ga_ref, gr_ref, cos_ref, sin_ref, shift_ref, qt_ref):
    cq = cq_ref[0].astype(F32)
    tm = cq.shape[0]
    ms = jnp.mean(cq * cq, axis=-1, keepdims=True)
    cqt = (cq * lax.rsqrt(ms + NORM_EPS) * g1_ref[...]).T.astype(BF16)
    q = jnp.dot(wt_ref[...], cqt, preferred_element_type=F32)
    half = MLA_ROPE // 2
    tile = lambda g: jnp.concatenate([g] * (tm // 128), axis=1)
    ga = tile(ga_ref[...])
    gr1, gr2 = tile(gr_ref[0:half, :]), tile(gr_ref[half:MLA_ROPE, :])
    cos, sin = cos_ref[...], sin_ref[...]
    pad_rows = tile(shift_ref[...]).astype(BF16)
    for h in range(MLA_HEADS):
        r0 = h * MLA_DQK
        a = q[r0:r0 + MLA_NOPE]
        x1 = q[r0 + MLA_NOPE:r0 + MLA_NOPE + half]
        x2 = q[r0 + MLA_NOPE + half:r0 + MLA_DQK]
        ss = (jnp.sum(a * a, axis=0, keepdims=True) + jnp.sum(x1 * x1, axis=0, keepdims=True)
              + jnp.sum(x2 * x2, axis=0, keepdims=True))
        inv = lax.rsqrt(ss * (1.0 / MLA_DQK) + NORM_EPS)
        x1 = x1 * inv * gr1
        x2 = x2 * inv * gr2
        qt_ref[0, h, 0:MLA_NOPE, :] = (a * inv * ga).astype(BF16)
        qt_ref[0, h, MLA_NOPE:MLA_NOPE + half, :] = (x1 * cos - x2 * sin).astype(BF16)
        qt_ref[0, h, MLA_NOPE + half:MLA_DQK, :] = (x2 * cos + x1 * sin).astype(BF16)
        qt_ref[0, h, MLA_DQK:MLA_DPAD, :] = pad_rows


def _mla_q(proj3, g1, w_uq_t, ga, gr, cos_t, sin_t, shift_rows, lp):
    b = proj3.shape[0]
    tm = _tile(lp, 384, 128)
    cqcol = MLA_V // MLA_Q_LORA
    const = lambda bi, i: (0, 0)
    return pl.pallas_call(
        _mla_q_kernel,
        grid=(b, lp // tm),
        in_specs=[
            pl.BlockSpec((1, tm, MLA_Q_LORA), lambda bi, i: (bi, i, cqcol)),
            pl.BlockSpec((1, MLA_Q_LORA), const),
            pl.BlockSpec((MLA_HEADS * MLA_DQK, MLA_Q_LORA), const),
            pl.BlockSpec((MLA_NOPE, 128), const),
            pl.BlockSpec((MLA_ROPE, 128), const),
            pl.BlockSpec((MLA_ROPE // 2, tm), lambda bi, i: (0, i)),
            pl.BlockSpec((MLA_ROPE // 2, tm), lambda bi, i: (0, i)),
            pl.BlockSpec((MLA_DPAD - MLA_DQK, 128), const),
        ],
        out_specs=pl.BlockSpec((1, MLA_HEADS, MLA_DPAD, tm), lambda bi, i: (bi, 0, 0, i)),
        out_shape=jax.ShapeDtypeStruct((b, MLA_HEADS, MLA_DPAD, lp), BF16),
        compiler_params=_cparams(("parallel", "parallel")),
        name="mla_q",
    )(proj3, g1, w_uq_t, ga, gr, cos_t, sin_t, shift_rows)


def _mla_kv_kernel(ckv_ref, kpe_ref, g1_ref, w_ref, ga_ref, gr_ref, cos_ref, nsin_ref, sin_ref,
                   k_ref, vt_ref):
    ckv = ckv_ref[0].astype(F32)
    ms = jnp.mean(ckv * ckv, axis=-1, keepdims=True)
    cn = (ckv * lax.rsqrt(ms + NORM_EPS) * g1_ref[...]).astype(BF16)
    kv = jnp.dot(cn, w_ref[...], preferred_element_type=F32)
    kpe = kpe_ref[0].astype(F32)
    sq_pe = kpe * kpe
    kr = _rope(kpe * gr_ref[...], cos_ref[...], nsin_ref[...], sin_ref[...])
    one_hot = jnp.where(lax.broadcasted_iota(jnp.int32, (1, 128), 1) == MLA_ROPE, 1.0, 0.0).astype(F32)
    for h in range(MLA_HEADS):
        kn = kv[:, h * MLA_NOPE:(h + 1) * MLA_NOPE]
        ss = jnp.sum(kn * kn + sq_pe, axis=-1, keepdims=True)
        inv = lax.rsqrt(ss * (1.0 / MLA_DQK) + NORM_EPS)
        k_ref[0, h, :, 0:MLA_NOPE] = (kn * inv * ga_ref[...]).astype(BF16)
        k_ref[0, h, :, MLA_NOPE:MLA_DPAD] = (kr * inv + one_hot).astype(BF16)
        v = kv[:, MLA_HEADS * MLA_NOPE + h * MLA_DV:MLA_HEADS * MLA_NOPE + (h + 1) * MLA_DV]
        vt_ref[0, h] = v.T.astype(BF16)


def _mla_kv(proj3, g1, w_ukv, ga, gr, cos, nsin, sin, lp):
    b = proj3.shape[0]
    tm = _tile(lp, 384, 128)
    ckvcol = (MLA_V + MLA_Q_LORA) // MLA_KV_LORA
    kpecol = (MLA_V + MLA_Q_LORA + MLA_KV_LORA) // 128
    const = lambda bi, i: (0, 0)
    return pl.pallas_call(
        _mla_kv_kernel,
        grid=(b, lp // tm),
        in_specs=[
            pl.BlockSpec((1, tm, MLA_KV_LORA), lambda bi, i: (bi, i, ckvcol)),
            pl.BlockSpec((1, tm, 128), lambda bi, i: (bi, i, kpecol)),
            pl.BlockSpec((1, MLA_KV_LORA), const),
            pl.BlockSpec((MLA_KV_LORA, MLA_HEADS * (MLA_NOPE + MLA_DV)), const),
            pl.BlockSpec((1, 128), const),
            pl.BlockSpec((1, 128), const),
            pl.BlockSpec((tm, 128), lambda bi, i: (i, 0)),
            pl.BlockSpec((tm, 128), lambda bi, i: (i, 0)),
            pl.BlockSpec((tm, 128), lambda bi, i: (i, 0)),
        ],
        out_specs=[
            pl.BlockSpec((1, MLA_HEADS, tm, MLA_DPAD), lambda bi, i: (bi, 0, i, 0)),
            pl.BlockSpec((1, MLA_HEADS, MLA_DV, tm), lambda bi, i: (bi, 0, 0, i)),
        ],
        out_shape=[
            jax.ShapeDtypeStruct((b, MLA_HEADS, lp, MLA_DPAD), BF16),
            jax.ShapeDtypeStruct((b, MLA_HEADS, MLA_DV, lp), BF16),
        ],
        compiler_params=_cparams(("parallel", "parallel")),
        name="mla_kv",
    )(proj3, proj3, g1, w_ukv, ga, gr, cos, nsin, sin)


ATT_TK = 1024
ATT_TQ = 512
ATT_AHEAD = 2
ATT_BOUND_MAX = 60.0


def _attn_kernel(bounded_ref, qt_ref, k_ref, vt_ref, o_ref, *, s_len):
    tq = qt_ref.shape[3]
    nstrip = tq // ATT_TQ
    chunks = [(ck * ATT_TK, ATT_TK) for ck in range(s_len // ATT_TK)] + [(s_len, TAIL)]
    units = [(i, s) for i in range(len(chunks)) for s in range(nstrip)]
    is_meta = lax.broadcasted_iota(jnp.int32, (TAIL, 1), 0) >= TAIL - N_META
    tail_bias = jnp.where(is_meta, 0.0, -jnp.inf).astype(F32)

    def scores(u):
        i, s = units[u]
        k0, nk = chunks[i]
        st = jnp.dot(k_ref[0, 0, k0:k0 + nk, :], qt_ref[0, 0, :, s * ATT_TQ:(s + 1) * ATT_TQ],
                     preferred_element_type=F32)
        return st + tail_bias if i == len(chunks) - 1 else st

    def run(bounded):
        m = [jnp.full((1, ATT_TQ), -jnp.inf, F32) for _ in range(nstrip)]
        l = [jnp.zeros((1, ATT_TQ), F32) for _ in range(nstrip)]
        acc = [jnp.zeros((MLA_DV, ATT_TQ), F32) for _ in range(nstrip)]
        pending = [scores(u) for u in range(min(ATT_AHEAD, len(units)))]
        for u, (i, s) in enumerate(units):
            if u + ATT_AHEAD < len(units):
                pending.append(scores(u + ATT_AHEAD))
            st = pending.pop(0)
            k0, nk = chunks[i]
            if bounded:
                p = jnp.exp2(st)
                l[s] = l[s] + jnp.sum(p, axis=0, keepdims=True)
                acc[s] = acc[s] + jnp.dot(vt_ref[0, 0, :, k0:k0 + nk], p.astype(BF16),
                                          preferred_element_type=F32)
            else:
                m_new = jnp.maximum(m[s], jnp.max(st, axis=0, keepdims=True))
                alpha = jnp.exp2(m[s] - m_new)
                p = jnp.exp2(st - m_new)
                l[s] = alpha * l[s] + jnp.sum(p, axis=0, keepdims=True)
                acc[s] = alpha * acc[s] + jnp.dot(vt_ref[0, 0, :, k0:k0 + nk], p.astype(BF16),
                                                  preferred_element_type=F32)
                m[s] = m_new
        for s in range(nstrip):
            o_ref[0, s * ATT_TQ:(s + 1) * ATT_TQ, :] = (acc[s] / l[s]).T.astype(o_ref.dtype)

    @pl.when(bounded_ref[0] == 1)
    def _():
        run(True)

    @pl.when(bounded_ref[0] != 1)
    def _():
        run(False)


def _attention(bounded, qt, k, vt, s_len):
    b = qt.shape[0]
    lp = k.shape[2]
    assert s_len % ATT_TK == 0 and lp == s_len + TAIL
    tq = _tile(s_len, 1024, ATT_TQ)
    kern = functools.partial(_attn_kernel, s_len=s_len)
    return pl.pallas_call(
        kern,
        grid=(b, MLA_HEADS, s_len // tq),
        in_specs=[
            pl.BlockSpec(memory_space=pltpu.SMEM),
            pl.BlockSpec((1, 1, MLA_DPAD, tq), lambda bi, h, i: (bi, h, 0, i)),
            pl.BlockSpec((1, 1, lp, MLA_DPAD), lambda bi, h, i: (bi, h, 0, 0)),
            pl.BlockSpec((1, 1, MLA_DV, lp), lambda bi, h, i: (bi, h, 0, 0)),
        ],
        out_specs=pl.BlockSpec((1, tq, MLA_DV), lambda bi, h, i: (bi, i, h)),
        out_shape=jax.ShapeDtypeStruct((b, s_len, MLA_V), BF16),
        compiler_params=_cparams(("parallel", "parallel", "arbitrary")),
        name="mla_attention",
    )(bounded, qt, k, vt)


def _mla_out_kernel(o_ref, z_ref, h_ref, w_ref, y_ref):
    z = z_ref[0].astype(F32)
    y = (o_ref[0].astype(F32) * (z * jax.nn.sigmoid(z))).astype(BF16)
    y_ref[0] = h_ref[0] + jnp.dot(y, w_ref[...], preferred_element_type=F32)


def _mla_out(o, proj3, h3, w_out, b0, nb, s_len):
    tm = _tile(s_len, 1024, 128)
    return pl.pallas_call(
        _mla_out_kernel,
        grid=(nb, s_len // tm),
        in_specs=[
            pl.BlockSpec((1, tm, MLA_V), lambda bi, i: (bi + b0, i, 0)),
            pl.BlockSpec((1, tm, MLA_V), lambda bi, i: (bi + b0, i, 0)),
            pl.BlockSpec((1, tm, D_MODEL), lambda bi, i: (bi + b0, i, 0)),
            pl.BlockSpec((MLA_V, D_MODEL), lambda bi, i: (0, 0)),
        ],
        out_specs=pl.BlockSpec((1, tm, D_MODEL), lambda bi, i: (bi, i, 0)),
        out_shape=jax.ShapeDtypeStruct((nb, s_len, D_MODEL), F32),
        compiler_params=_cparams(("parallel", "parallel")),
        name="mla_out",
    )(o, proj3, h3, w_out)


def _pad_cols(w, n):
    return jnp.pad(w, ((0, 0), (0, n - w.shape[1])))


def _lane_row(v, n=128):
    return jnp.pad(v.astype(F32), (0, n - v.shape[0]))[None, :]


def _trunk_all(xs, meta_tokens, ln_g, gdn_w_in, gdn_conv_w, gdn_a_log, gdn_dt_bias, gdn_o_norm_g,
               gdn_w_out, mla_w_in, mla_q_norm_g, mla_kv_norm_g, mla_w_uq, mla_w_ukv, mla_qk_q_g,
               mla_qk_k_g, mla_w_out):
    s_len = xs[0].shape[1]
    assert all(x.shape[1] == s_len for x in xs) and s_len % 128 == 0
    lp = s_len + TAIL
    x_all = jnp.concatenate(xs, axis=0)
    b = x_all.shape[0]
    meta = jnp.broadcast_to(meta_tokens[None].astype(F32), (b, N_META, D_MODEL))
    h0 = jnp.concatenate([x_all, jnp.zeros((b, TAIL - N_META, D_MODEL), F32), meta], axis=1)
    h0 = h0.reshape(b * lp, D_MODEL)

    w_in = gdn_w_in[0]
    ba = w_in[:, GDN_CONV_CH + GDN_V:].reshape(D_MODEL, 2, 2, GDN_HEADS)
    ba_dir = [_pad_cols(jnp.concatenate([ba[:, 0, d], ba[:, 1, d]], axis=1), 128) for d in range(2)]
    conv_w8 = jnp.pad(gdn_conv_w[0], ((0, 8 - GDN_CONV), (0, 0)))
    qkv, gates = _gdn_qkv_conv(h0.reshape(b, lp, D_MODEL), ln_g[0][None, :],
                               w_in[:, :GDN_CONV_CH].astype(BF16),
                               jnp.concatenate(ba_dir, axis=1).astype(BF16), conv_w8, lp)

    lane_a = lambda v: jnp.pad(v.astype(F32), ((0, 0), (GDN_HEADS, 128 - 2 * GDN_HEADS)))[:, None, :]
    o_fwd, o_bwd = _gdn_scan(qkv, gates, lane_a(gdn_a_log[0]), lane_a(gdn_dt_bias[0]), lp)
    h1 = _gdn_out(o_fwd.reshape(b * lp, GDN_V), o_bwd.reshape(b * lp, GDN_V), h0, ln_g[0][None, :],
                  w_in[:, GDN_CONV_CH:GDN_CONV_CH + GDN_V].astype(BF16),
                  gdn_o_norm_g[0][None, :], gdn_w_out[0].astype(BF16))

    w_in1 = mla_w_in[0]
    o1 = MLA_Q_LORA
    o2_ = o1 + MLA_KV_LORA
    o3 = o2_ + MLA_ROPE
    w1 = jnp.concatenate([w_in1[:, o3:], w_in1[:, :o1], w_in1[:, o1:o2_],
                          _pad_cols(w_in1[:, o2_:o3], 256)], axis=1).astype(BF16)
    proj1 = _norm_matmul(h1, ln_g[1][None, :], w1, 1536, BF16)
    proj1_3 = proj1.reshape(b, lp, proj1.shape[1])

    pos = jnp.concatenate([jnp.arange(s_len, dtype=F32) + N_META, jnp.zeros((TAIL - N_META,), F32),
                           jnp.arange(N_META, dtype=F32)])
    inv = ROPE_THETA ** (-jnp.arange(0, MLA_ROPE, 2, dtype=F32) / MLA_ROPE)
    ang = pos[:, None] * inv[None, :]
    zc = jnp.zeros_like(ang)
    cos_t = jnp.concatenate([jnp.cos(ang), jnp.cos(ang), zc, zc], axis=1)
    nsin_t = jnp.concatenate([-jnp.sin(ang), zc, zc, zc], axis=1)
    sin_t = jnp.concatenate([zc, jnp.sin(ang), zc, zc], axis=1)

    scale = MLA_DQK ** -0.5 * math.log2(math.e)
    gq = jnp.broadcast_to((mla_qk_q_g[0].astype(F32) * scale)[:, None], (MLA_DQK, 128))
    score_bound = (1.02 * MLA_DQK * scale * jnp.max(jnp.abs(mla_qk_q_g[0].astype(F32)))
                   * jnp.max(jnp.abs(mla_qk_k_g[0].astype(F32))))
    bounded = score_bound < ATT_BOUND_MAX
    shift_rows = jnp.zeros((MLA_DPAD - MLA_DQK, 128), F32).at[0, :].set(jnp.where(bounded, -score_bound, 0.0))
    qt = _mla_q(proj1_3, mla_q_norm_g[0][None, :], mla_w_uq[0].T.astype(BF16), gq[:MLA_NOPE],
                gq[MLA_NOPE:], jnp.cos(ang).T, jnp.sin(ang).T, shift_rows, lp)

    w_ukv = mla_w_ukv[0].reshape(MLA_KV_LORA, MLA_HEADS, MLA_NOPE + MLA_DV)
    w_ukv = jnp.concatenate([w_ukv[:, :, :MLA_NOPE].reshape(MLA_KV_LORA, -1),
                             w_ukv[:, :, MLA_NOPE:].reshape(MLA_KV_LORA, -1)], axis=1).astype(BF16)
    gk = mla_qk_k_g[0].astype(F32)
    k, vt = _mla_kv(proj1_3, mla_kv_norm_g[0][None, :], w_ukv, gk[None, :MLA_NOPE],
                    _lane_row(gk[MLA_NOPE:]), cos_t, nsin_t, sin_t, lp)

    o = _attention(bounded.astype(jnp.int32).reshape(1), qt, k, vt, s_len)

    h1_3 = h1.reshape(b, lp, D_MODEL)
    w_out1 = mla_w_out[0].astype(BF16)
    outs = []
    b0 = 0
    for x in xs:
        outs.append(_mla_out(o, proj1_3, h1_3, w_out1, b0, x.shape[0], s_len))
        b0 += x.shape[0]
    return tuple(outs)


def kernel(x_prompt, x_sample, meta_tokens, ln_g, gdn_w_in, gdn_conv_w, gdn_a_log, gdn_dt_bias,
           gdn_o_norm_g, gdn_w_out, mla_w_in, mla_q_norm_g, mla_kv_norm_g, mla_w_uq, mla_w_ukv,
           mla_qk_q_g, mla_qk_k_g, mla_w_out):
    return _trunk_all((x_prompt, x_sample), meta_tokens, ln_g, gdn_w_in, gdn_conv_w, gdn_a_log,
                      gdn_dt_bias, gdn_o_norm_g, gdn_w_out, mla_w_in, mla_q_norm_g, mla_kv_norm_g,
                      mla_w_uq, mla_w_ukv, mla_qk_q_g, mla_qk_k_g, mla_w_out)
```

```python
import functools
import math

import jax
import jax.numpy as jnp
from jax import lax
from jax.experimental import pallas as pl
from jax.experimental.pallas import tpu as pltpu

F32 = jnp.float32
BF16 = jnp.bfloat16

D_MODEL = 1024
N_META = 16
TAIL = 128
NORM_EPS = 1e-6

GDN_HEADS = 8
GDN_DK = 128
GDN_DV = 256
GDN_CONV = 5
GDN_CHUNK = 64
GDN_QK = GDN_HEADS * GDN_DK
GDN_V = GDN_HEADS * GDN_DV
GDN_CONV_CH = 2 * GDN_QK + GDN_V

MLA_HEADS = 16
MLA_Q_LORA = 512
MLA_KV_LORA = 256
MLA_NOPE = 128
MLA_ROPE = 64
MLA_DQK = MLA_NOPE + MLA_ROPE
MLA_DV = 128
MLA_V = MLA_HEADS * MLA_DV
MLA_DPAD = 256
ROPE_THETA = 10000.0

VMEM_LIMIT = 56 * 1024 * 1024


def _cparams(sem):
    return pltpu.CompilerParams(dimension_semantics=sem, vmem_limit_bytes=VMEM_LIMIT)


def _tile(n, target, mult):
    best = None
    t = mult
    while t <= min(n, target):
        if n % t == 0:
            best = t
        t += mult
    assert best is not None, (n, target, mult)
    return best


CONV_SUB = 64
CONV_LANES = GDN_DK
CONV_TC = 1024


def _qkv_conv_kernel(prev_ref, main_ref, next_ref, g_ref, w_ref, wba_ref, cw_ref, o_ref, ba_ref, ext_ref,
                     *, tr):
    half = GDN_CONV // 2
    nblk = GDN_CONV_CH // CONV_TC
    x = jnp.concatenate([prev_ref[0], main_ref[0], next_ref[0]], axis=0)
    ms = jnp.mean(x * x, axis=-1, keepdims=True)
    xn = (x * lax.rsqrt(ms + NORM_EPS) * g_ref[...]).astype(BF16)
    ba_ref[0] = jnp.dot(xn, wba_ref[...], preferred_element_type=F32)[8:8 + tr]

    def project(j):
        ext_ref[j] = jnp.dot(xn, w_ref[:, j * CONV_TC:(j + 1) * CONV_TC], preferred_element_type=F32)

    def conv_silu(j, sb, lb):
        lanes = slice(j * CONV_TC + lb * CONV_LANES, j * CONV_TC + (lb + 1) * CONV_LANES)
        nrow = CONV_SUB + 16
        x2 = ext_ref[j, sb * CONV_SUB:sb * CONV_SUB + nrow, lb * CONV_LANES:(lb + 1) * CONV_LANES]
        acc = x2[8:8 + CONV_SUB] * cw_ref[half:half + 1, lanes]
        for s in range(1, half + 1):
            down = pltpu.roll(x2, s, 0)
            acc = acc + down[8:8 + CONV_SUB] * cw_ref[half - s:half - s + 1, lanes]
            up = pltpu.roll(x2, nrow - s, 0)
            acc = acc + up[8:8 + CONV_SUB] * cw_ref[half + s:half + s + 1, lanes]
        return acc * jax.nn.sigmoid(acc)

    def finish(j):
        for sb in range(tr // CONV_SUB):
            rows = slice(sb * CONV_SUB, (sb + 1) * CONV_SUB)
            for lb in range(CONV_TC // CONV_LANES):
                y = conv_silu(j, sb, lb)
                lanes = slice(j * CONV_TC + lb * CONV_LANES, j * CONV_TC + (lb + 1) * CONV_LANES)
                if j < 2:
                    qscale = GDN_DK ** -0.5 if j == 0 else 1.0
                    ss = jnp.sum(y * y, axis=-1, keepdims=True)
                    y = y * (lax.rsqrt(ss + NORM_EPS) * qscale)
                o_ref[0, rows, lanes] = y.astype(BF16)

    project(0)
    for j in range(nblk):
        if j + 1 < nblk:
            project(j + 1)
        finish(j)


def _gdn_qkv_conv(h3, g, w_qkv, w_ba, conv_w8, lp):
    b = h3.shape[0]
    nba = w_ba.shape[1]
    tr = _tile(lp, 384, CONV_SUB)
    nb8 = lp // 8
    tb = tr // 8
    kern = functools.partial(_qkv_conv_kernel, tr=tr)
    const = lambda bi, i: (0, 0)
    return pl.pallas_call(
        kern,
        grid=(b, lp // tr),
        in_specs=[
            pl.BlockSpec((1, 8, D_MODEL), lambda bi, i: (bi, (i * tb + nb8 - 1) % nb8, 0)),
            pl.BlockSpec((1, tr, D_MODEL), lambda bi, i: (bi, i, 0)),
            pl.BlockSpec((1, 8, D_MODEL), lambda bi, i: (bi, ((i + 1) * tb) % nb8, 0)),
            pl.BlockSpec((1, D_MODEL), const),
            pl.BlockSpec((D_MODEL, GDN_CONV_CH), const),
            pl.BlockSpec((D_MODEL, nba), const),
            pl.BlockSpec((8, GDN_CONV_CH), const),
        ],
        out_specs=[
            pl.BlockSpec((1, tr, GDN_CONV_CH), lambda bi, i: (bi, i, 0)),
            pl.BlockSpec((1, tr, nba), lambda bi, i: (bi, i, 0)),
        ],
        out_shape=[
            jax.ShapeDtypeStruct((b, lp, GDN_CONV_CH), BF16),
            jax.ShapeDtypeStruct((b, lp, nba), F32),
        ],
        scratch_shapes=[pltpu.VMEM((GDN_CONV_CH // CONV_TC, tr + 16, CONV_TC), F32)],
        compiler_params=_cparams(("parallel", "parallel")),
        name="gdn_qkv_conv",
    )(h3, h3, h3, g, w_qkv, w_ba, conv_w8)


SCAN_NB = 2


def _split3(x):
    hi = x.astype(BF16)
    r1 = x - hi.astype(F32)
    mid = r1.astype(BF16)
    lo = (r1 - mid.astype(F32)).astype(BF16)
    return hi, mid, lo


def _gdn_scan_kernel(qf_ref, kf_ref, vf_ref, baf_ref, qb_ref, kb_ref, vb_ref, bab_ref, alog_ref, dtb_ref,
                     of_ref, ob_ref, s_ref, sb_ref, u_ref, qw_ref, attn_ref, kd_ref, eg_ref, *, nc):
    t = pl.program_id(1)
    c = GDN_CHUNK
    c2 = 2 * c
    npair = GDN_HEADS // 2
    o_refs = (of_ref, ob_ref)

    @pl.when(t == 0)
    def _():
        for ref in (s_ref, sb_ref, u_ref, qw_ref, attn_ref, kd_ref, eg_ref):
            ref[...] = jnp.zeros_like(ref)

    nb = of_ref.shape[0]
    nprob = nb * 2 * npair
    applied = [dict(sq=pi // npair, heads=(2 * (pi % npair), 2 * (pi % npair) + 1),
                    qs=[None, None], vnew=[None, None]) for pi in range(nprob)]

    def read_state(pi, hh):
        ap = applied[pi]
        rs = slice(hh * c, (hh + 1) * c)
        qws = jnp.dot(qw_ref[pi, hh], sb_ref[ap["sq"], ap["heads"][hh]], preferred_element_type=F32)
        ap["qs"][hh] = qws[:c]
        ap["vnew"][hh] = (u_ref[pi, rs, :] - qws[c:]).astype(BF16)

    def write_out(pi):
        ap = applied[pi]
        h0, h1 = ap["heads"]
        bi, d = divmod(ap["sq"], 2)
        vnew2 = jnp.concatenate(ap["vnew"], axis=0)
        o2 = jnp.concatenate(ap["qs"], axis=0) + jnp.dot(attn_ref[pi], vnew2, preferred_element_type=F32)
        o_refs[d][bi, :, h0 * GDN_DV:(h0 + 1) * GDN_DV] = o2[:c].astype(BF16)
        o_refs[d][bi, :, h1 * GDN_DV:(h1 + 1) * GDN_DV] = o2[c:].astype(BF16)

    def update_state(pi, hh):
        ap = applied[pi]
        sq, h = ap["sq"], ap["heads"][hh]
        rs = slice(hh * c, (hh + 1) * c)
        a_h = GDN_HEADS + h
        upd = lax.dot_general(kd_ref[pi, rs, :], ap["vnew"][hh], (((0,), (0,)), ((), ())),
                              preferred_element_type=F32)
        s_new = s_ref[sq, h] * eg_ref[sq, :, a_h:a_h + 1] + upd
        s_ref[sq, h] = s_new
        sb_ref[sq, h] = s_new.astype(BF16)

    pairs_hh = [(pi, hh) for pi in range(nprob) for hh in range(2)]
    apply_ops = ([functools.partial(read_state, pi, hh) for pi, hh in pairs_hh]
                 + [functools.partial(write_out, pi) for pi in range(nprob)]
                 + [functools.partial(update_state, pi, hh) for pi, hh in pairs_hh])

    def emit_apply(n):
        for _ in range(min(n, len(apply_ops))):
            apply_ops.pop(0)()

    emit_apply(len(pairs_hh) // 2)
    tp = jnp.minimum(t, nc - 1)
    ri = lax.broadcasted_iota(jnp.int32, (c2, c2), 0)
    ci = lax.broadcasted_iota(jnp.int32, (c2, c2), 1)
    same = (ri >> 6) == (ci >> 6)
    offdiag = ri != ci
    top =lax.broadcasted_iota(jnp.int32, (c2, 1), 0) < c
    left = lax.broadcasted_iota(jnp.int32, (1, c2), 1) < c
    row_id = lax.broadcasted_iota(jnp.int32, (c, 128), 0)

    in_refs = ((qf_ref, kf_ref, vf_ref, baf_ref), (qb_ref, kb_ref, vb_ref, bab_ref))
    seqs = []
    for sq in range(2 * nb):
        bi, d = divmod(sq, 2)
        q_ref, k_ref, v_ref, ba_ref = in_refs[d]
        seq = tp if d == 0 else nc - 1 - tp
        blk = jnp.where(seq < 2, nc - 2 + seq, seq - 2)
        first_valid = jnp.where(blk == nc - 2, c, jnp.where(blk == nc - 1, c - N_META, 0))
        valid = row_id >= first_valid
        ba = ba_ref[bi]
        beta = jnp.where(valid, jax.nn.sigmoid(ba), 0.0)
        xs = ba + dtb_ref[d]
        softplus = jnp.maximum(xs, 0.0) + jnp.log(1.0 + jnp.exp(-jnp.abs(xs)))
        g = jnp.where(valid, -jnp.exp(alog_ref[d]) * softplus, 0.0)
        incl = same & ((ri >= ci) if d == 0 else (ri <= ci))
        tri = jnp.where(incl, 1.0, 0.0).astype(BF16)
        ghi, gmid, glo = _split3(jnp.concatenate([g, g], axis=0))
        gc2 = (jnp.dot(tri, ghi, preferred_element_type=F32)
               + jnp.dot(tri, gmid, preferred_element_type=F32)
               + jnp.dot(tri, glo, preferred_element_type=F32))
        seqs.append(dict(q=q_ref.at[bi], k=k_ref.at[bi], v=v_ref.at[bi], incl=incl, gc2=gc2, gc2t=gc2.T,
                         beta2=jnp.concatenate([beta, beta], axis=0),
                         gtot=jnp.sum(g, axis=0, keepdims=True)))

    def stacked(ref, h0, h1, width):
        return jnp.concatenate([ref[:, h0 * width:(h0 + 1) * width],
                                ref[:, h1 * width:(h1 + 1) * width]], axis=0)

    probs = []
    for dd in seqs:
        for p in range(npair):
            h0, h1 = 2 * p, 2 * p + 1
            a0, a1 = GDN_HEADS + h0, GDN_HEADS + h1
            col = jnp.where(top, dd["gc2"][:, a0:a0 + 1], dd["gc2"][:, a1:a1 + 1])
            row = jnp.where(left, dd["gc2t"][a0:a0 + 1, :], dd["gc2t"][a1:a1 + 1, :])
            bcol = jnp.where(top, dd["beta2"][:, h0:h0 + 1], dd["beta2"][:, h1:h1 + 1])
            tot = jnp.where(top, dd["gtot"][:, a0:a0 + 1], dd["gtot"][:, a1:a1 + 1])
            dec = jnp.exp(jnp.where(dd["incl"], col - row, -jnp.inf))
            kst = stacked(dd["k"], h0, h1, GDN_DK)
            qst = stacked(dd["q"], h0, h1, GDN_DK)
            vst = stacked(dd["v"], h0, h1, GDN_DV)
            kf = kst.astype(F32)
            kb = kf * bcol
            sc = lax.dot_general(jnp.concatenate([qst, kb.astype(BF16)], axis=0), kst,
                                 (((1,), (1,)), ((), ())), preferred_element_type=F32)
            egc = jnp.exp(col)
            probs.append(dict(
                attn=(sc[:c2] * dec).astype(BF16),
                a=jnp.where(offdiag, sc[c2:] * dec, 0.0),
                rhs=jnp.concatenate([vst.astype(F32) * bcol, kb * egc], axis=1).astype(BF16),
                qg=(qst.astype(F32) * egc).astype(BF16),
                kd=(kf * jnp.exp(tot - col)).astype(BF16)))

    emit_apply(len(pairs_hh) // 2)

    for _ in _unit_lower_inverse_staged(probs):
        emit_apply(3 * nb)
    emit_apply(len(apply_ops))

    for pi, pr in enumerate(probs):
        uw = jnp.dot(pr["tinv"].astype(BF16), pr["rhs"], preferred_element_type=F32)
        u_ref[pi] = uw[:, :GDN_DV]
        w = uw[:, GDN_DV:].astype(BF16)
        for hh in range(2):
            rs = slice(hh * c, (hh + 1) * c)
            qw_ref[pi, hh] = jnp.concatenate([pr["qg"][rs], w[rs]], axis=0)
        attn_ref[pi] = pr["attn"]
        kd_ref[pi] = pr["kd"]
    for sq, dd in enumerate(seqs):
        eg_ref[sq] = jnp.exp(dd["gtot"])


def _unit_lower_inverse_staged(probs):
    c = GDN_CHUNK
    nside = 4
    ri = lax.broadcasted_iota(jnp.int32, (c, nside * c), 0)
    ci = lax.broadcasted_iota(jnp.int32, (c, nside * c), 1)
    lane_blk = ci >> 6
    within = ci & (c - 1)
    diag16 = (ri >> 4) == (within >> 4)
    eye = jnp.where(ri == within, 1.0, 0.0).astype(F32)
    left = lax.broadcasted_iota(jnp.int32, (1, 2 * c), 1) < c

    def blockdiag(y):
        return jnp.concatenate([jnp.where(lane_blk == r, y, 0.0) for r in range(nside)],
                               axis=0).astype(BF16)

    def mm(x, ybd):
        return jnp.dot(x.astype(BF16), ybd, preferred_element_type=F32)

    groups = []
    for g0 in range(0, len(probs), 2):
        pa, pb = probs[g0], probs[g0 + 1]
        a = jnp.concatenate([pa["a"][:c] + pa["a"][c:], pb["a"][:c] + pb["a"][c:]], axis=1)
        ad = jnp.where(diag16, a, 0.0)
        groups.append(dict(pairs=(pa, pb), ad=ad, an=blockdiag(a - ad), dinv=eye - ad))
    for gr in groups:
        gr["p"] = mm(gr["ad"], blockdiag(gr["ad"]))
    yield
    for gr in groups:
        pbd = blockdiag(gr["p"])
        gr["dinv"] = gr["dinv"] + mm(gr["dinv"], pbd)
        gr["p"] = mm(gr["p"], pbd)
    yield
    for gr in groups:
        pbd = blockdiag(gr["p"])
        gr["dinv"] = gr["dinv"] + mm(gr["dinv"], pbd)
        gr["p"] = mm(gr["p"], pbd)
    yield
    for gr in groups:
        gr["dinv"] = gr["dinv"] + mm(gr["dinv"], blockdiag(gr["p"]))
    yield
    for gr in groups:
        gr["m"] = mm(gr["dinv"], gr["an"])
    yield
    for gr in groups:
        gr["m2"] = blockdiag(mm(gr["m"], blockdiag(gr["m"])))
    yield
    for gr in groups:
        x = eye - gr["m"]
        gr["x"] = x + mm(x, gr["m2"])
    yield
    for gr in groups:
        t = mm(gr["x"], blockdiag(gr["dinv"]))
        for i, pr in enumerate(gr["pairs"]):
            half = t[:, 2 * c * i:2 * c * (i + 1)]
            pr["tinv"] = jnp.concatenate([jnp.where(left, half, 0.0), jnp.where(left, 0.0, half)],
                                         axis=0)


def _gdn_scan(qkv, proj3, alog_rows, dtb_rows, lp):
    b = qkv.shape[0]
    nc = lp // GDN_CHUNK
    ba_col0 = 0

    def blk_of(seq):
        return jnp.where(seq < 2, nc - 2 + seq, seq - 2)

    def chunk_specs(d):
        prep = lambda t: jnp.minimum(t, nc - 1)
        seq = prep if d == 0 else (lambda t: nc - 1 - prep(t))
        return [
            pl.BlockSpec((nb, GDN_CHUNK, GDN_QK), lambda bi, t: (bi, blk_of(seq(t)), 0)),
            pl.BlockSpec((nb, GDN_CHUNK, GDN_QK), lambda bi, t: (bi, blk_of(seq(t)), 1)),
            pl.BlockSpec((nb, GDN_CHUNK, GDN_V), lambda bi, t: (bi, blk_of(seq(t)), 1)),
            pl.BlockSpec((nb, GDN_CHUNK, 128), lambda bi, t: (bi, blk_of(seq(t)), ba_col0 + d)),
        ]

    nb = SCAN_NB if b % SCAN_NB == 0 else 1
    const = pl.BlockSpec((2, 1, 128), lambda bi, t: (0, 0, 0))
    kern = functools.partial(_gdn_scan_kernel, nc=nc)
    applied = lambda t: jnp.maximum(t - 1, 0)
    nseq = 2 * nb
    nprob = nseq * (GDN_HEADS // 2)
    c2 = 2 * GDN_CHUNK
    return pl.pallas_call(
        kern,
        grid=(b // nb, nc + 1),
        in_specs=chunk_specs(0) + chunk_specs(1) + [const, const],
        out_specs=[
            pl.BlockSpec((nb, GDN_CHUNK, GDN_V), lambda bi, t: (bi, blk_of(applied(t)), 0)),
            pl.BlockSpec((nb, GDN_CHUNK, GDN_V), lambda bi, t: (bi, blk_of(nc - 1 - applied(t)), 0)),
        ],
        out_shape=[jax.ShapeDtypeStruct((b, lp, GDN_V), BF16)] * 2,
        scratch_shapes=[
            pltpu.VMEM((nseq, GDN_HEADS, GDN_DK, GDN_DV), F32),
            pltpu.VMEM((nseq, GDN_HEADS, GDN_DK, GDN_DV), BF16),
            pltpu.VMEM((nprob, c2, GDN_DV), F32),
            pltpu.VMEM((nprob, 2, c2, GDN_DK), BF16),
            pltpu.VMEM((nprob, c2, c2), BF16),
            pltpu.VMEM((nprob, c2, GDN_DK), BF16),
            pltpu.VMEM((nseq, 1, 128), F32),
        ],
        compiler_params=_cparams(("parallel", "arbitrary")),
        name="gdn_scan",
    )(qkv, qkv, qkv, proj3, qkv, qkv, qkv, proj3, alog_rows, dtb_rows)


OUT_SUB = 256


def _gdn_out_kernel(of_ref, ob_ref, h_ref, ln_ref, wz_ref, g_ref, w_ref, o_ref):
    nsub = o_ref.shape[0] // OUT_SUB
    rows = [slice(sb * OUT_SUB, (sb + 1) * OUT_SUB) for sb in range(nsub)]

    def gate_logits(sb):
        x = h_ref[rows[sb], :]
        ms = jnp.mean(x * x, axis=-1, keepdims=True)
        xn = (x * lax.rsqrt(ms + NORM_EPS) * ln_ref[...]).astype(BF16)
        return jnp.dot(xn, wz_ref[...], preferred_element_type=F32)

    z_next = gate_logits(0)
    for sb in range(nsub):
        z = z_next
        if sb + 1 < nsub:
            z_next = gate_logits(sb + 1)
        o = of_ref[rows[sb], :].astype(F32) + ob_ref[rows[sb], :].astype(F32)
        gate = z * jax.nn.sigmoid(z)
        ys = []
        for h in range(GDN_HEADS):
            lanes = slice(h * GDN_DV, (h + 1) * GDN_DV)
            oh = o[:, lanes]
            ms = jnp.mean(oh * oh, axis=-1, keepdims=True)
            ys.append((oh * lax.rsqrt(ms + NORM_EPS) * g_ref[...] * gate[:, lanes]).astype(BF16))
        y = jnp.concatenate(ys, axis=1)
        o_ref[rows[sb], :] = h_ref[rows[sb], :] + jnp.dot(y, w_ref[...], preferred_element_type=F32)


def _gdn_out(o_fwd, o_bwd, h0, ln_row, w_z, g_row, w_out):
    rows = h0.shape[0]
    tm = _tile(rows, 768, OUT_SUB)
    const = lambda i: (0, 0)
    return pl.pallas_call(
        _gdn_out_kernel,
        grid=(rows // tm,),
        in_specs=[
            pl.BlockSpec((tm, GDN_V), lambda i: (i, 0)),
            pl.BlockSpec((tm, GDN_V), lambda i: (i, 0)),
            pl.BlockSpec((tm, D_MODEL), lambda i: (i, 0)),
            pl.BlockSpec((1, D_MODEL), const),
            pl.BlockSpec((D_MODEL, GDN_V), const),
            pl.BlockSpec((1, GDN_DV), const),
            pl.BlockSpec((GDN_V, D_MODEL), const),
        ],
        out_specs=pl.BlockSpec((tm, D_MODEL), lambda i: (i, 0)),
        out_shape=jax.ShapeDtypeStruct((rows, D_MODEL), F32),
        compiler_params=_cparams(("parallel",)),
        name="gdn_out",
    )(o_fwd, o_bwd, h0, ln_row, w_z, g_row, w_out)


def _rope(r, cos, nsin_lo, sin_hi):
    return r * cos + pltpu.roll(r, 96, 1) * nsin_lo + pltpu.roll(r, 32, 1) * sin_hi


def _mla_q_kernel(cq_ref, g1_ref, wt_ref, ga_ref, gr_ref, cos_ref, sin_ref, shift_ref, qt_ref):
    cq = cq_ref[0].astype(F32)
    tm = cq.shape[0]
    ms = jnp.mean(cq * cq, axis=-1, keepdims=True)
    cqt = (cq * lax.rsqrt(ms + NORM_EPS) * g1_ref[...]).T.astype(BF16)
    q = jnp.dot(wt_ref[...], cqt, preferred_element_type=F32)
    half = MLA_ROPE // 2
    tile = lambda g: jnp.concatenate([g] * (tm // 128), axis=1)
    ga = tile(ga_ref[...])
    gr1, gr2 = tile(gr_ref[0:half, :]), tile(gr_ref[half:MLA_ROPE, :])
    cos, sin = cos_ref[...], sin_ref[...]
    pad_rows = tile(shift_ref[...]).astype(BF16)
    for h in range(MLA_HEADS):
        r0 = h * MLA_DQK
        a = q[r0:r0 + MLA_NOPE]
        x1 = q[r0 + MLA_NOPE:r0 + MLA_NOPE + half]
        x2 = q[r0 + MLA_NOPE + half:r0 + MLA_DQK]
        ss = (jnp.sum(a * a, axis=0, keepdims=True) + jnp.sum(x1 * x1, axis=0, keepdims=True)
              + jnp.sum(x2 * x2, axis=0, keepdims=True))
        inv = lax.rsqrt(ss * (1.0 / MLA_DQK) + NORM_EPS)
        x1 = x1 * inv * gr1
        x2 = x2 * inv * gr2
        qt_ref[0, h, 0:MLA_NOPE, :] = (a * inv * ga).astype(BF16)
        qt_ref[0, h, MLA_NOPE:MLA_NOPE + half, :] = (x1 * cos - x2 * sin).astype(BF16)
        qt_ref[0, h, MLA_NOPE + half:MLA_DQK, :] = (x2 * cos + x1 * sin).astype(BF16)
        qt_ref[0, h, MLA_DQK:MLA_DPAD, :] = pad_rows


def _mla_q(proj3, g1, w_uq_t, ga, gr, cos_t, sin_t, shift_rows, lp):
    b = proj3.shape[0]
    tm = _tile(lp, 384, 128)
    cqcol = MLA_V // MLA_Q_LORA
    const = lambda bi, i: (0, 0)
    return pl.pallas_call(
        _mla_q_kernel,
        grid=(b, lp // tm),
        in_specs=[
            pl.BlockSpec((1, tm, MLA_Q_LORA), lambda bi, i: (bi, i, cqcol)),
            pl.BlockSpec((1, MLA_Q_LORA), const),
            pl.BlockSpec((MLA_HEADS * MLA_DQK, MLA_Q_LORA), const),
            pl.BlockSpec((MLA_NOPE, 128), const),
            pl.BlockSpec((MLA_ROPE, 128), const),
            pl.BlockSpec((MLA_ROPE // 2, tm), lambda bi, i: (0, i)),
            pl.BlockSpec((MLA_ROPE // 2, tm), lambda bi, i: (0, i)),
            pl.BlockSpec((MLA_DPAD - MLA_DQK, 128), const),
        ],
        out_specs=pl.BlockSpec((1, MLA_HEADS, MLA_DPAD, tm), lambda bi, i: (bi, 0, 0, i)),
        out_shape=jax.ShapeDtypeStruct((b, MLA_HEADS, MLA_DPAD, lp), BF16),
        compiler_params=_cparams(("parallel", "parallel")),
        name="mla_q",
    )(proj3, g1, w_uq_t, ga, gr, cos_t, sin_t, shift_rows)


def _mla_kv_kernel(h_ref, ln_ref, ws_ref, wb_ref, g1_ref, w_ref, ga_ref, gr_ref, cos_ref, nsin_ref, sin_ref,
                   zq_ref, k_ref, vt_ref):
    x = h_ref[0]
    xn = (x * lax.rsqrt(jnp.mean(x * x, axis=-1, keepdims=True) + NORM_EPS) * ln_ref[...]).astype(BF16)
    small = jnp.dot(xn, ws_ref[...], preferred_element_type=F32)
    ckv = small[:, :MLA_KV_LORA]
    ms = jnp.mean(ckv * ckv, axis=-1, keepdims=True)
    cn = (ckv * lax.rsqrt(ms + NORM_EPS) * g1_ref[...]).astype(BF16)
    kv = jnp.dot(cn, w_ref[...], preferred_element_type=F32)
    zq_ref[0] = jnp.dot(xn, wb_ref[...], preferred_element_type=F32).astype(BF16)
    kpe = small[:, MLA_KV_LORA:]
    sq_pe = kpe * kpe
    kr = _rope(kpe * gr_ref[...], cos_ref[...], nsin_ref[...], sin_ref[...])
    one_hot = jnp.where(lax.broadcasted_iota(jnp.int32, (1, 128), 1) == MLA_ROPE, 1.0, 0.0).astype(F32)
    for h in range(MLA_HEADS):
        kn = kv[:, h * MLA_NOPE:(h + 1) * MLA_NOPE]
        ss = jnp.sum(kn * kn + sq_pe, axis=-1, keepdims=True)
        inv = lax.rsqrt(ss * (1.0 / MLA_DQK) + NORM_EPS)
        k_ref[0, h, :, 0:MLA_NOPE] = (kn * inv * ga_ref[...]).astype(BF16)
        k_ref[0, h, :, MLA_NOPE:MLA_DPAD] = (kr * inv + one_hot).astype(BF16)
        v = kv[:, MLA_HEADS * MLA_NOPE + h * MLA_DV:MLA_HEADS * MLA_NOPE + (h + 1) * MLA_DV]
        vt_ref[0, h] = v.T.astype(BF16)


def _mla_proj_kv(h3, ln_row, w_small, w_big, g1, w_ukv, ga, gr, cos, nsin, sin, lp):
    b = h3.shape[0]
    tm = _tile(lp, 384, 128)
    const = lambda bi, i: (0, 0)
    nbig = w_big.shape[1]
    return pl.pallas_call(
        _mla_kv_kernel,
        grid=(b, lp // tm),
        in_specs=[
            pl.BlockSpec((1, tm, D_MODEL), lambda bi, i: (bi, i, 0)),
            pl.BlockSpec((1, D_MODEL), const),
            pl.BlockSpec((D_MODEL, MLA_KV_LORA + 128), const),
            pl.BlockSpec((D_MODEL, nbig), const),
            pl.BlockSpec((1, MLA_KV_LORA), const),
            pl.BlockSpec((MLA_KV_LORA, MLA_HEADS * (MLA_NOPE + MLA_DV)), const),
            pl.BlockSpec((1, 128), const),
            pl.BlockSpec((1, 128), const),
            pl.BlockSpec((tm, 128), lambda bi, i: (i, 0)),
            pl.BlockSpec((tm, 128), lambda bi, i: (i, 0)),
            pl.BlockSpec((tm, 128), lambda bi, i: (i, 0)),
        ],
        out_specs=[
            pl.BlockSpec((1, tm, nbig), lambda bi, i: (bi, i, 0)),
            pl.BlockSpec((1, MLA_HEADS, tm, MLA_DPAD), lambda bi, i: (bi, 0, i, 0)),
            pl.BlockSpec((1, MLA_HEADS, MLA_DV, tm), lambda bi, i: (bi, 0, 0, i)),
        ],
        out_shape=[
            jax.ShapeDtypeStruct((b, lp, nbig), BF16),
            jax.ShapeDtypeStruct((b, MLA_HEADS, lp, MLA_DPAD), BF16),
            jax.ShapeDtypeStruct((b, MLA_HEADS, MLA_DV, lp), BF16),
        ],
        compiler_params=_cparams(("parallel", "parallel")),
        name="mla_proj_kv",
    )(h3, ln_row, w_small, w_big, g1, w_ukv, ga, gr, cos, nsin, sin)


ATT_TK = 1024
ATT_TQ = 512
ATT_AHEAD = 2
ATT_BOUND_MAX = 60.0


def _attn_kernel(bounded_ref, qt_ref, k_ref, vt_ref, o_ref, *, s_len):
    tq = qt_ref.shape[3]
    nstrip = tq // ATT_TQ
    chunks = [(ck * ATT_TK, ATT_TK) for ck in range(s_len // ATT_TK)] + [(s_len, TAIL)]
    units = [(i, s) for i in range(len(chunks)) for s in range(nstrip)]
    is_meta = lax.broadcasted_iota(jnp.int32, (TAIL, 1), 0) >= TAIL - N_META
    tail_bias = jnp.where(is_meta, 0.0, -jnp.inf).astype(F32)

    def scores(u):
        i, s = units[u]
        k0, nk = chunks[i]
        st = jnp.dot(k_ref[0, 0, k0:k0 + nk, :], qt_ref[0, 0, :, s * ATT_TQ:(s + 1) * ATT_TQ],
                     preferred_element_type=F32)
        return st + tail_bias if i == len(chunks) - 1 else st

    def run(bounded):
        m = [jnp.full((1, ATT_TQ), -jnp.inf, F32) for _ in range(nstrip)]
        l = [jnp.zeros((1, ATT_TQ), F32) for _ in range(nstrip)]
        acc = [jnp.zeros((MLA_DV, ATT_TQ), F32) for _ in range(nstrip)]
        pending = [scores(u) for u in range(min(ATT_AHEAD, len(units)))]
        for u, (i, s) in enumerate(units):
            if u + ATT_AHEAD < len(units):
                pending.append(scores(u + ATT_AHEAD))
            st = pending.pop(0)
            k0, nk = chunks[i]
            if bounded:
                p = jnp.exp2(st)
                l[s] = l[s] + jnp.sum(p, axis=0, keepdims=True)
                acc[s] = acc[s] + jnp.dot(vt_ref[0, 0, :, k0:k0 + nk], p.astype(BF16),
                                          preferred_element_type=F32)
            else:
                m_new = jnp.maximum(m[s], jnp.max(st, axis=0, keepdims=True))
                alpha = jnp.exp2(m[s] - m_new)
                p = jnp.exp2(st - m_new)
                l[s] = alpha * l[s] + jnp.sum(p, axis=0, keepdims=True)
                acc[s] = alpha * acc[s] + jnp.dot(vt_ref[0, 0, :, k0:k0 + nk], p.astype(BF16),
                                                  preferred_element_type=F32)
                m[s] = m_new
        for s in range(nstrip):
            o_ref[0, s * ATT_TQ:(s + 1) * ATT_TQ, :] = (acc[s] / l[s]).T.astype(o_ref.dtype)

    @pl.when(bounded_ref[0] == 1)
    def _():
        run(True)

    @pl.when(bounded_ref[0] != 1)
    def _():
        run(False)


def _attention(bounded, qt, k, vt, s_len):
    b = qt.shape[0]
    lp = k.shape[2]
    assert s_len % ATT_TK == 0 and lp == s_len + TAIL
    tq = _tile(s_len, 1024, ATT_TQ)
    kern = functools.partial(_attn_kernel, s_len=s_len)
    return pl.pallas_call(
        kern,
        grid=(b, MLA_HEADS, s_len // tq),
        in_specs=[
            pl.BlockSpec(memory_space=pltpu.SMEM),
            pl.BlockSpec((1, 1, MLA_DPAD, tq), lambda bi, h, i: (bi, h, 0, i)),
            pl.BlockSpec((1, 1, lp, MLA_DPAD), lambda bi, h, i: (bi, h, 0, 0)),
            pl.BlockSpec((1, 1, MLA_DV, lp), lambda bi, h, i: (bi, h, 0, 0)),
        ],
        out_specs=pl.BlockSpec((1, tq, MLA_DV), lambda bi, h, i: (bi, i, h)),
        out_shape=jax.ShapeDtypeStruct((b, s_len, MLA_V), BF16),
        compiler_params=_cparams(("parallel", "parallel", "arbitrary")),
        name="mla_attention",
    )(bounded, qt, k, vt)


def _mla_out_kernel(o_ref, z_ref, h_ref, w_ref, y_ref):
    z = z_ref[0].astype(F32)
    y = (o_ref[0].astype(F32) * (z * jax.nn.sigmoid(z))).astype(BF16)
    y_ref[0] = h_ref[0] + jnp.dot(y, w_ref[...], preferred_element_type=F32)


def _mla_out(o, proj3, h3, w_out, b0, nb, s_len):
    tm = _tile(s_len, 1024, 128)
    return pl.pallas_call(
        _mla_out_kernel,
        grid=(nb, s_len // tm),
        in_specs=[
            pl.BlockSpec((1, tm, MLA_V), lambda bi, i: (bi + b0, i, 0)),
            pl.BlockSpec((1, tm, MLA_V), lambda bi, i: (bi + b0, i, 0)),
            pl.BlockSpec((1, tm, D_MODEL), lambda bi, i: (bi + b0, i, 0)),
            pl.BlockSpec((MLA_V, D_MODEL), lambda bi, i: (0, 0)),
        ],
        out_specs=pl.BlockSpec((1, tm, D_MODEL), lambda bi, i: (bi, i, 0)),
        out_shape=jax.ShapeDtypeStruct((nb, s_len, D_MODEL), F32),
        compiler_params=_cparams(("parallel", "parallel")),
        name="mla_out",
    )(o, proj3, h3, w_out)


def _pad_cols(w, n):
    return jnp.pad(w, ((0, 0), (0, n - w.shape[1])))


def _lane_row(v, n=128):
    return jnp.pad(v.astype(F32), (0, n - v.shape[0]))[None, :]


def _trunk_all(xs, meta_tokens, ln_g, gdn_w_in, gdn_conv_w, gdn_a_log, gdn_dt_bias, gdn_o_norm_g,
               gdn_w_out, mla_w_in, mla_q_norm_g, mla_kv_norm_g, mla_w_uq, mla_w_ukv, mla_qk_q_g,
               mla_qk_k_g, mla_w_out):
    s_len = xs[0].shape[1]
    assert all(x.shape[1] == s_len for x in xs) and s_len % 128 == 0
    lp = s_len + TAIL
    x_all = jnp.concatenate(xs, axis=0)
    b = x_all.shape[0]
    meta = jnp.broadcast_to(meta_tokens[None].astype(F32), (b, N_META, D_MODEL))
    h0 = jnp.concatenate([x_all, jnp.zeros((b, TAIL - N_META, D_MODEL), F32), meta], axis=1)
    h0 = h0.reshape(b * lp, D_MODEL)

    w_in = gdn_w_in[0]
    ba = w_in[:, GDN_CONV_CH + GDN_V:].reshape(D_MODEL, 2, 2, GDN_HEADS)
    ba_dir = [_pad_cols(jnp.concatenate([ba[:, 0, d], ba[:, 1, d]], axis=1), 128) for d in range(2)]
    conv_w8 = jnp.pad(gdn_conv_w[0], ((0, 8 - GDN_CONV), (0, 0)))
    qkv, gates = _gdn_qkv_conv(h0.reshape(b, lp, D_MODEL), ln_g[0][None, :],
                               w_in[:, :GDN_CONV_CH].astype(BF16),
                               jnp.concatenate(ba_dir, axis=1).astype(BF16), conv_w8, lp)

    lane_a = lambda v: jnp.pad(v.astype(F32), ((0, 0), (GDN_HEADS, 128 - 2 * GDN_HEADS)))[:, None, :]
    o_fwd, o_bwd = _gdn_scan(qkv, gates, lane_a(gdn_a_log[0]), lane_a(gdn_dt_bias[0]), lp)
    h1 = _gdn_out(o_fwd.reshape(b * lp, GDN_V), o_bwd.reshape(b * lp, GDN_V), h0, ln_g[0][None, :],
                  w_in[:, GDN_CONV_CH:GDN_CONV_CH + GDN_V].astype(BF16),
                  gdn_o_norm_g[0][None, :], gdn_w_out[0].astype(BF16))

    w_in1 = mla_w_in[0]
    o1 = MLA_Q_LORA
    o2_ = o1 + MLA_KV_LORA
    o3 = o2_ + MLA_ROPE
    w_small = jnp.concatenate([w_in1[:, o1:o2_], _pad_cols(w_in1[:, o2_:o3], 128)], axis=1).astype(BF16)
    w_big = jnp.concatenate([w_in1[:, o3:], w_in1[:, :o1]], axis=1).astype(BF16)
    h1_3 = h1.reshape(b, lp, D_MODEL)

    pos = jnp.concatenate([jnp.arange(s_len, dtype=F32) + N_META, jnp.zeros((TAIL - N_META,), F32),
                           jnp.arange(N_META, dtype=F32)])
    inv = ROPE_THETA ** (-jnp.arange(0, MLA_ROPE, 2, dtype=F32) / MLA_ROPE)
    ang = pos[:, None] * inv[None, :]
    zc = jnp.zeros_like(ang)
    cos_t = jnp.concatenate([jnp.cos(ang), jnp.cos(ang), zc, zc], axis=1)
    nsin_t = jnp.concatenate([-jnp.sin(ang), zc, zc, zc], axis=1)
    sin_t = jnp.concatenate([zc, jnp.sin(ang), zc, zc], axis=1)

    scale = MLA_DQK ** -0.5 * math.log2(math.e)
    gq = jnp.broadcast_to((mla_qk_q_g[0].astype(F32) * scale)[:, None], (MLA_DQK, 128))
    score_bound = (1.02 * MLA_DQK * scale * jnp.max(jnp.abs(mla_qk_q_g[0].astype(F32)))
                   * jnp.max(jnp.abs(mla_qk_k_g[0].astype(F32))))
    bounded = score_bound < ATT_BOUND_MAX
    shift_rows = jnp.zeros((MLA_DPAD - MLA_DQK, 128), F32).at[0, :].set(jnp.where(bounded, -score_bound, 0.0))
    w_ukv = mla_w_ukv[0].reshape(MLA_KV_LORA, MLA_HEADS, MLA_NOPE + MLA_DV)
    w_ukv = jnp.concatenate([w_ukv[:, :, :MLA_NOPE].reshape(MLA_KV_LORA, -1),
                             w_ukv[:, :, MLA_NOPE:].reshape(MLA_KV_LORA, -1)], axis=1).astype(BF16)
    gk = mla_qk_k_g[0].astype(F32)
    proj1_3, k, vt = _mla_proj_kv(h1_3, ln_g[1][None, :], w_small, w_big, mla_kv_norm_g[0][None, :], w_ukv,
                                  gk[None, :MLA_NOPE], _lane_row(gk[MLA_NOPE:]), cos_t, nsin_t, sin_t, lp)
    qt = _mla_q(proj1_3, mla_q_norm_g[0][None, :], mla_w_uq[0].T.astype(BF16), gq[:MLA_NOPE],
                gq[MLA_NOPE:], jnp.cos(ang).T, jnp.sin(ang).T, shift_rows, lp)

    o = _attention(bounded.astype(jnp.int32).reshape(1), qt, k, vt, s_len)

    w_out1 = mla_w_out[0].astype(BF16)
    outs = []
    b0 = 0
    for x in xs:
        outs.append(_mla_out(o, proj1_3, h1_3, w_out1, b0, x.shape[0], s_len))
        b0 += x.shape[0]
    return tuple(outs)


def kernel(x_prompt, x_sample, meta_tokens, ln_g, gdn_w_in, gdn_conv_w, gdn_a_log, gdn_dt_bias,
           gdn_o_norm_g, gdn_w_out, mla_w_in, mla_q_norm_g, mla_kv_norm_g, mla_w_uq, mla_w_ukv,
           mla_qk_q_g, mla_qk_k_g, mla_w_out):
    return _trunk_all((x_prompt, x_sample), meta_tokens, ln_g, gdn_w_in, gdn_conv_w, gdn_a_log,
                      gdn_dt_bias, gdn_o_norm_g, gdn_w_out, mla_w_in, mla_q_norm_g, mla_kv_norm_g,
                      mla_w_uq, mla_w_ukv, mla_qk_q_g, mla_qk_k_g, mla_w_out)
```

```python
import functools
import math

import jax
import jax.numpy as jnp
from jax import lax
from jax.experimental import pallas as pl
from jax.experimental.pallas import tpu as pltpu

F32 = jnp.float32
BF16 = jnp.bfloat16

D_MODEL = 1024
N_META = 16
TAIL = 128
NORM_EPS = 1e-6

GDN_HEADS = 8
GDN_DK = 128
GDN_DV = 256
GDN_CONV = 5
GDN_CHUNK = 64
GDN_QK = GDN_HEADS * GDN_DK
GDN_V = GDN_HEADS * GDN_DV
GDN_CONV_CH = 2 * GDN_QK + GDN_V

MLA_HEADS = 16
MLA_Q_LORA = 512
MLA_KV_LORA = 256
MLA_NOPE = 128
MLA_ROPE = 64
MLA_DQK = MLA_NOPE + MLA_ROPE
MLA_DV = 128
MLA_V = MLA_HEADS * MLA_DV
MLA_DPAD = 256
ROPE_THETA = 10000.0

VMEM_LIMIT = 56 * 1024 * 1024


def _cparams(sem):
    return pltpu.CompilerParams(dimension_semantics=sem, vmem_limit_bytes=VMEM_LIMIT)


def _tile(n, target, mult):
    best = None
    t = mult
    while t <= min(n, target):
        if n % t == 0:
            best = t
        t += mult
    assert best is not None, (n, target, mult)
    return best


CONV_SUB = 64
CONV_LANES = GDN_DK
CONV_TC = 1024


def _qkv_conv_kernel(prev_ref, main_ref, next_ref, g_ref, w_ref, wba_ref, cw_ref, o_ref, ba_ref, ext_ref,
                     *, tr):
    half = GDN_CONV // 2
    nblk = GDN_CONV_CH // CONV_TC
    x = jnp.concatenate([prev_ref[0], main_ref[0], next_ref[0]], axis=0)
    ms = jnp.mean(x * x, axis=-1, keepdims=True)
    xn = (x * lax.rsqrt(ms + NORM_EPS) * g_ref[...]).astype(BF16)
    ba_ref[0] = jnp.dot(xn, wba_ref[...], preferred_element_type=F32)[8:8 + tr]

    def project(j):
        ext_ref[j] = jnp.dot(xn, w_ref[:, j * CONV_TC:(j + 1) * CONV_TC], preferred_element_type=F32)

    def conv_silu(j, sb, lb):
        lanes = slice(j * CONV_TC + lb * CONV_LANES, j * CONV_TC + (lb + 1) * CONV_LANES)
        nrow = CONV_SUB + 16
        x2 = ext_ref[j, sb * CONV_SUB:sb * CONV_SUB + nrow, lb * CONV_LANES:(lb + 1) * CONV_LANES]
        acc = x2[8:8 + CONV_SUB] * cw_ref[half:half + 1, lanes]
        for s in range(1, half + 1):
            down = pltpu.roll(x2, s, 0)
            acc = acc + down[8:8 + CONV_SUB] * cw_ref[half - s:half - s + 1, lanes]
            up = pltpu.roll(x2, nrow - s, 0)
            acc = acc + up[8:8 + CONV_SUB] * cw_ref[half + s:half + s + 1, lanes]
        return acc * jax.nn.sigmoid(acc)

    def finish(j):
        for sb in range(tr // CONV_SUB):
            rows = slice(sb * CONV_SUB, (sb + 1) * CONV_SUB)
            for lb in range(CONV_TC // CONV_LANES):
                y = conv_silu(j, sb, lb)
                lanes = slice(j * CONV_TC + lb * CONV_LANES, j * CONV_TC + (lb + 1) * CONV_LANES)
                if j < 2:
                    qscale = GDN_DK ** -0.5 if j == 0 else 1.0
                    ss = jnp.sum(y * y, axis=-1, keepdims=True)
                    y = y * (lax.rsqrt(ss + NORM_EPS) * qscale)
                o_ref[0, rows, lanes] = y.astype(BF16)

    project(0)
    for j in range(nblk):
        if j + 1 < nblk:
            project(j + 1)
        finish(j)


def _gdn_qkv_conv(h3, g, w_qkv, w_ba, conv_w8, lp):
    b = h3.shape[0]
    nba = w_ba.shape[1]
    tr = _tile(lp, 384, CONV_SUB)
    nb8 = lp // 8
    tb = tr // 8
    kern = functools.partial(_qkv_conv_kernel, tr=tr)
    const = lambda bi, i: (0, 0)
    return pl.pallas_call(
        kern,
        grid=(b, lp // tr),
        in_specs=[
            pl.BlockSpec((1, 8, D_MODEL), lambda bi, i: (bi, (i * tb + nb8 - 1) % nb8, 0)),
            pl.BlockSpec((1, tr, D_MODEL), lambda bi, i: (bi, i, 0)),
            pl.BlockSpec((1, 8, D_MODEL), lambda bi, i: (bi, ((i + 1) * tb) % nb8, 0)),
            pl.BlockSpec((1, D_MODEL), const),
            pl.BlockSpec((D_MODEL, GDN_CONV_CH), const),
            pl.BlockSpec((D_MODEL, nba), const),
            pl.BlockSpec((8, GDN_CONV_CH), const),
        ],
        out_specs=[
            pl.BlockSpec((1, tr, GDN_CONV_CH), lambda bi, i: (bi, i, 0)),
            pl.BlockSpec((1, tr, nba), lambda bi, i: (bi, i, 0)),
        ],
        out_shape=[
            jax.ShapeDtypeStruct((b, lp, GDN_CONV_CH), BF16),
            jax.ShapeDtypeStruct((b, lp, nba), F32),
        ],
        scratch_shapes=[pltpu.VMEM((GDN_CONV_CH // CONV_TC, tr + 16, CONV_TC), F32)],
        compiler_params=_cparams(("parallel", "parallel")),
        name="gdn_qkv_conv",
    )(h3, h3, h3, g, w_qkv, w_ba, conv_w8)


SCAN_NB = 2


def _split3(x):
    hi = x.astype(BF16)
    r1 = x - hi.astype(F32)
    mid = r1.astype(BF16)
    lo = (r1 - mid.astype(F32)).astype(BF16)
    return hi, mid, lo


def _gdn_scan_kernel(qf_ref, kf_ref, vf_ref, baf_ref, qb_ref, kb_ref, vb_ref, bab_ref, alog_ref, dtb_ref,
                     of_ref, ob_ref, s_ref, sb_ref, u_ref, qw_ref, attn_ref, kd_ref, eg_ref, *, nc):
    t = pl.program_id(1)
    c = GDN_CHUNK
    c2 = 2 * c
    npair = GDN_HEADS // 2
    o_refs = (of_ref, ob_ref)

    @pl.when(t == 0)
    def _():
        for ref in (s_ref, sb_ref, u_ref, qw_ref, attn_ref, kd_ref, eg_ref):
            ref[...] = jnp.zeros_like(ref)

    nb = of_ref.shape[0]
    nprob = nb * 2 * npair
    applied = [dict(sq=pi // npair, heads=(2 * (pi % npair), 2 * (pi % npair) + 1),
                    qs=[None, None], vnew=[None, None]) for pi in range(nprob)]

    def read_state(pi, hh):
        ap = applied[pi]
        rs = slice(hh * c, (hh + 1) * c)
        qws = jnp.dot(qw_ref[pi, hh], sb_ref[ap["sq"], ap["heads"][hh]], preferred_element_type=F32)
        ap["qs"][hh] = qws[:c]
        ap["vnew"][hh] = (u_ref[pi, rs, :] - qws[c:]).astype(BF16)

    def write_out(pi):
        ap = applied[pi]
        h0, h1 = ap["heads"]
        bi, d = divmod(ap["sq"], 2)
        vnew2 = jnp.concatenate(ap["vnew"], axis=0)
        o2 = jnp.concatenate(ap["qs"], axis=0) + jnp.dot(attn_ref[pi], vnew2, preferred_element_type=F32)
        o_refs[d][bi, :, h0 * GDN_DV:(h0 + 1) * GDN_DV] = o2[:c].astype(BF16)
        o_refs[d][bi, :, h1 * GDN_DV:(h1 + 1) * GDN_DV] = o2[c:].astype(BF16)

    def update_state(pi, hh):
        ap = applied[pi]
        sq, h = ap["sq"], ap["heads"][hh]
        rs = slice(hh * c, (hh + 1) * c)
        a_h = GDN_HEADS + h
        upd = lax.dot_general(kd_ref[pi, rs, :], ap["vnew"][hh], (((0,), (0,)), ((), ())),
                              preferred_element_type=F32)
        s_new = s_ref[sq, h] * eg_ref[sq, :, a_h:a_h + 1] + upd
        s_ref[sq, h] = s_new
        sb_ref[sq, h] = s_new.astype(BF16)

    pairs_hh = [(pi, hh) for pi in range(nprob) for hh in range(2)]
    apply_ops = ([functools.partial(read_state, pi, hh) for pi, hh in pairs_hh]
                 + [functools.partial(write_out, pi) for pi in range(nprob)]
                 + [functools.partial(update_state, pi, hh) for pi, hh in pairs_hh])

    def emit_apply(n):
        for _ in range(min(n, len(apply_ops))):
            apply_ops.pop(0)()

    emit_apply(len(pairs_hh) // 2)
    tp = jnp.minimum(t, nc - 1)
    ri = lax.broadcasted_iota(jnp.int32, (c2, c2), 0)
    ci = lax.broadcasted_iota(jnp.int32, (c2, c2), 1)
    same = (ri >> 6) == (ci >> 6)
    offdiag = ri != ci
    top =lax.broadcasted_iota(jnp.int32, (c2, 1), 0) < c
    left = lax.broadcasted_iota(jnp.int32, (1, c2), 1) < c
    row_id = lax.broadcasted_iota(jnp.int32, (c, 128), 0)

    in_refs = ((qf_ref, kf_ref, vf_ref, baf_ref), (qb_ref, kb_ref, vb_ref, bab_ref))
    seqs = []
    for sq in range(2 * nb):
        bi, d = divmod(sq, 2)
        q_ref, k_ref, v_ref, ba_ref = in_refs[d]
        seq = tp if d == 0 else nc - 1 - tp
        blk = jnp.where(seq < 2, nc - 2 + seq, seq - 2)
        first_valid = jnp.where(blk == nc - 2, c, jnp.where(blk == nc - 1, c - N_META, 0))
        valid = row_id >= first_valid
        ba = ba_ref[bi]
        beta = jnp.where(valid, jax.nn.sigmoid(ba), 0.0)
        xs = ba + dtb_ref[d]
        softplus = jnp.maximum(xs, 0.0) + jnp.log(1.0 + jnp.exp(-jnp.abs(xs)))
        g = jnp.where(valid, -jnp.exp(alog_ref[d]) * softplus, 0.0)
        incl = same & ((ri >= ci) if d == 0 else (ri <= ci))
        tri = jnp.where(incl, 1.0, 0.0).astype(BF16)
        ghi, gmid, glo = _split3(jnp.concatenate([g, g], axis=0))
        gc2 = (jnp.dot(tri, ghi, preferred_element_type=F32)
               + jnp.dot(tri, gmid, preferred_element_type=F32)
               + jnp.dot(tri, glo, preferred_element_type=F32))
        seqs.append(dict(q=q_ref.at[bi], k=k_ref.at[bi], v=v_ref.at[bi], incl=incl, gc2=gc2, gc2t=gc2.T,
                         beta2=jnp.concatenate([beta, beta], axis=0),
                         gtot=jnp.sum(g, axis=0, keepdims=True)))

    def stacked(ref, h0, h1, width):
        return jnp.concatenate([ref[:, h0 * width:(h0 + 1) * width],
                                ref[:, h1 * width:(h1 + 1) * width]], axis=0)

    probs = []
    for dd in seqs:
        for p in range(npair):
            h0, h1 = 2 * p, 2 * p + 1
            a0, a1 = GDN_HEADS + h0, GDN_HEADS + h1
            col = jnp.where(top, dd["gc2"][:, a0:a0 + 1], dd["gc2"][:, a1:a1 + 1])
            row = jnp.where(left, dd["gc2t"][a0:a0 + 1, :], dd["gc2t"][a1:a1 + 1, :])
            bcol = jnp.where(top, dd["beta2"][:, h0:h0 + 1], dd["beta2"][:, h1:h1 + 1])
            tot = jnp.where(top, dd["gtot"][:, a0:a0 + 1], dd["gtot"][:, a1:a1 + 1])
            dec = jnp.exp(jnp.where(dd["incl"], col - row, -jnp.inf))
            kst = stacked(dd["k"], h0, h1, GDN_DK)
            qst = stacked(dd["q"], h0, h1, GDN_DK)
            vst = stacked(dd["v"], h0, h1, GDN_DV)
            kf = kst.astype(F32)
            kb = kf * bcol
            sc = lax.dot_general(jnp.concatenate([qst, kb.astype(BF16)], axis=0), kst,
                                 (((1,), (1,)), ((), ())), preferred_element_type=F32)
            egc = jnp.exp(col)
            probs.append(dict(
                attn=(sc[:c2] * dec).astype(BF16),
                a=jnp.where(offdiag, sc[c2:] * dec, 0.0),
                rhs=jnp.concatenate([vst.astype(F32) * bcol, kb * egc], axis=1).astype(BF16),
                qg=(qst.astype(F32) * egc).astype(BF16),
                kd=(kf * jnp.exp(tot - col)).astype(BF16)))

    emit_apply(len(pairs_hh) // 2)

    for _ in _unit_lower_inverse_staged(probs):
        emit_apply(3 * nb)
    emit_apply(len(apply_ops))

    for pi, pr in enumerate(probs):
        uw = jnp.dot(pr["tinv"].astype(BF16), pr["rhs"], preferred_element_type=F32)
        u_ref[pi] = uw[:, :GDN_DV]
        w = uw[:, GDN_DV:].astype(BF16)
        for hh in range(2):
            rs = slice(hh * c, (hh + 1) * c)
            qw_ref[pi, hh] = jnp.concatenate([pr["qg"][rs], w[rs]], axis=0)
        attn_ref[pi] = pr["attn"]
        kd_ref[pi] = pr["kd"]
    for sq, dd in enumerate(seqs):
        eg_ref[sq] = jnp.exp(dd["gtot"])


def _unit_lower_inverse_staged(probs):
    c = GDN_CHUNK
    nside = 4
    ri = lax.broadcasted_iota(jnp.int32, (c, nside * c), 0)
    ci = lax.broadcasted_iota(jnp.int32, (c, nside * c), 1)
    lane_blk = ci >> 6
    within = ci & (c - 1)
    diag16 = (ri >> 4) == (within >> 4)
    eye = jnp.where(ri == within, 1.0, 0.0).astype(F32)
    left = lax.broadcasted_iota(jnp.int32, (1, 2 * c), 1) < c

    def blockdiag(y):
        return jnp.concatenate([jnp.where(lane_blk == r, y, 0.0) for r in range(nside)],
                               axis=0).astype(BF16)

    def mm(x, ybd):
        return jnp.dot(x.astype(BF16), ybd, preferred_element_type=F32)

    groups = []
    for g0 in range(0, len(probs), 2):
        pa, pb = probs[g0], probs[g0 + 1]
        a = jnp.concatenate([pa["a"][:c] + pa["a"][c:], pb["a"][:c] + pb["a"][c:]], axis=1)
        ad = jnp.where(diag16, a, 0.0)
        groups.append(dict(pairs=(pa, pb), ad=ad, an=blockdiag(a - ad), dinv=eye - ad))
    for gr in groups:
        gr["p"] = mm(gr["ad"], blockdiag(gr["ad"]))
    yield
    for gr in groups:
        pbd = blockdiag(gr["p"])
        gr["dinv"] = gr["dinv"] + mm(gr["dinv"], pbd)
        gr["p"] = mm(gr["p"], pbd)
    yield
    for gr in groups:
        pbd = blockdiag(gr["p"])
        gr["dinv"] = gr["dinv"] + mm(gr["dinv"], pbd)
        gr["p"] = mm(gr["p"], pbd)
    yield
    for gr in groups:
        gr["dinv"] = gr["dinv"] + mm(gr["dinv"], blockdiag(gr["p"]))
    yield
    for gr in groups:
        gr["m"] = mm(gr["dinv"], gr["an"])
    yield
    for gr in groups:
        gr["m2"] = blockdiag(mm(gr["m"], blockdiag(gr["m"])))
    yield
    for gr in groups:
        x = eye - gr["m"]
        gr["x"] = x + mm(x, gr["m2"])
    yield
    for gr in groups:
        t = mm(gr["x"], blockdiag(gr["dinv"]))
        for i, pr in enumerate(gr["pairs"]):
            half = t[:, 2 * c * i:2 * c * (i + 1)]
            pr["tinv"] = jnp.concatenate([jnp.where(left, half, 0.0), jnp.where(left, 0.0, half)],
                                         axis=0)


def _gdn_scan(qkv, proj3, alog_rows, dtb_rows, lp):
    b = qkv.shape[0]
    nc = lp // GDN_CHUNK
    ba_col0 = 0

    def blk_of(seq):
        return jnp.where(seq < 2, nc - 2 + seq, seq - 2)

    def chunk_specs(d):
        prep = lambda t: jnp.minimum(t, nc - 1)
        seq = prep if d == 0 else (lambda t: nc - 1 - prep(t))
        return [
            pl.BlockSpec((nb, GDN_CHUNK, GDN_QK), lambda bi, t: (bi, blk_of(seq(t)), 0)),
            pl.BlockSpec((nb, GDN_CHUNK, GDN_QK), lambda bi, t: (bi, blk_of(seq(t)), 1)),
            pl.BlockSpec((nb, GDN_CHUNK, GDN_V), lambda bi, t: (bi, blk_of(seq(t)), 1)),
            pl.BlockSpec((nb, GDN_CHUNK, 128), lambda bi, t: (bi, blk_of(seq(t)), ba_col0 + d)),
        ]

    nb = SCAN_NB if b % SCAN_NB == 0 else 1
    const = pl.BlockSpec((2, 1, 128), lambda bi, t: (0, 0, 0))
    kern = functools.partial(_gdn_scan_kernel, nc=nc)
    applied = lambda t: jnp.maximum(t - 1, 0)
    nseq = 2 * nb
    nprob = nseq * (GDN_HEADS // 2)
    c2 = 2 * GDN_CHUNK
    return pl.pallas_call(
        kern,
        grid=(b // nb, nc + 1),
        in_specs=chunk_specs(0) + chunk_specs(1) + [const, const],
        out_specs=[
            pl.BlockSpec((nb, GDN_CHUNK, GDN_V), lambda bi, t: (bi, blk_of(applied(t)), 0)),
            pl.BlockSpec((nb, GDN_CHUNK, GDN_V), lambda bi, t: (bi, blk_of(nc - 1 - applied(t)), 0)),
        ],
        out_shape=[jax.ShapeDtypeStruct((b, lp, GDN_V), BF16)] * 2,
        scratch_shapes=[
            pltpu.VMEM((nseq, GDN_HEADS, GDN_DK, GDN_DV), F32),
            pltpu.VMEM((nseq, GDN_HEADS, GDN_DK, GDN_DV), BF16),
            pltpu.VMEM((nprob, c2, GDN_DV), F32),
            pltpu.VMEM((nprob, 2, c2, GDN_DK), BF16),
            pltpu.VMEM((nprob, c2, c2), BF16),
            pltpu.VMEM((nprob, c2, GDN_DK), BF16),
            pltpu.VMEM((nseq, 1, 128), F32),
        ],
        compiler_params=_cparams(("parallel", "arbitrary")),
        name="gdn_scan",
    )(qkv, qkv, qkv, proj3, qkv, qkv, qkv, proj3, alog_rows, dtb_rows)


OUT_SUB = 256


def _gdn_out_kernel(of_ref, ob_ref, h_ref, ln_ref, wz_ref, g_ref, w_ref, o_ref):
    nsub = o_ref.shape[0] // OUT_SUB
    rows = [slice(sb * OUT_SUB, (sb + 1) * OUT_SUB) for sb in range(nsub)]

    def gate_logits(sb):
        x = h_ref[rows[sb], :]
        ms = jnp.mean(x * x, axis=-1, keepdims=True)
        xn = (x * lax.rsqrt(ms + NORM_EPS) * ln_ref[...]).astype(BF16)
        return jnp.dot(xn, wz_ref[...], preferred_element_type=F32)

    z_next = gate_logits(0)
    for sb in range(nsub):
        z = z_next
        if sb + 1 < nsub:
            z_next = gate_logits(sb + 1)
        o = of_ref[rows[sb], :].astype(F32) + ob_ref[rows[sb], :].astype(F32)
        gate = z * jax.nn.sigmoid(z)
        ys = []
        for h in range(GDN_HEADS):
            lanes = slice(h * GDN_DV, (h + 1) * GDN_DV)
            oh = o[:, lanes]
            ms = jnp.mean(oh * oh, axis=-1, keepdims=True)
            ys.append((oh * lax.rsqrt(ms + NORM_EPS) * g_ref[...] * gate[:, lanes]).astype(BF16))
        y = jnp.concatenate(ys, axis=1)
        o_ref[rows[sb], :] = h_ref[rows[sb], :] + jnp.dot(y, w_ref[...], preferred_element_type=F32)


def _gdn_out(o_fwd, o_bwd, h0, ln_row, w_z, g_row, w_out):
    rows = h0.shape[0]
    tm = _tile(rows, 768, OUT_SUB)
    const = lambda i: (0, 0)
    return pl.pallas_call(
        _gdn_out_kernel,
        grid=(rows // tm,),
        in_specs=[
            pl.BlockSpec((tm, GDN_V), lambda i: (i, 0)),
            pl.BlockSpec((tm, GDN_V), lambda i: (i, 0)),
            pl.BlockSpec((tm, D_MODEL), lambda i: (i, 0)),
            pl.BlockSpec((1, D_MODEL), const),
            pl.BlockSpec((D_MODEL, GDN_V), const),
            pl.BlockSpec((1, GDN_DV), const),
            pl.BlockSpec((GDN_V, D_MODEL), const),
        ],
        out_specs=pl.BlockSpec((tm, D_MODEL), lambda i: (i, 0)),
        out_shape=jax.ShapeDtypeStruct((rows, D_MODEL), F32),
        compiler_params=_cparams(("parallel",)),
        name="gdn_out",
    )(o_fwd, o_bwd, h0, ln_row, w_z, g_row, w_out)


def _rope(r, cos, nsin_lo, sin_hi):
    return r * cos + pltpu.roll(r, 96, 1) * nsin_lo + pltpu.roll(r, 32, 1) * sin_hi


Q_GROUP = 2


def _mla_q_kernel(cq_ref, g1_ref, wt_ref, ga_ref, gr_ref, cos_ref, sin_ref, shift_ref, qt_ref):
    cq = cq_ref[0].astype(F32)
    tm = cq.shape[0]
    ms = jnp.mean(cq * cq, axis=-1, keepdims=True)
    cqt = (cq * lax.rsqrt(ms + NORM_EPS) * g1_ref[...]).T.astype(BF16)
    half = MLA_ROPE // 2
    tile = lambda g: jnp.concatenate([g] * (tm // 128), axis=1)
    ga = tile(ga_ref[...])
    gr1, gr2 = tile(gr_ref[0:half, :]), tile(gr_ref[half:MLA_ROPE, :])
    cos, sin = cos_ref[...], sin_ref[...]
    pad_rows = tile(shift_ref[...]).astype(BF16)

    def project(g):
        rows = slice(g * Q_GROUP * MLA_DQK, (g + 1) * Q_GROUP * MLA_DQK)
        return jnp.dot(wt_ref[rows, :], cqt, preferred_element_type=F32)

    q_next = project(0)
    for h in range(MLA_HEADS):
        if h % Q_GROUP == 0:
            q = q_next
            if h + Q_GROUP < MLA_HEADS:
                q_next = project(h // Q_GROUP + 1)
        r0 = (h % Q_GROUP) * MLA_DQK
        a = q[r0:r0 + MLA_NOPE]
        x1 = q[r0 + MLA_NOPE:r0 + MLA_NOPE + half]
        x2 = q[r0 + MLA_NOPE + half:r0 + MLA_DQK]
        ss = (jnp.sum(a * a, axis=0, keepdims=True) + jnp.sum(x1 * x1, axis=0, keepdims=True)
              + jnp.sum(x2 * x2, axis=0, keepdims=True))
        inv = lax.rsqrt(ss * (1.0 / MLA_DQK) + NORM_EPS)
        x1 = x1 * inv * gr1
        x2 = x2 * inv * gr2
        qt_ref[0, h, 0:MLA_NOPE, :] = (a * inv * ga).astype(BF16)
        qt_ref[0, h, MLA_NOPE:MLA_NOPE + half, :] = (x1 * cos - x2 * sin).astype(BF16)
        qt_ref[0, h, MLA_NOPE + half:MLA_DQK, :] = (x2 * cos + x1 * sin).astype(BF16)
        qt_ref[0, h, MLA_DQK:MLA_DPAD, :] = pad_rows


def _mla_q(proj3, g1, w_uq_t, ga, gr, cos_t, sin_t, shift_rows, lp):
    b = proj3.shape[0]
    tm = _tile(lp, 384, 128)
    cqcol = MLA_V // MLA_Q_LORA
    const = lambda bi, i: (0, 0)
    return pl.pallas_call(
        _mla_q_kernel,
        grid=(b, lp // tm),
        in_specs=[
            pl.BlockSpec((1, tm, MLA_Q_LORA), lambda bi, i: (bi, i, cqcol)),
            pl.BlockSpec((1, MLA_Q_LORA), const),
            pl.BlockSpec((MLA_HEADS * MLA_DQK, MLA_Q_LORA), const),
            pl.BlockSpec((MLA_NOPE, 128), const),
            pl.BlockSpec((MLA_ROPE, 128), const),
            pl.BlockSpec((MLA_ROPE // 2, tm), lambda bi, i: (0, i)),
            pl.BlockSpec((MLA_ROPE // 2, tm), lambda bi, i: (0, i)),
            pl.BlockSpec((MLA_DPAD - MLA_DQK, 128), const),
        ],
        out_specs=pl.BlockSpec((1, MLA_HEADS, MLA_DPAD, tm), lambda bi, i: (bi, 0, 0, i)),
        out_shape=jax.ShapeDtypeStruct((b, MLA_HEADS, MLA_DPAD, lp), BF16),
        compiler_params=_cparams(("parallel", "parallel")),
        name="mla_q",
    )(proj3, g1, w_uq_t, ga, gr, cos_t, sin_t, shift_rows)


def _mla_kv_kernel(h_ref, ln_ref, ws_ref, wb_ref, g1_ref, w_ref, ga_ref, gr_ref, cos_ref, nsin_ref, sin_ref,
                   zq_ref, k_ref, vt_ref):
    x = h_ref[0]
    xn = (x * lax.rsqrt(jnp.mean(x * x, axis=-1, keepdims=True) + NORM_EPS) * ln_ref[...]).astype(BF16)
    small = jnp.dot(xn, ws_ref[...], preferred_element_type=F32)
    ckv = small[:, :MLA_KV_LORA]
    ms = jnp.mean(ckv * ckv, axis=-1, keepdims=True)
    cn = (ckv * lax.rsqrt(ms + NORM_EPS) * g1_ref[...]).astype(BF16)
    kv = jnp.dot(cn, w_ref[...], preferred_element_type=F32)
    zq_ref[0] = jnp.dot(xn, wb_ref[...], preferred_element_type=F32).astype(BF16)
    kpe = small[:, MLA_KV_LORA:]
    sq_pe = kpe * kpe
    kr = _rope(kpe * gr_ref[...], cos_ref[...], nsin_ref[...], sin_ref[...])
    one_hot = jnp.where(lax.broadcasted_iota(jnp.int32, (1, 128), 1) == MLA_ROPE, 1.0, 0.0).astype(F32)
    for h in range(MLA_HEADS):
        kn = kv[:, h * MLA_NOPE:(h + 1) * MLA_NOPE]
        ss = jnp.sum(kn * kn + sq_pe, axis=-1, keepdims=True)
        inv = lax.rsqrt(ss * (1.0 / MLA_DQK) + NORM_EPS)
        k_ref[0, h, :, 0:MLA_NOPE] = (kn * inv * ga_ref[...]).astype(BF16)
        k_ref[0, h, :, MLA_NOPE:MLA_DPAD] = (kr * inv + one_hot).astype(BF16)
        v = kv[:, MLA_HEADS * MLA_NOPE + h * MLA_DV:MLA_HEADS * MLA_NOPE + (h + 1) * MLA_DV]
        vt_ref[0, h] = v.T.astype(BF16)


def _mla_proj_kv(h3, ln_row, w_small, w_big, g1, w_ukv, ga, gr, cos, nsin, sin, lp):
    b = h3.shape[0]
    tm = _tile(lp, 384, 128)
    const = lambda bi, i: (0, 0)
    nbig = w_big.shape[1]
    return pl.pallas_call(
        _mla_kv_kernel,
        grid=(b, lp // tm),
        in_specs=[
            pl.BlockSpec((1, tm, D_MODEL), lambda bi, i: (bi, i, 0)),
            pl.BlockSpec((1, D_MODEL), const),
            pl.BlockSpec((D_MODEL, MLA_KV_LORA + 128), const),
            pl.BlockSpec((D_MODEL, nbig), const),
            pl.BlockSpec((1, MLA_KV_LORA), const),
            pl.BlockSpec((MLA_KV_LORA, MLA_HEADS * (MLA_NOPE + MLA_DV)), const),
            pl.BlockSpec((1, 128), const),
            pl.BlockSpec((1, 128), const),
            pl.BlockSpec((tm, 128), lambda bi, i: (i, 0)),
            pl.BlockSpec((tm, 128), lambda bi, i: (i, 0)),
            pl.BlockSpec((tm, 128), lambda bi, i: (i, 0)),
        ],
        out_specs=[
            pl.BlockSpec((1, tm, nbig), lambda bi, i: (bi, i, 0)),
            pl.BlockSpec((1, MLA_HEADS, tm, MLA_DPAD), lambda bi, i: (bi, 0, i, 0)),
            pl.BlockSpec((1, MLA_HEADS, MLA_DV, tm), lambda bi, i: (bi, 0, 0, i)),
        ],
        out_shape=[
            jax.ShapeDtypeStruct((b, lp, nbig), BF16),
            jax.ShapeDtypeStruct((b, MLA_HEADS, lp, MLA_DPAD), BF16),
            jax.ShapeDtypeStruct((b, MLA_HEADS, MLA_DV, lp), BF16),
        ],
        compiler_params=_cparams(("parallel", "parallel")),
        name="mla_proj_kv",
    )(h3, ln_row, w_small, w_big, g1, w_ukv, ga, gr, cos, nsin, sin)


ATT_TK = 1024
ATT_TQ = 512
ATT_AHEAD = 2
ATT_BOUND_MAX = 60.0


def _attn_kernel(bounded_ref, qt_ref, k_ref, vt_ref, o_ref, *, s_len):
    tq = qt_ref.shape[3]
    nstrip = tq // ATT_TQ
    chunks = [(ck * ATT_TK, ATT_TK) for ck in range(s_len // ATT_TK)] + [(s_len, TAIL)]
    units = [(i, s) for i in range(len(chunks)) for s in range(nstrip)]
    is_meta = lax.broadcasted_iota(jnp.int32, (TAIL, 1), 0) >= TAIL - N_META
    tail_bias = jnp.where(is_meta, 0.0, -jnp.inf).astype(F32)

    def scores(u):
        i, s = units[u]
        k0, nk = chunks[i]
        st = jnp.dot(k_ref[0, 0, k0:k0 + nk, :], qt_ref[0, 0, :, s * ATT_TQ:(s + 1) * ATT_TQ],
                     preferred_element_type=F32)
        return st + tail_bias if i == len(chunks) - 1 else st

    def run(bounded):
        m = [jnp.full((1, ATT_TQ), -jnp.inf, F32) for _ in range(nstrip)]
        l = [jnp.zeros((1, ATT_TQ), F32) for _ in range(nstrip)]
        acc = [jnp.zeros((MLA_DV, ATT_TQ), F32) for _ in range(nstrip)]
        pending = [scores(u) for u in range(min(ATT_AHEAD, len(units)))]
        for u, (i, s) in enumerate(units):
            if u + ATT_AHEAD < len(units):
                pending.append(scores(u + ATT_AHEAD))
            st = pending.pop(0)
            k0, nk = chunks[i]
            if bounded:
                p = jnp.exp2(st)
                l[s] = l[s] + jnp.sum(p, axis=0, keepdims=True)
                acc[s] = acc[s] + jnp.dot(vt_ref[0, 0, :, k0:k0 + nk], p.astype(BF16),
                                          preferred_element_type=F32)
            else:
                m_new = jnp.maximum(m[s], jnp.max(st, axis=0, keepdims=True))
                alpha = jnp.exp2(m[s] - m_new)
                p = jnp.exp2(st - m_new)
                l[s] = alpha * l[s] + jnp.sum(p, axis=0, keepdims=True)
                acc[s] = alpha * acc[s] + jnp.dot(vt_ref[0, 0, :, k0:k0 + nk], p.astype(BF16),
                                                  preferred_element_type=F32)
                m[s] = m_new
        for s in range(nstrip):
            o_ref[0, s * ATT_TQ:(s + 1) * ATT_TQ, :] = (acc[s] / l[s]).T.astype(o_ref.dtype)

    @pl.when(bounded_ref[0] == 1)
    def _():
        run(True)

    @pl.when(bounded_ref[0] != 1)
    def _():
        run(False)


def _attention(bounded, qt, k, vt, s_len):
    b = qt.shape[0]
    lp = k.shape[2]
    assert s_len % ATT_TK == 0 and lp == s_len + TAIL
    tq = _tile(s_len, 1024, ATT_TQ)
    kern = functools.partial(_attn_kernel, s_len=s_len)
    return pl.pallas_call(
        kern,
        grid=(b, MLA_HEADS, s_len // tq),
        in_specs=[
            pl.BlockSpec(memory_space=pltpu.SMEM),
            pl.BlockSpec((1, 1, MLA_DPAD, tq), lambda bi, h, i: (bi, h, 0, i)),
            pl.BlockSpec((1, 1, lp, MLA_DPAD), lambda bi, h, i: (bi, h, 0, 0)),
            pl.BlockSpec((1, 1, MLA_DV, lp), lambda bi, h, i: (bi, h, 0, 0)),
        ],
        out_specs=pl.BlockSpec((1, tq, MLA_DV), lambda bi, h, i: (bi, i, h)),
        out_shape=jax.ShapeDtypeStruct((b, s_len, MLA_V), BF16),
        compiler_params=_cparams(("parallel", "parallel", "arbitrary")),
        name="mla_attention",
    )(bounded, qt, k, vt)


def _mla_out_kernel(o_ref, z_ref, h_ref, w_ref, y_ref):
    z = z_ref[0].astype(F32)
    y = (o_ref[0].astype(F32) * (z * jax.nn.sigmoid(z))).astype(BF16)
    y_ref[0] = h_ref[0] + jnp.dot(y, w_ref[...], preferred_element_type=F32)


def _mla_out(o, proj3, h3, w_out, b0, nb, s_len):
    tm = _tile(s_len, 1024, 128)
    return pl.pallas_call(
        _mla_out_kernel,
        grid=(nb, s_len // tm),
        in_specs=[
            pl.BlockSpec((1, tm, MLA_V), lambda bi, i: (bi + b0, i, 0)),
            pl.BlockSpec((1, tm, MLA_V), lambda bi, i: (bi + b0, i, 0)),
            pl.BlockSpec((1, tm, D_MODEL), lambda bi, i: (bi + b0, i, 0)),
            pl.BlockSpec((MLA_V, D_MODEL), lambda bi, i: (0, 0)),
        ],
        out_specs=pl.BlockSpec((1, tm, D_MODEL), lambda bi, i: (bi, i, 0)),
        out_shape=jax.ShapeDtypeStruct((nb, s_len, D_MODEL), F32),
        compiler_params=_cparams(("parallel", "parallel")),
        name="mla_out",
    )(o, proj3, h3, w_out)


def _pad_cols(w, n):
    return jnp.pad(w, ((0, 0), (0, n - w.shape[1])))


def _lane_row(v, n=128):
    return jnp.pad(v.astype(F32), (0, n - v.shape[0]))[None, :]


def _trunk_all(xs, meta_tokens, ln_g, gdn_w_in, gdn_conv_w, gdn_a_log, gdn_dt_bias, gdn_o_norm_g,
               gdn_w_out, mla_w_in, mla_q_norm_g, mla_kv_norm_g, mla_w_uq, mla_w_ukv, mla_qk_q_g,
               mla_qk_k_g, mla_w_out):
    s_len = xs[0].shape[1]
    assert all(x.shape[1] == s_len for x in xs) and s_len % 128 == 0
    lp = s_len + TAIL
    x_all = jnp.concatenate(xs, axis=0)
    b = x_all.shape[0]
    meta = jnp.broadcast_to(meta_tokens[None].astype(F32), (b, N_META, D_MODEL))
    h0 = jnp.concatenate([x_all, jnp.zeros((b, TAIL - N_META, D_MODEL), F32), meta], axis=1)
    h0 = h0.reshape(b * lp, D_MODEL)

    w_in = gdn_w_in[0]
    ba = w_in[:, GDN_CONV_CH + GDN_V:].reshape(D_MODEL, 2, 2, GDN_HEADS)
    ba_dir = [_pad_cols(jnp.concatenate([ba[:, 0, d], ba[:, 1, d]], axis=1), 128) for d in range(2)]
    conv_w8 = jnp.pad(gdn_conv_w[0], ((0, 8 - GDN_CONV), (0, 0)))
    qkv, gates = _gdn_qkv_conv(h0.reshape(b, lp, D_MODEL), ln_g[0][None, :],
                               w_in[:, :GDN_CONV_CH].astype(BF16),
                               jnp.concatenate(ba_dir, axis=1).astype(BF16), conv_w8, lp)

    lane_a = lambda v: jnp.pad(v.astype(F32), ((0, 0), (GDN_HEADS, 128 - 2 * GDN_HEADS)))[:, None, :]
    o_fwd, o_bwd = _gdn_scan(qkv, gates, lane_a(gdn_a_log[0]), lane_a(gdn_dt_bias[0]), lp)
    h1 = _gdn_out(o_fwd.reshape(b * lp, GDN_V), o_bwd.reshape(b * lp, GDN_V), h0, ln_g[0][None, :],
                  w_in[:, GDN_CONV_CH:GDN_CONV_CH + GDN_V].astype(BF16),
                  gdn_o_norm_g[0][None, :], gdn_w_out[0].astype(BF16))

    w_in1 = mla_w_in[0]
    o1 = MLA_Q_LORA
    o2_ = o1 + MLA_KV_LORA
    o3 = o2_ + MLA_ROPE
    w_small = jnp.concatenate([w_in1[:, o1:o2_], _pad_cols(w_in1[:, o2_:o3], 128)], axis=1).astype(BF16)
    w_big = jnp.concatenate([w_in1[:, o3:], w_in1[:, :o1]], axis=1).astype(BF16)
    h1_3 = h1.reshape(b, lp, D_MODEL)

    pos = jnp.concatenate([jnp.arange(s_len, dtype=F32) + N_META, jnp.zeros((TAIL - N_META,), F32),
                           jnp.arange(N_META, dtype=F32)])
    inv = ROPE_THETA ** (-jnp.arange(0, MLA_ROPE, 2, dtype=F32) / MLA_ROPE)
    ang = pos[:, None] * inv[None, :]
    zc = jnp.zeros_like(ang)
    cos_t = jnp.concatenate([jnp.cos(ang), jnp.cos(ang), zc, zc], axis=1)
    nsin_t = jnp.concatenate([-jnp.sin(ang), zc, zc, zc], axis=1)
    sin_t = jnp.concatenate([zc, jnp.sin(ang), zc, zc], axis=1)

    scale = MLA_DQK ** -0.5 * math.log2(math.e)
    gq = jnp.broadcast_to((mla_qk_q_g[0].astype(F32) * scale)[:, None], (MLA_DQK, 128))
    score_bound = (1.02 * MLA_DQK * scale * jnp.max(jnp.abs(mla_qk_q_g[0].astype(F32)))
                   * jnp.max(jnp.abs(mla_qk_k_g[0].astype(F32))))
    bounded = score_bound < ATT_BOUND_MAX
    shift_rows = jnp.zeros((MLA_DPAD - MLA_DQK, 128), F32).at[0, :].set(jnp.where(bounded, -score_bound, 0.0))
    w_ukv = mla_w_ukv[0].reshape(MLA_KV_LORA, MLA_HEADS, MLA_NOPE + MLA_DV)
    w_ukv = jnp.concatenate([w_ukv[:, :, :MLA_NOPE].reshape(MLA_KV_LORA, -1),
                             w_ukv[:, :, MLA_NOPE:].reshape(MLA_KV_LORA, -1)], axis=1).astype(BF16)
    gk = mla_qk_k_g[0].astype(F32)
    proj1_3, k, vt = _mla_proj_kv(h1_3, ln_g[1][None, :], w_small, w_big, mla_kv_norm_g[0][None, :], w_ukv,
                                  gk[None, :MLA_NOPE], _lane_row(gk[MLA_NOPE:]), cos_t, nsin_t, sin_t, lp)
    qt = _mla_q(proj1_3, mla_q_norm_g[0][None, :], mla_w_uq[0].T.astype(BF16), gq[:MLA_NOPE],
                gq[MLA_NOPE:], jnp.cos(ang).T, jnp.sin(ang).T, shift_rows, lp)

    o = _attention(bounded.astype(jnp.int32).reshape(1), qt, k, vt, s_len)

    w_out1 = mla_w_out[0].astype(BF16)
    outs = []
    b0 = 0
    for x in xs:
        outs.append(_mla_out(o, proj1_3, h1_3, w_out1, b0, x.shape[0], s_len))
        b0 += x.shape[0]
    return tuple(outs)


def kernel(x_prompt, x_sample, meta_tokens, ln_g, gdn_w_in, gdn_conv_w, gdn_a_log, gdn_dt_bias,
           gdn_o_norm_g, gdn_w_out, mla_w_in, mla_q_norm_g, mla_kv_norm_g, mla_w_uq, mla_w_ukv,
           mla_qk_q_g, mla_qk_k_g, mla_w_out):
    return _trunk_all((x_prompt, x_sample), meta_tokens, ln_g, gdn_w_in, gdn_conv_w, gdn_a_log,
                      gdn_dt_bias, gdn_o_norm_g, gdn_w_out, mla_w_in, mla_q_norm_g, mla_kv_norm_g,
                      mla_w_uq, mla_w_ukv, mla_qk_q_g, mla_qk_k_g, mla_w_out)
```

```python
import functools
import math

import jax
import jax.numpy as jnp
from jax import lax
from jax.experimental import pallas as pl
from jax.experimental.pallas import tpu as pltpu

F32 = jnp.float32
BF16 = jnp.bfloat16

D_MODEL = 1024
N_META = 16
TAIL = 128
NORM_EPS = 1e-6

GDN_HEADS = 8
GDN_DK = 128
GDN_DV = 256
GDN_CONV = 5
GDN_CHUNK = 64
GDN_QK = GDN_HEADS * GDN_DK
GDN_V = GDN_HEADS * GDN_DV
GDN_CONV_CH = 2 * GDN_QK + GDN_V

MLA_HEADS = 16
MLA_Q_LORA = 512
MLA_KV_LORA = 256
MLA_NOPE = 128
MLA_ROPE = 64
MLA_DQK = MLA_NOPE + MLA_ROPE
MLA_DV = 128
MLA_V = MLA_HEADS * MLA_DV
MLA_DPAD = 256
ROPE_THETA = 10000.0

VMEM_LIMIT = 56 * 1024 * 1024


def _cparams(sem):
    return pltpu.CompilerParams(dimension_semantics=sem, vmem_limit_bytes=VMEM_LIMIT)


def _tile(n, target, mult):
    best = None
    t = mult
    while t <= min(n, target):
        if n % t == 0:
            best = t
        t += mult
    assert best is not None, (n, target, mult)
    return best


CONV_SUB = 64
CONV_LANES = GDN_DK
CONV_TC = 1024


def _qkv_conv_kernel(prev_ref, main_ref, next_ref, g_ref, w_ref, wba_ref, cw_ref, o_ref, ba_ref, ext_ref,
                     *, tr):
    half = GDN_CONV // 2
    nblk = GDN_CONV_CH // CONV_TC
    x = jnp.concatenate([prev_ref[0], main_ref[0], next_ref[0]], axis=0)
    ms = jnp.mean(x * x, axis=-1, keepdims=True)
    xn = (x * lax.rsqrt(ms + NORM_EPS) * g_ref[...]).astype(BF16)
    ba_ref[0] = jnp.dot(xn, wba_ref[...], preferred_element_type=F32)[8:8 + tr]

    def project(j):
        ext_ref[j] = jnp.dot(xn, w_ref[:, j * CONV_TC:(j + 1) * CONV_TC], preferred_element_type=F32)

    def conv_silu(j, sb, lb):
        lanes = slice(j * CONV_TC + lb * CONV_LANES, j * CONV_TC + (lb + 1) * CONV_LANES)
        nrow = CONV_SUB + 16
        x2 = ext_ref[j, sb * CONV_SUB:sb * CONV_SUB + nrow, lb * CONV_LANES:(lb + 1) * CONV_LANES]
        acc = x2[8:8 + CONV_SUB] * cw_ref[half:half + 1, lanes]
        for s in range(1, half + 1):
            down = pltpu.roll(x2, s, 0)
            acc = acc + down[8:8 + CONV_SUB] * cw_ref[half - s:half - s + 1, lanes]
            up = pltpu.roll(x2, nrow - s, 0)
            acc = acc + up[8:8 + CONV_SUB] * cw_ref[half + s:half + s + 1, lanes]
        return acc * jax.nn.sigmoid(acc)

    def finish(j):
        for sb in range(tr // CONV_SUB):
            rows = slice(sb * CONV_SUB, (sb + 1) * CONV_SUB)
            for lb in range(CONV_TC // CONV_LANES):
                y = conv_silu(j, sb, lb)
                c0 = j * CONV_TC + lb * CONV_LANES
                lanes = slice(c0, c0 + CONV_LANES)
                if c0 < 2 * GDN_QK:
                    qscale = GDN_DK ** -0.5 if c0 < GDN_QK else 1.0
                    ss = jnp.sum(y * y, axis=-1, keepdims=True)
                    y = y * (lax.rsqrt(ss + NORM_EPS) * qscale)
                o_ref[0, rows, lanes] = y.astype(BF16)

    project(0)
    for j in range(nblk):
        if j + 1 < nblk:
            project(j + 1)
        finish(j)


def _gdn_qkv_conv(h3, g, w_qkv, w_ba, conv_w8, lp):
    b = h3.shape[0]
    nba = w_ba.shape[1]
    tr = _tile(lp, 384, CONV_SUB)
    nb8 = lp // 8
    tb = tr // 8
    kern = functools.partial(_qkv_conv_kernel, tr=tr)
    const = lambda bi, i: (0, 0)
    return pl.pallas_call(
        kern,
        grid=(b, lp // tr),
        in_specs=[
            pl.BlockSpec((1, 8, D_MODEL), lambda bi, i: (bi, (i * tb + nb8 - 1) % nb8, 0)),
            pl.BlockSpec((1, tr, D_MODEL), lambda bi, i: (bi, i, 0)),
            pl.BlockSpec((1, 8, D_MODEL), lambda bi, i: (bi, ((i + 1) * tb) % nb8, 0)),
            pl.BlockSpec((1, D_MODEL), const),
            pl.BlockSpec((D_MODEL, GDN_CONV_CH), const),
            pl.BlockSpec((D_MODEL, nba), const),
            pl.BlockSpec((8, GDN_CONV_CH), const),
        ],
        out_specs=[
            pl.BlockSpec((1, tr, GDN_CONV_CH), lambda bi, i: (bi, i, 0)),
            pl.BlockSpec((1, tr, nba), lambda bi, i: (bi, i, 0)),
        ],
        out_shape=[
            jax.ShapeDtypeStruct((b, lp, GDN_CONV_CH), BF16),
            jax.ShapeDtypeStruct((b, lp, nba), F32),
        ],
        scratch_shapes=[pltpu.VMEM((GDN_CONV_CH // CONV_TC, tr + 16, CONV_TC), F32)],
        compiler_params=_cparams(("parallel", "parallel")),
        name="gdn_qkv_conv",
    )(h3, h3, h3, g, w_qkv, w_ba, conv_w8)


SCAN_NB = 2


def _split3(x):
    hi = x.astype(BF16)
    r1 = x - hi.astype(F32)
    mid = r1.astype(BF16)
    lo = (r1 - mid.astype(F32)).astype(BF16)
    return hi, mid, lo


def _gdn_scan_kernel(qf_ref, kf_ref, vf_ref, baf_ref, qb_ref, kb_ref, vb_ref, bab_ref, alog_ref, dtb_ref,
                     of_ref, ob_ref, s_ref, sb_ref, u_ref, qw_ref, attn_ref, kd_ref, eg_ref, *, nc):
    t = pl.program_id(1)
    c = GDN_CHUNK
    c2 = 2 * c
    npair = GDN_HEADS // 2
    o_refs = (of_ref, ob_ref)

    @pl.when(t == 0)
    def _():
        for ref in (s_ref, sb_ref, u_ref, qw_ref, attn_ref, kd_ref, eg_ref):
            ref[...] = jnp.zeros_like(ref)

    nb = of_ref.shape[0]
    nprob = nb * 2 * npair
    applied = [dict(sq=pi // npair, heads=(2 * (pi % npair), 2 * (pi % npair) + 1),
                    qs=[None, None], vnew=[None, None]) for pi in range(nprob)]

    def read_state(pi, hh):
        ap = applied[pi]
        rs = slice(hh * c, (hh + 1) * c)
        qws = jnp.dot(qw_ref[pi, hh], sb_ref[ap["sq"], ap["heads"][hh]], preferred_element_type=F32)
        ap["qs"][hh] = qws[:c]
        ap["vnew"][hh] = (u_ref[pi, rs, :] - qws[c:]).astype(BF16)

    def write_out(pi):
        ap = applied[pi]
        h0, h1 = ap["heads"]
        bi, d = divmod(ap["sq"], 2)
        vnew2 = jnp.concatenate(ap["vnew"], axis=0)
        o2 = jnp.concatenate(ap["qs"], axis=0) + jnp.dot(attn_ref[pi], vnew2, preferred_element_type=F32)
        o_refs[d][bi, :, h0 * GDN_DV:(h0 + 1) * GDN_DV] = o2[:c].astype(BF16)
        o_refs[d][bi, :, h1 * GDN_DV:(h1 + 1) * GDN_DV] = o2[c:].astype(BF16)

    def update_state(pi, hh):
        ap = applied[pi]
        sq, h = ap["sq"], ap["heads"][hh]
        rs = slice(hh * c, (hh + 1) * c)
        a_h = GDN_HEADS + h
        upd = lax.dot_general(kd_ref[pi, rs, :], ap["vnew"][hh], (((0,), (0,)), ((), ())),
                              preferred_element_type=F32)
        s_new = s_ref[sq, h] * eg_ref[sq, :, a_h:a_h + 1] + upd
        s_ref[sq, h] = s_new
        sb_ref[sq, h] = s_new.astype(BF16)

    pairs_hh = [(pi, hh) for pi in range(nprob) for hh in range(2)]
    apply_ops = ([functools.partial(read_state, pi, hh) for pi, hh in pairs_hh]
                 + [functools.partial(write_out, pi) for pi in range(nprob)]
                 + [functools.partial(update_state, pi, hh) for pi, hh in pairs_hh])

    def emit_apply(n):
        for _ in range(min(n, len(apply_ops))):
            apply_ops.pop(0)()

    emit_apply(len(pairs_hh) // 2)
    tp = jnp.minimum(t, nc - 1)
    ri = lax.broadcasted_iota(jnp.int32, (c2, c2), 0)
    ci = lax.broadcasted_iota(jnp.int32, (c2, c2), 1)
    same = (ri >> 6) == (ci >> 6)
    offdiag = ri != ci
    top =lax.broadcasted_iota(jnp.int32, (c2, 1), 0) < c
    left = lax.broadcasted_iota(jnp.int32, (1, c2), 1) < c
    row_id = lax.broadcasted_iota(jnp.int32, (c, 128), 0)

    in_refs = ((qf_ref, kf_ref, vf_ref, baf_ref), (qb_ref, kb_ref, vb_ref, bab_ref))
    seqs = []
    for sq in range(2 * nb):
        bi, d = divmod(sq, 2)
        q_ref, k_ref, v_ref, ba_ref = in_refs[d]
        seq = tp if d == 0 else nc - 1 - tp
        blk = jnp.where(seq < 2, nc - 2 + seq, seq - 2)
        first_valid = jnp.where(blk == nc - 2, c, jnp.where(blk == nc - 1, c - N_META, 0))
        valid = row_id >= first_valid
        ba = ba_ref[bi]
        beta = jnp.where(valid, jax.nn.sigmoid(ba), 0.0)
        xs = ba + dtb_ref[d]
        softplus = jnp.maximum(xs, 0.0) + jnp.log(1.0 + jnp.exp(-jnp.abs(xs)))
        g = jnp.where(valid, -jnp.exp(alog_ref[d]) * softplus, 0.0)
        incl = same & ((ri >= ci) if d == 0 else (ri <= ci))
        tri = jnp.where(incl, 1.0, 0.0).astype(BF16)
        ghi, gmid, glo = _split3(jnp.concatenate([g, g], axis=0))
        gc2 = (jnp.dot(tri, ghi, preferred_element_type=F32)
               + jnp.dot(tri, gmid, preferred_element_type=F32)
               + jnp.dot(tri, glo, preferred_element_type=F32))
        seqs.append(dict(q=q_ref.at[bi], k=k_ref.at[bi], v=v_ref.at[bi], incl=incl, gc2=gc2, gc2t=gc2.T,
                         beta2=jnp.concatenate([beta, beta], axis=0),
                         gtot=jnp.sum(g, axis=0, keepdims=True)))

    def stacked(ref, h0, h1, width):
        return jnp.concatenate([ref[:, h0 * width:(h0 + 1) * width],
                                ref[:, h1 * width:(h1 + 1) * width]], axis=0)

    probs = []
    for dd in seqs:
        for p in range(npair):
            h0, h1 = 2 * p, 2 * p + 1
            a0, a1 = GDN_HEADS + h0, GDN_HEADS + h1
            col = jnp.where(top, dd["gc2"][:, a0:a0 + 1], dd["gc2"][:, a1:a1 + 1])
            row = jnp.where(left, dd["gc2t"][a0:a0 + 1, :], dd["gc2t"][a1:a1 + 1, :])
            bcol = jnp.where(top, dd["beta2"][:, h0:h0 + 1], dd["beta2"][:, h1:h1 + 1])
            tot = jnp.where(top, dd["gtot"][:, a0:a0 + 1], dd["gtot"][:, a1:a1 + 1])
            dec = jnp.exp(jnp.where(dd["incl"], col - row, -jnp.inf))
            kst = stacked(dd["k"], h0, h1, GDN_DK)
            qst = stacked(dd["q"], h0, h1, GDN_DK)
            vst = stacked(dd["v"], h0, h1, GDN_DV)
            kf = kst.astype(F32)
            kb = kf * bcol
            sc = lax.dot_general(jnp.concatenate([qst, kb.astype(BF16)], axis=0), kst,
                                 (((1,), (1,)), ((), ())), preferred_element_type=F32)
            egc = jnp.exp(col)
            probs.append(dict(
                attn=(sc[:c2] * dec).astype(BF16),
                a=jnp.where(offdiag, sc[c2:] * dec, 0.0),
                rhs=jnp.concatenate([vst.astype(F32) * bcol, kb * egc], axis=1).astype(BF16),
                qg=(qst.astype(F32) * egc).astype(BF16),
                kd=(kf * jnp.exp(tot - col)).astype(BF16)))

    emit_apply(len(pairs_hh) // 2)

    for _ in _unit_lower_inverse_staged(probs):
        emit_apply(3 * nb)
    emit_apply(len(apply_ops))

    for pi, pr in enumerate(probs):
        uw = jnp.dot(pr["tinv"].astype(BF16), pr["rhs"], preferred_element_type=F32)
        u_ref[pi] = uw[:, :GDN_DV]
        w = uw[:, GDN_DV:].astype(BF16)
        for hh in range(2):
            rs = slice(hh * c, (hh + 1) * c)
            qw_ref[pi, hh] = jnp.concatenate([pr["qg"][rs], w[rs]], axis=0)
        attn_ref[pi] = pr["attn"]
        kd_ref[pi] = pr["kd"]
    for sq, dd in enumerate(seqs):
        eg_ref[sq] = jnp.exp(dd["gtot"])


def _unit_lower_inverse_staged(probs):
    c = GDN_CHUNK
    nside = 4
    ri = lax.broadcasted_iota(jnp.int32, (c, nside * c), 0)
    ci = lax.broadcasted_iota(jnp.int32, (c, nside * c), 1)
    lane_blk = ci >> 6
    within = ci & (c - 1)
    diag16 = (ri >> 4) == (within >> 4)
    eye = jnp.where(ri == within, 1.0, 0.0).astype(F32)
    left = lax.broadcasted_iota(jnp.int32, (1, 2 * c), 1) < c

    def blockdiag(y):
        return jnp.concatenate([jnp.where(lane_blk == r, y, 0.0) for r in range(nside)],
                               axis=0).astype(BF16)

    def mm(x, ybd):
        return jnp.dot(x.astype(BF16), ybd, preferred_element_type=F32)

    groups = []
    for g0 in range(0, len(probs), 2):
        pa, pb = probs[g0], probs[g0 + 1]
        a = jnp.concatenate([pa["a"][:c] + pa["a"][c:], pb["a"][:c] + pb["a"][c:]], axis=1)
        ad = jnp.where(diag16, a, 0.0)
        groups.append(dict(pairs=(pa, pb), ad=ad, an=blockdiag(a - ad), dinv=eye - ad))
    for gr in groups:
        gr["p"] = mm(gr["ad"], blockdiag(gr["ad"]))
    yield
    for gr in groups:
        pbd = blockdiag(gr["p"])
        gr["dinv"] = gr["dinv"] + mm(gr["dinv"], pbd)
        gr["p"] = mm(gr["p"], pbd)
    yield
    for gr in groups:
        pbd = blockdiag(gr["p"])
        gr["dinv"] = gr["dinv"] + mm(gr["dinv"], pbd)
        gr["p"] = mm(gr["p"], pbd)
    yield
    for gr in groups:
        gr["dinv"] = gr["dinv"] + mm(gr["dinv"], blockdiag(gr["p"]))
    yield
    for gr in groups:
        gr["m"] = mm(gr["dinv"], gr["an"])
    yield
    for gr in groups:
        gr["m2"] = blockdiag(mm(gr["m"], blockdiag(gr["m"])))
    yield
    for gr in groups:
        x = eye - gr["m"]
        gr["x"] = x + mm(x, gr["m2"])
    yield
    for gr in groups:
        t = mm(gr["x"], blockdiag(gr["dinv"]))
        for i, pr in enumerate(gr["pairs"]):
            half = t[:, 2 * c * i:2 * c * (i + 1)]
            pr["tinv"] = jnp.concatenate([jnp.where(left, half, 0.0), jnp.where(left, 0.0, half)],
                                         axis=0)


def _gdn_scan(qkv, proj3, alog_rows, dtb_rows, lp):
    b = qkv.shape[0]
    nc = lp // GDN_CHUNK
    ba_col0 = 0

    def blk_of(seq):
        return jnp.where(seq < 2, nc - 2 + seq, seq - 2)

    def chunk_specs(d):
        prep = lambda t: jnp.minimum(t, nc - 1)
        seq = prep if d == 0 else (lambda t: nc - 1 - prep(t))
        return [
            pl.BlockSpec((nb, GDN_CHUNK, GDN_QK), lambda bi, t: (bi, blk_of(seq(t)), 0)),
            pl.BlockSpec((nb, GDN_CHUNK, GDN_QK), lambda bi, t: (bi, blk_of(seq(t)), 1)),
            pl.BlockSpec((nb, GDN_CHUNK, GDN_V), lambda bi, t: (bi, blk_of(seq(t)), 1)),
            pl.BlockSpec((nb, GDN_CHUNK, 128), lambda bi, t: (bi, blk_of(seq(t)), ba_col0 + d)),
        ]

    nb = SCAN_NB if b % SCAN_NB == 0 else 1
    const = pl.BlockSpec((2, 1, 128), lambda bi, t: (0, 0, 0))
    kern = functools.partial(_gdn_scan_kernel, nc=nc)
    applied = lambda t: jnp.maximum(t - 1, 0)
    nseq = 2 * nb
    nprob = nseq * (GDN_HEADS // 2)
    c2 = 2 * GDN_CHUNK
    return pl.pallas_call(
        kern,
        grid=(b // nb, nc + 1),
        in_specs=chunk_specs(0) + chunk_specs(1) + [const, const],
        out_specs=[
            pl.BlockSpec((nb, GDN_CHUNK, GDN_V), lambda bi, t: (bi, blk_of(applied(t)), 0)),
            pl.BlockSpec((nb, GDN_CHUNK, GDN_V), lambda bi, t: (bi, blk_of(nc - 1 - applied(t)), 0)),
        ],
        out_shape=[jax.ShapeDtypeStruct((b, lp, GDN_V), BF16)] * 2,
        scratch_shapes=[
            pltpu.VMEM((nseq, GDN_HEADS, GDN_DK, GDN_DV), F32),
            pltpu.VMEM((nseq, GDN_HEADS, GDN_DK, GDN_DV), BF16),
            pltpu.VMEM((nprob, c2, GDN_DV), F32),
            pltpu.VMEM((nprob, 2, c2, GDN_DK), BF16),
            pltpu.VMEM((nprob, c2, c2), BF16),
            pltpu.VMEM((nprob, c2, GDN_DK), BF16),
            pltpu.VMEM((nseq, 1, 128), F32),
        ],
        compiler_params=_cparams(("parallel", "arbitrary")),
        name="gdn_scan",
    )(qkv, qkv, qkv, proj3, qkv, qkv, qkv, proj3, alog_rows, dtb_rows)


OUT_SUB = 256


def _gdn_out_kernel(of_ref, ob_ref, h_ref, ln_ref, wz_ref, g_ref, w_ref, o_ref):
    nsub = o_ref.shape[0] // OUT_SUB
    rows = [slice(sb * OUT_SUB, (sb + 1) * OUT_SUB) for sb in range(nsub)]

    def gate_logits(sb):
        x = h_ref[rows[sb], :]
        ms = jnp.mean(x * x, axis=-1, keepdims=True)
        xn = (x * lax.rsqrt(ms + NORM_EPS) * ln_ref[...]).astype(BF16)
        return jnp.dot(xn, wz_ref[...], preferred_element_type=F32)

    z_next = gate_logits(0)
    for sb in range(nsub):
        z = z_next
        if sb + 1 < nsub:
            z_next = gate_logits(sb + 1)
        o = of_ref[rows[sb], :].astype(F32) + ob_ref[rows[sb], :].astype(F32)
        gate = z * jax.nn.sigmoid(z)
        ys = []
        for h in range(GDN_HEADS):
            lanes = slice(h * GDN_DV, (h + 1) * GDN_DV)
            oh = o[:, lanes]
            ms = jnp.mean(oh * oh, axis=-1, keepdims=True)
            ys.append((oh * lax.rsqrt(ms + NORM_EPS) * g_ref[...] * gate[:, lanes]).astype(BF16))
        y = jnp.concatenate(ys, axis=1)
        o_ref[rows[sb], :] = h_ref[rows[sb], :] + jnp.dot(y, w_ref[...], preferred_element_type=F32)


def _gdn_out(o_fwd, o_bwd, h0, ln_row, w_z, g_row, w_out):
    rows = h0.shape[0]
    tm = _tile(rows, 768, OUT_SUB)
    const = lambda i: (0, 0)
    return pl.pallas_call(
        _gdn_out_kernel,
        grid=(rows // tm,),
        in_specs=[
            pl.BlockSpec((tm, GDN_V), lambda i: (i, 0)),
            pl.BlockSpec((tm, GDN_V), lambda i: (i, 0)),
            pl.BlockSpec((tm, D_MODEL), lambda i: (i, 0)),
            pl.BlockSpec((1, D_MODEL), const),
            pl.BlockSpec((D_MODEL, GDN_V), const),
            pl.BlockSpec((1, GDN_DV), const),
            pl.BlockSpec((GDN_V, D_MODEL), const),
        ],
        out_specs=pl.BlockSpec((tm, D_MODEL), lambda i: (i, 0)),
        out_shape=jax.ShapeDtypeStruct((rows, D_MODEL), F32),
        compiler_params=_cparams(("parallel",)),
        name="gdn_out",
    )(o_fwd, o_bwd, h0, ln_row, w_z, g_row, w_out)


def _rope(r, cos, nsin_lo, sin_hi):
    return r * cos + pltpu.roll(r, 96, 1) * nsin_lo + pltpu.roll(r, 32, 1) * sin_hi


Q_GROUP = 2


def _mla_q_kernel(cq_ref, g1_ref, wt_ref, ga_ref, gr_ref, cos_ref, sin_ref, shift_ref, qt_ref):
    cq = cq_ref[0].astype(F32)
    tm = cq.shape[0]
    ms = jnp.mean(cq * cq, axis=-1, keepdims=True)
    cqt = (cq * lax.rsqrt(ms + NORM_EPS) * g1_ref[...]).T.astype(BF16)
    half = MLA_ROPE // 2
    tile = lambda g: jnp.concatenate([g] * (tm // 128), axis=1)
    ga = tile(ga_ref[...])
    gr1, gr2 = tile(gr_ref[0:half, :]), tile(gr_ref[half:MLA_ROPE, :])
    cos, sin = cos_ref[...], sin_ref[...]
    pad_rows = tile(shift_ref[...]).astype(BF16)

    def project(g):
        rows = slice(g * Q_GROUP * MLA_DQK, (g + 1) * Q_GROUP * MLA_DQK)
        return jnp.dot(wt_ref[rows, :], cqt, preferred_element_type=F32)

    q_next = project(0)
    for h in range(MLA_HEADS):
        if h % Q_GROUP == 0:
            q = q_next
            if h + Q_GROUP < MLA_HEADS:
                q_next = project(h // Q_GROUP + 1)
        r0 = (h % Q_GROUP) * MLA_DQK
        a = q[r0:r0 + MLA_NOPE]
        x1 = q[r0 + MLA_NOPE:r0 + MLA_NOPE + half]
        x2 = q[r0 + MLA_NOPE + half:r0 + MLA_DQK]
        ss = (jnp.sum(a * a, axis=0, keepdims=True) + jnp.sum(x1 * x1, axis=0, keepdims=True)
              + jnp.sum(x2 * x2, axis=0, keepdims=True))
        inv = lax.rsqrt(ss * (1.0 / MLA_DQK) + NORM_EPS)
        x1 = x1 * inv * gr1
        x2 = x2 * inv * gr2
        qt_ref[0, h, 0:MLA_NOPE, :] = (a * inv * ga).astype(BF16)
        qt_ref[0, h, MLA_NOPE:MLA_NOPE + half, :] = (x1 * cos - x2 * sin).astype(BF16)
        qt_ref[0, h, MLA_NOPE + half:MLA_DQK, :] = (x2 * cos + x1 * sin).astype(BF16)
        qt_ref[0, h, MLA_DQK:MLA_DPAD, :] = pad_rows


def _mla_q(proj3, g1, w_uq_t, ga, gr, cos_t, sin_t, shift_rows, lp):
    b = proj3.shape[0]
    tm = _tile(lp, 384, 128)
    cqcol = MLA_V // MLA_Q_LORA
    const = lambda bi, i: (0, 0)
    return pl.pallas_call(
        _mla_q_kernel,
        grid=(b, lp // tm),
        in_specs=[
            pl.BlockSpec((1, tm, MLA_Q_LORA), lambda bi, i: (bi, i, cqcol)),
            pl.BlockSpec((1, MLA_Q_LORA), const),
            pl.BlockSpec((MLA_HEADS * MLA_DQK, MLA_Q_LORA), const),
            pl.BlockSpec((MLA_NOPE, 128), const),
            pl.BlockSpec((MLA_ROPE, 128), const),
            pl.BlockSpec((MLA_ROPE // 2, tm), lambda bi, i: (0, i)),
            pl.BlockSpec((MLA_ROPE // 2, tm), lambda bi, i: (0, i)),
            pl.BlockSpec((MLA_DPAD - MLA_DQK, 128), const),
        ],
        out_specs=pl.BlockSpec((1, MLA_HEADS, MLA_DPAD, tm), lambda bi, i: (bi, 0, 0, i)),
        out_shape=jax.ShapeDtypeStruct((b, MLA_HEADS, MLA_DPAD, lp), BF16),
        compiler_params=_cparams(("parallel", "parallel")),
        name="mla_q",
    )(proj3, g1, w_uq_t, ga, gr, cos_t, sin_t, shift_rows)


def _mla_kv_kernel(h_ref, ln_ref, ws_ref, wb_ref, g1_ref, w_ref, ga_ref, gr_ref, cos_ref, nsin_ref, sin_ref,
                   zq_ref, k_ref, vt_ref):
    x = h_ref[0]
    xn = (x * lax.rsqrt(jnp.mean(x * x, axis=-1, keepdims=True) + NORM_EPS) * ln_ref[...]).astype(BF16)
    small = jnp.dot(xn, ws_ref[...], preferred_element_type=F32)
    ckv = small[:, :MLA_KV_LORA]
    ms = jnp.mean(ckv * ckv, axis=-1, keepdims=True)
    cn = (ckv * lax.rsqrt(ms + NORM_EPS) * g1_ref[...]).astype(BF16)
    kv = jnp.dot(cn, w_ref[...], preferred_element_type=F32)
    zq_ref[0] = jnp.dot(xn, wb_ref[...], preferred_element_type=F32).astype(BF16)
    kpe = small[:, MLA_KV_LORA:]
    sq_pe = kpe * kpe
    kr = _rope(kpe * gr_ref[...], cos_ref[...], nsin_ref[...], sin_ref[...])
    one_hot = jnp.where(lax.broadcasted_iota(jnp.int32, (1, 128), 1) == MLA_ROPE, 1.0, 0.0).astype(F32)
    for h in range(MLA_HEADS):
        kn = kv[:, h * MLA_NOPE:(h + 1) * MLA_NOPE]
        ss = jnp.sum(kn * kn + sq_pe, axis=-1, keepdims=True)
        inv = lax.rsqrt(ss * (1.0 / MLA_DQK) + NORM_EPS)
        k_ref[0, h, :, 0:MLA_NOPE] = (kn * inv * ga_ref[...]).astype(BF16)
        k_ref[0, h, :, MLA_NOPE:MLA_DPAD] = (kr * inv + one_hot).astype(BF16)
        v = kv[:, MLA_HEADS * MLA_NOPE + h * MLA_DV:MLA_HEADS * MLA_NOPE + (h + 1) * MLA_DV]
        vt_ref[0, h] = v.T.astype(BF16)


def _mla_proj_kv(h3, ln_row, w_small, w_big, g1, w_ukv, ga, gr, cos, nsin, sin, lp):
    b = h3.shape[0]
    tm = _tile(lp, 384, 128)
    const = lambda bi, i: (0, 0)
    nbig = w_big.shape[1]
    return pl.pallas_call(
        _mla_kv_kernel,
        grid=(b, lp // tm),
        in_specs=[
            pl.BlockSpec((1, tm, D_MODEL), lambda bi, i: (bi, i, 0)),
            pl.BlockSpec((1, D_MODEL), const),
            pl.BlockSpec((D_MODEL, MLA_KV_LORA + 128), const),
            pl.BlockSpec((D_MODEL, nbig), const),
            pl.BlockSpec((1, MLA_KV_LORA), const),
            pl.BlockSpec((MLA_KV_LORA, MLA_HEADS * (MLA_NOPE + MLA_DV)), const),
            pl.BlockSpec((1, 128), const),
            pl.BlockSpec((1, 128), const),
            pl.BlockSpec((tm, 128), lambda bi, i: (i, 0)),
            pl.BlockSpec((tm, 128), lambda bi, i: (i, 0)),
            pl.BlockSpec((tm, 128), lambda bi, i: (i, 0)),
        ],
        out_specs=[
            pl.BlockSpec((1, tm, nbig), lambda bi, i: (bi, i, 0)),
            pl.BlockSpec((1, MLA_HEADS, tm, MLA_DPAD), lambda bi, i: (bi, 0, i, 0)),
            pl.BlockSpec((1, MLA_HEADS, MLA_DV, tm), lambda bi, i: (bi, 0, 0, i)),
        ],
        out_shape=[
            jax.ShapeDtypeStruct((b, lp, nbig), BF16),
            jax.ShapeDtypeStruct((b, MLA_HEADS, lp, MLA_DPAD), BF16),
            jax.ShapeDtypeStruct((b, MLA_HEADS, MLA_DV, lp), BF16),
        ],
        compiler_params=_cparams(("parallel", "parallel")),
        name="mla_proj_kv",
    )(h3, ln_row, w_small, w_big, g1, w_ukv, ga, gr, cos, nsin, sin)


ATT_TK = 1024
ATT_TQ = 512
ATT_AHEAD = 2
ATT_BOUND_MAX = 60.0


def _attn_kernel(bounded_ref, qt_ref, k_ref, vt_ref, o_ref, *, s_len):
    tq = qt_ref.shape[3]
    nstrip = tq // ATT_TQ
    chunks = [(ck * ATT_TK, ATT_TK) for ck in range(s_len // ATT_TK)] + [(s_len, TAIL)]
    units = [(i, s) for i in range(len(chunks)) for s in range(nstrip)]
    is_meta = lax.broadcasted_iota(jnp.int32, (TAIL, 1), 0) >= TAIL - N_META
    tail_bias = jnp.where(is_meta, 0.0, -jnp.inf).astype(F32)

    def scores(u):
        i, s = units[u]
        k0, nk = chunks[i]
        st = jnp.dot(k_ref[0, 0, k0:k0 + nk, :], qt_ref[0, 0, :, s * ATT_TQ:(s + 1) * ATT_TQ],
                     preferred_element_type=F32)
        return st + tail_bias if i == len(chunks) - 1 else st

    def run(bounded):
        m = [jnp.full((1, ATT_TQ), -jnp.inf, F32) for _ in range(nstrip)]
        l = [jnp.zeros((1, ATT_TQ), F32) for _ in range(nstrip)]
        acc = [jnp.zeros((MLA_DV, ATT_TQ), F32) for _ in range(nstrip)]
        pending = [scores(u) for u in range(min(ATT_AHEAD, len(units)))]
        for u, (i, s) in enumerate(units):
            if u + ATT_AHEAD < len(units):
                pending.append(scores(u + ATT_AHEAD))
            st = pending.pop(0)
            k0, nk = chunks[i]
            if bounded:
                p = jnp.exp2(st)
                l[s] = l[s] + jnp.sum(p, axis=0, keepdims=True)
                acc[s] = acc[s] + jnp.dot(vt_ref[0, 0, :, k0:k0 + nk], p.astype(BF16),
                                          preferred_element_type=F32)
            else:
                m_new = jnp.maximum(m[s], jnp.max(st, axis=0, keepdims=True))
                alpha = jnp.exp2(m[s] - m_new)
                p = jnp.exp2(st - m_new)
                l[s] = alpha * l[s] + jnp.sum(p, axis=0, keepdims=True)
                acc[s] = alpha * acc[s] + jnp.dot(vt_ref[0, 0, :, k0:k0 + nk], p.astype(BF16),
                                                  preferred_element_type=F32)
                m[s] = m_new
        for s in range(nstrip):
            o_ref[0, s * ATT_TQ:(s + 1) * ATT_TQ, :] = (acc[s] / l[s]).T.astype(o_ref.dtype)

    @pl.when(bounded_ref[0] == 1)
    def _():
        run(True)

    @pl.when(bounded_ref[0] != 1)
    def _():
        run(False)


def _attention(bounded, qt, k, vt, s_len):
    b = qt.shape[0]
    lp = k.shape[2]
    assert s_len % ATT_TK == 0 and lp == s_len + TAIL
    tq = _tile(s_len, 1024, ATT_TQ)
    kern = functools.partial(_attn_kernel, s_len=s_len)
    return pl.pallas_call(
        kern,
        grid=(b, MLA_HEADS, s_len // tq),
        in_specs=[
            pl.BlockSpec(memory_space=pltpu.SMEM),
            pl.BlockSpec((1, 1, MLA_DPAD, tq), lambda bi, h, i: (bi, h, 0, i)),
            pl.BlockSpec((1, 1, lp, MLA_DPAD), lambda bi, h, i: (bi, h, 0, 0)),
            pl.BlockSpec((1, 1, MLA_DV, lp), lambda bi, h, i: (bi, h, 0, 0)),
        ],
        out_specs=pl.BlockSpec((1, tq, MLA_DV), lambda bi, h, i: (bi, i, h)),
        out_shape=jax.ShapeDtypeStruct((b, s_len, MLA_V), BF16),
        compiler_params=_cparams(("parallel", "parallel", "arbitrary")),
        name="mla_attention",
    )(bounded, qt, k, vt)


def _mla_out_kernel(o_ref, z_ref, h_ref, w_ref, y_ref):
    z = z_ref[0].astype(F32)
    y = (o_ref[0].astype(F32) * (z * jax.nn.sigmoid(z))).astype(BF16)
    y_ref[0] = h_ref[0] + jnp.dot(y, w_ref[...], preferred_element_type=F32)


def _mla_out(o, proj3, h3, w_out, b0, nb, s_len):
    tm = _tile(s_len, 1024, 128)
    return pl.pallas_call(
        _mla_out_kernel,
        grid=(nb, s_len // tm),
        in_specs=[
            pl.BlockSpec((1, tm, MLA_V), lambda bi, i: (bi + b0, i, 0)),
            pl.BlockSpec((1, tm, MLA_V), lambda bi, i: (bi + b0, i, 0)),
            pl.BlockSpec((1, tm, D_MODEL), lambda bi, i: (bi + b0, i, 0)),
            pl.BlockSpec((MLA_V, D_MODEL), lambda bi, i: (0, 0)),
        ],
        out_specs=pl.BlockSpec((1, tm, D_MODEL), lambda bi, i: (bi, i, 0)),
        out_shape=jax.ShapeDtypeStruct((nb, s_len, D_MODEL), F32),
        compiler_params=_cparams(("parallel", "parallel")),
        name="mla_out",
    )(o, proj3, h3, w_out)


def _pad_cols(w, n):
    return jnp.pad(w, ((0, 0), (0, n - w.shape[1])))


def _lane_row(v, n=128):
    return jnp.pad(v.astype(F32), (0, n - v.shape[0]))[None, :]


def _trunk_all(xs, meta_tokens, ln_g, gdn_w_in, gdn_conv_w, gdn_a_log, gdn_dt_bias, gdn_o_norm_g,
               gdn_w_out, mla_w_in, mla_q_norm_g, mla_kv_norm_g, mla_w_uq, mla_w_ukv, mla_qk_q_g,
               mla_qk_k_g, mla_w_out):
    s_len = xs[0].shape[1]
    assert all(x.shape[1] == s_len for x in xs) and s_len % 128 == 0
    lp = s_len + TAIL
    x_all = jnp.concatenate(xs, axis=0)
    b = x_all.shape[0]
    meta = jnp.broadcast_to(meta_tokens[None].astype(F32), (b, N_META, D_MODEL))
    h0 = jnp.concatenate([x_all, jnp.zeros((b, TAIL - N_META, D_MODEL), F32), meta], axis=1)
    h0 = h0.reshape(b * lp, D_MODEL)

    w_in = gdn_w_in[0]
    ba = w_in[:, GDN_CONV_CH + GDN_V:].reshape(D_MODEL, 2, 2, GDN_HEADS)
    ba_dir = [_pad_cols(jnp.concatenate([ba[:, 0, d], ba[:, 1, d]], axis=1), 128) for d in range(2)]
    conv_w8 = jnp.pad(gdn_conv_w[0], ((0, 8 - GDN_CONV), (0, 0)))
    qkv, gates = _gdn_qkv_conv(h0.reshape(b, lp, D_MODEL), ln_g[0][None, :],
                               w_in[:, :GDN_CONV_CH].astype(BF16),
                               jnp.concatenate(ba_dir, axis=1).astype(BF16), conv_w8, lp)

    lane_a = lambda v: jnp.pad(v.astype(F32), ((0, 0), (GDN_HEADS, 128 - 2 * GDN_HEADS)))[:, None, :]
    o_fwd, o_bwd = _gdn_scan(qkv, gates, lane_a(gdn_a_log[0]), lane_a(gdn_dt_bias[0]), lp)
    h1 = _gdn_out(o_fwd.reshape(b * lp, GDN_V), o_bwd.reshape(b * lp, GDN_V), h0, ln_g[0][None, :],
                  w_in[:, GDN_CONV_CH:GDN_CONV_CH + GDN_V].astype(BF16),
                  gdn_o_norm_g[0][None, :], gdn_w_out[0].astype(BF16))

    w_in1 = mla_w_in[0]
    o1 = MLA_Q_LORA
    o2_ = o1 + MLA_KV_LORA
    o3 = o2_ + MLA_ROPE
    w_small = jnp.concatenate([w_in1[:, o1:o2_], _pad_cols(w_in1[:, o2_:o3], 128)], axis=1).astype(BF16)
    w_big = jnp.concatenate([w_in1[:, o3:], w_in1[:, :o1]], axis=1).astype(BF16)
    h1_3 = h1.reshape(b, lp, D_MODEL)

    pos = jnp.concatenate([jnp.arange(s_len, dtype=F32) + N_META, jnp.zeros((TAIL - N_META,), F32),
                           jnp.arange(N_META, dtype=F32)])
    inv = ROPE_THETA ** (-jnp.arange(0, MLA_ROPE, 2, dtype=F32) / MLA_ROPE)
    ang = pos[:, None] * inv[None, :]
    zc = jnp.zeros_like(ang)
    cos_t = jnp.concatenate([jnp.cos(ang), jnp.cos(ang), zc, zc], axis=1)
    nsin_t = jnp.concatenate([-jnp.sin(ang), zc, zc, zc], axis=1)
    sin_t = jnp.concatenate([zc, jnp.sin(ang), zc, zc], axis=1)

    scale = MLA_DQK ** -0.5 * math.log2(math.e)
    gq = jnp.broadcast_to((mla_qk_q_g[0].astype(F32) * scale)[:, None], (MLA_DQK, 128))
    score_bound = (1.02 * MLA_DQK * scale * jnp.max(jnp.abs(mla_qk_q_g[0].astype(F32)))
                   * jnp.max(jnp.abs(mla_qk_k_g[0].astype(F32))))
    bounded = score_bound < ATT_BOUND_MAX
    shift_rows = jnp.zeros((MLA_DPAD - MLA_DQK, 128), F32).at[0, :].set(jnp.where(bounded, -score_bound, 0.0))
    w_ukv = mla_w_ukv[0].reshape(MLA_KV_LORA, MLA_HEADS, MLA_NOPE + MLA_DV)
    w_ukv = jnp.concatenate([w_ukv[:, :, :MLA_NOPE].reshape(MLA_KV_LORA, -1),
                             w_ukv[:, :, MLA_NOPE:].reshape(MLA_KV_LORA, -1)], axis=1).astype(BF16)
    gk = mla_qk_k_g[0].astype(F32)
    proj1_3, k, vt = _mla_proj_kv(h1_3, ln_g[1][None, :], w_small, w_big, mla_kv_norm_g[0][None, :], w_ukv,
                                  gk[None, :MLA_NOPE], _lane_row(gk[MLA_NOPE:]), cos_t, nsin_t, sin_t, lp)
    qt = _mla_q(proj1_3, mla_q_norm_g[0][None, :], mla_w_uq[0].T.astype(BF16), gq[:MLA_NOPE],
                gq[MLA_NOPE:], jnp.cos(ang).T, jnp.sin(ang).T, shift_rows, lp)

    o = _attention(bounded.astype(jnp.int32).reshape(1), qt, k, vt, s_len)

    w_out1 = mla_w_out[0].astype(BF16)
    outs = []
    b0 = 0
    for x in xs:
        outs.append(_mla_out(o, proj1_3, h1_3, w_out1, b0, x.shape[0], s_len))
        b0 += x.shape[0]
    return tuple(outs)


def kernel(x_prompt, x_sample, meta_tokens, ln_g, gdn_w_in, gdn_conv_w, gdn_a_log, gdn_dt_bias,
           gdn_o_norm_g, gdn_w_out, mla_w_in, mla_q_norm_g, mla_kv_norm_g, mla_w_uq, mla_w_ukv,
           mla_qk_q_g, mla_qk_k_g, mla_w_out):
    return _trunk_all((x_prompt, x_sample), meta_tokens, ln_g, gdn_w_in, gdn_conv_w, gdn_a_log,
                      gdn_dt_bias, gdn_o_norm_g, gdn_w_out, mla_w_in, mla_q_norm_g, mla_kv_norm_g,
                      mla_w_uq, mla_w_ukv, mla_qk_q_g, mla_qk_k_g, mla_w_out)
```

```python
import functools
import math

import jax
import jax.numpy as jnp
from jax import lax
from jax.experimental import pallas as pl
from jax.experimental.pallas import tpu as pltpu

F32 = jnp.float32
BF16 = jnp.bfloat16

D_MODEL = 1024
N_META = 16
TAIL = 128
NORM_EPS = 1e-6

GDN_HEADS = 8
GDN_DK = 128
GDN_DV = 256
GDN_CONV = 5
GDN_CHUNK = 64
GDN_QK = GDN_HEADS * GDN_DK
GDN_V = GDN_HEADS * GDN_DV
GDN_CONV_CH = 2 * GDN_QK + GDN_V

MLA_HEADS = 16
MLA_Q_LORA = 512
MLA_KV_LORA = 256
MLA_NOPE = 128
MLA_ROPE = 64
MLA_DQK = MLA_NOPE + MLA_ROPE
MLA_DV = 128
MLA_V = MLA_HEADS * MLA_DV
MLA_DPAD = 256
ROPE_THETA = 10000.0

VMEM_LIMIT = 56 * 1024 * 1024


def _cparams(sem):
    return pltpu.CompilerParams(dimension_semantics=sem, vmem_limit_bytes=VMEM_LIMIT)


def _tile(n, target, mult):
    best = None
    t = mult
    while t <= min(n, target):
        if n % t == 0:
            best = t
        t += mult
    assert best is not None, (n, target, mult)
    return best


CONV_SUB = 64
CONV_LANES = GDN_DK
CONV_TC = 1024


def _qkv_conv_kernel(prev_ref, main_ref, next_ref, g_ref, w_ref, wba_ref, cw_ref, o_ref, ba_ref, ext_ref,
                     *, tr):
    half = GDN_CONV // 2
    nblk = GDN_CONV_CH // CONV_TC
    x = jnp.concatenate([prev_ref[0], main_ref[0], next_ref[0]], axis=0)
    ms = jnp.mean(x * x, axis=-1, keepdims=True)
    xn = (x * lax.rsqrt(ms + NORM_EPS) * g_ref[...]).astype(BF16)
    ba_ref[0] = jnp.dot(xn, wba_ref[...], preferred_element_type=F32)[8:8 + tr]

    def project(j):
        ext_ref[j] = jnp.dot(xn, w_ref[:, j * CONV_TC:(j + 1) * CONV_TC], preferred_element_type=F32)

    def conv_silu(j, sb, lb):
        lanes = slice(j * CONV_TC + lb * CONV_LANES, j * CONV_TC + (lb + 1) * CONV_LANES)
        nrow = CONV_SUB + 16
        x2 = ext_ref[j, sb * CONV_SUB:sb * CONV_SUB + nrow, lb * CONV_LANES:(lb + 1) * CONV_LANES]
        acc = x2[8:8 + CONV_SUB] * cw_ref[half:half + 1, lanes]
        for s in range(1, half + 1):
            down = pltpu.roll(x2, s, 0)
            acc = acc + down[8:8 + CONV_SUB] * cw_ref[half - s:half - s + 1, lanes]
            up = pltpu.roll(x2, nrow - s, 0)
            acc = acc + up[8:8 + CONV_SUB] * cw_ref[half + s:half + s + 1, lanes]
        return acc * jax.nn.sigmoid(acc)

    def finish(j):
        for sb in range(tr // CONV_SUB):
            rows = slice(sb * CONV_SUB, (sb + 1) * CONV_SUB)
            for lb in range(CONV_TC // CONV_LANES):
                y = conv_silu(j, sb, lb)
                c0 = j * CONV_TC + lb * CONV_LANES
                lanes = slice(c0, c0 + CONV_LANES)
                if c0 < 2 * GDN_QK:
                    qscale = GDN_DK ** -0.5 if c0 < GDN_QK else 1.0
                    ss = jnp.sum(y * y, axis=-1, keepdims=True)
                    y = y * (lax.rsqrt(ss + NORM_EPS) * qscale)
                o_ref[0, rows, lanes] = y.astype(BF16)

    project(0)
    for j in range(nblk):
        if j + 1 < nblk:
            project(j + 1)
        finish(j)


def _gdn_qkv_conv(h3, g, w_qkv, w_ba, conv_w8, lp):
    b = h3.shape[0]
    nba = w_ba.shape[1]
    tr = _tile(lp, 704, CONV_SUB)
    nb8 = lp // 8
    tb = tr // 8
    kern = functools.partial(_qkv_conv_kernel, tr=tr)
    const = lambda bi, i: (0, 0)
    return pl.pallas_call(
        kern,
        grid=(b, lp // tr),
        in_specs=[
            pl.BlockSpec((1, 8, D_MODEL), lambda bi, i: (bi, (i * tb + nb8 - 1) % nb8, 0)),
            pl.BlockSpec((1, tr, D_MODEL), lambda bi, i: (bi, i, 0)),
            pl.BlockSpec((1, 8, D_MODEL), lambda bi, i: (bi, ((i + 1) * tb) % nb8, 0)),
            pl.BlockSpec((1, D_MODEL), const),
            pl.BlockSpec((D_MODEL, GDN_CONV_CH), const),
            pl.BlockSpec((D_MODEL, nba), const),
            pl.BlockSpec((8, GDN_CONV_CH), const),
        ],
        out_specs=[
            pl.BlockSpec((1, tr, GDN_CONV_CH), lambda bi, i: (bi, i, 0)),
            pl.BlockSpec((1, tr, nba), lambda bi, i: (bi, i, 0)),
        ],
        out_shape=[
            jax.ShapeDtypeStruct((b, lp, GDN_CONV_CH), BF16),
            jax.ShapeDtypeStruct((b, lp, nba), F32),
        ],
        scratch_shapes=[pltpu.VMEM((GDN_CONV_CH // CONV_TC, tr + 16, CONV_TC), F32)],
        compiler_params=_cparams(("parallel", "parallel")),
        name="gdn_qkv_conv",
    )(h3, h3, h3, g, w_qkv, w_ba, conv_w8)


SCAN_NB = 4


def _split3(x):
    hi = x.astype(BF16)
    r1 = x - hi.astype(F32)
    mid = r1.astype(BF16)
    lo = (r1 - mid.astype(F32)).astype(BF16)
    return hi, mid, lo


def _gdn_scan_kernel(qf_ref, kf_ref, vf_ref, baf_ref, qb_ref, kb_ref, vb_ref, bab_ref, alog_ref, dtb_ref,
                     of_ref, ob_ref, s_ref, sb_ref, u_ref, qw_ref, attn_ref, kd_ref, eg_ref, *, nc):
    t = pl.program_id(1)
    c = GDN_CHUNK
    c2 = 2 * c
    npair = GDN_HEADS // 2
    o_refs = (of_ref, ob_ref)

    @pl.when(t == 0)
    def _():
        for ref in (s_ref, sb_ref, u_ref, qw_ref, attn_ref, kd_ref, eg_ref):
            ref[...] = jnp.zeros_like(ref)

    nb = of_ref.shape[0]
    nprob = nb * 2 * npair
    applied = [dict(sq=pi // npair, heads=(2 * (pi % npair), 2 * (pi % npair) + 1),
                    qs=[None, None], vnew=[None, None]) for pi in range(nprob)]

    def read_state(pi, hh):
        ap = applied[pi]
        rs = slice(hh * c, (hh + 1) * c)
        qws = jnp.dot(qw_ref[pi, hh], sb_ref[ap["sq"], ap["heads"][hh]], preferred_element_type=F32)
        ap["qs"][hh] = qws[:c]
        ap["vnew"][hh] = (u_ref[pi, rs, :] - qws[c:]).astype(BF16)

    def write_out(pi):
        ap = applied[pi]
        h0, h1 = ap["heads"]
        bi, d = divmod(ap["sq"], 2)
        vnew2 = jnp.concatenate(ap["vnew"], axis=0)
        o2 = jnp.concatenate(ap["qs"], axis=0) + jnp.dot(attn_ref[pi], vnew2, preferred_element_type=F32)
        o_refs[d][bi, :, h0 * GDN_DV:(h0 + 1) * GDN_DV] = o2[:c].astype(BF16)
        o_refs[d][bi, :, h1 * GDN_DV:(h1 + 1) * GDN_DV] = o2[c:].astype(BF16)

    def update_state(pi, hh):
        ap = applied[pi]
        sq, h = ap["sq"], ap["heads"][hh]
        rs = slice(hh * c, (hh + 1) * c)
        a_h = GDN_HEADS + h
        upd = lax.dot_general(kd_ref[pi, rs, :], ap["vnew"][hh], (((0,), (0,)), ((), ())),
                              preferred_element_type=F32)
        s_new = s_ref[sq, h] * eg_ref[sq, :, a_h:a_h + 1] + upd
        s_ref[sq, h] = s_new
        sb_ref[sq, h] = s_new.astype(BF16)

    pairs_hh = [(pi, hh) for pi in range(nprob) for hh in range(2)]
    apply_ops = ([functools.partial(read_state, pi, hh) for pi, hh in pairs_hh]
                 + [functools.partial(write_out, pi) for pi in range(nprob)]
                 + [functools.partial(update_state, pi, hh) for pi, hh in pairs_hh])

    def emit_apply(n):
        for _ in range(min(n, len(apply_ops))):
            apply_ops.pop(0)()

    emit_apply(len(pairs_hh) // 2)
    tp = jnp.minimum(t, nc - 1)
    ri = lax.broadcasted_iota(jnp.int32, (c2, c2), 0)
    ci = lax.broadcasted_iota(jnp.int32, (c2, c2), 1)
    same = (ri >> 6) == (ci >> 6)
    offdiag = ri != ci
    top =lax.broadcasted_iota(jnp.int32, (c2, 1), 0) < c
    left = lax.broadcasted_iota(jnp.int32, (1, c2), 1) < c
    row_id = lax.broadcasted_iota(jnp.int32, (c, 128), 0)

    in_refs = ((qf_ref, kf_ref, vf_ref, baf_ref), (qb_ref, kb_ref, vb_ref, bab_ref))
    seqs = []
    for sq in range(2 * nb):
        bi, d = divmod(sq, 2)
        q_ref, k_ref, v_ref, ba_ref = in_refs[d]
        seq = tp if d == 0 else nc - 1 - tp
        blk = jnp.where(seq < 2, nc - 2 + seq, seq - 2)
        first_valid = jnp.where(blk == nc - 2, c, jnp.where(blk == nc - 1, c - N_META, 0))
        valid = row_id >= first_valid
        ba = ba_ref[bi]
        beta = jnp.where(valid, jax.nn.sigmoid(ba), 0.0)
        xs = ba + dtb_ref[d]
        softplus = jnp.maximum(xs, 0.0) + jnp.log(1.0 + jnp.exp(-jnp.abs(xs)))
        g = jnp.where(valid, -jnp.exp(alog_ref[d]) * softplus, 0.0)
        incl = same & ((ri >= ci) if d == 0 else (ri <= ci))
        tri = jnp.where(incl, 1.0, 0.0).astype(BF16)
        ghi, gmid, glo = _split3(jnp.concatenate([g, g], axis=0))
        gc2 = (jnp.dot(tri, ghi, preferred_element_type=F32)
               + jnp.dot(tri, gmid, preferred_element_type=F32)
               + jnp.dot(tri, glo, preferred_element_type=F32))
        seqs.append(dict(q=q_ref.at[bi], k=k_ref.at[bi], v=v_ref.at[bi], incl=incl, gc2=gc2, gc2t=gc2.T,
                         beta2=jnp.concatenate([beta, beta], axis=0),
                         gtot=jnp.sum(g, axis=0, keepdims=True)))

    def stacked(ref, h0, h1, width):
        return jnp.concatenate([ref[:, h0 * width:(h0 + 1) * width],
                                ref[:, h1 * width:(h1 + 1) * width]], axis=0)

    probs = []
    for dd in seqs:
        for p in range(npair):
            h0, h1 = 2 * p, 2 * p + 1
            a0, a1 = GDN_HEADS + h0, GDN_HEADS + h1
            col = jnp.where(top, dd["gc2"][:, a0:a0 + 1], dd["gc2"][:, a1:a1 + 1])
            row = jnp.where(left, dd["gc2t"][a0:a0 + 1, :], dd["gc2t"][a1:a1 + 1, :])
            bcol = jnp.where(top, dd["beta2"][:, h0:h0 + 1], dd["beta2"][:, h1:h1 + 1])
            tot = jnp.where(top, dd["gtot"][:, a0:a0 + 1], dd["gtot"][:, a1:a1 + 1])
            dec = jnp.exp(jnp.where(dd["incl"], col - row, -jnp.inf))
            kst = stacked(dd["k"], h0, h1, GDN_DK)
            qst = stacked(dd["q"], h0, h1, GDN_DK)
            vst = stacked(dd["v"], h0, h1, GDN_DV)
            kf = kst.astype(F32)
            kb = kf * bcol
            sc = lax.dot_general(jnp.concatenate([qst, kb.astype(BF16)], axis=0), kst,
                                 (((1,), (1,)), ((), ())), preferred_element_type=F32)
            egc = jnp.exp(col)
            probs.append(dict(
                attn=(sc[:c2] * dec).astype(BF16),
                a=jnp.where(offdiag, sc[c2:] * dec, 0.0),
                rhs=jnp.concatenate([vst.astype(F32) * bcol, kb * egc], axis=1).astype(BF16),
                qg=(qst.astype(F32) * egc).astype(BF16),
                kd=(kf * jnp.exp(tot - col)).astype(BF16)))

    emit_apply(len(pairs_hh) // 2)

    for _ in _unit_lower_inverse_staged(probs):
        emit_apply(3 * nb)
    emit_apply(len(apply_ops))

    for pi, pr in enumerate(probs):
        uw = jnp.dot(pr["tinv"].astype(BF16), pr["rhs"], preferred_element_type=F32)
        u_ref[pi] = uw[:, :GDN_DV]
        w = uw[:, GDN_DV:].astype(BF16)
        for hh in range(2):
            rs = slice(hh * c, (hh + 1) * c)
            qw_ref[pi, hh] = jnp.concatenate([pr["qg"][rs], w[rs]], axis=0)
        attn_ref[pi] = pr["attn"]
        kd_ref[pi] = pr["kd"]
    for sq, dd in enumerate(seqs):
        eg_ref[sq] = jnp.exp(dd["gtot"])


def _unit_lower_inverse_staged(probs):
    c = GDN_CHUNK
    nside = 4
    ri = lax.broadcasted_iota(jnp.int32, (c, nside * c), 0)
    ci = lax.broadcasted_iota(jnp.int32, (c, nside * c), 1)
    lane_blk = ci >> 6
    within = ci & (c - 1)
    diag16 = (ri >> 4) == (within >> 4)
    eye = jnp.where(ri == within, 1.0, 0.0).astype(F32)
    left = lax.broadcasted_iota(jnp.int32, (1, 2 * c), 1) < c

    def blockdiag(y):
        return jnp.concatenate([jnp.where(lane_blk == r, y, 0.0) for r in range(nside)],
                               axis=0).astype(BF16)

    def mm(x, ybd):
        return jnp.dot(x.astype(BF16), ybd, preferred_element_type=F32)

    groups = []
    for g0 in range(0, len(probs), 2):
        pa, pb = probs[g0], probs[g0 + 1]
        a = jnp.concatenate([pa["a"][:c] + pa["a"][c:], pb["a"][:c] + pb["a"][c:]], axis=1)
        ad = jnp.where(diag16, a, 0.0)
        groups.append(dict(pairs=(pa, pb), ad=ad, an=blockdiag(a - ad), dinv=eye - ad))
    for gr in groups:
        gr["p"] = mm(gr["ad"], blockdiag(gr["ad"]))
    yield
    for gr in groups:
        pbd = blockdiag(gr["p"])
        gr["dinv"] = gr["dinv"] + mm(gr["dinv"], pbd)
        gr["p"] = mm(gr["p"], pbd)
    yield
    for gr in groups:
        pbd = blockdiag(gr["p"])
        gr["dinv"] = gr["dinv"] + mm(gr["dinv"], pbd)
        gr["p"] = mm(gr["p"], pbd)
    yield
    for gr in groups:
        gr["dinv"] = gr["dinv"] + mm(gr["dinv"], blockdiag(gr["p"]))
    yield
    for gr in groups:
        gr["m"] = mm(gr["dinv"], gr["an"])
    yield
    for gr in groups:
        gr["m2"] = blockdiag(mm(gr["m"], blockdiag(gr["m"])))
    yield
    for gr in groups:
        x = eye - gr["m"]
        gr["x"] = x + mm(x, gr["m2"])
    yield
    for gr in groups:
        t = mm(gr["x"], blockdiag(gr["dinv"]))
        for i, pr in enumerate(gr["pairs"]):
            half = t[:, 2 * c * i:2 * c * (i + 1)]
            pr["tinv"] = jnp.concatenate([jnp.where(left, half, 0.0), jnp.where(left, 0.0, half)],
                                         axis=0)


def _gdn_scan(qkv, proj3, alog_rows, dtb_rows, lp):
    b = qkv.shape[0]
    nc = lp // GDN_CHUNK
    ba_col0 = 0

    def blk_of(seq):
        return jnp.where(seq < 2, nc - 2 + seq, seq - 2)

    def chunk_specs(d):
        prep = lambda t: jnp.minimum(t, nc - 1)
        seq = prep if d == 0 else (lambda t: nc - 1 - prep(t))
        return [
            pl.BlockSpec((nb, GDN_CHUNK, GDN_QK), lambda bi, t: (bi, blk_of(seq(t)), 0)),
            pl.BlockSpec((nb, GDN_CHUNK, GDN_QK), lambda bi, t: (bi, blk_of(seq(t)), 1)),
            pl.BlockSpec((nb, GDN_CHUNK, GDN_V), lambda bi, t: (bi, blk_of(seq(t)), 1)),
            pl.BlockSpec((nb, GDN_CHUNK, 128), lambda bi, t: (bi, blk_of(seq(t)), ba_col0 + d)),
        ]

    nb = SCAN_NB if b % SCAN_NB == 0 else 1
    const = pl.BlockSpec((2, 1, 128), lambda bi, t: (0, 0, 0))
    kern = functools.partial(_gdn_scan_kernel, nc=nc)
    applied = lambda t: jnp.maximum(t - 1, 0)
    nseq = 2 * nb
    nprob = nseq * (GDN_HEADS // 2)
    c2 = 2 * GDN_CHUNK
    return pl.pallas_call(
        kern,
        grid=(b // nb, nc + 1),
        in_specs=chunk_specs(0) + chunk_specs(1) + [const, const],
        out_specs=[
            pl.BlockSpec((nb, GDN_CHUNK, GDN_V), lambda bi, t: (bi, blk_of(applied(t)), 0)),
            pl.BlockSpec((nb, GDN_CHUNK, GDN_V), lambda bi, t: (bi, blk_of(nc - 1 - applied(t)), 0)),
        ],
        out_shape=[jax.ShapeDtypeStruct((b, lp, GDN_V), BF16)] * 2,
        scratch_shapes=[
            pltpu.VMEM((nseq, GDN_HEADS, GDN_DK, GDN_DV), F32),
            pltpu.VMEM((nseq, GDN_HEADS, GDN_DK, GDN_DV), BF16),
            pltpu.VMEM((nprob, c2, GDN_DV), F32),
            pltpu.VMEM((nprob, 2, c2, GDN_DK), BF16),
            pltpu.VMEM((nprob, c2, c2), BF16),
            pltpu.VMEM((nprob, c2, GDN_DK), BF16),
            pltpu.VMEM((nseq, 1, 128), F32),
        ],
        compiler_params=_cparams(("parallel", "arbitrary")),
        name="gdn_scan",
    )(qkv, qkv, qkv, proj3, qkv, qkv, qkv, proj3, alog_rows, dtb_rows)


OUT_SUB = 256


def _gdn_out_kernel(of_ref, ob_ref, h_ref, ln_ref, wz_ref, g_ref, w_ref, o_ref):
    nsub = o_ref.shape[0] // OUT_SUB
    rows = [slice(sb * OUT_SUB, (sb + 1) * OUT_SUB) for sb in range(nsub)]

    def gate_logits(sb):
        x = h_ref[rows[sb], :]
        ms = jnp.mean(x * x, axis=-1, keepdims=True)
        xn = (x * lax.rsqrt(ms + NORM_EPS) * ln_ref[...]).astype(BF16)
        return jnp.dot(xn, wz_ref[...], preferred_element_type=F32)

    z_next = gate_logits(0)
    for sb in range(nsub):
        z = z_next
        if sb + 1 < nsub:
            z_next = gate_logits(sb + 1)
        o = of_ref[rows[sb], :].astype(F32) + ob_ref[rows[sb], :].astype(F32)
        gate = z * jax.nn.sigmoid(z)
        ys = []
        for h in range(GDN_HEADS):
            lanes = slice(h * GDN_DV, (h + 1) * GDN_DV)
            oh = o[:, lanes]
            ms = jnp.mean(oh * oh, axis=-1, keepdims=True)
            ys.append((oh * lax.rsqrt(ms + NORM_EPS) * g_ref[...] * gate[:, lanes]).astype(BF16))
        y = jnp.concatenate(ys, axis=1)
        o_ref[rows[sb], :] = h_ref[rows[sb], :] + jnp.dot(y, w_ref[...], preferred_element_type=F32)


def _gdn_out(o_fwd, o_bwd, h0, ln_row, w_z, g_row, w_out):
    rows = h0.shape[0]
    tm = _tile(rows, 768, OUT_SUB)
    const = lambda i: (0, 0)
    return pl.pallas_call(
        _gdn_out_kernel,
        grid=(rows // tm,),
        in_specs=[
            pl.BlockSpec((tm, GDN_V), lambda i: (i, 0)),
            pl.BlockSpec((tm, GDN_V), lambda i: (i, 0)),
            pl.BlockSpec((tm, D_MODEL), lambda i: (i, 0)),
            pl.BlockSpec((1, D_MODEL), const),
            pl.BlockSpec((D_MODEL, GDN_V), const),
            pl.BlockSpec((1, GDN_DV), const),
            pl.BlockSpec((GDN_V, D_MODEL), const),
        ],
        out_specs=pl.BlockSpec((tm, D_MODEL), lambda i: (i, 0)),
        out_shape=jax.ShapeDtypeStruct((rows, D_MODEL), F32),
        compiler_params=_cparams(("parallel",)),
        name="gdn_out",
    )(o_fwd, o_bwd, h0, ln_row, w_z, g_row, w_out)


def _rope(r, cos, nsin_lo, sin_hi):
    return r * cos + pltpu.roll(r, 96, 1) * nsin_lo + pltpu.roll(r, 32, 1) * sin_hi


Q_GROUP = 2


def _mla_q_kernel(cq_ref, g1_ref, wt_ref, ga_ref, gr_ref, cos_ref, sin_ref, shift_ref, qt_ref):
    cq = cq_ref[0].astype(F32)
    tm = cq.shape[0]
    ms = jnp.mean(cq * cq, axis=-1, keepdims=True)
    cqt = (cq * lax.rsqrt(ms + NORM_EPS) * g1_ref[...]).T.astype(BF16)
    half = MLA_ROPE // 2
    tile = lambda g: jnp.concatenate([g] * (tm // 128), axis=1)
    ga = tile(ga_ref[...])
    gr1, gr2 = tile(gr_ref[0:half, :]), tile(gr_ref[half:MLA_ROPE, :])
    cos, sin = cos_ref[...], sin_ref[...]
    pad_rows = tile(shift_ref[...]).astype(BF16)

    def project(g):
        rows = slice(g * Q_GROUP * MLA_DQK, (g + 1) * Q_GROUP * MLA_DQK)
        return jnp.dot(wt_ref[rows, :], cqt, preferred_element_type=F32)

    q_next = project(0)
    for h in range(MLA_HEADS):
        if h % Q_GROUP == 0:
            q = q_next
            if h + Q_GROUP < MLA_HEADS:
                q_next = project(h // Q_GROUP + 1)
        r0 = (h % Q_GROUP) * MLA_DQK
        a = q[r0:r0 + MLA_NOPE]
        x1 = q[r0 + MLA_NOPE:r0 + MLA_NOPE + half]
        x2 = q[r0 + MLA_NOPE + half:r0 + MLA_DQK]
        ss = (jnp.sum(a * a, axis=0, keepdims=True) + jnp.sum(x1 * x1, axis=0, keepdims=True)
              + jnp.sum(x2 * x2, axis=0, keepdims=True))
        inv = lax.rsqrt(ss * (1.0 / MLA_DQK) + NORM_EPS)
        x1 = x1 * inv * gr1
        x2 = x2 * inv * gr2
        qt_ref[0, h, 0:MLA_NOPE, :] = (a * inv * ga).astype(BF16)
        qt_ref[0, h, MLA_NOPE:MLA_NOPE + half, :] = (x1 * cos - x2 * sin).astype(BF16)
        qt_ref[0, h, MLA_NOPE + half:MLA_DQK, :] = (x2 * cos + x1 * sin).astype(BF16)
        qt_ref[0, h, MLA_DQK:MLA_DPAD, :] = pad_rows


def _mla_q(proj3, g1, w_uq_t, ga, gr, cos_t, sin_t, shift_rows, lp):
    b = proj3.shape[0]
    tm = _tile(lp, 384, 128)
    cqcol = MLA_V // MLA_Q_LORA
    const = lambda bi, i: (0, 0)
    return pl.pallas_call(
        _mla_q_kernel,
        grid=(b, lp // tm),
        in_specs=[
            pl.BlockSpec((1, tm, MLA_Q_LORA), lambda bi, i: (bi, i, cqcol)),
            pl.BlockSpec((1, MLA_Q_LORA), const),
            pl.BlockSpec((MLA_HEADS * MLA_DQK, MLA_Q_LORA), const),
            pl.BlockSpec((MLA_NOPE, 128), const),
            pl.BlockSpec((MLA_ROPE, 128), const),
            pl.BlockSpec((MLA_ROPE // 2, tm), lambda bi, i: (0, i)),
            pl.BlockSpec((MLA_ROPE // 2, tm), lambda bi, i: (0, i)),
            pl.BlockSpec((MLA_DPAD - MLA_DQK, 128), const),
        ],
        out_specs=pl.BlockSpec((1, MLA_HEADS, MLA_DPAD, tm), lambda bi, i: (bi, 0, 0, i)),
        out_shape=jax.ShapeDtypeStruct((b, MLA_HEADS, MLA_DPAD, lp), BF16),
        compiler_params=_cparams(("parallel", "parallel")),
        name="mla_q",
    )(proj3, g1, w_uq_t, ga, gr, cos_t, sin_t, shift_rows)


def _mla_kv_kernel(h_ref, ln_ref, ws_ref, wb_ref, g1_ref, w_ref, ga_ref, gr_ref, cos_ref, nsin_ref, sin_ref,
                   zq_ref, k_ref, vt_ref):
    x = h_ref[0]
    xn = (x * lax.rsqrt(jnp.mean(x * x, axis=-1, keepdims=True) + NORM_EPS) * ln_ref[...]).astype(BF16)
    small = jnp.dot(xn, ws_ref[...], preferred_element_type=F32)
    ckv = small[:, :MLA_KV_LORA]
    ms = jnp.mean(ckv * ckv, axis=-1, keepdims=True)
    cn = (ckv * lax.rsqrt(ms + NORM_EPS) * g1_ref[...]).astype(BF16)
    kv = jnp.dot(cn, w_ref[...], preferred_element_type=F32)
    zq_ref[0] = jnp.dot(xn, wb_ref[...], preferred_element_type=F32).astype(BF16)
    kpe = small[:, MLA_KV_LORA:]
    sq_pe = kpe * kpe
    kr = _rope(kpe * gr_ref[...], cos_ref[...], nsin_ref[...], sin_ref[...])
    one_hot = jnp.where(lax.broadcasted_iota(jnp.int32, (1, 128), 1) == MLA_ROPE, 1.0, 0.0).astype(F32)
    for h in range(MLA_HEADS):
        kn = kv[:, h * MLA_NOPE:(h + 1) * MLA_NOPE]
        ss = jnp.sum(kn * kn + sq_pe, axis=-1, keepdims=True)
        inv = lax.rsqrt(ss * (1.0 / MLA_DQK) + NORM_EPS)
        k_ref[0, h, :, 0:MLA_NOPE] = (kn * inv * ga_ref[...]).astype(BF16)
        k_ref[0, h, :, MLA_NOPE:MLA_DPAD] = (kr * inv + one_hot).astype(BF16)
        v = kv[:, MLA_HEADS * MLA_NOPE + h * MLA_DV:MLA_HEADS * MLA_NOPE + (h + 1) * MLA_DV]
        vt_ref[0, h] = v.T.astype(BF16)


def _mla_proj_kv(h3, ln_row, w_small, w_big, g1, w_ukv, ga, gr, cos, nsin, sin, lp):
    b = h3.shape[0]
    tm = _tile(lp, 384, 128)
    const = lambda bi, i: (0, 0)
    nbig = w_big.shape[1]
    return pl.pallas_call(
        _mla_kv_kernel,
        grid=(b, lp // tm),
        in_specs=[
            pl.BlockSpec((1, tm, D_MODEL), lambda bi, i: (bi, i, 0)),
            pl.BlockSpec((1, D_MODEL), const),
            pl.BlockSpec((D_MODEL, MLA_KV_LORA + 128), const),
            pl.BlockSpec((D_MODEL, nbig), const),
            pl.BlockSpec((1, MLA_KV_LORA), const),
            pl.BlockSpec((MLA_KV_LORA, MLA_HEADS * (MLA_NOPE + MLA_DV)), const),
            pl.BlockSpec((1, 128), const),
            pl.BlockSpec((1, 128), const),
            pl.BlockSpec((tm, 128), lambda bi, i: (i, 0)),
            pl.BlockSpec((tm, 128), lambda bi, i: (i, 0)),
            pl.BlockSpec((tm, 128), lambda bi, i: (i, 0)),
        ],
        out_specs=[
            pl.BlockSpec((1, tm, nbig), lambda bi, i: (bi, i, 0)),
            pl.BlockSpec((1, MLA_HEADS, tm, MLA_DPAD), lambda bi, i: (bi, 0, i, 0)),
            pl.BlockSpec((1, MLA_HEADS, MLA_DV, tm), lambda bi, i: (bi, 0, 0, i)),
        ],
        out_shape=[
            jax.ShapeDtypeStruct((b, lp, nbig), BF16),
            jax.ShapeDtypeStruct((b, MLA_HEADS, lp, MLA_DPAD), BF16),
            jax.ShapeDtypeStruct((b, MLA_HEADS, MLA_DV, lp), BF16),
        ],
        compiler_params=_cparams(("parallel", "parallel")),
        name="mla_proj_kv",
    )(h3, ln_row, w_small, w_big, g1, w_ukv, ga, gr, cos, nsin, sin)


ATT_TK = 1024
ATT_TQ = 512
ATT_AHEAD = 2
ATT_BOUND_MAX = 60.0


def _attn_kernel(bounded_ref, qt_ref, k_ref, vt_ref, o_ref, *, s_len):
    tq = qt_ref.shape[3]
    nstrip = tq // ATT_TQ
    chunks = [(ck * ATT_TK, ATT_TK) for ck in range(s_len // ATT_TK)] + [(s_len, TAIL)]
    units = [(i, s) for i in range(len(chunks)) for s in range(nstrip)]
    is_meta = lax.broadcasted_iota(jnp.int32, (TAIL, 1), 0) >= TAIL - N_META
    tail_bias = jnp.where(is_meta, 0.0, -jnp.inf).astype(F32)

    def scores(u):
        i, s = units[u]
        k0, nk = chunks[i]
        st = jnp.dot(k_ref[0, 0, k0:k0 + nk, :], qt_ref[0, 0, :, s * ATT_TQ:(s + 1) * ATT_TQ],
                     preferred_element_type=F32)
        return st + tail_bias if i == len(chunks) - 1 else st

    def run(bounded):
        m = [jnp.full((1, ATT_TQ), -jnp.inf, F32) for _ in range(nstrip)]
        l = [jnp.zeros((1, ATT_TQ), F32) for _ in range(nstrip)]
        acc = [jnp.zeros((MLA_DV, ATT_TQ), F32) for _ in range(nstrip)]
        pending = [scores(u) for u in range(min(ATT_AHEAD, len(units)))]
        for u, (i, s) in enumerate(units):
            if u + ATT_AHEAD < len(units):
                pending.append(scores(u + ATT_AHEAD))
            st = pending.pop(0)
            k0, nk = chunks[i]
            if bounded:
                p = jnp.exp2(st)
                l[s] = l[s] + jnp.sum(p, axis=0, keepdims=True)
                acc[s] = acc[s] + jnp.dot(vt_ref[0, 0, :, k0:k0 + nk], p.astype(BF16),
                                          preferred_element_type=F32)
            else:
                m_new = jnp.maximum(m[s], jnp.max(st, axis=0, keepdims=True))
                alpha = jnp.exp2(m[s] - m_new)
                p = jnp.exp2(st - m_new)
                l[s] = alpha * l[s] + jnp.sum(p, axis=0, keepdims=True)
                acc[s] = alpha * acc[s] + jnp.dot(vt_ref[0, 0, :, k0:k0 + nk], p.astype(BF16),
                                                  preferred_element_type=F32)
                m[s] = m_new
        for s in range(nstrip):
            o_ref[0, s * ATT_TQ:(s + 1) * ATT_TQ, :] = (acc[s] / l[s]).T.astype(o_ref.dtype)

    @pl.when(bounded_ref[0] == 1)
    def _():
        run(True)

    @pl.when(bounded_ref[0] != 1)
    def _():
        run(False)


def _attention(bounded, qt, k, vt, s_len):
    b = qt.shape[0]
    lp = k.shape[2]
    assert s_len % ATT_TK == 0 and lp == s_len + TAIL
    tq = _tile(s_len, 1024, ATT_TQ)
    kern = functools.partial(_attn_kernel, s_len=s_len)
    return pl.pallas_call(
        kern,
        grid=(b, MLA_HEADS, s_len // tq),
        in_specs=[
            pl.BlockSpec(memory_space=pltpu.SMEM),
            pl.BlockSpec((1, 1, MLA_DPAD, tq), lambda bi, h, i: (bi, h, 0, i)),
            pl.BlockSpec((1, 1, lp, MLA_DPAD), lambda bi, h, i: (bi, h, 0, 0)),
            pl.BlockSpec((1, 1, MLA_DV, lp), lambda bi, h, i: (bi, h, 0, 0)),
        ],
        out_specs=pl.BlockSpec((1, tq, MLA_DV), lambda bi, h, i: (bi, i, h)),
        out_shape=jax.ShapeDtypeStruct((b, s_len, MLA_V), BF16),
        compiler_params=_cparams(("parallel", "parallel", "arbitrary")),
        name="mla_attention",
    )(bounded, qt, k, vt)


def _mla_out_kernel(o_ref, z_ref, h_ref, w_ref, y_ref):
    z = z_ref[0].astype(F32)
    y = (o_ref[0].astype(F32) * (z * jax.nn.sigmoid(z))).astype(BF16)
    y_ref[0] = h_ref[0] + jnp.dot(y, w_ref[...], preferred_element_type=F32)


def _mla_out(o, proj3, h3, w_out, b0, nb, s_len):
    tm = _tile(s_len, 1024, 128)
    return pl.pallas_call(
        _mla_out_kernel,
        grid=(nb, s_len // tm),
        in_specs=[
            pl.BlockSpec((1, tm, MLA_V), lambda bi, i: (bi + b0, i, 0)),
            pl.BlockSpec((1, tm, MLA_V), lambda bi, i: (bi + b0, i, 0)),
            pl.BlockSpec((1, tm, D_MODEL), lambda bi, i: (bi + b0, i, 0)),
            pl.BlockSpec((MLA_V, D_MODEL), lambda bi, i: (0, 0)),
        ],
        out_specs=pl.BlockSpec((1, tm, D_MODEL), lambda bi, i: (bi, i, 0)),
        out_shape=jax.ShapeDtypeStruct((nb, s_len, D_MODEL), F32),
        compiler_params=_cparams(("parallel", "parallel")),
        name="mla_out",
    )(o, proj3, h3, w_out)


def _pad_cols(w, n):
    return jnp.pad(w, ((0, 0), (0, n - w.shape[1])))


def _lane_row(v, n=128):
    return jnp.pad(v.astype(F32), (0, n - v.shape[0]))[None, :]


def _trunk_all(xs, meta_tokens, ln_g, gdn_w_in, gdn_conv_w, gdn_a_log, gdn_dt_bias, gdn_o_norm_g,
               gdn_w_out, mla_w_in, mla_q_norm_g, mla_kv_norm_g, mla_w_uq, mla_w_ukv, mla_qk_q_g,
               mla_qk_k_g, mla_w_out):
    s_len = xs[0].shape[1]
    assert all(x.shape[1] == s_len for x in xs) and s_len % 128 == 0
    lp = s_len + TAIL
    x_all = jnp.concatenate(xs, axis=0)
    b = x_all.shape[0]
    meta = jnp.broadcast_to(meta_tokens[None].astype(F32), (b, N_META, D_MODEL))
    h0 = jnp.concatenate([x_all, jnp.zeros((b, TAIL - N_META, D_MODEL), F32), meta], axis=1)
    h0 = h0.reshape(b * lp, D_MODEL)

    w_in = gdn_w_in[0]
    ba = w_in[:, GDN_CONV_CH + GDN_V:].reshape(D_MODEL, 2, 2, GDN_HEADS)
    ba_dir = [_pad_cols(jnp.concatenate([ba[:, 0, d], ba[:, 1, d]], axis=1), 128) for d in range(2)]
    conv_w8 = jnp.pad(gdn_conv_w[0], ((0, 8 - GDN_CONV), (0, 0)))
    qkv, gates = _gdn_qkv_conv(h0.reshape(b, lp, D_MODEL), ln_g[0][None, :],
                               w_in[:, :GDN_CONV_CH].astype(BF16),
                               jnp.concatenate(ba_dir, axis=1).astype(BF16), conv_w8, lp)

    lane_a = lambda v: jnp.pad(v.astype(F32), ((0, 0), (GDN_HEADS, 128 - 2 * GDN_HEADS)))[:, None, :]
    o_fwd, o_bwd = _gdn_scan(qkv, gates, lane_a(gdn_a_log[0]), lane_a(gdn_dt_bias[0]), lp)
    h1 = _gdn_out(o_fwd.reshape(b * lp, GDN_V), o_bwd.reshape(b * lp, GDN_V), h0, ln_g[0][None, :],
                  w_in[:, GDN_CONV_CH:GDN_CONV_CH + GDN_V].astype(BF16),
                  gdn_o_norm_g[0][None, :], gdn_w_out[0].astype(BF16))

    w_in1 = mla_w_in[0]
    o1 = MLA_Q_LORA
    o2_ = o1 + MLA_KV_LORA
    o3 = o2_ + MLA_ROPE
    w_small = jnp.concatenate([w_in1[:, o1:o2_], _pad_cols(w_in1[:, o2_:o3], 128)], axis=1).astype(BF16)
    w_big = jnp.concatenate([w_in1[:, o3:], w_in1[:, :o1]], axis=1).astype(BF16)
    h1_3 = h1.reshape(b, lp, D_MODEL)

    pos = jnp.concatenate([jnp.arange(s_len, dtype=F32) + N_META, jnp.zeros((TAIL - N_META,), F32),
                           jnp.arange(N_META, dtype=F32)])
    inv = ROPE_THETA ** (-jnp.arange(0, MLA_ROPE, 2, dtype=F32) / MLA_ROPE)
    ang = pos[:, None] * inv[None, :]
    zc = jnp.zeros_like(ang)
    cos_t = jnp.concatenate([jnp.cos(ang), jnp.cos(ang), zc, zc], axis=1)
    nsin_t = jnp.concatenate([-jnp.sin(ang), zc, zc, zc], axis=1)
    sin_t = jnp.concatenate([zc, jnp.sin(ang), zc, zc], axis=1)

    scale = MLA_DQK ** -0.5 * math.log2(math.e)
    gq = jnp.broadcast_to((mla_qk_q_g[0].astype(F32) * scale)[:, None], (MLA_DQK, 128))
    score_bound = (1.02 * MLA_DQK * scale * jnp.max(jnp.abs(mla_qk_q_g[0].astype(F32)))
                   * jnp.max(jnp.abs(mla_qk_k_g[0].astype(F32))))
    bounded = score_bound < ATT_BOUND_MAX
    shift_rows = jnp.zeros((MLA_DPAD - MLA_DQK, 128), F32).at[0, :].set(jnp.where(bounded, -score_bound, 0.0))
    w_ukv = mla_w_ukv[0].reshape(MLA_KV_LORA, MLA_HEADS, MLA_NOPE + MLA_DV)
    w_ukv = jnp.concatenate([w_ukv[:, :, :MLA_NOPE].reshape(MLA_KV_LORA, -1),
                             w_ukv[:, :, MLA_NOPE:].reshape(MLA_KV_LORA, -1)], axis=1).astype(BF16)
    gk = mla_qk_k_g[0].astype(F32)
    proj1_3, k, vt = _mla_proj_kv(h1_3, ln_g[1][None, :], w_small, w_big, mla_kv_norm_g[0][None, :], w_ukv,
                                  gk[None, :MLA_NOPE], _lane_row(gk[MLA_NOPE:]), cos_t, nsin_t, sin_t, lp)
    qt = _mla_q(proj1_3, mla_q_norm_g[0][None, :], mla_w_uq[0].T.astype(BF16), gq[:MLA_NOPE],
                gq[MLA_NOPE:], jnp.cos(ang).T, jnp.sin(ang).T, shift_rows, lp)

    o = _attention(bounded.astype(jnp.int32).reshape(1), qt, k, vt, s_len)

    w_out1 = mla_w_out[0].astype(BF16)
    outs = []
    b0 = 0
    for x in xs:
        outs.append(_mla_out(o, proj1_3, h1_3, w_out1, b0, x.shape[0], s_len))
        b0 += x.shape[0]
    return tuple(outs)


def kernel(x_prompt, x_sample, meta_tokens, ln_g, gdn_w_in, gdn_conv_w, gdn_a_log, gdn_dt_bias,
           gdn_o_norm_g, gdn_w_out, mla_w_in, mla_q_norm_g, mla_kv_norm_g, mla_w_uq, mla_w_ukv,
           mla_qk_q_g, mla_qk_k_g, mla_w_out):
    return _trunk_all((x_prompt, x_sample), meta_tokens, ln_g, gdn_w_in, gdn_conv_w, gdn_a_log,
                      gdn_dt_bias, gdn_o_norm_g, gdn_w_out, mla_w_in, mla_q_norm_g, mla_kv_norm_g,
                      mla_w_uq, mla_w_ukv, mla_qk_q_g, mla_qk_k_g, mla_w_out)
```

```python
import functools
import math

import jax
import jax.numpy as jnp
from jax import lax
from jax.experimental import pallas as pl
from jax.experimental.pallas import tpu as pltpu

F32 = jnp.float32
BF16 = jnp.bfloat16

D_MODEL = 1024
N_META = 16
TAIL = 128
NORM_EPS = 1e-6

GDN_HEADS = 8
GDN_DK = 128
GDN_DV = 256
GDN_CONV = 5
GDN_CHUNK = 64
GDN_QK = GDN_HEADS * GDN_DK
GDN_V = GDN_HEADS * GDN_DV
GDN_CONV_CH = 2 * GDN_QK + GDN_V

MLA_HEADS = 16
MLA_Q_LORA = 512
MLA_KV_LORA = 256
MLA_NOPE = 128
MLA_ROPE = 64
MLA_DQK = MLA_NOPE + MLA_ROPE
MLA_DV = 128
MLA_V = MLA_HEADS * MLA_DV
MLA_DPAD = 256
ROPE_THETA = 10000.0

VMEM_LIMIT = 56 * 1024 * 1024


def _cparams(sem):
    return pltpu.CompilerParams(dimension_semantics=sem, vmem_limit_bytes=VMEM_LIMIT)


def _tile(n, target, mult):
    best = None
    t = mult
    while t <= min(n, target):
        if n % t == 0:
            best = t
        t += mult
    assert best is not None, (n, target, mult)
    return best


CONV_SUB = 64
CONV_LANES = GDN_DK
CONV_TC = 1024


def _qkv_conv_kernel(prev_ref, main_ref, next_ref, g_ref, w_ref, wba_ref, cw_ref, o_ref, ba_ref, ext_ref,
                     *, tr):
    half = GDN_CONV // 2
    nblk = GDN_CONV_CH // CONV_TC
    x = jnp.concatenate([prev_ref[0], main_ref[0], next_ref[0]], axis=0)
    ms = jnp.mean(x * x, axis=-1, keepdims=True)
    xn = (x * lax.rsqrt(ms + NORM_EPS) * g_ref[...]).astype(BF16)
    ba_ref[0] = jnp.dot(xn, wba_ref[...], preferred_element_type=F32)[8:8 + tr]

    def project(j):
        ext_ref[j] = jnp.dot(xn, w_ref[:, j * CONV_TC:(j + 1) * CONV_TC], preferred_element_type=F32)

    def conv_silu(j, sb, lb):
        lanes = slice(j * CONV_TC + lb * CONV_LANES, j * CONV_TC + (lb + 1) * CONV_LANES)
        nrow = CONV_SUB + 16
        x2 = ext_ref[j, sb * CONV_SUB:sb * CONV_SUB + nrow, lb * CONV_LANES:(lb + 1) * CONV_LANES]
        acc = x2[8:8 + CONV_SUB] * cw_ref[half:half + 1, lanes]
        for s in range(1, half + 1):
            down = pltpu.roll(x2, s, 0)
            acc = acc + down[8:8 + CONV_SUB] * cw_ref[half - s:half - s + 1, lanes]
            up = pltpu.roll(x2, nrow - s, 0)
            acc = acc + up[8:8 + CONV_SUB] * cw_ref[half + s:half + s + 1, lanes]
        return acc * jax.nn.sigmoid(acc)

    def finish(j):
        for sb in range(tr // CONV_SUB):
            rows = slice(sb * CONV_SUB, (sb + 1) * CONV_SUB)
            for lb in range(CONV_TC // CONV_LANES):
                y = conv_silu(j, sb, lb)
                c0 = j * CONV_TC + lb * CONV_LANES
                lanes = slice(c0, c0 + CONV_LANES)
                if c0 < 2 * GDN_QK:
                    qscale = GDN_DK ** -0.5 if c0 < GDN_QK else 1.0
                    ss = jnp.sum(y * y, axis=-1, keepdims=True)
                    y = y * (lax.rsqrt(ss + NORM_EPS) * qscale)
                o_ref[0, rows, lanes] = y.astype(BF16)

    project(0)
    for j in range(nblk):
        if j + 1 < nblk:
            project(j + 1)
        finish(j)


def _gdn_qkv_conv(h3, g, w_qkv, w_ba, conv_w8, lp):
    b = h3.shape[0]
    nba = w_ba.shape[1]
    tr = _tile(lp, 704, CONV_SUB)
    nb8 = lp // 8
    tb = tr // 8
    kern = functools.partial(_qkv_conv_kernel, tr=tr)
    const = lambda bi, i: (0, 0)
    return pl.pallas_call(
        kern,
        grid=(b, lp // tr),
        in_specs=[
            pl.BlockSpec((1, 8, D_MODEL), lambda bi, i: (bi, (i * tb + nb8 - 1) % nb8, 0)),
            pl.BlockSpec((1, tr, D_MODEL), lambda bi, i: (bi, i, 0)),
            pl.BlockSpec((1, 8, D_MODEL), lambda bi, i: (bi, ((i + 1) * tb) % nb8, 0)),
            pl.BlockSpec((1, D_MODEL), const),
            pl.BlockSpec((D_MODEL, GDN_CONV_CH), const),
            pl.BlockSpec((D_MODEL, nba), const),
            pl.BlockSpec((8, GDN_CONV_CH), const),
        ],
        out_specs=[
            pl.BlockSpec((1, tr, GDN_CONV_CH), lambda bi, i: (bi, i, 0)),
            pl.BlockSpec((1, tr, nba), lambda bi, i: (bi, i, 0)),
        ],
        out_shape=[
            jax.ShapeDtypeStruct((b, lp, GDN_CONV_CH), BF16),
            jax.ShapeDtypeStruct((b, lp, nba), F32),
        ],
        scratch_shapes=[pltpu.VMEM((GDN_CONV_CH // CONV_TC, tr + 16, CONV_TC), F32)],
        compiler_params=_cparams(("parallel", "parallel")),
        name="gdn_qkv_conv",
    )(h3, h3, h3, g, w_qkv, w_ba, conv_w8)


SCAN_NB = 4


def _split3(x):
    hi = x.astype(BF16)
    r1 = x - hi.astype(F32)
    mid = r1.astype(BF16)
    lo = (r1 - mid.astype(F32)).astype(BF16)
    return hi, mid, lo


def _gdn_scan_kernel(qf_ref, kf_ref, vf_ref, baf_ref, qb_ref, kb_ref, vb_ref, bab_ref, alog_ref, dtb_ref,
                     of_ref, ob_ref, s_ref, sb_ref, u_ref, qw_ref, attn_ref, kd_ref, eg_ref, *, nc):
    t = pl.program_id(1)
    c = GDN_CHUNK
    c2 = 2 * c
    npair = GDN_HEADS // 2
    o_refs = (of_ref, ob_ref)

    @pl.when(t == 0)
    def _():
        for ref in (s_ref, sb_ref, u_ref, qw_ref, attn_ref, kd_ref, eg_ref):
            ref[...] = jnp.zeros_like(ref)

    nb = of_ref.shape[0]
    nprob = nb * 2 * npair
    applied = [dict(sq=pi // npair, heads=(2 * (pi % npair), 2 * (pi % npair) + 1),
                    qs=[None, None], vnew=[None, None]) for pi in range(nprob)]

    def read_state(pi, hh):
        ap = applied[pi]
        rs = slice(hh * c, (hh + 1) * c)
        qws = jnp.dot(qw_ref[pi, hh], sb_ref[ap["sq"], ap["heads"][hh]], preferred_element_type=F32)
        ap["qs"][hh] = qws[:c]
        ap["vnew"][hh] = (u_ref[pi, rs, :] - qws[c:]).astype(BF16)

    def write_out(pi):
        ap = applied[pi]
        h0, h1 = ap["heads"]
        bi, d = divmod(ap["sq"], 2)
        vnew2 = jnp.concatenate(ap["vnew"], axis=0)
        o2 = jnp.concatenate(ap["qs"], axis=0) + jnp.dot(attn_ref[pi], vnew2, preferred_element_type=F32)
        o_refs[d][bi, :, h0 * GDN_DV:(h0 + 1) * GDN_DV] = o2[:c].astype(BF16)
        o_refs[d][bi, :, h1 * GDN_DV:(h1 + 1) * GDN_DV] = o2[c:].astype(BF16)

    def update_state(pi, hh):
        ap = applied[pi]
        sq, h = ap["sq"], ap["heads"][hh]
        rs = slice(hh * c, (hh + 1) * c)
        a_h = GDN_HEADS + h
        upd = lax.dot_general(kd_ref[pi, rs, :], ap["vnew"][hh], (((0,), (0,)), ((), ())),
                              preferred_element_type=F32)
        s_new = s_ref[sq, h] * eg_ref[sq, :, a_h:a_h + 1] + upd
        s_ref[sq, h] = s_new
        sb_ref[sq, h] = s_new.astype(BF16)

    pairs_hh = [(pi, hh) for pi in range(nprob) for hh in range(2)]
    apply_ops = ([functools.partial(read_state, pi, hh) for pi, hh in pairs_hh]
                 + [functools.partial(write_out, pi) for pi in range(nprob)]
                 + [functools.partial(update_state, pi, hh) for pi, hh in pairs_hh])

    def emit_apply(n):
        for _ in range(min(n, len(apply_ops))):
            apply_ops.pop(0)()

    emit_apply(len(pairs_hh) // 2)
    tp = jnp.minimum(t, nc - 1)
    ri = lax.broadcasted_iota(jnp.int32, (c2, c2), 0)
    ci = lax.broadcasted_iota(jnp.int32, (c2, c2), 1)
    same = (ri >> 6) == (ci >> 6)
    offdiag = ri != ci
    top =lax.broadcasted_iota(jnp.int32, (c2, 1), 0) < c
    left = lax.broadcasted_iota(jnp.int32, (1, c2), 1) < c
    row_id = lax.broadcasted_iota(jnp.int32, (c, 128), 0)

    in_refs = ((qf_ref, kf_ref, vf_ref, baf_ref), (qb_ref, kb_ref, vb_ref, bab_ref))
    seqs = []
    for sq in range(2 * nb):
        bi, d = divmod(sq, 2)
        q_ref, k_ref, v_ref, ba_ref = in_refs[d]
        seq = tp if d == 0 else nc - 1 - tp
        blk = jnp.where(seq < 2, nc - 2 + seq, seq - 2)
        first_valid = jnp.where(blk == nc - 2, c, jnp.where(blk == nc - 1, c - N_META, 0))
        valid = row_id >= first_valid
        ba = ba_ref[bi]
        beta = jnp.where(valid, jax.nn.sigmoid(ba), 0.0)
        xs = ba + dtb_ref[d]
        softplus = jnp.maximum(xs, 0.0) + jnp.log(1.0 + jnp.exp(-jnp.abs(xs)))
        g = jnp.where(valid, -jnp.exp(alog_ref[d]) * softplus, 0.0)
        incl = same & ((ri >= ci) if d == 0 else (ri <= ci))
        tri = jnp.where(incl, 1.0, 0.0).astype(BF16)
        ghi, gmid, glo = _split3(jnp.concatenate([g, g], axis=0))
        gc2 = (jnp.dot(tri, ghi, preferred_element_type=F32)
               + jnp.dot(tri, gmid, preferred_element_type=F32)
               + jnp.dot(tri, glo, preferred_element_type=F32))
        seqs.append(dict(q=q_ref.at[bi], k=k_ref.at[bi], v=v_ref.at[bi], incl=incl, gc2=gc2, gc2t=gc2.T,
                         beta2=jnp.concatenate([beta, beta], axis=0),
                         gtot=jnp.sum(g, axis=0, keepdims=True)))

    def stacked(ref, h0, h1, width):
        return jnp.concatenate([ref[:, h0 * width:(h0 + 1) * width],
                                ref[:, h1 * width:(h1 + 1) * width]], axis=0)

    probs = []
    for dd in seqs:
        for p in range(npair):
            h0, h1 = 2 * p, 2 * p + 1
            a0, a1 = GDN_HEADS + h0, GDN_HEADS + h1
            col = jnp.where(top, dd["gc2"][:, a0:a0 + 1], dd["gc2"][:, a1:a1 + 1])
            row = jnp.where(left, dd["gc2t"][a0:a0 + 1, :], dd["gc2t"][a1:a1 + 1, :])
            bcol = jnp.where(top, dd["beta2"][:, h0:h0 + 1], dd["beta2"][:, h1:h1 + 1])
            tot = jnp.where(top, dd["gtot"][:, a0:a0 + 1], dd["gtot"][:, a1:a1 + 1])
            dec = jnp.exp(jnp.where(dd["incl"], col - row, -jnp.inf))
            kst = stacked(dd["k"], h0, h1, GDN_DK)
            qst = stacked(dd["q"], h0, h1, GDN_DK)
            vst = stacked(dd["v"], h0, h1, GDN_DV)
            kf = kst.astype(F32)
            kb = kf * bcol
            sc = lax.dot_general(jnp.concatenate([qst, kb.astype(BF16)], axis=0), kst,
                                 (((1,), (1,)), ((), ())), preferred_element_type=F32)
            egc = jnp.exp(col)
            probs.append(dict(
                attn=(sc[:c2] * dec).astype(BF16),
                a=jnp.where(offdiag, sc[c2:] * dec, 0.0),
                rhs=jnp.concatenate([vst.astype(F32) * bcol, kb * egc], axis=1).astype(BF16),
                qg=(qst.astype(F32) * egc).astype(BF16),
                kd=(kf * jnp.exp(tot - col)).astype(BF16)))

    emit_apply(len(pairs_hh) // 2)

    for _ in _unit_lower_inverse_staged(probs):
        emit_apply(3 * nb)
    emit_apply(len(apply_ops))

    for pi, pr in enumerate(probs):
        uw = jnp.dot(pr["tinv"].astype(BF16), pr["rhs"], preferred_element_type=F32)
        u_ref[pi] = uw[:, :GDN_DV]
        w = uw[:, GDN_DV:].astype(BF16)
        for hh in range(2):
            rs = slice(hh * c, (hh + 1) * c)
            qw_ref[pi, hh] = jnp.concatenate([pr["qg"][rs], w[rs]], axis=0)
        attn_ref[pi] = pr["attn"]
        kd_ref[pi] = pr["kd"]
    for sq, dd in enumerate(seqs):
        eg_ref[sq] = jnp.exp(dd["gtot"])


def _unit_lower_inverse_staged(probs):
    c = GDN_CHUNK
    nside = 4
    ri = lax.broadcasted_iota(jnp.int32, (c, nside * c), 0)
    ci = lax.broadcasted_iota(jnp.int32, (c, nside * c), 1)
    lane_blk = ci >> 6
    within = ci & (c - 1)
    diag16 = (ri >> 4) == (within >> 4)
    eye = jnp.where(ri == within, 1.0, 0.0).astype(F32)
    left = lax.broadcasted_iota(jnp.int32, (1, 2 * c), 1) < c

    def blockdiag(y):
        return jnp.concatenate([jnp.where(lane_blk == r, y, 0.0) for r in range(nside)],
                               axis=0).astype(BF16)

    def mm(x, ybd):
        return jnp.dot(x.astype(BF16), ybd, preferred_element_type=F32)

    groups = []
    for g0 in range(0, len(probs), 2):
        pa, pb = probs[g0], probs[g0 + 1]
        a = jnp.concatenate([pa["a"][:c] + pa["a"][c:], pb["a"][:c] + pb["a"][c:]], axis=1)
        ad = jnp.where(diag16, a, 0.0)
        groups.append(dict(pairs=(pa, pb), ad=ad, an=blockdiag(a - ad), dinv=eye - ad))
    for gr in groups:
        gr["p"] = mm(gr["ad"], blockdiag(gr["ad"]))
    yield
    for gr in groups:
        pbd = blockdiag(gr["p"])
        gr["dinv"] = gr["dinv"] + mm(gr["dinv"], pbd)
        gr["p"] = mm(gr["p"], pbd)
    yield
    for gr in groups:
        pbd = blockdiag(gr["p"])
        gr["dinv"] = gr["dinv"] + mm(gr["dinv"], pbd)
        gr["p"] = mm(gr["p"], pbd)
    yield
    for gr in groups:
        gr["dinv"] = gr["dinv"] + mm(gr["dinv"], blockdiag(gr["p"]))
    yield
    for gr in groups:
        gr["m"] = mm(gr["dinv"], gr["an"])
    yield
    for gr in groups:
        gr["m2"] = blockdiag(mm(gr["m"], blockdiag(gr["m"])))
    yield
    for gr in groups:
        x = eye - gr["m"]
        gr["x"] = x + mm(x, gr["m2"])
    yield
    for gr in groups:
        t = mm(gr["x"], blockdiag(gr["dinv"]))
        for i, pr in enumerate(gr["pairs"]):
            half = t[:, 2 * c * i:2 * c * (i + 1)]
            pr["tinv"] = jnp.concatenate([jnp.where(left, half, 0.0), jnp.where(left, 0.0, half)],
                                         axis=0)


def _gdn_scan(qkv, proj3, alog_rows, dtb_rows, lp):
    b = qkv.shape[0]
    nc = lp // GDN_CHUNK
    ba_col0 = 0

    def blk_of(seq):
        return jnp.where(seq < 2, nc - 2 + seq, seq - 2)

    def chunk_specs(d):
        prep = lambda t: jnp.minimum(t, nc - 1)
        seq = prep if d == 0 else (lambda t: nc - 1 - prep(t))
        return [
            pl.BlockSpec((nb, GDN_CHUNK, GDN_QK), lambda bi, t: (bi, blk_of(seq(t)), 0)),
            pl.BlockSpec((nb, GDN_CHUNK, GDN_QK), lambda bi, t: (bi, blk_of(seq(t)), 1)),
            pl.BlockSpec((nb, GDN_CHUNK, GDN_V), lambda bi, t: (bi, blk_of(seq(t)), 1)),
            pl.BlockSpec((nb, GDN_CHUNK, 128), lambda bi, t: (bi, blk_of(seq(t)), ba_col0 + d)),
        ]

    nb = SCAN_NB if b % SCAN_NB == 0 else 1
    const = pl.BlockSpec((2, 1, 128), lambda bi, t: (0, 0, 0))
    kern = functools.partial(_gdn_scan_kernel, nc=nc)
    applied = lambda t: jnp.maximum(t - 1, 0)
    nseq = 2 * nb
    nprob = nseq * (GDN_HEADS // 2)
    c2 = 2 * GDN_CHUNK
    return pl.pallas_call(
        kern,
        grid=(b // nb, nc + 1),
        in_specs=chunk_specs(0) + chunk_specs(1) + [const, const],
        out_specs=[
            pl.BlockSpec((nb, GDN_CHUNK, GDN_V), lambda bi, t: (bi, blk_of(applied(t)), 0)),
            pl.BlockSpec((nb, GDN_CHUNK, GDN_V), lambda bi, t: (bi, blk_of(nc - 1 - applied(t)), 0)),
        ],
        out_shape=[jax.ShapeDtypeStruct((b, lp, GDN_V), BF16)] * 2,
        scratch_shapes=[
            pltpu.VMEM((nseq, GDN_HEADS, GDN_DK, GDN_DV), F32),
            pltpu.VMEM((nseq, GDN_HEADS, GDN_DK, GDN_DV), BF16),
            pltpu.VMEM((nprob, c2, GDN_DV), F32),
            pltpu.VMEM((nprob, 2, c2, GDN_DK), BF16),
            pltpu.VMEM((nprob, c2, c2), BF16),
            pltpu.VMEM((nprob, c2, GDN_DK), BF16),
            pltpu.VMEM((nseq, 1, 128), F32),
        ],
        compiler_params=_cparams(("parallel", "arbitrary")),
        name="gdn_scan",
    )(qkv, qkv, qkv, proj3, qkv, qkv, qkv, proj3, alog_rows, dtb_rows)


OUT_SUB = 256


def _gdn_out_kernel(of_ref, ob_ref, h_ref, ln_ref, wz_ref, g_ref, w_ref, o_ref):
    nsub = o_ref.shape[0] // OUT_SUB
    rows = [slice(sb * OUT_SUB, (sb + 1) * OUT_SUB) for sb in range(nsub)]

    def gate_logits(sb):
        x = h_ref[rows[sb], :]
        ms = jnp.mean(x * x, axis=-1, keepdims=True)
        xn = (x * lax.rsqrt(ms + NORM_EPS) * ln_ref[...]).astype(BF16)
        return jnp.dot(xn, wz_ref[...], preferred_element_type=F32)

    z_next = gate_logits(0)
    for sb in range(nsub):
        z = z_next
        if sb + 1 < nsub:
            z_next = gate_logits(sb + 1)
        o = of_ref[rows[sb], :].astype(F32) + ob_ref[rows[sb], :].astype(F32)
        gate = z * jax.nn.sigmoid(z)
        ys = []
        for h in range(GDN_HEADS):
            lanes = slice(h * GDN_DV, (h + 1) * GDN_DV)
            oh = o[:, lanes]
            ms = jnp.mean(oh * oh, axis=-1, keepdims=True)
            ys.append((oh * lax.rsqrt(ms + NORM_EPS) * g_ref[...] * gate[:, lanes]).astype(BF16))
        y = jnp.concatenate(ys, axis=1)
        o_ref[rows[sb], :] = h_ref[rows[sb], :] + jnp.dot(y, w_ref[...], preferred_element_type=F32)


def _gdn_out(o_fwd, o_bwd, h0, ln_row, w_z, g_row, w_out):
    rows = h0.shape[0]
    tm = _tile(rows, 768, OUT_SUB)
    const = lambda i: (0, 0)
    return pl.pallas_call(
        _gdn_out_kernel,
        grid=(rows // tm,),
        in_specs=[
            pl.BlockSpec((tm, GDN_V), lambda i: (i, 0)),
            pl.BlockSpec((tm, GDN_V), lambda i: (i, 0)),
            pl.BlockSpec((tm, D_MODEL), lambda i: (i, 0)),
            pl.BlockSpec((1, D_MODEL), const),
            pl.BlockSpec((D_MODEL, GDN_V), const),
            pl.BlockSpec((1, GDN_DV), const),
            pl.BlockSpec((GDN_V, D_MODEL), const),
        ],
        out_specs=pl.BlockSpec((tm, D_MODEL), lambda i: (i, 0)),
        out_shape=jax.ShapeDtypeStruct((rows, D_MODEL), F32),
        compiler_params=_cparams(("parallel",)),
        name="gdn_out",
    )(o_fwd, o_bwd, h0, ln_row, w_z, g_row, w_out)


def _rope(r, cos, nsin_lo, sin_hi):
    return r * cos + pltpu.roll(r, 96, 1) * nsin_lo + pltpu.roll(r, 32, 1) * sin_hi


Q_GROUP = 2


def _mla_q_kernel(cq_ref, g1_ref, wt_ref, ga_ref, gr_ref, cos_ref, sin_ref, shift_ref, qt_ref):
    cq = cq_ref[0].astype(F32)
    tm = cq.shape[0]
    ms = jnp.mean(cq * cq, axis=-1, keepdims=True)
    cqt = (cq * lax.rsqrt(ms + NORM_EPS) * g1_ref[...]).T.astype(BF16)
    half = MLA_ROPE // 2
    tile = lambda g: jnp.concatenate([g] * (tm // 128), axis=1)
    ga = tile(ga_ref[...])
    gr1, gr2 = tile(gr_ref[0:half, :]), tile(gr_ref[half:MLA_ROPE, :])
    cos, sin = cos_ref[...], sin_ref[...]
    pad_rows = tile(shift_ref[...]).astype(BF16)

    def project(g):
        rows = slice(g * Q_GROUP * MLA_DQK, (g + 1) * Q_GROUP * MLA_DQK)
        return jnp.dot(wt_ref[rows, :], cqt, preferred_element_type=F32)

    q_next = project(0)
    for h in range(MLA_HEADS):
        if h % Q_GROUP == 0:
            q = q_next
            if h + Q_GROUP < MLA_HEADS:
                q_next = project(h // Q_GROUP + 1)
        r0 = (h % Q_GROUP) * MLA_DQK
        a = q[r0:r0 + MLA_NOPE]
        x1 = q[r0 + MLA_NOPE:r0 + MLA_NOPE + half]
        x2 = q[r0 + MLA_NOPE + half:r0 + MLA_DQK]
        ss = (jnp.sum(a * a, axis=0, keepdims=True) + jnp.sum(x1 * x1, axis=0, keepdims=True)
              + jnp.sum(x2 * x2, axis=0, keepdims=True))
        inv = lax.rsqrt(ss * (1.0 / MLA_DQK) + NORM_EPS)
        x1 = x1 * inv * gr1
        x2 = x2 * inv * gr2
        qt_ref[0, h, 0:MLA_NOPE, :] = (a * inv * ga).astype(BF16)
        qt_ref[0, h, MLA_NOPE:MLA_NOPE + half, :] = (x1 * cos - x2 * sin).astype(BF16)
        qt_ref[0, h, MLA_NOPE + half:MLA_DQK, :] = (x2 * cos + x1 * sin).astype(BF16)
        qt_ref[0, h, MLA_DQK:MLA_DPAD, :] = pad_rows


def _mla_q(proj3, g1, w_uq_t, ga, gr, cos_t, sin_t, shift_rows, lp):
    b = proj3.shape[0]
    tm = _tile(lp, 384, 128)
    cqcol = MLA_V // MLA_Q_LORA
    const = lambda bi, i: (0, 0)
    return pl.pallas_call(
        _mla_q_kernel,
        grid=(b, lp // tm),
        in_specs=[
            pl.BlockSpec((1, tm, MLA_Q_LORA), lambda bi, i: (bi, i, cqcol)),
            pl.BlockSpec((1, MLA_Q_LORA), const),
            pl.BlockSpec((MLA_HEADS * MLA_DQK, MLA_Q_LORA), const),
            pl.BlockSpec((MLA_NOPE, 128), const),
            pl.BlockSpec((MLA_ROPE, 128), const),
            pl.BlockSpec((MLA_ROPE // 2, tm), lambda bi, i: (0, i)),
            pl.BlockSpec((MLA_ROPE // 2, tm), lambda bi, i: (0, i)),
            pl.BlockSpec((MLA_DPAD - MLA_DQK, 128), const),
        ],
        out_specs=pl.BlockSpec((1, MLA_HEADS, MLA_DPAD, tm), lambda bi, i: (bi, 0, 0, i)),
        out_shape=jax.ShapeDtypeStruct((b, MLA_HEADS, MLA_DPAD, lp), BF16),
        compiler_params=_cparams(("parallel", "parallel")),
        name="mla_q",
    )(proj3, g1, w_uq_t, ga, gr, cos_t, sin_t, shift_rows)


def _mla_kv_kernel(h_ref, ln_ref, ws_ref, wb_ref, g1_ref, w_ref, ga_ref, gr_ref, cos_ref, nsin_ref, sin_ref,
                   zq_ref, k_ref, vt_ref):
    x = h_ref[0]
    xn = (x * lax.rsqrt(jnp.mean(x * x, axis=-1, keepdims=True) + NORM_EPS) * ln_ref[...]).astype(BF16)
    small = jnp.dot(xn, ws_ref[...], preferred_element_type=F32)
    ckv = small[:, :MLA_KV_LORA]
    ms = jnp.mean(ckv * ckv, axis=-1, keepdims=True)
    cn = (ckv * lax.rsqrt(ms + NORM_EPS) * g1_ref[...]).astype(BF16)
    kv = jnp.dot(cn, w_ref[...], preferred_element_type=F32)
    zq_ref[0] = jnp.dot(xn, wb_ref[...], preferred_element_type=F32).astype(BF16)
    kpe = small[:, MLA_KV_LORA:]
    sq_pe = kpe * kpe
    kr = _rope(kpe * gr_ref[...], cos_ref[...], nsin_ref[...], sin_ref[...])
    one_hot = jnp.where(lax.broadcasted_iota(jnp.int32, (1, 128), 1) == MLA_ROPE, 1.0, 0.0).astype(F32)
    for h in range(MLA_HEADS):
        kn = kv[:, h * MLA_NOPE:(h + 1) * MLA_NOPE]
        ss = jnp.sum(kn * kn + sq_pe, axis=-1, keepdims=True)
        inv = lax.rsqrt(ss * (1.0 / MLA_DQK) + NORM_EPS)
        k_ref[0, h, :, 0:MLA_NOPE] = (kn * inv * ga_ref[...]).astype(BF16)
        k_ref[0, h, :, MLA_NOPE:MLA_DPAD] = (kr * inv + one_hot).astype(BF16)
        v = kv[:, MLA_HEADS * MLA_NOPE + h * MLA_DV:MLA_HEADS * MLA_NOPE + (h + 1) * MLA_DV]
        vt_ref[0, h] = v.T.astype(BF16)


def _mla_proj_kv(h3, ln_row, w_small, w_big, g1, w_ukv, ga, gr, cos, nsin, sin, lp):
    b = h3.shape[0]
    tm = _tile(lp, 384, 128)
    const = lambda bi, i: (0, 0)
    nbig = w_big.shape[1]
    return pl.pallas_call(
        _mla_kv_kernel,
        grid=(b, lp // tm),
        in_specs=[
            pl.BlockSpec((1, tm, D_MODEL), lambda bi, i: (bi, i, 0)),
            pl.BlockSpec((1, D_MODEL), const),
            pl.BlockSpec((D_MODEL, MLA_KV_LORA + 128), const),
            pl.BlockSpec((D_MODEL, nbig), const),
            pl.BlockSpec((1, MLA_KV_LORA), const),
            pl.BlockSpec((MLA_KV_LORA, MLA_HEADS * (MLA_NOPE + MLA_DV)), const),
            pl.BlockSpec((1, 128), const),
            pl.BlockSpec((1, 128), const),
            pl.BlockSpec((tm, 128), lambda bi, i: (i, 0)),
            pl.BlockSpec((tm, 128), lambda bi, i: (i, 0)),
            pl.BlockSpec((tm, 128), lambda bi, i: (i, 0)),
        ],
        out_specs=[
            pl.BlockSpec((1, tm, nbig), lambda bi, i: (bi, i, 0)),
            pl.BlockSpec((1, MLA_HEADS, tm, MLA_DPAD), lambda bi, i: (bi, 0, i, 0)),
            pl.BlockSpec((1, MLA_HEADS, MLA_DV, tm), lambda bi, i: (bi, 0, 0, i)),
        ],
        out_shape=[
            jax.ShapeDtypeStruct((b, lp, nbig), BF16),
            jax.ShapeDtypeStruct((b, MLA_HEADS, lp, MLA_DPAD), BF16),
            jax.ShapeDtypeStruct((b, MLA_HEADS, MLA_DV, lp), BF16),
        ],
        compiler_params=_cparams(("parallel", "parallel")),
        name="mla_proj_kv",
    )(h3, ln_row, w_small, w_big, g1, w_ukv, ga, gr, cos, nsin, sin)


ATT_TK = 1024
ATT_TQ = 512
ATT_AHEAD = 2
ATT_BOUND_MAX = 60.0


def _attn_kernel(bounded_ref, qt_ref, k_ref, vt_ref, o_ref, *, s_len):
    tq = qt_ref.shape[3]
    nstrip = tq // ATT_TQ
    chunks = [(ck * ATT_TK, ATT_TK) for ck in range(s_len // ATT_TK)] + [(s_len, TAIL)]
    units = [(i, s) for i in range(len(chunks)) for s in range(nstrip)]
    is_meta = lax.broadcasted_iota(jnp.int32, (TAIL, 1), 0) >= TAIL - N_META
    tail_bias = jnp.where(is_meta, 0.0, -jnp.inf).astype(F32)

    def scores(u):
        i, s = units[u]
        k0, nk = chunks[i]
        st = jnp.dot(k_ref[0, 0, k0:k0 + nk, :], qt_ref[0, 0, :, s * ATT_TQ:(s + 1) * ATT_TQ],
                     preferred_element_type=F32)
        return st + tail_bias if i == len(chunks) - 1 else st

    def run(bounded):
        m = [jnp.full((1, ATT_TQ), -jnp.inf, F32) for _ in range(nstrip)]
        l = [jnp.zeros((1, ATT_TQ), F32) for _ in range(nstrip)]
        acc = [jnp.zeros((MLA_DV, ATT_TQ), F32) for _ in range(nstrip)]
        pending = [scores(u) for u in range(min(ATT_AHEAD, len(units)))]
        for u, (i, s) in enumerate(units):
            if u + ATT_AHEAD < len(units):
                pending.append(scores(u + ATT_AHEAD))
            st = pending.pop(0)
            k0, nk = chunks[i]
            if bounded:
                p = jnp.exp2(st)
                l[s] = l[s] + jnp.sum(p, axis=0, keepdims=True)
                acc[s] = acc[s] + jnp.dot(vt_ref[0, 0, :, k0:k0 + nk], p.astype(BF16),
                                          preferred_element_type=F32)
            else:
                m_new = jnp.maximum(m[s], jnp.max(st, axis=0, keepdims=True))
                alpha = jnp.exp2(m[s] - m_new)
                p = jnp.exp2(st - m_new)
                l[s] = alpha * l[s] + jnp.sum(p, axis=0, keepdims=True)
                acc[s] = alpha * acc[s] + jnp.dot(vt_ref[0, 0, :, k0:k0 + nk], p.astype(BF16),
                                                  preferred_element_type=F32)
                m[s] = m_new
        for s in range(nstrip):
            o_ref[0, s * ATT_TQ:(s + 1) * ATT_TQ, :] = (acc[s] / l[s]).T.astype(o_ref.dtype)

    @pl.when(bounded_ref[0] == 1)
    def _():
        run(True)

    @pl.when(bounded_ref[0] != 1)
    def _():
        run(False)


def _attention(bounded, qt, k, vt, s_len):
    b = qt.shape[0]
    lp = k.shape[2]
    assert s_len % ATT_TK == 0 and lp == s_len + TAIL
    tq = _tile(s_len, 2048, ATT_TQ)
    kern = functools.partial(_attn_kernel, s_len=s_len)
    return pl.pallas_call(
        kern,
        grid=(b, MLA_HEADS, s_len // tq),
        in_specs=[
            pl.BlockSpec(memory_space=pltpu.SMEM),
            pl.BlockSpec((1, 1, MLA_DPAD, tq), lambda bi, h, i: (bi, h, 0, i)),
            pl.BlockSpec((1, 1, lp, MLA_DPAD), lambda bi, h, i: (bi, h, 0, 0)),
            pl.BlockSpec((1, 1, MLA_DV, lp), lambda bi, h, i: (bi, h, 0, 0)),
        ],
        out_specs=pl.BlockSpec((1, tq, MLA_DV), lambda bi, h, i: (bi, i, h)),
        out_shape=jax.ShapeDtypeStruct((b, s_len, MLA_V), BF16),
        compiler_params=_cparams(("parallel", "parallel", "arbitrary")),
        name="mla_attention",
    )(bounded, qt, k, vt)


def _mla_out_kernel(o_ref, z_ref, h_ref, w_ref, y_ref):
    z = z_ref[0].astype(F32)
    y = (o_ref[0].astype(F32) * (z * jax.nn.sigmoid(z))).astype(BF16)
    y_ref[0] = h_ref[0] + jnp.dot(y, w_ref[...], preferred_element_type=F32)


def _mla_out(o, proj3, h3, w_out, b0, nb, s_len):
    tm = _tile(s_len, 1024, 128)
    return pl.pallas_call(
        _mla_out_kernel,
        grid=(nb, s_len // tm),
        in_specs=[
            pl.BlockSpec((1, tm, MLA_V), lambda bi, i: (bi + b0, i, 0)),
            pl.BlockSpec((1, tm, MLA_V), lambda bi, i: (bi + b0, i, 0)),
            pl.BlockSpec((1, tm, D_MODEL), lambda bi, i: (bi + b0, i, 0)),
            pl.BlockSpec((MLA_V, D_MODEL), lambda bi, i: (0, 0)),
        ],
        out_specs=pl.BlockSpec((1, tm, D_MODEL), lambda bi, i: (bi, i, 0)),
        out_shape=jax.ShapeDtypeStruct((nb, s_len, D_MODEL), F32),
        compiler_params=_cparams(("parallel", "parallel")),
        name="mla_out",
    )(o, proj3, h3, w_out)


def _pad_cols(w, n):
    return jnp.pad(w, ((0, 0), (0, n - w.shape[1])))


def _lane_row(v, n=128):
    return jnp.pad(v.astype(F32), (0, n - v.shape[0]))[None, :]


def _trunk_all(xs, meta_tokens, ln_g, gdn_w_in, gdn_conv_w, gdn_a_log, gdn_dt_bias, gdn_o_norm_g,
               gdn_w_out, mla_w_in, mla_q_norm_g, mla_kv_norm_g, mla_w_uq, mla_w_ukv, mla_qk_q_g,
               mla_qk_k_g, mla_w_out):
    s_len = xs[0].shape[1]
    assert all(x.shape[1] == s_len for x in xs) and s_len % 128 == 0
    lp = s_len + TAIL
    x_all = jnp.concatenate(xs, axis=0)
    b = x_all.shape[0]
    meta = jnp.broadcast_to(meta_tokens[None].astype(F32), (b, N_META, D_MODEL))
    h0 = jnp.concatenate([x_all, jnp.zeros((b, TAIL - N_META, D_MODEL), F32), meta], axis=1)
    h0 = h0.reshape(b * lp, D_MODEL)

    w_in = gdn_w_in[0]
    ba = w_in[:, GDN_CONV_CH + GDN_V:].reshape(D_MODEL, 2, 2, GDN_HEADS)
    ba_dir = [_pad_cols(jnp.concatenate([ba[:, 0, d], ba[:, 1, d]], axis=1), 128) for d in range(2)]
    conv_w8 = jnp.pad(gdn_conv_w[0], ((0, 8 - GDN_CONV), (0, 0)))
    qkv, gates = _gdn_qkv_conv(h0.reshape(b, lp, D_MODEL), ln_g[0][None, :],
                               w_in[:, :GDN_CONV_CH].astype(BF16),
                               jnp.concatenate(ba_dir, axis=1).astype(BF16), conv_w8, lp)

    lane_a = lambda v: jnp.pad(v.astype(F32), ((0, 0), (GDN_HEADS, 128 - 2 * GDN_HEADS)))[:, None, :]
    o_fwd, o_bwd = _gdn_scan(qkv, gates, lane_a(gdn_a_log[0]), lane_a(gdn_dt_bias[0]), lp)
    h1 = _gdn_out(o_fwd.reshape(b * lp, GDN_V), o_bwd.reshape(b * lp, GDN_V), h0, ln_g[0][None, :],
                  w_in[:, GDN_CONV_CH:GDN_CONV_CH + GDN_V].astype(BF16),
                  gdn_o_norm_g[0][None, :], gdn_w_out[0].astype(BF16))

    w_in1 = mla_w_in[0]
    o1 = MLA_Q_LORA
    o2_ = o1 + MLA_KV_LORA
    o3 = o2_ + MLA_ROPE
    w_small = jnp.concatenate([w_in1[:, o1:o2_], _pad_cols(w_in1[:, o2_:o3], 128)], axis=1).astype(BF16)
    w_big = jnp.concatenate([w_in1[:, o3:], w_in1[:, :o1]], axis=1).astype(BF16)
    h1_3 = h1.reshape(b, lp, D_MODEL)

    pos = jnp.concatenate([jnp.arange(s_len, dtype=F32) + N_META, jnp.zeros((TAIL - N_META,), F32),
                           jnp.arange(N_META, dtype=F32)])
    inv = ROPE_THETA ** (-jnp.arange(0, MLA_ROPE, 2, dtype=F32) / MLA_ROPE)
    ang = pos[:, None] * inv[None, :]
    zc = jnp.zeros_like(ang)
    cos_t = jnp.concatenate([jnp.cos(ang), jnp.cos(ang), zc, zc], axis=1)
    nsin_t = jnp.concatenate([-jnp.sin(ang), zc, zc, zc], axis=1)
    sin_t = jnp.concatenate([zc, jnp.sin(ang), zc, zc], axis=1)

    scale = MLA_DQK ** -0.5 * math.log2(math.e)
    gq = jnp.broadcast_to((mla_qk_q_g[0].astype(F32) * scale)[:, None], (MLA_DQK, 128))
    score_bound = (1.02 * MLA_DQK * scale * jnp.max(jnp.abs(mla_qk_q_g[0].astype(F32)))
                   * jnp.max(jnp.abs(mla_qk_k_g[0].astype(F32))))
    bounded = score_bound < ATT_BOUND_MAX
    shift_rows = jnp.zeros((MLA_DPAD - MLA_DQK, 128), F32).at[0, :].set(jnp.where(bounded, -score_bound, 0.0))
    w_ukv = mla_w_ukv[0].reshape(MLA_KV_LORA, MLA_HEADS, MLA_NOPE + MLA_DV)
    w_ukv = jnp.concatenate([w_ukv[:, :, :MLA_NOPE].reshape(MLA_KV_LORA, -1),
                             w_ukv[:, :, MLA_NOPE:].reshape(MLA_KV_LORA, -1)], axis=1).astype(BF16)
    gk = mla_qk_k_g[0].astype(F32)
    proj1_3, k, vt = _mla_proj_kv(h1_3, ln_g[1][None, :], w_small, w_big, mla_kv_norm_g[0][None, :], w_ukv,
                                  gk[None, :MLA_NOPE], _lane_row(gk[MLA_NOPE:]), cos_t, nsin_t, sin_t, lp)
    qt = _mla_q(proj1_3, mla_q_norm_g[0][None, :], mla_w_uq[0].T.astype(BF16), gq[:MLA_NOPE],
                gq[MLA_NOPE:], jnp.cos(ang).T, jnp.sin(ang).T, shift_rows, lp)

    o = _attention(bounded.astype(jnp.int32).reshape(1), qt, k, vt, s_len)

    w_out1 = mla_w_out[0].astype(BF16)
    outs = []
    b0 = 0
    for x in xs:
        outs.append(_mla_out(o, proj1_3, h1_3, w_out1, b0, x.shape[0], s_len))
        b0 += x.shape[0]
    return tuple(outs)


def kernel(x_prompt, x_sample, meta_tokens, ln_g, gdn_w_in, gdn_conv_w, gdn_a_log, gdn_dt_bias,
           gdn_o_norm_g, gdn_w_out, mla_w_in, mla_q_norm_g, mla_kv_norm_g, mla_w_uq, mla_w_ukv,
           mla_qk_q_g, mla_qk_k_g, mla_w_out):
    return _trunk_all((x_prompt, x_sample), meta_tokens, ln_g, gdn_w_in, gdn_conv_w, gdn_a_log,
                      gdn_dt_bias, gdn_o_norm_g, gdn_w_out, mla_w_in, mla_q_norm_g, mla_kv_norm_g,
                      mla_w_uq, mla_w_ukv, mla_qk_q_g, mla_qk_k_g, mla_w_out)
```

```python
import functools
import math

import jax
import jax.numpy as jnp
from jax import lax
from jax.experimental import pallas as pl
from jax.experimental.pallas import tpu as pltpu

F32 = jnp.float32
BF16 = jnp.bfloat16

D_MODEL = 1024
N_META = 16
TAIL = 128
NORM_EPS = 1e-6

GDN_HEADS = 8
GDN_DK = 128
GDN_DV = 256
GDN_CONV = 5
GDN_CHUNK = 64
GDN_QK = GDN_HEADS * GDN_DK
GDN_V = GDN_HEADS * GDN_DV
GDN_CONV_CH = 2 * GDN_QK + GDN_V

MLA_HEADS = 16
MLA_Q_LORA = 512
MLA_KV_LORA = 256
MLA_NOPE = 128
MLA_ROPE = 64
MLA_DQK = MLA_NOPE + MLA_ROPE
MLA_DV = 128
MLA_V = MLA_HEADS * MLA_DV
MLA_DPAD = 256
ROPE_THETA = 10000.0

VMEM_LIMIT = 56 * 1024 * 1024


def _cparams(sem):
    return pltpu.CompilerParams(dimension_semantics=sem, vmem_limit_bytes=VMEM_LIMIT)


def _tile(n, target, mult):
    best = None
    t = mult
    while t <= min(n, target):
        if n % t == 0:
            best = t
        t += mult
    assert best is not None, (n, target, mult)
    return best


CONV_SUB = 64
CONV_LANES = GDN_DK
CONV_TC = 1024


def _qkv_conv_kernel(prev_ref, main_ref, next_ref, g_ref, w_ref, wba_ref, cw_ref, o_ref, ba_ref, ext_ref,
                     *, tr):
    half = GDN_CONV // 2
    nblk = GDN_CONV_CH // CONV_TC
    x = jnp.concatenate([prev_ref[0], main_ref[0], next_ref[0]], axis=0)
    ms = jnp.mean(x * x, axis=-1, keepdims=True)
    xn = (x * lax.rsqrt(ms + NORM_EPS) * g_ref[...]).astype(BF16)
    ba_ref[0] = jnp.dot(xn, wba_ref[...], preferred_element_type=F32)[8:8 + tr]

    def project(j):
        ext_ref[j] = jnp.dot(xn, w_ref[:, j * CONV_TC:(j + 1) * CONV_TC], preferred_element_type=F32)

    def conv_silu(j, sb, lb):
        lanes = slice(j * CONV_TC + lb * CONV_LANES, j * CONV_TC + (lb + 1) * CONV_LANES)
        nrow = CONV_SUB + 16
        x2 = ext_ref[j, sb * CONV_SUB:sb * CONV_SUB + nrow, lb * CONV_LANES:(lb + 1) * CONV_LANES]
        acc = x2[8:8 + CONV_SUB] * cw_ref[half:half + 1, lanes]
        for s in range(1, half + 1):
            down = pltpu.roll(x2, s, 0)
            acc = acc + down[8:8 + CONV_SUB] * cw_ref[half - s:half - s + 1, lanes]
            up = pltpu.roll(x2, nrow - s, 0)
            acc = acc + up[8:8 + CONV_SUB] * cw_ref[half + s:half + s + 1, lanes]
        return acc * jax.nn.sigmoid(acc)

    def finish(j):
        for sb in range(tr // CONV_SUB):
            rows = slice(sb * CONV_SUB, (sb + 1) * CONV_SUB)
            for lb in range(CONV_TC // CONV_LANES):
                y = conv_silu(j, sb, lb)
                c0 = j * CONV_TC + lb * CONV_LANES
                lanes = slice(c0, c0 + CONV_LANES)
                if c0 < 2 * GDN_QK:
                    qscale = GDN_DK ** -0.5 if c0 < GDN_QK else 1.0
                    ss = jnp.sum(y * y, axis=-1, keepdims=True)
                    y = y * (lax.rsqrt(ss + NORM_EPS) * qscale)
                o_ref[0, rows, lanes] = y.astype(BF16)

    project(0)
    for j in range(nblk):
        if j + 1 < nblk:
            project(j + 1)
        finish(j)


def _gdn_qkv_conv(h3, g, w_qkv, w_ba, conv_w8, lp):
    b = h3.shape[0]
    nba = w_ba.shape[1]
    tr = _tile(lp, 704, CONV_SUB)
    nb8 = lp // 8
    tb = tr // 8
    kern = functools.partial(_qkv_conv_kernel, tr=tr)
    const = lambda bi, i: (0, 0)
    return pl.pallas_call(
        kern,
        grid=(b, lp // tr),
        in_specs=[
            pl.BlockSpec((1, 8, D_MODEL), lambda bi, i: (bi, (i * tb + nb8 - 1) % nb8, 0)),
            pl.BlockSpec((1, tr, D_MODEL), lambda bi, i: (bi, i, 0)),
            pl.BlockSpec((1, 8, D_MODEL), lambda bi, i: (bi, ((i + 1) * tb) % nb8, 0)),
            pl.BlockSpec((1, D_MODEL), const),
            pl.BlockSpec((D_MODEL, GDN_CONV_CH), const),
            pl.BlockSpec((D_MODEL, nba), const),
            pl.BlockSpec((8, GDN_CONV_CH), const),
        ],
        out_specs=[
            pl.BlockSpec((1, tr, GDN_CONV_CH), lambda bi, i: (bi, i, 0)),
            pl.BlockSpec((1, tr, nba), lambda bi, i: (bi, i, 0)),
        ],
        out_shape=[
            jax.ShapeDtypeStruct((b, lp, GDN_CONV_CH), BF16),
            jax.ShapeDtypeStruct((b, lp, nba), F32),
        ],
        scratch_shapes=[pltpu.VMEM((GDN_CONV_CH // CONV_TC, tr + 16, CONV_TC), F32)],
        compiler_params=_cparams(("parallel", "parallel")),
        name="gdn_qkv_conv",
    )(h3, h3, h3, g, w_qkv, w_ba, conv_w8)


SCAN_NB = 4


def _split3(x):
    hi = x.astype(BF16)
    r1 = x - hi.astype(F32)
    mid = r1.astype(BF16)
    lo = (r1 - mid.astype(F32)).astype(BF16)
    return hi, mid, lo


def _gdn_scan_kernel(qf_ref, kf_ref, vf_ref, baf_ref, qb_ref, kb_ref, vb_ref, bab_ref, alog_ref, dtb_ref,
                     of_ref, ob_ref, s_ref, sb_ref, u_ref, qw_ref, attn_ref, kd_ref, eg_ref, *, nc):
    t = pl.program_id(1)
    c = GDN_CHUNK
    c2 = 2 * c
    npair = GDN_HEADS // 2
    o_refs = (of_ref, ob_ref)

    @pl.when(t == 0)
    def _():
        for ref in (s_ref, sb_ref, u_ref, qw_ref, attn_ref, kd_ref, eg_ref):
            ref[...] = jnp.zeros_like(ref)

    nb = of_ref.shape[0]
    nprob = nb * 2 * npair
    applied = [dict(sq=pi // npair, heads=(2 * (pi % npair), 2 * (pi % npair) + 1),
                    qs=[None, None], vnew=[None, None]) for pi in range(nprob)]

    def read_state(pi, hh):
        ap = applied[pi]
        rs = slice(hh * c, (hh + 1) * c)
        qws = jnp.dot(qw_ref[pi, hh], sb_ref[ap["sq"], ap["heads"][hh]], preferred_element_type=F32)
        ap["qs"][hh] = qws[:c]
        ap["vnew"][hh] = (u_ref[pi, rs, :] - qws[c:]).astype(BF16)

    def write_out(pi):
        ap = applied[pi]
        h0, h1 = ap["heads"]
        bi, d = divmod(ap["sq"], 2)
        vnew2 = jnp.concatenate(ap["vnew"], axis=0)
        o2 = jnp.concatenate(ap["qs"], axis=0) + jnp.dot(attn_ref[pi], vnew2, preferred_element_type=F32)
        o_refs[d][bi, :, h0 * GDN_DV:(h0 + 1) * GDN_DV] = o2[:c].astype(BF16)
        o_refs[d][bi, :, h1 * GDN_DV:(h1 + 1) * GDN_DV] = o2[c:].astype(BF16)

    def update_state(pi, hh):
        ap = applied[pi]
        sq, h = ap["sq"], ap["heads"][hh]
        rs = slice(hh * c, (hh + 1) * c)
        a_h = GDN_HEADS + h
        upd = lax.dot_general(kd_ref[pi, rs, :], ap["vnew"][hh], (((0,), (0,)), ((), ())),
                              preferred_element_type=F32)
        s_new = s_ref[sq, h] * eg_ref[sq, :, a_h:a_h + 1] + upd
        s_ref[sq, h] = s_new
        sb_ref[sq, h] = s_new.astype(BF16)

    pairs_hh = [(pi, hh) for pi in range(nprob) for hh in range(2)]
    apply_ops = ([functools.partial(read_state, pi, hh) for pi, hh in pairs_hh]
                 + [functools.partial(write_out, pi) for pi in range(nprob)]
                 + [functools.partial(update_state, pi, hh) for pi, hh in pairs_hh])

    def emit_apply(n):
        for _ in range(min(n, len(apply_ops))):
            apply_ops.pop(0)()

    emit_apply(len(pairs_hh) // 2)
    tp = jnp.minimum(t, nc - 1)
    ri = lax.broadcasted_iota(jnp.int32, (c2, c2), 0)
    ci = lax.broadcasted_iota(jnp.int32, (c2, c2), 1)
    same = (ri >> 6) == (ci >> 6)
    offdiag = ri != ci
    top =lax.broadcasted_iota(jnp.int32, (c2, 1), 0) < c
    left = lax.broadcasted_iota(jnp.int32, (1, c2), 1) < c
    row_id = lax.broadcasted_iota(jnp.int32, (c, 128), 0)

    in_refs = ((qf_ref, kf_ref, vf_ref, baf_ref), (qb_ref, kb_ref, vb_ref, bab_ref))
    seqs = []
    for sq in range(2 * nb):
        bi, d = divmod(sq, 2)
        q_ref, k_ref, v_ref, ba_ref = in_refs[d]
        seq = tp if d == 0 else nc - 1 - tp
        blk = jnp.where(seq < 2, nc - 2 + seq, seq - 2)
        first_valid = jnp.where(blk == nc - 2, c, jnp.where(blk == nc - 1, c - N_META, 0))
        valid = row_id >= first_valid
        ba = ba_ref[bi]
        beta = jnp.where(valid, jax.nn.sigmoid(ba), 0.0)
        xs = ba + dtb_ref[d]
        softplus = jnp.maximum(xs, 0.0) + jnp.log(1.0 + jnp.exp(-jnp.abs(xs)))
        g = jnp.where(valid, -jnp.exp(alog_ref[d]) * softplus, 0.0)
        incl = same & ((ri >= ci) if d == 0 else (ri <= ci))
        tri = jnp.where(incl, 1.0, 0.0).astype(BF16)
        ghi, gmid, glo = _split3(jnp.concatenate([g, g], axis=0))
        gc2 = (jnp.dot(tri, ghi, preferred_element_type=F32)
               + jnp.dot(tri, gmid, preferred_element_type=F32)
               + jnp.dot(tri, glo, preferred_element_type=F32))
        seqs.append(dict(q=q_ref.at[bi], k=k_ref.at[bi], v=v_ref.at[bi], incl=incl, gc2=gc2, gc2t=gc2.T,
                         beta2=jnp.concatenate([beta, beta], axis=0),
                         gtot=jnp.sum(g, axis=0, keepdims=True)))

    def stacked(ref, h0, h1, width):
        return jnp.concatenate([ref[:, h0 * width:(h0 + 1) * width],
                                ref[:, h1 * width:(h1 + 1) * width]], axis=0)

    probs = []
    for dd in seqs:
        for p in range(npair):
            h0, h1 = 2 * p, 2 * p + 1
            a0, a1 = GDN_HEADS + h0, GDN_HEADS + h1
            col = jnp.where(top, dd["gc2"][:, a0:a0 + 1], dd["gc2"][:, a1:a1 + 1])
            row = jnp.where(left, dd["gc2t"][a0:a0 + 1, :], dd["gc2t"][a1:a1 + 1, :])
            bcol = jnp.where(top, dd["beta2"][:, h0:h0 + 1], dd["beta2"][:, h1:h1 + 1])
            tot = jnp.where(top, dd["gtot"][:, a0:a0 + 1], dd["gtot"][:, a1:a1 + 1])
            dec = jnp.exp(jnp.where(dd["incl"], col - row, -jnp.inf))
            kst = stacked(dd["k"], h0, h1, GDN_DK)
            qst = stacked(dd["q"], h0, h1, GDN_DK)
            vst = stacked(dd["v"], h0, h1, GDN_DV)
            kf = kst.astype(F32)
            kb = kf * bcol
            sc = lax.dot_general(jnp.concatenate([qst, kb.astype(BF16)], axis=0), kst,
                                 (((1,), (1,)), ((), ())), preferred_element_type=F32)
            egc = jnp.exp(col)
            probs.append(dict(
                attn=(sc[:c2] * dec).astype(BF16),
                a=jnp.where(offdiag, sc[c2:] * dec, 0.0),
                rhs=jnp.concatenate([vst.astype(F32) * bcol, kb * egc], axis=1).astype(BF16),
                qg=(qst.astype(F32) * egc).astype(BF16),
                kd=(kf * jnp.exp(tot - col)).astype(BF16)))

    emit_apply(len(pairs_hh) // 2)

    for _ in _unit_lower_inverse_staged(probs):
        emit_apply(3 * nb)
    emit_apply(len(apply_ops))

    for pi, pr in enumerate(probs):
        uw = jnp.dot(pr["tinv"].astype(BF16), pr["rhs"], preferred_element_type=F32)
        u_ref[pi] = uw[:, :GDN_DV]
        w = uw[:, GDN_DV:].astype(BF16)
        for hh in range(2):
            rs = slice(hh * c, (hh + 1) * c)
            qw_ref[pi, hh] = jnp.concatenate([pr["qg"][rs], w[rs]], axis=0)
        attn_ref[pi] = pr["attn"]
        kd_ref[pi] = pr["kd"]
    for sq, dd in enumerate(seqs):
        eg_ref[sq] = jnp.exp(dd["gtot"])


def _unit_lower_inverse_staged(probs):
    c = GDN_CHUNK
    nside = 4
    ri = lax.broadcasted_iota(jnp.int32, (c, nside * c), 0)
    ci = lax.broadcasted_iota(jnp.int32, (c, nside * c), 1)
    lane_blk = ci >> 6
    within = ci & (c - 1)
    diag16 = (ri >> 4) == (within >> 4)
    eye = jnp.where(ri == within, 1.0, 0.0).astype(F32)
    left = lax.broadcasted_iota(jnp.int32, (1, 2 * c), 1) < c

    def blockdiag(y):
        return jnp.concatenate([jnp.where(lane_blk == r, y, 0.0) for r in range(nside)],
                               axis=0).astype(BF16)

    def mm(x, ybd):
        return jnp.dot(x.astype(BF16), ybd, preferred_element_type=F32)

    groups = []
    for g0 in range(0, len(probs), 2):
        pa, pb = probs[g0], probs[g0 + 1]
        a = jnp.concatenate([pa["a"][:c] + pa["a"][c:], pb["a"][:c] + pb["a"][c:]], axis=1)
        ad = jnp.where(diag16, a, 0.0)
        groups.append(dict(pairs=(pa, pb), ad=ad, an=blockdiag(a - ad), dinv=eye - ad))
    for gr in groups:
        gr["p"] = mm(gr["ad"], blockdiag(gr["ad"]))
    yield
    for gr in groups:
        pbd = blockdiag(gr["p"])
        gr["dinv"] = gr["dinv"] + mm(gr["dinv"], pbd)
        gr["p"] = mm(gr["p"], pbd)
    yield
    for gr in groups:
        pbd = blockdiag(gr["p"])
        gr["dinv"] = gr["dinv"] + mm(gr["dinv"], pbd)
        gr["p"] = mm(gr["p"], pbd)
    yield
    for gr in groups:
        gr["dinv"] = gr["dinv"] + mm(gr["dinv"], blockdiag(gr["p"]))
    yield
    for gr in groups:
        gr["m"] = mm(gr["dinv"], gr["an"])
    yield
    for gr in groups:
        gr["m2"] = blockdiag(mm(gr["m"], blockdiag(gr["m"])))
    yield
    for gr in groups:
        x = eye - gr["m"]
        gr["x"] = x + mm(x, gr["m2"])
    yield
    for gr in groups:
        t = mm(gr["x"], blockdiag(gr["dinv"]))
        for i, pr in enumerate(gr["pairs"]):
            half = t[:, 2 * c * i:2 * c * (i + 1)]
            pr["tinv"] = jnp.concatenate([jnp.where(left, half, 0.0), jnp.where(left, 0.0, half)],
                                         axis=0)


def _gdn_scan(qkv, proj3, alog_rows, dtb_rows, lp):
    b = qkv.shape[0]
    nc = lp // GDN_CHUNK
    ba_col0 = 0

    def blk_of(seq):
        return jnp.where(seq < 2, nc - 2 + seq, seq - 2)

    def chunk_specs(d):
        prep = lambda t: jnp.minimum(t, nc - 1)
        seq = prep if d == 0 else (lambda t: nc - 1 - prep(t))
        return [
            pl.BlockSpec((nb, GDN_CHUNK, GDN_QK), lambda bi, t: (bi, blk_of(seq(t)), 0)),
            pl.BlockSpec((nb, GDN_CHUNK, GDN_QK), lambda bi, t: (bi, blk_of(seq(t)), 1)),
            pl.BlockSpec((nb, GDN_CHUNK, GDN_V), lambda bi, t: (bi, blk_of(seq(t)), 1)),
            pl.BlockSpec((nb, GDN_CHUNK, 128), lambda bi, t: (bi, blk_of(seq(t)), ba_col0 + d)),
        ]

    nb = SCAN_NB if b % SCAN_NB == 0 else 1
    const = pl.BlockSpec((2, 1, 128), lambda bi, t: (0, 0, 0))
    kern = functools.partial(_gdn_scan_kernel, nc=nc)
    applied = lambda t: jnp.maximum(t - 1, 0)
    nseq = 2 * nb
    nprob = nseq * (GDN_HEADS // 2)
    c2 = 2 * GDN_CHUNK
    return pl.pallas_call(
        kern,
        grid=(b // nb, nc + 1),
        in_specs=chunk_specs(0) + chunk_specs(1) + [const, const],
        out_specs=[
            pl.BlockSpec((nb, GDN_CHUNK, GDN_V), lambda bi, t: (bi, blk_of(applied(t)), 0)),
            pl.BlockSpec((nb, GDN_CHUNK, GDN_V), lambda bi, t: (bi, blk_of(nc - 1 - applied(t)), 0)),
        ],
        out_shape=[jax.ShapeDtypeStruct((b, lp, GDN_V), BF16)] * 2,
        scratch_shapes=[
            pltpu.VMEM((nseq, GDN_HEADS, GDN_DK, GDN_DV), F32),
            pltpu.VMEM((nseq, GDN_HEADS, GDN_DK, GDN_DV), BF16),
            pltpu.VMEM((nprob, c2, GDN_DV), F32),
            pltpu.VMEM((nprob, 2, c2, GDN_DK), BF16),
            pltpu.VMEM((nprob, c2, c2), BF16),
            pltpu.VMEM((nprob, c2, GDN_DK), BF16),
            pltpu.VMEM((nseq, 1, 128), F32),
        ],
        compiler_params=_cparams(("parallel", "arbitrary")),
        name="gdn_scan",
    )(qkv, qkv, qkv, proj3, qkv, qkv, qkv, proj3, alog_rows, dtb_rows)


OUT_SUB = 256


def _gdn_out_kernel(of_ref, ob_ref, h_ref, ln_ref, wz_ref, g_ref, w_ref, o_ref):
    nsub = o_ref.shape[0] // OUT_SUB
    rows = [slice(sb * OUT_SUB, (sb + 1) * OUT_SUB) for sb in range(nsub)]

    def gate_logits(sb):
        x = h_ref[rows[sb], :]
        ms = jnp.mean(x * x, axis=-1, keepdims=True)
        xn = (x * lax.rsqrt(ms + NORM_EPS) * ln_ref[...]).astype(BF16)
        return jnp.dot(xn, wz_ref[...], preferred_element_type=F32)

    z_next = gate_logits(0)
    for sb in range(nsub):
        z = z_next
        if sb + 1 < nsub:
            z_next = gate_logits(sb + 1)
        o = of_ref[rows[sb], :].astype(F32) + ob_ref[rows[sb], :].astype(F32)
        gate = z * jax.nn.sigmoid(z)
        ys = []
        for h in range(GDN_HEADS):
            lanes = slice(h * GDN_DV, (h + 1) * GDN_DV)
            oh = o[:, lanes]
            ms = jnp.mean(oh * oh, axis=-1, keepdims=True)
            ys.append((oh * lax.rsqrt(ms + NORM_EPS) * g_ref[...] * gate[:, lanes]).astype(BF16))
        y = jnp.concatenate(ys, axis=1)
        o_ref[rows[sb], :] = h_ref[rows[sb], :] + jnp.dot(y, w_ref[...], preferred_element_type=F32)


def _gdn_out(o_fwd, o_bwd, h0, ln_row, w_z, g_row, w_out):
    rows = h0.shape[0]
    tm = _tile(rows, 768, OUT_SUB)
    const = lambda i: (0, 0)
    return pl.pallas_call(
        _gdn_out_kernel,
        grid=(rows // tm,),
        in_specs=[
            pl.BlockSpec((tm, GDN_V), lambda i: (i, 0)),
            pl.BlockSpec((tm, GDN_V), lambda i: (i, 0)),
            pl.BlockSpec((tm, D_MODEL), lambda i: (i, 0)),
            pl.BlockSpec((1, D_MODEL), const),
            pl.BlockSpec((D_MODEL, GDN_V), const),
            pl.BlockSpec((1, GDN_DV), const),
            pl.BlockSpec((GDN_V, D_MODEL), const),
        ],
        out_specs=pl.BlockSpec((tm, D_MODEL), lambda i: (i, 0)),
        out_shape=jax.ShapeDtypeStruct((rows, D_MODEL), F32),
        compiler_params=_cparams(("parallel",)),
        name="gdn_out",
    )(o_fwd, o_bwd, h0, ln_row, w_z, g_row, w_out)


def _rope(r, cos, nsin_lo, sin_hi):
    return r * cos + pltpu.roll(r, 96, 1) * nsin_lo + pltpu.roll(r, 32, 1) * sin_hi


Q_GROUP = 2


def _mla_q_kernel(cq_ref, g1_ref, wt_ref, ga_ref, gr_ref, cos_ref, sin_ref, shift_ref, qt_ref):
    cq = cq_ref[0].astype(F32)
    tm = cq.shape[0]
    ms = jnp.mean(cq * cq, axis=-1, keepdims=True)
    cqt = (cq * lax.rsqrt(ms + NORM_EPS) * g1_ref[...]).T.astype(BF16)
    half = MLA_ROPE // 2
    tile = lambda g: jnp.concatenate([g] * (tm // 128), axis=1)
    ga = tile(ga_ref[...])
    gr1, gr2 = tile(gr_ref[0:half, :]), tile(gr_ref[half:MLA_ROPE, :])
    cos, sin = cos_ref[...], sin_ref[...]
    pad_rows = tile(shift_ref[...]).astype(BF16)

    def project(g):
        rows = slice(g * Q_GROUP * MLA_DQK, (g + 1) * Q_GROUP * MLA_DQK)
        return jnp.dot(wt_ref[rows, :], cqt, preferred_element_type=F32)

    q_next = project(0)
    for h in range(MLA_HEADS):
        if h % Q_GROUP == 0:
            q = q_next
            if h + Q_GROUP < MLA_HEADS:
                q_next = project(h // Q_GROUP + 1)
        r0 = (h % Q_GROUP) * MLA_DQK
        a = q[r0:r0 + MLA_NOPE]
        x1 = q[r0 + MLA_NOPE:r0 + MLA_NOPE + half]
        x2 = q[r0 + MLA_NOPE + half:r0 + MLA_DQK]
        ss = (jnp.sum(a * a, axis=0, keepdims=True) + jnp.sum(x1 * x1, axis=0, keepdims=True)
              + jnp.sum(x2 * x2, axis=0, keepdims=True))
        inv = lax.rsqrt(ss * (1.0 / MLA_DQK) + NORM_EPS)
        x1 = x1 * inv * gr1
        x2 = x2 * inv * gr2
        qt_ref[0, h, 0:MLA_NOPE, :] = (a * inv * ga).astype(BF16)
        qt_ref[0, h, MLA_NOPE:MLA_NOPE + half, :] = (x1 * cos - x2 * sin).astype(BF16)
        qt_ref[0, h, MLA_NOPE + half:MLA_DQK, :] = (x2 * cos + x1 * sin).astype(BF16)
        qt_ref[0, h, MLA_DQK:MLA_DPAD, :] = pad_rows


def _mla_q(proj3, g1, w_uq_t, ga, gr, cos_t, sin_t, shift_rows, lp):
    b = proj3.shape[0]
    tm = _tile(lp, 384, 128)
    cqcol = MLA_V // MLA_Q_LORA
    const = lambda bi, i: (0, 0)
    return pl.pallas_call(
        _mla_q_kernel,
        grid=(b, lp // tm),
        in_specs=[
            pl.BlockSpec((1, tm, MLA_Q_LORA), lambda bi, i: (bi, i, cqcol)),
            pl.BlockSpec((1, MLA_Q_LORA), const),
            pl.BlockSpec((MLA_HEADS * MLA_DQK, MLA_Q_LORA), const),
            pl.BlockSpec((MLA_NOPE, 128), const),
            pl.BlockSpec((MLA_ROPE, 128), const),
            pl.BlockSpec((MLA_ROPE // 2, tm), lambda bi, i: (0, i)),
            pl.BlockSpec((MLA_ROPE // 2, tm), lambda bi, i: (0, i)),
            pl.BlockSpec((MLA_DPAD - MLA_DQK, 128), const),
        ],
        out_specs=pl.BlockSpec((1, MLA_HEADS, MLA_DPAD, tm), lambda bi, i: (bi, 0, 0, i)),
        out_shape=jax.ShapeDtypeStruct((b, MLA_HEADS, MLA_DPAD, lp), BF16),
        compiler_params=_cparams(("parallel", "parallel")),
        name="mla_q",
    )(proj3, g1, w_uq_t, ga, gr, cos_t, sin_t, shift_rows)


def _mla_kv_kernel(h_ref, ln_ref, ws_ref, wb_ref, g1_ref, w_ref, ga_ref, gr_ref, cos_ref, nsin_ref, sin_ref,
                   zq_ref, k_ref, vt_ref):
    x = h_ref[0]
    xn = (x * lax.rsqrt(jnp.mean(x * x, axis=-1, keepdims=True) + NORM_EPS) * ln_ref[...]).astype(BF16)
    small = jnp.dot(xn, ws_ref[...], preferred_element_type=F32)
    ckv = small[:, :MLA_KV_LORA]
    ms = jnp.mean(ckv * ckv, axis=-1, keepdims=True)
    cn = (ckv * lax.rsqrt(ms + NORM_EPS) * g1_ref[...]).astype(BF16)
    kv = jnp.dot(cn, w_ref[...], preferred_element_type=F32)
    zq_ref[0] = jnp.dot(xn, wb_ref[...], preferred_element_type=F32).astype(BF16)
    kpe = small[:, MLA_KV_LORA:]
    sq_pe = kpe * kpe
    kr = _rope(kpe * gr_ref[...], cos_ref[...], nsin_ref[...], sin_ref[...])
    one_hot = jnp.where(lax.broadcasted_iota(jnp.int32, (1, 128), 1) == MLA_ROPE, 1.0, 0.0).astype(F32)
    for h in range(MLA_HEADS):
        kn = kv[:, h * MLA_NOPE:(h + 1) * MLA_NOPE]
        ss = jnp.sum(kn * kn + sq_pe, axis=-1, keepdims=True)
        inv = lax.rsqrt(ss * (1.0 / MLA_DQK) + NORM_EPS)
        k_ref[0, h, :, 0:MLA_NOPE] = (kn * inv * ga_ref[...]).astype(BF16)
        k_ref[0, h, :, MLA_NOPE:MLA_DPAD] = (kr * inv + one_hot).astype(BF16)
        v = kv[:, MLA_HEADS * MLA_NOPE + h * MLA_DV:MLA_HEADS * MLA_NOPE + (h + 1) * MLA_DV]
        vt_ref[0, h] = v.T.astype(BF16)


def _mla_proj_kv(h3, ln_row, w_small, w_big, g1, w_ukv, ga, gr, cos, nsin, sin, lp):
    b = h3.shape[0]
    tm = _tile(lp, 384, 128)
    const = lambda bi, i: (0, 0)
    nbig = w_big.shape[1]
    return pl.pallas_call(
        _mla_kv_kernel,
        grid=(b, lp // tm),
        in_specs=[
            pl.BlockSpec((1, tm, D_MODEL), lambda bi, i: (bi, i, 0)),
            pl.BlockSpec((1, D_MODEL), const),
            pl.BlockSpec((D_MODEL, MLA_KV_LORA + 128), const),
            pl.BlockSpec((D_MODEL, nbig), const),
            pl.BlockSpec((1, MLA_KV_LORA), const),
            pl.BlockSpec((MLA_KV_LORA, MLA_HEADS * (MLA_NOPE + MLA_DV)), const),
            pl.BlockSpec((1, 128), const),
            pl.BlockSpec((1, 128), const),
            pl.BlockSpec((tm, 128), lambda bi, i: (i, 0)),
            pl.BlockSpec((tm, 128), lambda bi, i: (i, 0)),
            pl.BlockSpec((tm, 128), lambda bi, i: (i, 0)),
        ],
        out_specs=[
            pl.BlockSpec((1, tm, nbig), lambda bi, i: (bi, i, 0)),
            pl.BlockSpec((1, MLA_HEADS, tm, MLA_DPAD), lambda bi, i: (bi, 0, i, 0)),
            pl.BlockSpec((1, MLA_HEADS, MLA_DV, tm), lambda bi, i: (bi, 0, 0, i)),
        ],
        out_shape=[
            jax.ShapeDtypeStruct((b, lp, nbig), BF16),
            jax.ShapeDtypeStruct((b, MLA_HEADS, lp, MLA_DPAD), BF16),
            jax.ShapeDtypeStruct((b, MLA_HEADS, MLA_DV, lp), BF16),
        ],
        compiler_params=_cparams(("parallel", "parallel")),
        name="mla_proj_kv",
    )(h3, ln_row, w_small, w_big, g1, w_ukv, ga, gr, cos, nsin, sin)


ATT_TK = 1024
ATT_TQ = 512
ATT_AHEAD = 2
ATT_BOUND_MAX = 60.0


def _attn_kernel(bounded_ref, qt_ref, k_ref, vt_ref, o_ref, *, s_len):
    tq = qt_ref.shape[3]
    nstrip = tq // ATT_TQ
    chunks = [(ck * ATT_TK, ATT_TK) for ck in range(s_len // ATT_TK)] + [(s_len, TAIL)]
    units = [(i, s) for i in range(len(chunks)) for s in range(nstrip)]
    is_meta = lax.broadcasted_iota(jnp.int32, (TAIL, 1), 0) >= TAIL - N_META
    tail_bias = jnp.where(is_meta, 0.0, -jnp.inf).astype(F32)

    def scores(u):
        i, s = units[u]
        k0, nk = chunks[i]
        st = jnp.dot(k_ref[0, 0, k0:k0 + nk, :], qt_ref[0, 0, :, s * ATT_TQ:(s + 1) * ATT_TQ],
                     preferred_element_type=F32)
        return st + tail_bias if i == len(chunks) - 1 else st

    def run(bounded):
        m = [jnp.full((1, ATT_TQ), -jnp.inf, F32) for _ in range(nstrip)]
        l = [jnp.zeros((1, ATT_TQ), F32) for _ in range(nstrip)]
        acc = [jnp.zeros((MLA_DV, ATT_TQ), F32) for _ in range(nstrip)]
        pending = [scores(u) for u in range(min(ATT_AHEAD, len(units)))]
        for u, (i, s) in enumerate(units):
            if u + ATT_AHEAD < len(units):
                pending.append(scores(u + ATT_AHEAD))
            st = pending.pop(0)
            k0, nk = chunks[i]
            if bounded:
                p = jnp.exp2(st)
                l[s] = l[s] + jnp.sum(p, axis=0, keepdims=True)
                acc[s] = acc[s] + jnp.dot(vt_ref[0, 0, :, k0:k0 + nk], p.astype(BF16),
                                          preferred_element_type=F32)
            else:
                m_new = jnp.maximum(m[s], jnp.max(st, axis=0, keepdims=True))
                alpha = jnp.exp2(m[s] - m_new)
                p = jnp.exp2(st - m_new)
                l[s] = alpha * l[s] + jnp.sum(p, axis=0, keepdims=True)
                acc[s] = alpha * acc[s] + jnp.dot(vt_ref[0, 0, :, k0:k0 + nk], p.astype(BF16),
                                                  preferred_element_type=F32)
                m[s] = m_new
        for s in range(nstrip):
            o_ref[0, s * ATT_TQ:(s + 1) * ATT_TQ, :] = (acc[s] / l[s]).T.astype(o_ref.dtype)

    @pl.when(bounded_ref[0] == 1)
    def _():
        run(True)

    @pl.when(bounded_ref[0] != 1)
    def _():
        run(False)


def _attention(bounded, qt, k, vt, s_len):
    b = qt.shape[0]
    lp = k.shape[2]
    assert s_len % ATT_TK == 0 and lp == s_len + TAIL
    tq = _tile(s_len, 4096, ATT_TQ)
    kern = functools.partial(_attn_kernel, s_len=s_len)
    return pl.pallas_call(
        kern,
        grid=(b, MLA_HEADS, s_len // tq),
        in_specs=[
            pl.BlockSpec(memory_space=pltpu.SMEM),
            pl.BlockSpec((1, 1, MLA_DPAD, tq), lambda bi, h, i: (bi, h, 0, i)),
            pl.BlockSpec((1, 1, lp, MLA_DPAD), lambda bi, h, i: (bi, h, 0, 0)),
            pl.BlockSpec((1, 1, MLA_DV, lp), lambda bi, h, i: (bi, h, 0, 0)),
        ],
        out_specs=pl.BlockSpec((1, tq, MLA_DV), lambda bi, h, i: (bi, i, h)),
        out_shape=jax.ShapeDtypeStruct((b, s_len, MLA_V), BF16),
        compiler_params=_cparams(("parallel", "parallel", "arbitrary")),
        name="mla_attention",
    )(bounded, qt, k, vt)


def _mla_out_kernel(o_ref, z_ref, h_ref, w_ref, y_ref):
    z = z_ref[0].astype(F32)
    y = (o_ref[0].astype(F32) * (z * jax.nn.sigmoid(z))).astype(BF16)
    y_ref[0] = h_ref[0] + jnp.dot(y, w_ref[...], preferred_element_type=F32)


def _mla_out(o, proj3, h3, w_out, b0, nb, s_len):
    tm = _tile(s_len, 1024, 128)
    return pl.pallas_call(
        _mla_out_kernel,
        grid=(nb, s_len // tm),
        in_specs=[
            pl.BlockSpec((1, tm, MLA_V), lambda bi, i: (bi + b0, i, 0)),
            pl.BlockSpec((1, tm, MLA_V), lambda bi, i: (bi + b0, i, 0)),
            pl.BlockSpec((1, tm, D_MODEL), lambda bi, i: (bi + b0, i, 0)),
            pl.BlockSpec((MLA_V, D_MODEL), lambda bi, i: (0, 0)),
        ],
        out_specs=pl.BlockSpec((1, tm, D_MODEL), lambda bi, i: (bi, i, 0)),
        out_shape=jax.ShapeDtypeStruct((nb, s_len, D_MODEL), F32),
        compiler_params=_cparams(("parallel", "parallel")),
        name="mla_out",
    )(o, proj3, h3, w_out)


def _pad_cols(w, n):
    return jnp.pad(w, ((0, 0), (0, n - w.shape[1])))


def _lane_row(v, n=128):
    return jnp.pad(v.astype(F32), (0, n - v.shape[0]))[None, :]


def _trunk_all(xs, meta_tokens, ln_g, gdn_w_in, gdn_conv_w, gdn_a_log, gdn_dt_bias, gdn_o_norm_g,
               gdn_w_out, mla_w_in, mla_q_norm_g, mla_kv_norm_g, mla_w_uq, mla_w_ukv, mla_qk_q_g,
               mla_qk_k_g, mla_w_out):
    s_len = xs[0].shape[1]
    assert all(x.shape[1] == s_len for x in xs) and s_len % 128 == 0
    lp = s_len + TAIL
    x_all = jnp.concatenate(xs, axis=0)
    b = x_all.shape[0]
    meta = jnp.broadcast_to(meta_tokens[None].astype(F32), (b, N_META, D_MODEL))
    h0 = jnp.concatenate([x_all, jnp.zeros((b, TAIL - N_META, D_MODEL), F32), meta], axis=1)
    h0 = h0.reshape(b * lp, D_MODEL)

    w_in = gdn_w_in[0]
    ba = w_in[:, GDN_CONV_CH + GDN_V:].reshape(D_MODEL, 2, 2, GDN_HEADS)
    ba_dir = [_pad_cols(jnp.concatenate([ba[:, 0, d], ba[:, 1, d]], axis=1), 128) for d in range(2)]
    conv_w8 = jnp.pad(gdn_conv_w[0], ((0, 8 - GDN_CONV), (0, 0)))
    qkv, gates = _gdn_qkv_conv(h0.reshape(b, lp, D_MODEL), ln_g[0][None, :],
                               w_in[:, :GDN_CONV_CH].astype(BF16),
                               jnp.concatenate(ba_dir, axis=1).astype(BF16), conv_w8, lp)

    lane_a = lambda v: jnp.pad(v.astype(F32), ((0, 0), (GDN_HEADS, 128 - 2 * GDN_HEADS)))[:, None, :]
    o_fwd, o_bwd = _gdn_scan(qkv, gates, lane_a(gdn_a_log[0]), lane_a(gdn_dt_bias[0]), lp)
    h1 = _gdn_out(o_fwd.reshape(b * lp, GDN_V), o_bwd.reshape(b * lp, GDN_V), h0, ln_g[0][None, :],
                  w_in[:, GDN_CONV_CH:GDN_CONV_CH + GDN_V].astype(BF16),
                  gdn_o_norm_g[0][None, :], gdn_w_out[0].astype(BF16))

    w_in1 = mla_w_in[0]
    o1 = MLA_Q_LORA
    o2_ = o1 + MLA_KV_LORA
    o3 = o2_ + MLA_ROPE
    w_small = jnp.concatenate([w_in1[:, o1:o2_], _pad_cols(w_in1[:, o2_:o3], 128)], axis=1).astype(BF16)
    w_big = jnp.concatenate([w_in1[:, o3:], w_in1[:, :o1]], axis=1).astype(BF16)
    h1_3 = h1.reshape(b, lp, D_MODEL)

    pos = jnp.concatenate([jnp.arange(s_len, dtype=F32) + N_META, jnp.zeros((TAIL - N_META,), F32),
                           jnp.arange(N_META, dtype=F32)])
    inv = ROPE_THETA ** (-jnp.arange(0, MLA_ROPE, 2, dtype=F32) / MLA_ROPE)
    ang = pos[:, None] * inv[None, :]
    zc = jnp.zeros_like(ang)
    cos_t = jnp.concatenate([jnp.cos(ang), jnp.cos(ang), zc, zc], axis=1)
    nsin_t = jnp.concatenate([-jnp.sin(ang), zc, zc, zc], axis=1)
    sin_t = jnp.concatenate([zc, jnp.sin(ang), zc, zc], axis=1)

    scale = MLA_DQK ** -0.5 * math.log2(math.e)
    gq = jnp.broadcast_to((mla_qk_q_g[0].astype(F32) * scale)[:, None], (MLA_DQK, 128))
    score_bound = (1.02 * MLA_DQK * scale * jnp.max(jnp.abs(mla_qk_q_g[0].astype(F32)))
                   * jnp.max(jnp.abs(mla_qk_k_g[0].astype(F32))))
    bounded = score_bound < ATT_BOUND_MAX
    shift_rows = jnp.zeros((MLA_DPAD - MLA_DQK, 128), F32).at[0, :].set(jnp.where(bounded, -score_bound, 0.0))
    w_ukv = mla_w_ukv[0].reshape(MLA_KV_LORA, MLA_HEADS, MLA_NOPE + MLA_DV)
    w_ukv = jnp.concatenate([w_ukv[:, :, :MLA_NOPE].reshape(MLA_KV_LORA, -1),
                             w_ukv[:, :, MLA_NOPE:].reshape(MLA_KV_LORA, -1)], axis=1).astype(BF16)
    gk = mla_qk_k_g[0].astype(F32)
    proj1_3, k, vt = _mla_proj_kv(h1_3, ln_g[1][None, :], w_small, w_big, mla_kv_norm_g[0][None, :], w_ukv,
                                  gk[None, :MLA_NOPE], _lane_row(gk[MLA_NOPE:]), cos_t, nsin_t, sin_t, lp)
    qt = _mla_q(proj1_3, mla_q_norm_g[0][None, :], mla_w_uq[0].T.astype(BF16), gq[:MLA_NOPE],
                gq[MLA_NOPE:], jnp.cos(ang).T, jnp.sin(ang).T, shift_rows, lp)

    o = _attention(bounded.astype(jnp.int32).reshape(1), qt, k, vt, s_len)

    w_out1 = mla_w_out[0].astype(BF16)
    outs = []
    b0 = 0
    for x in xs:
        outs.append(_mla_out(o, proj1_3, h1_3, w_out1, b0, x.shape[0], s_len))
        b0 += x.shape[0]
    return tuple(outs)


def kernel(x_prompt, x_sample, meta_tokens, ln_g, gdn_w_in, gdn_conv_w, gdn_a_log, gdn_dt_bias,
           gdn_o_norm_g, gdn_w_out, mla_w_in, mla_q_norm_g, mla_kv_norm_g, mla_w_uq, mla_w_ukv,
           mla_qk_q_g, mla_qk_k_g, mla_w_out):
    return _trunk_all((x_prompt, x_sample), meta_tokens, ln_g, gdn_w_in, gdn_conv_w, gdn_a_log,
                      gdn_dt_bias, gdn_o_norm_g, gdn_w_out, mla_w_in, mla_q_norm_g, mla_kv_norm_g,
                      mla_w_uq, mla_w_ukv, mla_qk_q_g, mla_qk_k_g, mla_w_out)
```

```python
import functools
import math

import jax
import jax.numpy as jnp
from jax import lax
from jax.experimental import pallas as pl
from jax.experimental.pallas import tpu as pltpu

F32 = jnp.float32
BF16 = jnp.bfloat16

D_MODEL = 1024
N_META = 16
TAIL = 128
NORM_EPS = 1e-6

GDN_HEADS = 8
GDN_DK = 128
GDN_DV = 256
GDN_CONV = 5
GDN_CHUNK = 64
GDN_QK = GDN_HEADS * GDN_DK
GDN_V = GDN_HEADS * GDN_DV
GDN_CONV_CH = 2 * GDN_QK + GDN_V

MLA_HEADS = 16
MLA_Q_LORA = 512
MLA_KV_LORA = 256
MLA_NOPE = 128
MLA_ROPE = 64
MLA_DQK = MLA_NOPE + MLA_ROPE
MLA_DV = 128
MLA_V = MLA_HEADS * MLA_DV
MLA_DPAD = 256
ROPE_THETA = 10000.0

VMEM_LIMIT = 56 * 1024 * 1024


def _cparams(sem):
    return pltpu.CompilerParams(dimension_semantics=sem, vmem_limit_bytes=VMEM_LIMIT)


def _tile(n, target, mult):
    best = None
    t = mult
    while t <= min(n, target):
        if n % t == 0:
            best = t
        t += mult
    assert best is not None, (n, target, mult)
    return best


CONV_SUB = 64
CONV_LANES = GDN_DK
CONV_TC = 1024


def _qkv_conv_kernel(prev_ref, main_ref, next_ref, g_ref, w_ref, wba_ref, cw_ref, o_ref, ba_ref, ext_ref,
                     *, tr):
    half = GDN_CONV // 2
    nblk = GDN_CONV_CH // CONV_TC
    x = jnp.concatenate([prev_ref[0], main_ref[0], next_ref[0]], axis=0)
    ms = jnp.mean(x * x, axis=-1, keepdims=True)
    xn = (x * lax.rsqrt(ms + NORM_EPS) * g_ref[...]).astype(BF16)
    ba_ref[0] = jnp.dot(xn, wba_ref[...], preferred_element_type=F32)[8:8 + tr]

    def project(j):
        ext_ref[j] = jnp.dot(xn, w_ref[:, j * CONV_TC:(j + 1) * CONV_TC], preferred_element_type=F32)

    def conv_silu(j, sb, lb):
        lanes = slice(j * CONV_TC + lb * CONV_LANES, j * CONV_TC + (lb + 1) * CONV_LANES)
        nrow = CONV_SUB + 16
        x2 = ext_ref[j, sb * CONV_SUB:sb * CONV_SUB + nrow, lb * CONV_LANES:(lb + 1) * CONV_LANES]
        acc = x2[8:8 + CONV_SUB] * cw_ref[half:half + 1, lanes]
        for s in range(1, half + 1):
            down = pltpu.roll(x2, s, 0)
            acc = acc + down[8:8 + CONV_SUB] * cw_ref[half - s:half - s + 1, lanes]
            up = pltpu.roll(x2, nrow - s, 0)
            acc = acc + up[8:8 + CONV_SUB] * cw_ref[half + s:half + s + 1, lanes]
        return acc * jax.nn.sigmoid(acc)

    def finish(j):
        for sb in range(tr // CONV_SUB):
            rows = slice(sb * CONV_SUB, (sb + 1) * CONV_SUB)
            for lb in range(CONV_TC // CONV_LANES):
                y = conv_silu(j, sb, lb)
                c0 = j * CONV_TC + lb * CONV_LANES
                lanes = slice(c0, c0 + CONV_LANES)
                if c0 < 2 * GDN_QK:
                    qscale = GDN_DK ** -0.5 if c0 < GDN_QK else 1.0
                    ss = jnp.sum(y * y, axis=-1, keepdims=True)
                    y = y * (lax.rsqrt(ss + NORM_EPS) * qscale)
                o_ref[0, rows, lanes] = y.astype(BF16)

    project(0)
    for j in range(nblk):
        if j + 1 < nblk:
            project(j + 1)
        finish(j)


def _gdn_qkv_conv(h3, g, w_qkv, w_ba, conv_w8, lp):
    b = h3.shape[0]
    nba = w_ba.shape[1]
    tr = _tile(lp, 704, CONV_SUB)
    nb8 = lp // 8
    tb = tr // 8
    kern = functools.partial(_qkv_conv_kernel, tr=tr)
    const = lambda bi, i: (0, 0)
    return pl.pallas_call(
        kern,
        grid=(b, lp // tr),
        in_specs=[
            pl.BlockSpec((1, 8, D_MODEL), lambda bi, i: (bi, (i * tb + nb8 - 1) % nb8, 0)),
            pl.BlockSpec((1, tr, D_MODEL), lambda bi, i: (bi, i, 0)),
            pl.BlockSpec((1, 8, D_MODEL), lambda bi, i: (bi, ((i + 1) * tb) % nb8, 0)),
            pl.BlockSpec((1, D_MODEL), const),
            pl.BlockSpec((D_MODEL, GDN_CONV_CH), const),
            pl.BlockSpec((D_MODEL, nba), const),
            pl.BlockSpec((8, GDN_CONV_CH), const),
        ],
        out_specs=[
            pl.BlockSpec((1, tr, GDN_CONV_CH), lambda bi, i: (bi, i, 0)),
            pl.BlockSpec((1, tr, nba), lambda bi, i: (bi, i, 0)),
        ],
        out_shape=[
            jax.ShapeDtypeStruct((b, lp, GDN_CONV_CH), BF16),
            jax.ShapeDtypeStruct((b, lp, nba), F32),
        ],
        scratch_shapes=[pltpu.VMEM((GDN_CONV_CH // CONV_TC, tr + 16, CONV_TC), F32)],
        compiler_params=_cparams(("parallel", "parallel")),
        name="gdn_qkv_conv",
    )(h3, h3, h3, g, w_qkv, w_ba, conv_w8)


SCAN_NB = 4


def _split3(x):
    hi = x.astype(BF16)
    r1 = x - hi.astype(F32)
    mid = r1.astype(BF16)
    lo = (r1 - mid.astype(F32)).astype(BF16)
    return hi, mid, lo


def _gdn_scan_kernel(qf_ref, kf_ref, vf_ref, baf_ref, qb_ref, kb_ref, vb_ref, bab_ref, alog_ref, dtb_ref,
                     of_ref, ob_ref, s_ref, sb_ref, u_ref, qw_ref, attn_ref, kd_ref, eg_ref, *, nc):
    t = pl.program_id(1)
    c = GDN_CHUNK
    c2 = 2 * c
    npair = GDN_HEADS // 2
    o_refs = (of_ref, ob_ref)

    @pl.when(t == 0)
    def _():
        for ref in (s_ref, sb_ref, u_ref, qw_ref, attn_ref, kd_ref, eg_ref):
            ref[...] = jnp.zeros_like(ref)

    nb = of_ref.shape[0]
    nprob = nb * 2 * npair
    applied = [dict(sq=pi // npair, heads=(2 * (pi % npair), 2 * (pi % npair) + 1),
                    qs=[None, None], vnew=[None, None]) for pi in range(nprob)]

    def read_state(pi, hh):
        ap = applied[pi]
        rs = slice(hh * c, (hh + 1) * c)
        qws = jnp.dot(qw_ref[pi, hh], sb_ref[ap["sq"], ap["heads"][hh]], preferred_element_type=F32)
        ap["qs"][hh] = qws[:c]
        ap["vnew"][hh] = (u_ref[pi, rs, :] - qws[c:]).astype(BF16)

    def write_out(pi):
        ap = applied[pi]
        h0, h1 = ap["heads"]
        bi, d = divmod(ap["sq"], 2)
        vnew2 = jnp.concatenate(ap["vnew"], axis=0)
        o2 = jnp.concatenate(ap["qs"], axis=0) + jnp.dot(attn_ref[pi], vnew2, preferred_element_type=F32)
        o_refs[d][bi, :, h0 * GDN_DV:(h0 + 1) * GDN_DV] = o2[:c].astype(BF16)
        o_refs[d][bi, :, h1 * GDN_DV:(h1 + 1) * GDN_DV] = o2[c:].astype(BF16)

    def update_state(pi, hh):
        ap = applied[pi]
        sq, h = ap["sq"], ap["heads"][hh]
        rs = slice(hh * c, (hh + 1) * c)
        a_h = GDN_HEADS + h
        upd = lax.dot_general(kd_ref[pi, rs, :], ap["vnew"][hh], (((0,), (0,)), ((), ())),
                              preferred_element_type=F32)
        s_new = s_ref[sq, h] * eg_ref[sq, :, a_h:a_h + 1] + upd
        s_ref[sq, h] = s_new
        sb_ref[sq, h] = s_new.astype(BF16)

    pairs_hh = [(pi, hh) for pi in range(nprob) for hh in range(2)]
    apply_ops = ([functools.partial(read_state, pi, hh) for pi, hh in pairs_hh]
                 + [functools.partial(write_out, pi) for pi in range(nprob)]
                 + [functools.partial(update_state, pi, hh) for pi, hh in pairs_hh])

    def emit_apply(n):
        for _ in range(min(n, len(apply_ops))):
            apply_ops.pop(0)()

    emit_apply(len(pairs_hh) // 2)
    tp = jnp.minimum(t, nc - 1)
    ri = lax.broadcasted_iota(jnp.int32, (c2, c2), 0)
    ci = lax.broadcasted_iota(jnp.int32, (c2, c2), 1)
    same = (ri >> 6) == (ci >> 6)
    offdiag = ri != ci
    top =lax.broadcasted_iota(jnp.int32, (c2, 1), 0) < c
    left = lax.broadcasted_iota(jnp.int32, (1, c2), 1) < c
    row_id = lax.broadcasted_iota(jnp.int32, (c, 128), 0)

    in_refs = ((qf_ref, kf_ref, vf_ref, baf_ref), (qb_ref, kb_ref, vb_ref, bab_ref))
    seqs = []
    for sq in range(2 * nb):
        bi, d = divmod(sq, 2)
        q_ref, k_ref, v_ref, ba_ref = in_refs[d]
        seq = tp if d == 0 else nc - 1 - tp
        blk = jnp.where(seq < 2, nc - 2 + seq, seq - 2)
        first_valid = jnp.where(blk == nc - 2, c, jnp.where(blk == nc - 1, c - N_META, 0))
        valid = row_id >= first_valid
        ba = ba_ref[bi]
        beta = jnp.where(valid, jax.nn.sigmoid(ba), 0.0)
        xs = ba + dtb_ref[d]
        softplus = jnp.maximum(xs, 0.0) + jnp.log(1.0 + jnp.exp(-jnp.abs(xs)))
        g = jnp.where(valid, -jnp.exp(alog_ref[d]) * softplus, 0.0)
        incl = same & ((ri >= ci) if d == 0 else (ri <= ci))
        tri = jnp.where(incl, 1.0, 0.0).astype(BF16)
        ghi, gmid, glo = _split3(jnp.concatenate([g, g], axis=0))
        gc2 = (jnp.dot(tri, ghi, preferred_element_type=F32)
               + jnp.dot(tri, gmid, preferred_element_type=F32)
               + jnp.dot(tri, glo, preferred_element_type=F32))
        seqs.append(dict(q=q_ref.at[bi], k=k_ref.at[bi], v=v_ref.at[bi], incl=incl, gc2=gc2, gc2t=gc2.T,
                         beta2=jnp.concatenate([beta, beta], axis=0),
                         gtot=jnp.sum(g, axis=0, keepdims=True)))

    def stacked(ref, h0, h1, width):
        return jnp.concatenate([ref[:, h0 * width:(h0 + 1) * width],
                                ref[:, h1 * width:(h1 + 1) * width]], axis=0)

    probs = []
    for dd in seqs:
        for p in range(npair):
            h0, h1 = 2 * p, 2 * p + 1
            a0, a1 = GDN_HEADS + h0, GDN_HEADS + h1
            col = jnp.where(top, dd["gc2"][:, a0:a0 + 1], dd["gc2"][:, a1:a1 + 1])
            row = jnp.where(left, dd["gc2t"][a0:a0 + 1, :], dd["gc2t"][a1:a1 + 1, :])
            bcol = jnp.where(top, dd["beta2"][:, h0:h0 + 1], dd["beta2"][:, h1:h1 + 1])
            tot = jnp.where(top, dd["gtot"][:, a0:a0 + 1], dd["gtot"][:, a1:a1 + 1])
            dec = jnp.exp(jnp.where(dd["incl"], col - row, -jnp.inf))
            kst = stacked(dd["k"], h0, h1, GDN_DK)
            qst = stacked(dd["q"], h0, h1, GDN_DK)
            vst = stacked(dd["v"], h0, h1, GDN_DV)
            kf = kst.astype(F32)
            kb = kf * bcol
            sc = lax.dot_general(jnp.concatenate([qst, kb.astype(BF16)], axis=0), kst,
                                 (((1,), (1,)), ((), ())), preferred_element_type=F32)
            egc = jnp.exp(col)
            probs.append(dict(
                attn=(sc[:c2] * dec).astype(BF16),
                a=jnp.where(offdiag, sc[c2:] * dec, 0.0),
                rhs=jnp.concatenate([vst.astype(F32) * bcol, kb * egc], axis=1).astype(BF16),
                qg=(qst.astype(F32) * egc).astype(BF16),
                kd=(kf * jnp.exp(tot - col)).astype(BF16)))

    emit_apply(len(pairs_hh) // 2)

    for _ in _unit_lower_inverse_staged(probs):
        emit_apply(3 * nb)
    emit_apply(len(apply_ops))

    for pi, pr in enumerate(probs):
        uw = jnp.dot(pr["tinv"].astype(BF16), pr["rhs"], preferred_element_type=F32)
        u_ref[pi] = uw[:, :GDN_DV]
        w = uw[:, GDN_DV:].astype(BF16)
        for hh in range(2):
            rs = slice(hh * c, (hh + 1) * c)
            qw_ref[pi, hh] = jnp.concatenate([pr["qg"][rs], w[rs]], axis=0)
        attn_ref[pi] = pr["attn"]
        kd_ref[pi] = pr["kd"]
    for sq, dd in enumerate(seqs):
        eg_ref[sq] = jnp.exp(dd["gtot"])


def _unit_lower_inverse_staged(probs):
    c = GDN_CHUNK
    nside = 4
    ri = lax.broadcasted_iota(jnp.int32, (c, nside * c), 0)
    ci = lax.broadcasted_iota(jnp.int32, (c, nside * c), 1)
    lane_blk = ci >> 6
    within = ci & (c - 1)
    diag16 = (ri >> 4) == (within >> 4)
    eye = jnp.where(ri == within, 1.0, 0.0).astype(F32)
    left = lax.broadcasted_iota(jnp.int32, (1, 2 * c), 1) < c

    def blockdiag(y):
        return jnp.concatenate([jnp.where(lane_blk == r, y, 0.0) for r in range(nside)],
                               axis=0).astype(BF16)

    def mm(x, ybd):
        return jnp.dot(x.astype(BF16), ybd, preferred_element_type=F32)

    groups = []
    for g0 in range(0, len(probs), 2):
        pa, pb = probs[g0], probs[g0 + 1]
        a = jnp.concatenate([pa["a"][:c] + pa["a"][c:], pb["a"][:c] + pb["a"][c:]], axis=1)
        ad = jnp.where(diag16, a, 0.0)
        groups.append(dict(pairs=(pa, pb), ad=ad, an=blockdiag(a - ad), dinv=eye - ad))
    for gr in groups:
        gr["p"] = mm(gr["ad"], blockdiag(gr["ad"]))
    yield
    for gr in groups:
        pbd = blockdiag(gr["p"])
        gr["dinv"] = gr["dinv"] + mm(gr["dinv"], pbd)
        gr["p"] = mm(gr["p"], pbd)
    yield
    for gr in groups:
        pbd = blockdiag(gr["p"])
        gr["dinv"] = gr["dinv"] + mm(gr["dinv"], pbd)
        gr["p"] = mm(gr["p"], pbd)
    yield
    for gr in groups:
        gr["dinv"] = gr["dinv"] + mm(gr["dinv"], blockdiag(gr["p"]))
    yield
    for gr in groups:
        gr["m"] = mm(gr["dinv"], gr["an"])
    yield
    for gr in groups:
        gr["m2"] = blockdiag(mm(gr["m"], blockdiag(gr["m"])))
    yield
    for gr in groups:
        x = eye - gr["m"]
        gr["x"] = x + mm(x, gr["m2"])
    yield
    for gr in groups:
        t = mm(gr["x"], blockdiag(gr["dinv"]))
        for i, pr in enumerate(gr["pairs"]):
            half = t[:, 2 * c * i:2 * c * (i + 1)]
            pr["tinv"] = jnp.concatenate([jnp.where(left, half, 0.0), jnp.where(left, 0.0, half)],
                                         axis=0)


def _gdn_scan(qkv, proj3, alog_rows, dtb_rows, lp):
    b = qkv.shape[0]
    nc = lp // GDN_CHUNK
    ba_col0 = 0

    def blk_of(seq):
        return jnp.where(seq < 2, nc - 2 + seq, seq - 2)

    def chunk_specs(d):
        prep = lambda t: jnp.minimum(t, nc - 1)
        seq = prep if d == 0 else (lambda t: nc - 1 - prep(t))
        return [
            pl.BlockSpec((nb, GDN_CHUNK, GDN_QK), lambda bi, t: (bi, blk_of(seq(t)), 0)),
            pl.BlockSpec((nb, GDN_CHUNK, GDN_QK), lambda bi, t: (bi, blk_of(seq(t)), 1)),
            pl.BlockSpec((nb, GDN_CHUNK, GDN_V), lambda bi, t: (bi, blk_of(seq(t)), 1)),
            pl.BlockSpec((nb, GDN_CHUNK, 128), lambda bi, t: (bi, blk_of(seq(t)), ba_col0 + d)),
        ]

    nb = SCAN_NB if b % SCAN_NB == 0 else 1
    const = pl.BlockSpec((2, 1, 128), lambda bi, t: (0, 0, 0))
    kern = functools.partial(_gdn_scan_kernel, nc=nc)
    applied = lambda t: jnp.maximum(t - 1, 0)
    nseq = 2 * nb
    nprob = nseq * (GDN_HEADS // 2)
    c2 = 2 * GDN_CHUNK
    return pl.pallas_call(
        kern,
        grid=(b // nb, nc + 1),
        in_specs=chunk_specs(0) + chunk_specs(1) + [const, const],
        out_specs=[
            pl.BlockSpec((nb, GDN_CHUNK, GDN_V), lambda bi, t: (bi, blk_of(applied(t)), 0)),
            pl.BlockSpec((nb, GDN_CHUNK, GDN_V), lambda bi, t: (bi, blk_of(nc - 1 - applied(t)), 0)),
        ],
        out_shape=[jax.ShapeDtypeStruct((b, lp, GDN_V), BF16)] * 2,
        scratch_shapes=[
            pltpu.VMEM((nseq, GDN_HEADS, GDN_DK, GDN_DV), F32),
            pltpu.VMEM((nseq, GDN_HEADS, GDN_DK, GDN_DV), BF16),
            pltpu.VMEM((nprob, c2, GDN_DV), F32),
            pltpu.VMEM((nprob, 2, c2, GDN_DK), BF16),
            pltpu.VMEM((nprob, c2, c2), BF16),
            pltpu.VMEM((nprob, c2, GDN_DK), BF16),
            pltpu.VMEM((nseq, 1, 128), F32),
        ],
        compiler_params=_cparams(("parallel", "arbitrary")),
        name="gdn_scan",
    )(qkv, qkv, qkv, proj3, qkv, qkv, qkv, proj3, alog_rows, dtb_rows)


OUT_SUB = 256


def _gdn_out_kernel(of_ref, ob_ref, h_ref, ln_ref, wz_ref, g_ref, w_ref, o_ref):
    nsub = o_ref.shape[0] // OUT_SUB
    rows = [slice(sb * OUT_SUB, (sb + 1) * OUT_SUB) for sb in range(nsub)]

    def gate_logits(sb):
        x = h_ref[rows[sb], :]
        ms = jnp.mean(x * x, axis=-1, keepdims=True)
        xn = (x * lax.rsqrt(ms + NORM_EPS) * ln_ref[...]).astype(BF16)
        return jnp.dot(xn, wz_ref[...], preferred_element_type=F32)

    z_next = gate_logits(0)
    for sb in range(nsub):
        z = z_next
        if sb + 1 < nsub:
            z_next = gate_logits(sb + 1)
        o = of_ref[rows[sb], :].astype(F32) + ob_ref[rows[sb], :].astype(F32)
        gate = z * jax.nn.sigmoid(z)
        ys = []
        for h in range(GDN_HEADS):
            lanes = slice(h * GDN_DV, (h + 1) * GDN_DV)
            oh = o[:, lanes]
            ms = jnp.mean(oh * oh, axis=-1, keepdims=True)
            ys.append((oh * lax.rsqrt(ms + NORM_EPS) * g_ref[...] * gate[:, lanes]).astype(BF16))
        y = jnp.concatenate(ys, axis=1)
        o_ref[rows[sb], :] = h_ref[rows[sb], :] + jnp.dot(y, w_ref[...], preferred_element_type=F32)


def _gdn_out(o_fwd, o_bwd, h0, ln_row, w_z, g_row, w_out):
    rows = h0.shape[0]
    tm = _tile(rows, 768, OUT_SUB)
    const = lambda i: (0, 0)
    return pl.pallas_call(
        _gdn_out_kernel,
        grid=(rows // tm,),
        in_specs=[
            pl.BlockSpec((tm, GDN_V), lambda i: (i, 0)),
            pl.BlockSpec((tm, GDN_V), lambda i: (i, 0)),
            pl.BlockSpec((tm, D_MODEL), lambda i: (i, 0)),
            pl.BlockSpec((1, D_MODEL), const),
            pl.BlockSpec((D_MODEL, GDN_V), const),
            pl.BlockSpec((1, GDN_DV), const),
            pl.BlockSpec((GDN_V, D_MODEL), const),
        ],
        out_specs=pl.BlockSpec((tm, D_MODEL), lambda i: (i, 0)),
        out_shape=jax.ShapeDtypeStruct((rows, D_MODEL), F32),
        compiler_params=_cparams(("parallel",)),
        name="gdn_out",
    )(o_fwd, o_bwd, h0, ln_row, w_z, g_row, w_out)


def _rope(r, cos, nsin_lo, sin_hi):
    return r * cos + pltpu.roll(r, 96, 1) * nsin_lo + pltpu.roll(r, 32, 1) * sin_hi


Q_GROUP = 2


def _mla_q_kernel(cq_ref, g1_ref, wt_ref, ga_ref, gr_ref, cos_ref, sin_ref, shift_ref, qt_ref):
    cq = cq_ref[0].astype(F32)
    tm = cq.shape[0]
    ms = jnp.mean(cq * cq, axis=-1, keepdims=True)
    cqt = (cq * lax.rsqrt(ms + NORM_EPS) * g1_ref[...]).T.astype(BF16)
    half = MLA_ROPE // 2
    tile = lambda g: jnp.concatenate([g] * (tm // 128), axis=1)
    ga = tile(ga_ref[...])
    gr1, gr2 = tile(gr_ref[0:half, :]), tile(gr_ref[half:MLA_ROPE, :])
    cos, sin = cos_ref[...], sin_ref[...]
    pad_rows = tile(shift_ref[...]).astype(BF16)

    def project(g):
        rows = slice(g * Q_GROUP * MLA_DQK, (g + 1) * Q_GROUP * MLA_DQK)
        return jnp.dot(wt_ref[rows, :], cqt, preferred_element_type=F32)

    q_next = project(0)
    for h in range(MLA_HEADS):
        if h % Q_GROUP == 0:
            q = q_next
            if h + Q_GROUP < MLA_HEADS:
                q_next = project(h // Q_GROUP + 1)
        r0 = (h % Q_GROUP) * MLA_DQK
        a = q[r0:r0 + MLA_NOPE]
        x1 = q[r0 + MLA_NOPE:r0 + MLA_NOPE + half]
        x2 = q[r0 + MLA_NOPE + half:r0 + MLA_DQK]
        ss = (jnp.sum(a * a, axis=0, keepdims=True) + jnp.sum(x1 * x1, axis=0, keepdims=True)
              + jnp.sum(x2 * x2, axis=0, keepdims=True))
        inv = lax.rsqrt(ss * (1.0 / MLA_DQK) + NORM_EPS)
        x1 = x1 * inv * gr1
        x2 = x2 * inv * gr2
        qt_ref[0, h, 0:MLA_NOPE, :] = (a * inv * ga).astype(BF16)
        qt_ref[0, h, MLA_NOPE:MLA_NOPE + half, :] = (x1 * cos - x2 * sin).astype(BF16)
        qt_ref[0, h, MLA_NOPE + half:MLA_DQK, :] = (x2 * cos + x1 * sin).astype(BF16)
        qt_ref[0, h, MLA_DQK:MLA_DPAD, :] = pad_rows


def _mla_q(proj3, g1, w_uq_t, ga, gr, cos_t, sin_t, shift_rows, lp):
    b = proj3.shape[0]
    tm = _tile(lp, 384, 128)
    cqcol = MLA_V // MLA_Q_LORA
    const = lambda bi, i: (0, 0)
    return pl.pallas_call(
        _mla_q_kernel,
        grid=(b, lp // tm),
        in_specs=[
            pl.BlockSpec((1, tm, MLA_Q_LORA), lambda bi, i: (bi, i, cqcol)),
            pl.BlockSpec((1, MLA_Q_LORA), const),
            pl.BlockSpec((MLA_HEADS * MLA_DQK, MLA_Q_LORA), const),
            pl.BlockSpec((MLA_NOPE, 128), const),
            pl.BlockSpec((MLA_ROPE, 128), const),
            pl.BlockSpec((MLA_ROPE // 2, tm), lambda bi, i: (0, i)),
            pl.BlockSpec((MLA_ROPE // 2, tm), lambda bi, i: (0, i)),
            pl.BlockSpec((MLA_DPAD - MLA_DQK, 128), const),
        ],
        out_specs=pl.BlockSpec((1, MLA_HEADS, MLA_DPAD, tm), lambda bi, i: (bi, 0, 0, i)),
        out_shape=jax.ShapeDtypeStruct((b, MLA_HEADS, MLA_DPAD, lp), BF16),
        compiler_params=_cparams(("parallel", "parallel")),
        name="mla_q",
    )(proj3, g1, w_uq_t, ga, gr, cos_t, sin_t, shift_rows)


def _mla_kv_kernel(h_ref, ln_ref, ws_ref, wb_ref, g1_ref, w_ref, ga_ref, gr_ref, cos_ref, nsin_ref, sin_ref,
                   zq_ref, k_ref, vt_ref):
    x = h_ref[0]
    xn = (x * lax.rsqrt(jnp.mean(x * x, axis=-1, keepdims=True) + NORM_EPS) * ln_ref[...]).astype(BF16)
    small = jnp.dot(xn, ws_ref[...], preferred_element_type=F32)
    ckv = small[:, :MLA_KV_LORA]
    ms = jnp.mean(ckv * ckv, axis=-1, keepdims=True)
    cn = (ckv * lax.rsqrt(ms + NORM_EPS) * g1_ref[...]).astype(BF16)
    kv = jnp.dot(cn, w_ref[...], preferred_element_type=F32)
    zq_ref[0] = jnp.dot(xn, wb_ref[...], preferred_element_type=F32).astype(BF16)
    kpe = small[:, MLA_KV_LORA:]
    sq_pe = kpe * kpe
    kr = _rope(kpe * gr_ref[...], cos_ref[...], nsin_ref[...], sin_ref[...])
    one_hot = jnp.where(lax.broadcasted_iota(jnp.int32, (1, 128), 1) == MLA_ROPE, 1.0, 0.0).astype(F32)
    for h in range(MLA_HEADS):
        kn = kv[:, h * MLA_NOPE:(h + 1) * MLA_NOPE]
        ss = jnp.sum(kn * kn + sq_pe, axis=-1, keepdims=True)
        inv = lax.rsqrt(ss * (1.0 / MLA_DQK) + NORM_EPS)
        k_ref[0, h, :, 0:MLA_NOPE] = (kn * inv * ga_ref[...]).astype(BF16)
        k_ref[0, h, :, MLA_NOPE:MLA_DPAD] = (kr * inv + one_hot).astype(BF16)
        v = kv[:, MLA_HEADS * MLA_NOPE + h * MLA_DV:MLA_HEADS * MLA_NOPE + (h + 1) * MLA_DV]
        vt_ref[0, h] = v.T.astype(BF16)


def _mla_proj_kv(h3, ln_row, w_small, w_big, g1, w_ukv, ga, gr, cos, nsin, sin, lp):
    b = h3.shape[0]
    tm = _tile(lp, 384, 128)
    const = lambda bi, i: (0, 0)
    nbig = w_big.shape[1]
    return pl.pallas_call(
        _mla_kv_kernel,
        grid=(b, lp // tm),
        in_specs=[
            pl.BlockSpec((1, tm, D_MODEL), lambda bi, i: (bi, i, 0)),
            pl.BlockSpec((1, D_MODEL), const),
            pl.BlockSpec((D_MODEL, MLA_KV_LORA + 128), const),
            pl.BlockSpec((D_MODEL, nbig), const),
            pl.BlockSpec((1, MLA_KV_LORA), const),
            pl.BlockSpec((MLA_KV_LORA, MLA_HEADS * (MLA_NOPE + MLA_DV)), const),
            pl.BlockSpec((1, 128), const),
            pl.BlockSpec((1, 128), const),
            pl.BlockSpec((tm, 128), lambda bi, i: (i, 0)),
            pl.BlockSpec((tm, 128), lambda bi, i: (i, 0)),
            pl.BlockSpec((tm, 128), lambda bi, i: (i, 0)),
        ],
        out_specs=[
            pl.BlockSpec((1, tm, nbig), lambda bi, i: (bi, i, 0)),
            pl.BlockSpec((1, MLA_HEADS, tm, MLA_DPAD), lambda bi, i: (bi, 0, i, 0)),
            pl.BlockSpec((1, MLA_HEADS, MLA_DV, tm), lambda bi, i: (bi, 0, 0, i)),
        ],
        out_shape=[
            jax.ShapeDtypeStruct((b, lp, nbig), BF16),
            jax.ShapeDtypeStruct((b, MLA_HEADS, lp, MLA_DPAD), BF16),
            jax.ShapeDtypeStruct((b, MLA_HEADS, MLA_DV, lp), BF16),
        ],
        compiler_params=_cparams(("parallel", "parallel")),
        name="mla_proj_kv",
    )(h3, ln_row, w_small, w_big, g1, w_ukv, ga, gr, cos, nsin, sin)


ATT_TK = 1024
ATT_TQ = 512
ATT_AHEAD = 2
ATT_BOUND_MAX = 60.0


def _attn_kernel(bounded_ref, qt_ref, k_ref, vt_ref, o_ref, *, s_len):
    tq = qt_ref.shape[3]
    nstrip = tq // ATT_TQ
    chunks = [(ck * ATT_TK, ATT_TK) for ck in range(s_len // ATT_TK)] + [(s_len, TAIL)]
    units = [(i, s) for i in range(len(chunks)) for s in range(nstrip)]
    is_meta = lax.broadcasted_iota(jnp.int32, (TAIL, 1), 0) >= TAIL - N_META
    tail_bias = jnp.where(is_meta, 0.0, -jnp.inf).astype(F32)

    def scores(u):
        i, s = units[u]
        k0, nk = chunks[i]
        st = jnp.dot(k_ref[0, 0, k0:k0 + nk, :], qt_ref[0, 0, :, s * ATT_TQ:(s + 1) * ATT_TQ],
                     preferred_element_type=F32)
        return st + tail_bias if i == len(chunks) - 1 else st

    def run(bounded):
        m = [jnp.full((1, ATT_TQ), -jnp.inf, F32) for _ in range(nstrip)]
        l = [jnp.zeros((1, ATT_TQ), F32) for _ in range(nstrip)]
        acc = [jnp.zeros((MLA_DV, ATT_TQ), F32) for _ in range(nstrip)]
        pending = [scores(u) for u in range(min(ATT_AHEAD, len(units)))]
        for u, (i, s) in enumerate(units):
            if u + ATT_AHEAD < len(units):
                pending.append(scores(u + ATT_AHEAD))
            st = pending.pop(0)
            k0, nk = chunks[i]
            if bounded:
                p = jnp.exp2(st)
                l[s] = l[s] + jnp.sum(p, axis=0, keepdims=True)
                acc[s] = acc[s] + jnp.dot(vt_ref[0, 0, :, k0:k0 + nk], p.astype(BF16),
                                          preferred_element_type=F32)
            else:
                m_new = jnp.maximum(m[s], jnp.max(st, axis=0, keepdims=True))
                alpha = jnp.exp2(m[s] - m_new)
                p = jnp.exp2(st - m_new)
                l[s] = alpha * l[s] + jnp.sum(p, axis=0, keepdims=True)
                acc[s] = alpha * acc[s] + jnp.dot(vt_ref[0, 0, :, k0:k0 + nk], p.astype(BF16),
                                                  preferred_element_type=F32)
                m[s] = m_new
        for s in range(nstrip):
            o_ref[0, s * ATT_TQ:(s + 1) * ATT_TQ, :] = (acc[s] / l[s]).T.astype(o_ref.dtype)

    @pl.when(bounded_ref[0] == 1)
    def _():
        run(True)

    @pl.when(bounded_ref[0] != 1)
    def _():
        run(False)


def _attention(bounded, qt, k, vt, s_len):
    b = qt.shape[0]
    lp = k.shape[2]
    assert s_len % ATT_TK == 0 and lp == s_len + TAIL
    tq = _tile(s_len, 4096, ATT_TQ)
    kern = functools.partial(_attn_kernel, s_len=s_len)
    return pl.pallas_call(
        kern,
        grid=(b, MLA_HEADS, s_len // tq),
        in_specs=[
            pl.BlockSpec(memory_space=pltpu.SMEM),
            pl.BlockSpec((1, 1, MLA_DPAD, tq), lambda bi, h, i: (bi, h, 0, i)),
            pl.BlockSpec((1, 1, lp, MLA_DPAD), lambda bi, h, i: (bi, h, 0, 0)),
            pl.BlockSpec((1, 1, MLA_DV, lp), lambda bi, h, i: (bi, h, 0, 0)),
        ],
        out_specs=pl.BlockSpec((1, tq, MLA_DV), lambda bi, h, i: (bi, i, h)),
        out_shape=jax.ShapeDtypeStruct((b, s_len, MLA_V), BF16),
        compiler_params=_cparams(("parallel", "parallel", "arbitrary")),
        name="mla_attention",
    )(bounded, qt, k, vt)


def _mla_out_kernel(o_ref, z_ref, h_ref, w_ref, y_ref):
    z = z_ref[0].astype(F32)
    y = (o_ref[0].astype(F32) * (z * jax.nn.sigmoid(z))).astype(BF16)
    y_ref[0] = h_ref[0] + jnp.dot(y, w_ref[...], preferred_element_type=F32)


OUT_SLOTS = 3


def _mla_out_ring_kernel(o_hbm, z_hbm, h_hbm, w_ref, y_ref, o_buf, z_buf, h_buf, sem, *, b0, n_i, tm, total):
    s = pl.program_id(0) * n_i + pl.program_id(1)

    def copies(step, slot):
        bi = step // n_i + b0
        rows = pl.ds((step % n_i) * tm, tm)
        return (pltpu.make_async_copy(o_hbm.at[bi, rows, :], o_buf.at[slot], sem.at[0, slot]),
                pltpu.make_async_copy(z_hbm.at[bi, rows, pl.ds(0, MLA_V)], z_buf.at[slot], sem.at[1, slot]),
                pltpu.make_async_copy(h_hbm.at[bi, rows, :], h_buf.at[slot], sem.at[2, slot]))

    @pl.when(s == 0)
    def _():
        for ahead in range(min(OUT_SLOTS - 1, total)):
            for c in copies(ahead, ahead):
                c.start()

    @pl.when(s + OUT_SLOTS - 1 < total)
    def _():
        nxt = s + OUT_SLOTS - 1
        for c in copies(nxt, nxt % OUT_SLOTS):
            c.start()

    slot = s % OUT_SLOTS
    for c in copies(s, slot):
        c.wait()
    z = z_buf[slot].astype(F32)
    y = (o_buf[slot].astype(F32) * (z * jax.nn.sigmoid(z))).astype(BF16)
    y_ref[0] = h_buf[slot] + jnp.dot(y, w_ref[...], preferred_element_type=F32)


def _mla_out(o, proj3, h3, w_out, b0, nb, s_len):
    tm = _tile(s_len, 512, 128)
    n_i = s_len // tm
    kern = functools.partial(_mla_out_ring_kernel, b0=b0, n_i=n_i, tm=tm, total=nb * n_i)
    hbm = pl.BlockSpec(memory_space=pl.ANY)
    return pl.pallas_call(
        kern,
        grid=(nb, n_i),
        in_specs=[hbm, hbm, hbm, pl.BlockSpec((MLA_V, D_MODEL), lambda bi, i: (0, 0))],
        out_specs=pl.BlockSpec((1, tm, D_MODEL), lambda bi, i: (bi, i, 0)),
        out_shape=jax.ShapeDtypeStruct((nb, s_len, D_MODEL), F32),
        scratch_shapes=[
            pltpu.VMEM((OUT_SLOTS, tm, MLA_V), BF16),
            pltpu.VMEM((OUT_SLOTS, tm, MLA_V), BF16),
            pltpu.VMEM((OUT_SLOTS, tm, D_MODEL), F32),
            pltpu.SemaphoreType.DMA((3, OUT_SLOTS)),
        ],
        compiler_params=_cparams(("arbitrary", "arbitrary")),
        name="mla_out",
    )(o, proj3, h3, w_out)


def _pad_cols(w, n):
    return jnp.pad(w, ((0, 0), (0, n - w.shape[1])))


def _lane_row(v, n=128):
    return jnp.pad(v.astype(F32), (0, n - v.shape[0]))[None, :]


def _trunk_all(xs, meta_tokens, ln_g, gdn_w_in, gdn_conv_w, gdn_a_log, gdn_dt_bias, gdn_o_norm_g,
               gdn_w_out, mla_w_in, mla_q_norm_g, mla_kv_norm_g, mla_w_uq, mla_w_ukv, mla_qk_q_g,
               mla_qk_k_g, mla_w_out):
    s_len = xs[0].shape[1]
    assert all(x.shape[1] == s_len for x in xs) and s_len % 128 == 0
    lp = s_len + TAIL
    x_all = jnp.concatenate(xs, axis=0)
    b = x_all.shape[0]
    meta = jnp.broadcast_to(meta_tokens[None].astype(F32), (b, N_META, D_MODEL))
    h0 = jnp.concatenate([x_all, jnp.zeros((b, TAIL - N_META, D_MODEL), F32), meta], axis=1)
    h0 = h0.reshape(b * lp, D_MODEL)

    w_in = gdn_w_in[0]
    ba = w_in[:, GDN_CONV_CH + GDN_V:].reshape(D_MODEL, 2, 2, GDN_HEADS)
    ba_dir = [_pad_cols(jnp.concatenate([ba[:, 0, d], ba[:, 1, d]], axis=1), 128) for d in range(2)]
    conv_w8 = jnp.pad(gdn_conv_w[0], ((0, 8 - GDN_CONV), (0, 0)))
    qkv, gates = _gdn_qkv_conv(h0.reshape(b, lp, D_MODEL), ln_g[0][None, :],
                               w_in[:, :GDN_CONV_CH].astype(BF16),
                               jnp.concatenate(ba_dir, axis=1).astype(BF16), conv_w8, lp)

    lane_a = lambda v: jnp.pad(v.astype(F32), ((0, 0), (GDN_HEADS, 128 - 2 * GDN_HEADS)))[:, None, :]
    o_fwd, o_bwd = _gdn_scan(qkv, gates, lane_a(gdn_a_log[0]), lane_a(gdn_dt_bias[0]), lp)
    h1 = _gdn_out(o_fwd.reshape(b * lp, GDN_V), o_bwd.reshape(b * lp, GDN_V), h0, ln_g[0][None, :],
                  w_in[:, GDN_CONV_CH:GDN_CONV_CH + GDN_V].astype(BF16),
                  gdn_o_norm_g[0][None, :], gdn_w_out[0].astype(BF16))

    w_in1 = mla_w_in[0]
    o1 = MLA_Q_LORA
    o2_ = o1 + MLA_KV_LORA
    o3 = o2_ + MLA_ROPE
    w_small = jnp.concatenate([w_in1[:, o1:o2_], _pad_cols(w_in1[:, o2_:o3], 128)], axis=1).astype(BF16)
    w_big = jnp.concatenate([w_in1[:, o3:], w_in1[:, :o1]], axis=1).astype(BF16)
    h1_3 = h1.reshape(b, lp, D_MODEL)

    pos = jnp.concatenate([jnp.arange(s_len, dtype=F32) + N_META, jnp.zeros((TAIL - N_META,), F32),
                           jnp.arange(N_META, dtype=F32)])
    inv = ROPE_THETA ** (-jnp.arange(0, MLA_ROPE, 2, dtype=F32) / MLA_ROPE)
    ang = pos[:, None] * inv[None, :]
    zc = jnp.zeros_like(ang)
    cos_t = jnp.concatenate([jnp.cos(ang), jnp.cos(ang), zc, zc], axis=1)
    nsin_t = jnp.concatenate([-jnp.sin(ang), zc, zc, zc], axis=1)
    sin_t = jnp.concatenate([zc, jnp.sin(ang), zc, zc], axis=1)

    scale = MLA_DQK ** -0.5 * math.log2(math.e)
    gq = jnp.broadcast_to((mla_qk_q_g[0].astype(F32) * scale)[:, None], (MLA_DQK, 128))
    score_bound = (1.02 * MLA_DQK * scale * jnp.max(jnp.abs(mla_qk_q_g[0].astype(F32)))
                   * jnp.max(jnp.abs(mla_qk_k_g[0].astype(F32))))
    bounded = score_bound < ATT_BOUND_MAX
    shift_rows = jnp.zeros((MLA_DPAD - MLA_DQK, 128), F32).at[0, :].set(jnp.where(bounded, -score_bound, 0.0))
    w_ukv = mla_w_ukv[0].reshape(MLA_KV_LORA, MLA_HEADS, MLA_NOPE + MLA_DV)
    w_ukv = jnp.concatenate([w_ukv[:, :, :MLA_NOPE].reshape(MLA_KV_LORA, -1),
                             w_ukv[:, :, MLA_NOPE:].reshape(MLA_KV_LORA, -1)], axis=1).astype(BF16)
    gk = mla_qk_k_g[0].astype(F32)
    proj1_3, k, vt = _mla_proj_kv(h1_3, ln_g[1][None, :], w_small, w_big, mla_kv_norm_g[0][None, :], w_ukv,
                                  gk[None, :MLA_NOPE], _lane_row(gk[MLA_NOPE:]), cos_t, nsin_t, sin_t, lp)
    qt = _mla_q(proj1_3, mla_q_norm_g[0][None, :], mla_w_uq[0].T.astype(BF16), gq[:MLA_NOPE],
                gq[MLA_NOPE:], jnp.cos(ang).T, jnp.sin(ang).T, shift_rows, lp)

    o = _attention(bounded.astype(jnp.int32).reshape(1), qt, k, vt, s_len)

    w_out1 = mla_w_out[0].astype(BF16)
    outs = []
    b0 = 0
    for x in xs:
        outs.append(_mla_out(o, proj1_3, h1_3, w_out1, b0, x.shape[0], s_len))
        b0 += x.shape[0]
    return tuple(outs)


def kernel(x_prompt, x_sample, meta_tokens, ln_g, gdn_w_in, gdn_conv_w, gdn_a_log, gdn_dt_bias,
           gdn_o_norm_g, gdn_w_out, mla_w_in, mla_q_norm_g, mla_kv_norm_g, mla_w_uq, mla_w_ukv,
           mla_qk_q_g, mla_qk_k_g, mla_w_out):
    return _trunk_all((x_prompt, x_sample), meta_tokens, ln_g, gdn_w_in, gdn_conv_w, gdn_a_log,
                      gdn_dt_bias, gdn_o_norm_g, gdn_w_out, mla_w_in, mla_q_norm_g, mla_kv_norm_g,
                      mla_w_uq, mla_w_ukv, mla_qk_q_g, mla_qk_k_g, mla_w_out)
```
